```python
import math
import jax
import jax.numpy as jnp
from jax import lax
import numpy as np

D_MODEL = 1024
BATCH = 4
SEQ = 8192
DEPTH = 4

GDN_HEAD_DIM = 128
GDN_HEADS = D_MODEL // GDN_HEAD_DIM
GDN_WIDTH = GDN_HEADS * GDN_HEAD_DIM
GDN_CONV = 4
GDN_CHUNK = 64
NSA_HEAD_DIM = 128
NSA_HEADS = D_MODEL // NSA_HEAD_DIM
NSA_GROUPS = 2
NSA_WIDTH = NSA_HEADS * NSA_HEAD_DIM
NSA_KV_WIDTH = NSA_GROUPS * NSA_HEAD_DIM
NSA_CMP_LEN = 32
NSA_CMP_STRIDE = 16
NSA_SEL_LEN = 64
NSA_SEL_TOP = 16
NSA_WINDOW = 512
NSA_Q_BLOCK = 64
ROPE_THETA = 10000.0
N_EXPERTS = 64
N_EXPERT_GROUPS = 8
TOPK_GROUPS = 4
TOP_K = 8
EXPERT_FF = 256
SHARED_FF = 256
ROUTED_SCALE = 2.5
MOE_ROW_BLOCK = 256

LN_EPS = 1e-5
NEG_INF = -1e30
SEL_FORCE = 1e6

IN_SPLITS = (3 * GDN_WIDTH, GDN_WIDTH, GDN_HEADS, GDN_HEADS,
             NSA_WIDTH, 6 * NSA_KV_WIDTH, 3 * NSA_HEADS, D_MODEL, D_MODEL)
N_IN_COLS = sum(IN_SPLITS)

kernel_name = 'hybrid_gdn_nsa_moe_deepnorm_adaln'


def _split_points(sizes):
    pts, acc = [], 0
    for s in sizes[:-1]:
        acc += s
        pts.append(acc)
    return pts


def _normalize(x):
    xf = x.astype(jnp.float32)
    mu = jnp.mean(xf, -1, keepdims=True)
    var = jnp.mean(jnp.square(xf - mu), -1, keepdims=True)
    return ((xf - mu) * lax.rsqrt(var + LN_EPS)).astype(x.dtype)


def layer_norm(x, g, b):
    return _normalize(x) * g + b


def l2_normalize(x):
    xf = x.astype(jnp.float32)
    return (xf * lax.rsqrt(jnp.sum(xf * xf, -1, keepdims=True) + 1e-6)).astype(x.dtype)


def head_rms_norm(x, w):
    xf = x.astype(jnp.float32)
    return (xf * lax.rsqrt(jnp.mean(xf * xf, -1, keepdims=True) + 1e-6) * w.astype(jnp.float32)).astype(x.dtype)


def rope(x, pos):
    half = x.shape[-1] // 2
    inv_freq = ROPE_THETA ** (-jnp.arange(half, dtype=jnp.float32) / half)
    ang = pos.astype(jnp.float32)[..., None] * inv_freq
    cos = jnp.cos(ang)[:, :, None, :].astype(x.dtype)
    sin = jnp.sin(ang)[:, :, None, :].astype(x.dtype)
    x1, x2 = x[..., :half], x[..., half:]
    return jnp.concatenate([x1 * cos - x2 * sin, x2 * cos + x1 * sin], axis=-1)


def causal_depthwise_conv(x, w):
    K, C = w.shape
    return lax.conv_general_dilated(x, w[:, None, :], window_strides=(1,), padding=[(K - 1, 0)],
                                    dimension_numbers=('NWC', 'WIO', 'NWC'), feature_group_count=C)


def gated_delta_rule(q, k, v, g, beta):
    B, S, H, dk = q.shape
    dv = v.shape[-1]
    C = GDN_CHUNK
    N = S // C
    f32 = jnp.float32

    def chunks(t):
        return t.astype(f32).reshape(B, N, C, H, -1).transpose(1, 0, 3, 2, 4)

    qc, kc, vc = chunks(q), chunks(k), chunks(v)
    gc = jnp.cumsum(g.astype(f32).reshape(B, N, C, H).transpose(1, 0, 3, 2), axis=-1)
    bc = beta.astype(f32).reshape(B, N, C, H).transpose(1, 0, 3, 2)
    incl = jnp.tril(jnp.ones((C, C), bool))
    strict = jnp.tril(jnp.ones((C, C), bool), -1)
    diff = gc[..., :, None] - gc[..., None, :]
    decay = jnp.where(incl, jnp.exp(jnp.where(incl, diff, 0.0)), 0.0)
    kb = kc * bc[..., None]
    a_mat = jnp.eye(C, dtype=f32) + jnp.where(strict, jnp.einsum('nbhid,nbhjd->nbhij', kb, kc) * decay, 0.0)
    rhs = jnp.concatenate([vc * bc[..., None], kb * jnp.exp(gc)[..., None]], axis=-1)
    sol = lax.linalg.triangular_solve(a_mat, rhs, left_side=True, lower=True, unit_diagonal=True)
    u, w = sol[..., :dv], sol[..., dv:]
    attn = jnp.einsum('nbhid,nbhjd->nbhij', qc, kc) * decay

    def step(state, inp):
        q_i, k_i, u_i, w_i, g_i, attn_i = inp
        v_new = u_i - jnp.einsum('bhck,bhkv->bhcv', w_i, state)
        o = (jnp.einsum('bhck,bhkv->bhcv', q_i * jnp.exp(g_i)[..., None], state)
             + jnp.einsum('bhij,bhjv->bhiv', attn_i, v_new))
        g_last = g_i[..., -1:]
        state = (state * jnp.exp(g_last)[..., None]
                 + jnp.einsum('bhck,bhcv->bhkv', k_i * jnp.exp(g_last - g_i)[..., None], v_new))
        return state, o

    state0 = jnp.zeros((B, H, dk, dv), f32)
    _, o = lax.scan(step, state0, (qc, kc, u, w, gc, attn))
    return o.transpose(1, 0, 3, 2, 4).reshape(B, S, H, dv).astype(v.dtype)


def _masked_softmax(s, mask):
    return jax.nn.softmax(jnp.where(mask, s.astype(jnp.float32), NEG_INF), axis=-1)


def native_sparse_attention(q, k_cmp, v_cmp, k_sel, v_sel, k_win, v_win, gates, positions,
                            pos_k, pos_v, ck1, ck2, cv1, cv2):
    B, S, H, dh = q.shape
    G = NSA_GROUPS
    hpg = H // G
    Qb = NSA_Q_BLOCK
    Ls = NSA_SEL_LEN
    scale = dh ** -0.5
    q = rope(q, positions)
    k_sel = rope(k_sel, positions)
    k_win = rope(k_win, positions)

    n_cmp = (S - NSA_CMP_LEN) // NSA_CMP_STRIDE + 1
    cmp_idx = jnp.arange(n_cmp)[:, None] * NSA_CMP_STRIDE + jnp.arange(NSA_CMP_LEN)[None, :]
    cmp_end = cmp_idx[:, -1]

    def compress(t, pos_emb, w1, w2):
        blk = t[:, cmp_idx] + pos_emb[:, None, :]
        blk = blk.transpose(0, 1, 3, 2, 4).reshape(B, n_cmp, G, NSA_CMP_LEN * dh)
        return jax.nn.silu(blk @ w1) @ w2

    kc = rope(compress(k_cmp, pos_k, ck1, ck2), positions[:, cmp_end])
    vc = compress(v_cmp, pos_v, cv1, cv2)

    n_sel = S // Ls
    n_top = min(NSA_SEL_TOP, n_sel)
    sel_start = jnp.arange(n_sel) * Ls
    overlap = ((cmp_idx[:, :1] <= sel_start[None, :] + Ls - 1)
               & (cmp_end[:, None] >= sel_start[None, :])).astype(jnp.float32)
    k_blocks = k_sel.reshape(B, n_sel, Ls, G, dh).transpose(0, 3, 1, 2, 4)
    v_blocks = v_sel.reshape(B, n_sel, Ls, G, dh).transpose(0, 3, 1, 2, 4)
    pad = ((0, 0), (NSA_WINDOW, 0), (0, 0), (0, 0))
    k_win_p = jnp.pad(k_win, pad)
    v_win_p = jnp.pad(v_win, pad)

    nq = S // Qb
    q_blk = q.reshape(B, nq, Qb, G, hpg, dh).transpose(1, 0, 3, 4, 2, 5)
    g_blk = gates.reshape(B, nq, Qb, G, hpg, 3).transpose(1, 0, 3, 4, 2, 5)
    b_ix = jnp.arange(B)[:, None, None, None]
    g_ix = jnp.arange(G)[None, :, None, None]
    sel_off = jnp.arange(Ls)
    win_off = jnp.arange(NSA_WINDOW + Qb)
    blk_ids = jnp.arange(n_sel)

    def block(args):
        qb, gb, bi = args
        t = bi * Qb + jnp.arange(Qb)
        s_c = jnp.einsum('bghqd,bngd->bghqn', qb, kc) * scale
        valid_c = cmp_end[None, :] <= t[:, None]
        p_c = jnp.where(valid_c, _masked_softmax(s_c, valid_c), 0.0)
        o_c = jnp.einsum('bghqn,bngd->bghqd', p_c.astype(vc.dtype), vc)
        imp = jnp.einsum('bghqn,nj->bgqj', p_c, overlap)
        cur = t // Ls
        forced = ((blk_ids[None, :] == 0) | (blk_ids[None, :] == cur[:, None])
                  | (blk_ids[None, :] == cur[:, None] - 1))
        valid_s = sel_start[None, :] <= t[:, None]
        score = jnp.where(forced, SEL_FORCE, jnp.where(valid_s, imp, -1.0))
        _, sel = lax.top_k(score, n_top)
        kg = k_blocks[b_ix, g_ix, sel]
        vg = v_blocks[b_ix, g_ix, sel]
        s_s = jnp.einsum('bghqd,bgqnld->bghqnl', qb, kg) * scale
        tok = sel[..., None] * Ls + sel_off
        valid_t = (tok <= t[:, None, None])[:, :, None]
        p_s = _masked_softmax(s_s.reshape(B, G, hpg, Qb, n_top * Ls),
                              valid_t.reshape(B, G, 1, Qb, n_top * Ls)).reshape(s_s.shape)
        o_s = jnp.einsum('bghqnl,bgqnld->bghqd', p_s.astype(vg.dtype), vg)
        kw = lax.dynamic_slice_in_dim(k_win_p, bi * Qb, NSA_WINDOW + Qb, axis=1)
        vw = lax.dynamic_slice_in_dim(v_win_p, bi * Qb, NSA_WINDOW + Qb, axis=1)
        spos = bi * Qb - NSA_WINDOW + win_off
        valid_w = ((spos[None, :] >= 0) & (spos[None, :] <= t[:, None])
                   & (spos[None, :] > t[:, None] - NSA_WINDOW))
        s_w = jnp.einsum('bghqd,bkgd->bghqk', qb, kw) * scale
        p_w = _masked_softmax(s_w, valid_w)
        o_w = jnp.einsum('bghqk,bkgd->bghqd', p_w.astype(vw.dtype), vw)
        return gb[..., 0:1] * o_c + gb[..., 1:2] * o_s + gb[..., 2:3] * o_w

    o = lax.map(block, (q_blk, g_blk, jnp.arange(nq)))
    return o.transpose(1, 0, 4, 2, 3, 5).reshape(B, S, H * dh)


def hybrid_mixer(h, positions, w_in, conv_w, a_log, dt_bias, norm_w, w_gdn_branch,
                 pos_k, pos_v, ck1, ck2, cv1, cv2, w_nsa_branch, w_out):
    B, S, _ = h.shape
    f32 = jnp.float32
    (qkv_a, z_a, beta_a, decay_a, q_b, kv_b, gate_b, merge_a, merge_b) = jnp.split(
        h @ w_in, _split_points(IN_SPLITS), axis=-1)
    qkv_a = jax.nn.silu(causal_depthwise_conv(qkv_a, conv_w)).reshape(B, S, 3, GDN_HEADS, GDN_HEAD_DIM)
    q_a = l2_normalize(qkv_a[:, :, 0]) * (GDN_HEAD_DIM ** -0.5)
    k_a = l2_normalize(qkv_a[:, :, 1])
    v_a = qkv_a[:, :, 2]
    beta = jax.nn.sigmoid(beta_a.astype(f32))
    log_decay = -jnp.exp(a_log.astype(f32)) * jax.nn.softplus(decay_a.astype(f32) + dt_bias.astype(f32))
    o_a = gated_delta_rule(q_a, k_a, v_a, log_decay, beta)
    o_a = head_rms_norm(o_a, norm_w) * jax.nn.silu(z_a.reshape(B, S, GDN_HEADS, GDN_HEAD_DIM))
    y_a = o_a.reshape(B, S, GDN_WIDTH) @ w_gdn_branch
    q_b = q_b.reshape(B, S, NSA_HEADS, NSA_HEAD_DIM)
    k_cmp, v_cmp, k_sel, v_sel, k_win, v_win = [
        t.reshape(B, S, NSA_GROUPS, NSA_HEAD_DIM) for t in jnp.split(kv_b, 6, axis=-1)]
    gates = jax.nn.sigmoid(gate_b).reshape(B, S, NSA_HEADS, 3)
    o_b = native_sparse_attention(q_b, k_cmp, v_cmp, k_sel, v_sel, k_win, v_win, gates, positions,
                                  pos_k, pos_v, ck1, ck2, cv1, cv2)
    y_b = o_b @ w_nsa_branch
    mixed = jax.nn.sigmoid(merge_a) * y_a + jax.nn.sigmoid(merge_b) * y_b
    return mixed @ w_out


def shared_expert(h, w_gate, w_up, w_down):
    return (jax.nn.silu(h @ w_gate) * (h @ w_up)) @ w_down


def routed_experts(h, router_w, router_bias, w_gate, w_up, w_down):
    B, S, D = h.shape
    T = B * S
    E, K, BM = N_EXPERTS, TOP_K, MOE_ROW_BLOCK
    xf = h.reshape(T, D)
    scores = jax.nn.sigmoid((xf @ router_w).astype(jnp.float32))
    biased = scores + router_bias.astype(jnp.float32)
    grouped = biased.reshape(T, N_EXPERT_GROUPS, E // N_EXPERT_GROUPS)
    group_score = lax.top_k(grouped, 2)[0].sum(-1)
    _, group_idx = lax.top_k(group_score, TOPK_GROUPS)
    group_mask = jax.nn.one_hot(group_idx, N_EXPERT_GROUPS, dtype=jnp.float32).sum(-2) > 0
    expert_mask = jnp.repeat(group_mask, E // N_EXPERT_GROUPS, axis=-1)
    _, expert_idx = lax.top_k(jnp.where(expert_mask, biased, NEG_INF), K)
    weights = jnp.take_along_axis(scores, expert_idx, axis=-1)
    weights = weights / jnp.sum(weights, -1, keepdims=True) * ROUTED_SCALE
    TK = T * K
    e_flat = expert_idx.reshape(TK)
    order = jnp.argsort(e_flat)
    e_sorted = e_flat[order]
    tok_sorted = (order // K).astype(jnp.int32)
    w_sorted = weights.reshape(TK)[order]
    counts = jnp.bincount(e_flat, length=E)
    padded = (counts + BM - 1) // BM * BM
    pad_end = jnp.cumsum(padded)
    dest = (pad_end - padded)[e_sorted] + jnp.arange(TK) - (jnp.cumsum(counts) - counts)[e_sorted]
    n_blocks = (TK + E * (BM - 1) + BM - 1) // BM
    rows = n_blocks * BM
    row_tok = jnp.full((rows,), T, jnp.int32).at[dest].set(tok_sorted)
    row_w = jnp.zeros((rows,), jnp.float32).at[dest].set(w_sorted)
    block_expert = jnp.minimum(jnp.searchsorted(pad_end, jnp.arange(n_blocks) * BM, side='right'), E - 1)
    x_pad = jnp.concatenate([xf, jnp.zeros((1, D), xf.dtype)], axis=0)

    def step(acc, blk):
        tok, w, e = blk
        xb = x_pad[tok]
        hb = jax.nn.silu(xb @ w_gate[e]) * (xb @ w_up[e])
        yb = (hb @ w_down[e]) * w[:, None].astype(xb.dtype)
        return acc.at[tok].add(yb.astype(acc.dtype)), None

    acc, _ = lax.scan(step, jnp.zeros((T + 1, D), h.dtype),
                      (row_tok.reshape(n_blocks, BM), row_w.reshape(n_blocks, BM),
                       block_expert.astype(jnp.int32)))
    return acc[:T].reshape(B, S, D)


def setup_inputs(seed: int = 0) -> dict:
    key = jax.random.key(seed)
    keys = iter(jax.random.split(key, 40))
    f32 = jnp.float32
    L, D, E = DEPTH, D_MODEL, N_EXPERTS
    out_scale = (8.0 * DEPTH) ** -0.25

    def nrm(shape, scale):
        return jax.random.normal(next(keys), shape, f32) * scale

    x = nrm((BATCH, SEQ, D), 1.0)
    c = nrm((BATCH, D), 1.0)
    positions = (jnp.arange(SEQ, dtype=jnp.int32)[None, :]
                 + jax.random.randint(next(keys), (BATCH, 1), 0, 4096, dtype=jnp.int32))
    ada_w = nrm((L, D, 6 * D), 0.2 * D ** -0.5)
    ada_b = nrm((L, 6 * D), 0.01)
    w_in = nrm((L, D, N_IN_COLS), D ** -0.5)
    gdn_conv_w = nrm((L, GDN_CONV, 3 * GDN_WIDTH), GDN_CONV ** -0.5)
    gdn_a_log = jnp.log(jax.random.uniform(next(keys), (L, GDN_HEADS), f32, 1.0, 16.0))
    dt = jnp.exp(jax.random.uniform(next(keys), (L, GDN_HEADS), f32, math.log(1e-3), math.log(1e-1)))
    gdn_dt_bias = dt + jnp.log(-jnp.expm1(-dt))
    gdn_norm_w = 1.0 + nrm((L, GDN_HEAD_DIM), 0.1)
    w_gdn_branch = nrm((L, GDN_WIDTH, D), GDN_WIDTH ** -0.5)
    nsa_cmp_pos_k = nrm((L, NSA_CMP_LEN, NSA_HEAD_DIM), 0.5)
    nsa_cmp_pos_v = nrm((L, NSA_CMP_LEN, NSA_HEAD_DIM), 0.5)
    nsa_cmp_k_w1 = nrm((L, NSA_CMP_LEN * NSA_HEAD_DIM, NSA_HEAD_DIM), (NSA_CMP_LEN * NSA_HEAD_DIM) ** -0.5)
    nsa_cmp_k_w2 = nrm((L, NSA_HEAD_DIM, NSA_HEAD_DIM), NSA_HEAD_DIM ** -0.5)
    nsa_cmp_v_w1 = nrm((L, NSA_CMP_LEN * NSA_HEAD_DIM, NSA_HEAD_DIM), (NSA_CMP_LEN * NSA_HEAD_DIM) ** -0.5)
    nsa_cmp_v_w2 = nrm((L, NSA_HEAD_DIM, NSA_HEAD_DIM), NSA_HEAD_DIM ** -0.5)
    w_nsa_branch = nrm((L, NSA_WIDTH, D), NSA_WIDTH ** -0.5)
    w_out = nrm((L, D, D), out_scale * D ** -0.5)
    ln1_g = 1.0 + nrm((L, D), 0.1)
    ln1_b = nrm((L, D), 0.01)
    router_w = nrm((L, D, E), D ** -0.5)
    router_bias = nrm((L, E), 0.01)
    w_sh_gate = nrm((L, D, SHARED_FF), D ** -0.5)
    w_sh_up = nrm((L, D, SHARED_FF), D ** -0.5)
    w_sh_down = nrm((L, SHARED_FF, D), out_scale * SHARED_FF ** -0.5)
    w_e_gate = nrm((L, E, D, EXPERT_FF), D ** -0.5)
    w_e_up = nrm((L, E, D, EXPERT_FF), D ** -0.5)
    w_e_down = nrm((L, E, EXPERT_FF, D), out_scale * EXPERT_FF ** -0.5)
    ln2_g = 1.0 + nrm((L, D), 0.1)
    ln2_b = nrm((L, D), 0.01)
    return {'x': x, 'c': c, 'positions': positions, 'ada_w': ada_w, 'ada_b': ada_b, 'w_in': w_in,
            'gdn_conv_w': gdn_conv_w, 'gdn_a_log': gdn_a_log, 'gdn_dt_bias': gdn_dt_bias,
            'gdn_norm_w': gdn_norm_w, 'w_gdn_branch': w_gdn_branch,
            'nsa_cmp_pos_k': nsa_cmp_pos_k, 'nsa_cmp_pos_v': nsa_cmp_pos_v,
            'nsa_cmp_k_w1': nsa_cmp_k_w1, 'nsa_cmp_k_w2': nsa_cmp_k_w2,
            'nsa_cmp_v_w1': nsa_cmp_v_w1, 'nsa_cmp_v_w2': nsa_cmp_v_w2,
            'w_nsa_branch': w_nsa_branch, 'w_out': w_out, 'ln1_g': ln1_g, 'ln1_b': ln1_b,
            'router_w': router_w, 'router_bias': router_bias,
            'w_sh_gate': w_sh_gate, 'w_sh_up': w_sh_up, 'w_sh_down': w_sh_down,
            'w_e_gate': w_e_gate, 'w_e_up': w_e_up, 'w_e_down': w_e_down,
            'ln2_g': ln2_g, 'ln2_b': ln2_b}


def reference(x, c, positions, ada_w, ada_b, w_in, gdn_conv_w, gdn_a_log, gdn_dt_bias, gdn_norm_w,
              w_gdn_branch, nsa_cmp_pos_k, nsa_cmp_pos_v, nsa_cmp_k_w1, nsa_cmp_k_w2, nsa_cmp_v_w1,
              nsa_cmp_v_w2, w_nsa_branch, w_out, ln1_g, ln1_b, router_w, router_bias,
              w_sh_gate, w_sh_up, w_sh_down, w_e_gate, w_e_up, w_e_down, ln2_g, ln2_b):
    alpha = (2.0 * DEPTH) ** 0.25
    cond = jax.nn.silu(c)
    for l in range(DEPTH):
        ada = cond @ ada_w[l] + ada_b[l]
        sh1, sc1, gt1, sh2, sc2, gt2 = jnp.split(ada[:, None, :], 6, axis=-1)
        h = _normalize(x) * (1.0 + sc1) + sh1
        y = hybrid_mixer(h, positions, w_in[l], gdn_conv_w[l], gdn_a_log[l], gdn_dt_bias[l], gdn_norm_w[l],
                         w_gdn_branch[l], nsa_cmp_pos_k[l], nsa_cmp_pos_v[l], nsa_cmp_k_w1[l],
                         nsa_cmp_k_w2[l], nsa_cmp_v_w1[l], nsa_cmp_v_w2[l], w_nsa_branch[l], w_out[l])
        x = layer_norm(alpha * x + (1.0 + gt1) * y, ln1_g[l], ln1_b[l])
        h = _normalize(x) * (1.0 + sc2) + sh2
        y = (shared_expert(h, w_sh_gate[l], w_sh_up[l], w_sh_down[l])
             + routed_experts(h, router_w[l], router_bias[l], w_e_gate[l], w_e_up[l], w_e_down[l]))
        x = layer_norm(alpha * x + (1.0 + gt2) * y, ln2_g[l], ln2_b[l])
    return x
```

```python
import functools
import math

import jax
import jax.numpy as jnp
from jax import lax
from jax.experimental import pallas as pl
from jax.experimental.pallas import tpu as pltpu

F32 = jnp.float32
BF16 = jnp.bfloat16

HEAD_DIM = 128
GDN_CONV = 4
GDN_CHUNK = 64
NSA_GROUPS = 2
NSA_CMP_LEN = 32
NSA_CMP_STRIDE = 16
NSA_SEL_LEN = 64
NSA_SEL_TOP = 16
NSA_WINDOW = 512
ROPE_THETA = 10000.0
N_EXPERTS = 64
N_EXPERT_GROUPS = 8
TOPK_GROUPS = 4
TOP_K = 8
ROUTED_SCALE = 2.5
LN_EPS = 1e-5
NEG_INF = -1e30
SEL_FORCE = 1e6
MOE_ROW_BLOCK = 256

_ARB = "arbitrary"


def _cparams(n_axes):
    return pltpu.CompilerParams(dimension_semantics=(_ARB,) * n_axes)


def _sigmoid(x):
    return 1.0 / (1.0 + jnp.exp(-x))


def _silu(x):
    return x * _sigmoid(x)


def _dot(a, b):
    return jnp.dot(a, b, preferred_element_type=F32)


def _dot_nt(a, b):
    return lax.dot_general(a, b, (((1,), (1,)), ((), ())), preferred_element_type=F32)


def _dot_tn(a, b):
    return lax.dot_general(a, b, (((0,), (0,)), ((), ())), preferred_element_type=F32)


def _split3(x):
    x0 = x.astype(BF16)
    r1 = x - x0.astype(F32)
    x1 = r1.astype(BF16)
    x2 = (r1 - x1.astype(F32)).astype(BF16)
    return x0, x1, x2


def _dot_sel_right(x, sel_bf16):
    x0, x1, x2 = _split3(x)
    return _dot(x0, sel_bf16) + _dot(x1, sel_bf16) + _dot(x2, sel_bf16)


def _dot_sel_left(sel_bf16, x):
    x0, x1, x2 = _split3(x)
    return _dot(sel_bf16, x0) + _dot(sel_bf16, x1) + _dot(sel_bf16, x2)


def _normalize_rows(x):
    mu = jnp.mean(x, axis=-1, keepdims=True)
    xc = x - mu
    var = jnp.mean(xc * xc, axis=-1, keepdims=True)
    return xc * lax.rsqrt(var + LN_EPS)


def _ada_kernel(c_ref, w_ref, b_ref, o_ref):
    cond = _silu(c_ref[...])
    o_ref[0] = jnp.dot(cond, w_ref[0], preferred_element_type=F32,
                       precision=lax.Precision.HIGHEST) + b_ref[0]


def _ada(c, ada_w, ada_b):
    L, D, N = ada_w.shape
    B = c.shape[0]
    tn = min(N, 1536)
    return pl.pallas_call(
        _ada_kernel,
        grid=(L, N // tn),
        in_specs=[pl.BlockSpec((B, D), lambda l, j: (0, 0)),
                  pl.BlockSpec((1, D, tn), lambda l, j: (l, 0, j)),
                  pl.BlockSpec((1, 1, tn), lambda l, j: (l, 0, j))],
        out_specs=pl.BlockSpec((1, B, tn), lambda l, j: (l, 0, j)),
        out_shape=jax.ShapeDtypeStruct((L, B, N), F32),
        compiler_params=_cparams(2),
        name="ada",
    )(c, ada_w, ada_b.reshape(L, 1, N))


def _proj_kernel(x_ref, sh_ref, sc_ref, w_ref, o_ref, h_ref):
    @pl.when(pl.program_id(2) == 0)
    def _():
        h = _normalize_rows(x_ref[0]) * (1.0 + sc_ref[0]) + sh_ref[0]
        h_ref[...] = h.astype(BF16)

    o_ref[0] = _dot(h_ref[...], w_ref[...]).astype(o_ref.dtype)


def _proj(x, ada_l, sh_col, sc_col, w, out_dtype, tn):
    B, S, D = x.shape
    N = w.shape[1]
    tm = min(S, 1024)
    return pl.pallas_call(
        _proj_kernel,
        grid=(B, S // tm, N // tn),
        in_specs=[pl.BlockSpec((1, tm, D), lambda b, i, j: (b, i, 0)),
                  pl.BlockSpec((1, 1, D), lambda b, i, j: (b, 0, sh_col)),
                  pl.BlockSpec((1, 1, D), lambda b, i, j: (b, 0, sc_col)),
                  pl.BlockSpec((D, tn), lambda b, i, j: (0, j))],
        out_specs=pl.BlockSpec((1, tm, tn), lambda b, i, j: (b, i, j)),
        out_shape=jax.ShapeDtypeStruct((B, S, N), out_dtype),
        scratch_shapes=[pltpu.VMEM((tm, D), BF16)],
        compiler_params=_cparams(3),
        name="proj",
    )(x, ada_l, ada_l, w)


def _gdn_prep_kernel(x_ref, w_ref, q_ref, k_ref, v_ref, carry_ref):
    ts = x_ref.shape[1]
    width = q_ref.shape[2]
    nh = width // HEAD_DIM

    @pl.when(pl.program_id(1) == 0)
    def _():
        carry_ref[...] = jnp.zeros_like(carry_ref)

    for part, o_ref in enumerate((q_ref, k_ref, v_ref)):
        for h in range(nh):
            c0 = part * width + h * HEAD_DIM
            cols = slice(c0, c0 + HEAD_DIM)
            xx = jnp.concatenate([carry_ref[:, cols], x_ref[0, :, cols].astype(F32)], axis=0)
            w = w_ref[:, cols]
            y = xx[8:8 + ts] * w[3:4]
            for kk in range(GDN_CONV - 1):
                off = 8 - (GDN_CONV - 1) + kk
                y = y + xx[off:off + ts] * w[kk:kk + 1]
            y = _silu(y)
            if part < 2:
                y = y * lax.rsqrt(jnp.sum(y * y, axis=-1, keepdims=True) + 1e-6)
            if part == 0:
                y = y * (HEAD_DIM ** -0.5)
            o_ref[0, :, h * HEAD_DIM:(h + 1) * HEAD_DIM] = y.astype(o_ref.dtype)
    carry_ref[...] = x_ref[0, ts - 8:ts, :].astype(F32)


def _gdn_prep(big, conv_w, width):
    B, S, _ = big.shape
    ts = min(S, 512)
    out = jax.ShapeDtypeStruct((B, S, width), BF16)
    ospec = pl.BlockSpec((1, ts, width), lambda b, i: (b, i, 0))
    return pl.pallas_call(
        _gdn_prep_kernel,
        grid=(B, S // ts),
        in_specs=[pl.BlockSpec((1, ts, 3 * width), lambda b, i: (b, i, 0)),
                  pl.BlockSpec((GDN_CONV, 3 * width), lambda b, i: (0, 0))],
        out_specs=[ospec, ospec, ospec],
        out_shape=[out, out, out],
        scratch_shapes=[pltpu.VMEM((8, 3 * width), F32)],
        compiler_params=_cparams(2),
        name="gdn_prep",
    )(big, conv_w)


def _softplus(x):
    return jnp.maximum(x, 0.0) + jnp.log(1.0 + jnp.exp(-jnp.abs(x)))


def _gdn_kernel(q_ref, k_ref, v_ref, z_ref, sm_ref, smt_ref, alog_ref, alogt_ref, dtb_ref, dtbt_ref,
                nw_ref, o_ref, state_ref):
    ts = q_ref.shape[1]
    nh = q_ref.shape[2] // HEAD_DIM
    C = GDN_CHUNK

    @pl.when(pl.program_id(1) == 0)
    def _():
        state_ref[...] = jnp.zeros_like(state_ref)

    ii = lax.broadcasted_iota(jnp.int32, (C, C), 0)
    jj = lax.broadcasted_iota(jnp.int32, (C, C), 1)
    incl = ii >= jj
    strict = ii > jj
    tril = jnp.where(incl, 1.0, 0.0).astype(BF16)
    triu = jnp.where(jj >= ii, 1.0, 0.0).astype(BF16)
    nw = nw_ref[...]

    def chunk(c, carry):
        rows = pl.ds(pl.multiple_of(c * C, C), C)
        sm = sm_ref[0, rows, :]
        beta = _sigmoid(sm[:, 0:nh])
        g = -jnp.exp(alog_ref[...]) * _softplus(sm[:, nh:2 * nh] + dtb_ref[...])
        gt = -jnp.exp(alogt_ref[...]) * _softplus(smt_ref[0, c] + dtbt_ref[...])
        gc = _dot_sel_left(tril, g)
        gcr = _dot_sel_right(gt, triu)
        for h in range(nh):
            cols = slice(h * HEAD_DIM, (h + 1) * HEAD_DIM)
            q = q_ref[0, rows, cols].astype(F32)
            k = k_ref[0, rows, cols].astype(F32)
            v = v_ref[0, rows, cols].astype(F32)
            b_col = beta[:, h:h + 1]
            g_col = gc[:, h:h + 1]
            g_row = gcr[h:h + 1, :]
            decay = jnp.where(incl, jnp.exp(jnp.where(incl, g_col - g_row, 0.0)), 0.0)
            kb = k * b_col
            k16 = k.astype(BF16)
            a = jnp.where(strict, _dot_nt(kb.astype(BF16), k16) * decay, 0.0)
            attn = _dot_nt(q.astype(BF16), k16) * decay
            eg = jnp.exp(g_col)
            r = jnp.concatenate([v * b_col, kb * eg], axis=1)
            r = r - _dot(a.astype(BF16), r.astype(BF16))
            p = a
            n = 2
            while n < C:
                p = _dot(p.astype(BF16), p.astype(BF16))
                r = r + _dot(p.astype(BF16), r.astype(BF16))
                n *= 2
            u = r[:, :HEAD_DIM]
            w = r[:, HEAD_DIM:]
            st = state_ref[h]
            st16 = st.astype(BF16)
            v_new = u - _dot(w.astype(BF16), st16)
            o = _dot((q * eg).astype(BF16), st16) + _dot(attn.astype(BF16), v_new.astype(BF16))
            g_last = g_col[C - 1:C, :]
            kd = k * jnp.exp(g_last - g_col)
            state_ref[h] = st * jnp.exp(g_last) + _dot_tn(kd.astype(BF16), v_new.astype(BF16))
            o = o * lax.rsqrt(jnp.mean(o * o, axis=-1, keepdims=True) + 1e-6) * nw
            z = z_ref[0, rows, cols].astype(F32)
            o_ref[0, rows, cols] = (o * _silu(z)).astype(o_ref.dtype)
        return carry

    lax.fori_loop(0, ts // C, chunk, 0)


def _gdn(q, k, v, big, z_blk, small, small_t, a_log, dt_bias, norm_w):
    B, S, W = q.shape
    nh = W // HEAD_DIM
    ts = min(S, 512)
    spec = pl.BlockSpec((1, ts, W), lambda b, i: (b, i, 0))
    full = lambda shape: pl.BlockSpec(shape, lambda b, i: (0,) * len(shape))
    return pl.pallas_call(
        _gdn_kernel,
        grid=(B, S // ts),
        in_specs=[spec, spec, spec,
                  pl.BlockSpec((1, ts, W), lambda b, i: (b, i, z_blk)),
                  pl.BlockSpec((1, ts, small.shape[2]), lambda b, i: (b, i, 0)),
                  pl.BlockSpec((1, ts // GDN_CHUNK, nh, GDN_CHUNK), lambda b, i: (b, i, 0, 0)),
                  full((1, nh)), full((nh, 1)), full((1, nh)), full((nh, 1)),
                  full((1, HEAD_DIM))],
        out_specs=spec,
        out_shape=jax.ShapeDtypeStruct((B, S, W), BF16),
        scratch_shapes=[pltpu.VMEM((nh, HEAD_DIM, HEAD_DIM), F32)],
        compiler_params=_cparams(2),
        name="gdn",
    )(q, k, v, big, small, small_t, a_log.reshape(1, nh), a_log.reshape(nh, 1),
      dt_bias.reshape(1, nh), dt_bias.reshape(nh, 1), norm_w.reshape(1, HEAD_DIM))


def _rope_table_kernel(pos_ref, invf_ref, sign_ref, cos_ref, sin_ref):
    ang = pos_ref[0].astype(F32) * invf_ref[...]
    cos_ref[0] = jnp.cos(ang)
    sin_ref[0] = jnp.sin(ang) * sign_ref[...]


def _rope_tables(positions):
    B, S = positions.shape
    half = HEAD_DIM // 2
    inv = ROPE_THETA ** (-jnp.arange(half, dtype=F32) / half)
    invf = jnp.concatenate([inv, inv]).reshape(1, HEAD_DIM)
    sign = jnp.concatenate([-jnp.ones((half,), F32), jnp.ones((half,), F32)]).reshape(1, HEAD_DIM)
    ts = min(S, 1024)
    out = jax.ShapeDtypeStruct((B, S, HEAD_DIM), F32)
    ospec = pl.BlockSpec((1, ts, HEAD_DIM), lambda b, i: (b, i, 0))
    return pl.pallas_call(
        _rope_table_kernel,
        grid=(B, S // ts),
        in_specs=[pl.BlockSpec((1, ts, 1), lambda b, i: (b, i, 0)),
                  pl.BlockSpec((1, HEAD_DIM), lambda b, i: (0, 0)),
                  pl.BlockSpec((1, HEAD_DIM), lambda b, i: (0, 0))],
        out_specs=[ospec, ospec],
        out_shape=[out, out],
        compiler_params=_cparams(2),
        name="rope_table",
    )(positions.reshape(B, S, 1), invf, sign)


def _rope_rows(x, cosf, sinf):
    return x * cosf + pltpu.roll(x, HEAD_DIM // 2, 1) * sinf


def _rope_apply_kernel(q_ref, ks_ref, kw_ref, cos_ref, sin_ref, qo_ref, kso_ref, kwo_ref):
    cosf = cos_ref[0]
    sinf = sin_ref[0]
    for src, dst in ((q_ref, qo_ref), (ks_ref, kso_ref), (kw_ref, kwo_ref)):
        for h in range(src.shape[2] // HEAD_DIM):
            cols = slice(h * HEAD_DIM, (h + 1) * HEAD_DIM)
            dst[0, :, cols] = _rope_rows(src[0, :, cols].astype(F32), cosf, sinf).astype(dst.dtype)


def _rope_apply(big, q_blk, ksel_blk, kwin_blk, wq, wkv, cosf, sinf):
    B, S, _ = big.shape
    ts = min(S, 512)
    tab = pl.BlockSpec((1, ts, HEAD_DIM), lambda b, i: (b, i, 0))
    return pl.pallas_call(
        _rope_apply_kernel,
        grid=(B, S // ts),
        in_specs=[pl.BlockSpec((1, ts, wq), lambda b, i: (b, i, q_blk)),
                  pl.BlockSpec((1, ts, wkv), lambda b, i: (b, i, ksel_blk)),
                  pl.BlockSpec((1, ts, wkv), lambda b, i: (b, i, kwin_blk)),
                  tab, tab],
        out_specs=[pl.BlockSpec((1, ts, wq), lambda b, i: (b, i, 0)),
                   pl.BlockSpec((1, ts, wkv), lambda b, i: (b, i, 0)),
                   pl.BlockSpec((1, ts, wkv), lambda b, i: (b, i, 0))],
        out_shape=[jax.ShapeDtypeStruct((B, S, wq), BF16),
                   jax.ShapeDtypeStruct((B, S, wkv), BF16),
                   jax.ShapeDtypeStruct((B, S, wkv), BF16)],
        compiler_params=_cparams(2),
        name="rope_apply",
    )(big, big, big, cosf, sinf)


def _compress_kernel(hb_ref, pos_ref, w1_ref, w2_ref, cos_ref, sin_ref, o_ref):
    hb = hb_ref[0, 0]
    w1 = w1_ref[0]
    half = hb.shape[1]
    p0 = _dot(hb, w1[:half])
    p1 = _dot(hb, w1[half:])
    nc = p0.shape[0]
    pos8 = jnp.broadcast_to(pos_ref[0], (8, 2 * half)).astype(BF16)
    pb = _dot(pos8, w1)[0:1]
    pre = p0 + pltpu.roll(p1, nc - 1, 0) + pb
    out = _dot(_silu(pre).astype(BF16), w2_ref[0])
    roped = _rope_rows(out, cos_ref[0], sin_ref[0])
    is_key = pl.program_id(1) < NSA_GROUPS
    o_ref[0, 0] = jnp.where(is_key, roped, out).astype(o_ref.dtype)


def _compress(hb, pos_flat, w1, w2, cosc, sinc):
    B, J, NC, HW = hb.shape
    G = NSA_GROUPS
    return pl.pallas_call(
        _compress_kernel,
        grid=(B, J),
        in_specs=[pl.BlockSpec((1, 1, NC, HW), lambda b, j: (b, j, 0, 0)),
                  pl.BlockSpec((1, 1, 2 * HW), lambda b, j: (j // G, 0, 0)),
                  pl.BlockSpec((1, 2 * HW, HEAD_DIM), lambda b, j: (j // G, 0, 0)),
                  pl.BlockSpec((1, HEAD_DIM, HEAD_DIM), lambda b, j: (j // G, 0, 0)),
                  pl.BlockSpec((1, NC, HEAD_DIM), lambda b, j: (b, 0, 0)),
                  pl.BlockSpec((1, NC, HEAD_DIM), lambda b, j: (b, 0, 0))],
        out_specs=pl.BlockSpec((1, 1, NC, HEAD_DIM), lambda b, j: (b, j, 0, 0)),
        out_shape=jax.ShapeDtypeStruct((B, J, NC, HEAD_DIM), BF16),
        compiler_params=_cparams(2),
        name="nsa_compress",
    )(hb, pos_flat, w1, w2, cosc, sinc)


def _gate_col(g_ref, nh_gdn, group, hpg, h, branch):
    col = None
    for gi in range(NSA_GROUPS):
        c = 2 * nh_gdn + (gi * hpg + h) * 3 + branch
        cand = g_ref[0, :, c:c + 1]
        col = cand if col is None else jnp.where(group == gi, cand, col)
    return _sigmoid(col)


def _cmp_attn_kernel(q_ref, kc_ref, vc_ref, sm_ref, o_ref, sel_ref, *, nh_gdn, hpg):
    tq = q_ref.shape[1]
    nc = kc_ref.shape[2]
    nb = sel_ref.shape[3]
    g = pl.program_id(1)
    t0 = pl.program_id(2) * tq
    scale = HEAD_DIM ** -0.5
    t_nc = t0 + lax.broadcasted_iota(jnp.int32, (tq, nc), 0)
    n_nc = lax.broadcasted_iota(jnp.int32, (tq, nc), 1)
    valid = (n_nc * NSA_CMP_STRIDE + (NSA_CMP_LEN - 1)) <= t_nc
    kc = kc_ref[0, 0]
    vc = vc_ref[0, 0]
    psum = jnp.zeros((tq, nc), F32)
    for h in range(hpg):
        cols = slice(h * HEAD_DIM, (h + 1) * HEAD_DIM)
        s = _dot_nt(q_ref[0, :, cols], kc) * scale
        s = jnp.where(valid, s, NEG_INF)
        e = jnp.exp(s - jnp.max(s, axis=-1, keepdims=True))
        p = e / jnp.sum(e, axis=-1, keepdims=True)
        p = jnp.where(valid, p, 0.0)
        gate = _gate_col(sm_ref, nh_gdn, g, hpg, h, 0)
        o_ref[0, :, cols] = (_dot(p.astype(BF16), vc) * gate).astype(o_ref.dtype)
        psum = psum + p
    cn = lax.broadcasted_iota(jnp.int32, (nc, nb), 0) * NSA_CMP_STRIDE
    sj = lax.broadcasted_iota(jnp.int32, (nc, nb), 1) * NSA_SEL_LEN
    overlap = jnp.where((cn <= sj + (NSA_SEL_LEN - 1)) & (cn + (NSA_CMP_LEN - 1) >= sj), 1.0, 0.0).astype(BF16)
    imp = _dot_sel_right(psum, overlap)
    t_nb = t0 + lax.broadcasted_iota(jnp.int32, (tq, nb), 0)
    blk = lax.broadcasted_iota(jnp.int32, (tq, nb), 1)
    cur = t_nb // NSA_SEL_LEN
    forced = (blk == 0) | (blk == cur) | (blk == cur - 1)
    score = jnp.where(forced, SEL_FORCE, jnp.where(blk * NSA_SEL_LEN <= t_nb, imp, -1.0))
    sel = jnp.zeros((tq, nb), F32)
    blk_f = blk.astype(F32)
    for _ in range(min(NSA_SEL_TOP, nb)):
        _, first = _first_argmax(score, blk_f, float(nb))
        pick = blk_f == first
        sel = jnp.where(pick, 1.0, sel)
        score = jnp.where(pick, -jnp.inf, score)
    sel_ref[0, 0] = sel.astype(sel_ref.dtype)


def _cmp_attn(q_r, kvc, small, nh_gdn, n_sel):
    B, S, WQ = q_r.shape
    G = NSA_GROUPS
    hpg = WQ // HEAD_DIM // G
    NC = kvc.shape[2]
    tq = min(S, 128)
    return pl.pallas_call(
        functools.partial(_cmp_attn_kernel, nh_gdn=nh_gdn, hpg=hpg),
        grid=(B, G, S // tq),
        in_specs=[pl.BlockSpec((1, tq, hpg * HEAD_DIM), lambda b, g, i: (b, i, g)),
                  pl.BlockSpec((1, 1, NC, HEAD_DIM), lambda b, g, i: (b, g, 0, 0)),
                  pl.BlockSpec((1, 1, NC, HEAD_DIM), lambda b, g, i: (b, G + g, 0, 0)),
                  pl.BlockSpec((1, tq, small.shape[2]), lambda b, g, i: (b, i, 0))],
        out_specs=[pl.BlockSpec((1, tq, hpg * HEAD_DIM), lambda b, g, i: (b, i, g)),
                   pl.BlockSpec((1, 1, tq, n_sel), lambda b, g, i: (b, g, i, 0))],
        out_shape=[jax.ShapeDtypeStruct((B, S, WQ), BF16),
                   jax.ShapeDtypeStruct((B, G, S, n_sel), BF16)],
        compiler_params=_cparams(3),
        name="nsa_cmp_attn",
    )(q_r, kvc, kvc, small)


def _sel_attn_kernel(q_ref, k_ref, v_ref, sel_ref, sm_ref, o_ref, m_ref, l_ref, acc_ref,
                     *, nh_gdn, hpg, tk):
    tq = q_ref.shape[1]
    nb = sel_ref.shape[3]
    g = pl.program_id(1)
    i = pl.program_id(2)
    t0 = i * tq
    scale = HEAD_DIM ** -0.5
    rows = hpg * tq
    q = jnp.concatenate([q_ref[0, :, h * HEAD_DIM:(h + 1) * HEAD_DIM] for h in range(hpg)], axis=0)
    selm = sel_ref[0, 0]
    selm = jnp.concatenate([selm] * hpg, axis=0)
    t_row = t0 + lax.broadcasted_iota(jnp.int32, (tq, tk), 0)
    t_row = jnp.concatenate([t_row] * hpg, axis=0)
    c_key = lax.broadcasted_iota(jnp.int32, (rows, tk), 1)
    e_blk = lax.broadcasted_iota(jnp.int32, (nb, tk), 0)
    e_key = lax.broadcasted_iota(jnp.int32, (nb, tk), 1)
    m_ref[...] = jnp.full_like(m_ref, NEG_INF)
    l_ref[...] = jnp.zeros_like(l_ref)
    acc_ref[...] = jnp.zeros_like(acc_ref)

    def step(kt, carry):
        k0 = pl.multiple_of(kt * tk, tk)
        k = k_ref[0, pl.ds(k0, tk), :]
        v = v_ref[0, pl.ds(k0, tk), :]
        s = _dot_nt(q, k) * scale
        expand = jnp.where(e_blk == (k0 + e_key) // NSA_SEL_LEN, 1.0, 0.0).astype(BF16)
        chosen = _dot(selm, expand) > 0.5
        mask = chosen & ((k0 + c_key) <= t_row)
        s = jnp.where(mask, s, NEG_INF)
        m_old = m_ref[...]
        m_new = jnp.maximum(m_old, jnp.max(s, axis=-1, keepdims=True))
        alpha = jnp.exp(m_old - m_new)
        p = jnp.where(mask, jnp.exp(s - m_new), 0.0)
        l_ref[...] = alpha * l_ref[...] + jnp.sum(p, axis=-1, keepdims=True)
        acc_ref[...] = alpha * acc_ref[...] + _dot(p.astype(BF16), v)
        m_ref[...] = m_new
        return carry

    n_kt = (t0 + tq + tk - 1) // tk
    lax.fori_loop(0, n_kt, step, 0)
    out = acc_ref[...] / l_ref[...]
    for h in range(hpg):
        gate = _gate_col(sm_ref, nh_gdn, g, hpg, h, 1)
        o_ref[0, :, h * HEAD_DIM:(h + 1) * HEAD_DIM] = (out[h * tq:(h + 1) * tq] * gate).astype(o_ref.dtype)


def _sel_attn(q_r, ksel_r, big, vsel_blk, sel, small, nh_gdn):
    B, S, WQ = q_r.shape
    G = NSA_GROUPS
    hpg = WQ // HEAD_DIM // G
    n_sel = sel.shape[3]
    tq = min(S, 128)
    tk = min(S, 512)
    rows = hpg * tq
    return pl.pallas_call(
        functools.partial(_sel_attn_kernel, nh_gdn=nh_gdn, hpg=hpg, tk=tk),
        grid=(B, G, S // tq),
        in_specs=[pl.BlockSpec((1, tq, hpg * HEAD_DIM), lambda b, g, i: (b, i, g)),
                  pl.BlockSpec((1, S, HEAD_DIM), lambda b, g, i: (b, 0, g)),
                  pl.BlockSpec((1, S, HEAD_DIM), lambda b, g, i: (b, 0, vsel_blk + g)),
                  pl.BlockSpec((1, 1, tq, n_sel), lambda b, g, i: (b, g, i, 0)),
                  pl.BlockSpec((1, tq, small.shape[2]), lambda b, g, i: (b, i, 0))],
        out_specs=pl.BlockSpec((1, tq, hpg * HEAD_DIM), lambda b, g, i: (b, i, g)),
        out_shape=jax.ShapeDtypeStruct((B, S, WQ), BF16),
        scratch_shapes=[pltpu.VMEM((rows, 1), F32), pltpu.VMEM((rows, 1), F32),
                        pltpu.VMEM((rows, HEAD_DIM), F32)],
        compiler_params=_cparams(3),
        name="nsa_sel_attn",
    )(q_r, ksel_r, big, sel, small)


def _win_attn_kernel(q_ref, k_ref, v_ref, sm_ref, o_ref, *, nh_gdn, hpg):
    tq = q_ref.shape[1]
    span = tq + NSA_WINDOW
    g = pl.program_id(1)
    t0 = pl.program_id(2) * tq
    scale = HEAD_DIM ** -0.5
    k0 = pl.multiple_of(jnp.maximum(t0 - NSA_WINDOW, 0), tq)
    k = k_ref[0, pl.ds(k0, span), :]
    v = v_ref[0, pl.ds(k0, span), :]
    t_row = t0 + lax.broadcasted_iota(jnp.int32, (tq, span), 0)
    spos = k0 + lax.broadcasted_iota(jnp.int32, (tq, span), 1)
    mask = (spos <= t_row) & (spos > t_row - NSA_WINDOW)
    for h in range(hpg):
        cols = slice(h * HEAD_DIM, (h + 1) * HEAD_DIM)
        s = _dot_nt(q_ref[0, :, cols], k) * scale
        s = jnp.where(mask, s, NEG_INF)
        e = jnp.exp(s - jnp.max(s, axis=-1, keepdims=True))
        p = e / jnp.sum(e, axis=-1, keepdims=True)
        gate = _gate_col(sm_ref, nh_gdn, g, hpg, h, 2)
        o_ref[0, :, cols] = (_dot(p.astype(BF16), v) * gate).astype(o_ref.dtype)


def _win_attn(q_r, kwin_r, big, vwin_blk, small, nh_gdn):
    B, S, WQ = q_r.shape
    G = NSA_GROUPS
    hpg = WQ // HEAD_DIM // G
    tq = min(S, 128)
    assert S >= tq + NSA_WINDOW
    return pl.pallas_call(
        functools.partial(_win_attn_kernel, nh_gdn=nh_gdn, hpg=hpg),
        grid=(B, G, S // tq),
        in_specs=[pl.BlockSpec((1, tq, hpg * HEAD_DIM), lambda b, g, i: (b, i, g)),
                  pl.BlockSpec((1, S, HEAD_DIM), lambda b, g, i: (b, 0, g)),
                  pl.BlockSpec((1, S, HEAD_DIM), lambda b, g, i: (b, 0, vwin_blk + g)),
                  pl.BlockSpec((1, tq, small.shape[2]), lambda b, g, i: (b, i, 0))],
        out_specs=pl.BlockSpec((1, tq, hpg * HEAD_DIM), lambda b, g, i: (b, i, g)),
        out_shape=jax.ShapeDtypeStruct((B, S, WQ), BF16),
        compiler_params=_cparams(3),
        name="nsa_win_attn",
    )(q_r, kwin_r, big, small)


def _post_norm(x, y, gt, g, b, alpha):
    r = alpha * x + (1.0 + gt) * y
    return _normalize_rows(r) * g + b


def _mixer_out_kernel(oa_ref, oc_ref, os_ref, ow_ref, ma_ref, mb_ref, x_ref, gt_ref, lg_ref, lb_ref,
                      wg_ref, wn_ref, wo_ref, o_ref, *, alpha):
    y_a = _dot(oa_ref[0], wg_ref[...])
    o_b = oc_ref[0].astype(F32) + os_ref[0].astype(F32) + ow_ref[0].astype(F32)
    y_b = _dot(o_b.astype(BF16), wn_ref[...])
    mixed = _sigmoid(ma_ref[0].astype(F32)) * y_a + _sigmoid(mb_ref[0].astype(F32)) * y_b
    y = _dot(mixed.astype(BF16), wo_ref[...])
    o_ref[0] = _post_norm(x_ref[0], y, gt_ref[0], lg_ref[...], lb_ref[...], alpha)


def _mixer_out(o_a, o_c, o_s, o_w, big, ma_blk, mb_blk, x, ada_l, gt_col, ln_g, ln_b, wg, wn, wo, alpha):
    B, S, D = x.shape
    tm = min(S, 512)
    row = lambda blk: pl.BlockSpec((1, tm, D), lambda b, i: (b, i, blk))
    full = lambda shape: pl.BlockSpec(shape, lambda b, i: (0,) * len(shape))
    return pl.pallas_call(
        functools.partial(_mixer_out_kernel, alpha=alpha),
        grid=(B, S // tm),
        in_specs=[row(0), row(0), row(0), row(0), row(ma_blk), row(mb_blk), row(0),
                  pl.BlockSpec((1, 1, D), lambda b, i: (b, 0, gt_col)),
                  full((1, D)), full((1, D)), full((D, D)), full((D, D)), full((D, D))],
        out_specs=row(0),
        out_shape=jax.ShapeDtypeStruct((B, S, D), F32),
        compiler_params=_cparams(2),
        name="mixer_out",
    )(o_a, o_c, o_s, o_w, big, big, x, ada_l, ln_g.reshape(1, D), ln_b.reshape(1, D), wg, wn, wo)


def _first_argmax(vals, lane, big):
    m = jnp.max(vals, axis=-1, keepdims=True)
    first = jnp.min(jnp.where(vals == m, lane, big), axis=-1, keepdims=True)
    return m, first


def _moe_router_kernel(x_ref, sh_ref, sc_ref, rw_ref, rb_ref, wsg_ref, wsu_ref, wsd_ref,
                       h_ref, idx_ref, wt_ref, shared_ref):
    h = _normalize_rows(x_ref[0]) * (1.0 + sc_ref[0]) + sh_ref[0]
    h16 = h.astype(BF16)
    h_ref[0] = h16
    h_lo = (h - h16.astype(F32)).astype(BF16)
    rw = rw_ref[...]
    rw_hi = rw.astype(BF16)
    rw_lo = (rw - rw_hi.astype(F32)).astype(BF16)
    logits = _dot(h16, rw_hi) + _dot(h16, rw_lo) + _dot(h_lo, rw_hi)
    scores = _sigmoid(logits)
    biased = scores + rb_ref[...]
    tm, E = scores.shape
    per = E // N_EXPERT_GROUPS
    lane_i = lax.broadcasted_iota(jnp.int32, (tm, E), 1)
    lane = lane_i.astype(F32)
    grp_i = lane_i // per
    grp = grp_i.astype(F32)
    gscore = jnp.zeros((tm, E), F32)
    for gi in range(N_EXPERT_GROUPS):
        in_g = grp_i == gi
        vals = jnp.where(in_g, biased, -jnp.inf)
        m1, first = _first_argmax(vals, lane, float(E))
        m2 = jnp.max(jnp.where(lane == first, -jnp.inf, vals), axis=-1, keepdims=True)
        gscore = jnp.where(in_g, m1 + m2, gscore)
    cand = jnp.full((tm, E), NEG_INF, F32)
    for _ in range(TOPK_GROUPS):
        _, first_g = _first_argmax(gscore, grp, float(N_EXPERT_GROUPS))
        pick = grp == first_g
        cand = jnp.where(pick, biased, cand)
        gscore = jnp.where(pick, -jnp.inf, gscore)
    out_lane = lax.broadcasted_iota(jnp.int32, (tm, idx_ref.shape[2]), 1)
    idx_out = jnp.zeros((tm, idx_ref.shape[2]), F32)
    wt_out = jnp.zeros((tm, idx_ref.shape[2]), F32)
    wsum = jnp.zeros((tm, 1), F32)
    for kk in range(TOP_K):
        _, first = _first_argmax(cand, lane, float(E))
        pick = lane == first
        w = jnp.sum(jnp.where(pick, scores, 0.0), axis=-1, keepdims=True)
        cand = jnp.where(pick, -jnp.inf, cand)
        idx_out = jnp.where(out_lane == kk, first, idx_out)
        wt_out = jnp.where(out_lane == kk, w, wt_out)
        wsum = wsum + w
    idx_ref[0] = idx_out.astype(jnp.int32)
    wt_ref[0] = wt_out / wsum * ROUTED_SCALE
    act = _silu(_dot(h16, wsg_ref[...])) * _dot(h16, wsu_ref[...])
    shared_ref[0] = _dot(act.astype(BF16), wsd_ref[...])


def _moe_router(x, ada_l, sh_col, sc_col, router_w, router_b, wsg, wsu, wsd):
    B, S, D = x.shape
    E = router_w.shape[1]
    FF = wsg.shape[1]
    tm = min(S, 512)
    KP = 128
    row = lambda w: pl.BlockSpec((1, tm, w), lambda b, i: (b, i, 0))
    full = lambda shape: pl.BlockSpec(shape, lambda b, i: (0,) * len(shape))
    return pl.pallas_call(
        _moe_router_kernel,
        grid=(B, S // tm),
        in_specs=[row(D),
                  pl.BlockSpec((1, 1, D), lambda b, i: (b, 0, sh_col)),
                  pl.BlockSpec((1, 1, D), lambda b, i: (b, 0, sc_col)),
                  full((D, E)), full((1, E)), full((D, FF)), full((D, FF)), full((FF, D))],
        out_specs=[row(D), row(KP), row(KP), row(D)],
        out_shape=[jax.ShapeDtypeStruct((B, S, D), BF16),
                   jax.ShapeDtypeStruct((B, S, KP), jnp.int32),
                   jax.ShapeDtypeStruct((B, S, KP), F32),
                   jax.ShapeDtypeStruct((B, S, D), F32)],
        compiler_params=_cparams(2),
        name="moe_router",
    )(x, ada_l, ada_l, router_w, router_b.reshape(1, E), wsg, wsu, wsd)


def _moe_experts_kernel(be_ref, nu_ref, xs_ref, wg_ref, wu_ref, wd_ref, o_ref):
    @pl.when(pl.program_id(0) < nu_ref[0])
    def _():
        xb = xs_ref[...]
        act = _silu(_dot(xb, wg_ref[0])) * _dot(xb, wu_ref[0])
        o_ref[...] = _dot(act.astype(BF16), wd_ref[0]).astype(o_ref.dtype)


def _moe_experts(block_expert, n_used, xs, wg, wu, wd):
    rows, D = xs.shape
    BM = MOE_ROW_BLOCK
    FF = wg.shape[2]
    grid_spec = pltpu.PrefetchScalarGridSpec(
        num_scalar_prefetch=2,
        grid=(rows // BM,),
        in_specs=[pl.BlockSpec((BM, D), lambda i, be, nu: (i, 0)),
                  pl.BlockSpec((1, D, FF), lambda i, be, nu: (be[i], 0, 0)),
                  pl.BlockSpec((1, D, FF), lambda i, be, nu: (be[i], 0, 0)),
                  pl.BlockSpec((1, FF, D), lambda i, be, nu: (be[i], 0, 0))],
        out_specs=pl.BlockSpec((BM, D), lambda i, be, nu: (i, 0)),
    )
    return pl.pallas_call(
        _moe_experts_kernel,
        grid_spec=grid_spec,
        out_shape=jax.ShapeDtypeStruct((rows, D), BF16),
        compiler_params=_cparams(1),
        name="moe_experts",
    )(block_expert, n_used, xs, wg, wu, wd)


def _moe_out_kernel(x_ref, shared_ref, yg_ref, wt_ref, gt_ref, lg_ref, lb_ref, o_ref, *, alpha):
    D = x_ref.shape[2]
    y = shared_ref[0]
    wt = wt_ref[0]
    for kk in range(TOP_K):
        y = y + wt[:, kk:kk + 1] * yg_ref[0, :, kk * D:(kk + 1) * D].astype(F32)
    o_ref[0] = _post_norm(x_ref[0], y, gt_ref[0], lg_ref[...], lb_ref[...], alpha)


def _moe_out(x, shared, yg, wt, ada_l, gt_col, ln_g, ln_b, alpha):
    B, S, D = x.shape
    tm = min(S, 256)
    row = lambda w: pl.BlockSpec((1, tm, w), lambda b, i: (b, i, 0))
    full = lambda shape: pl.BlockSpec(shape, lambda b, i: (0,) * len(shape))
    return pl.pallas_call(
        functools.partial(_moe_out_kernel, alpha=alpha),
        grid=(B, S // tm),
        in_specs=[row(D), row(D), row(TOP_K * D), row(wt.shape[2]),
                  pl.BlockSpec((1, 1, D), lambda b, i: (b, 0, gt_col)),
                  full((1, D)), full((1, D))],
        out_specs=row(D),
        out_shape=jax.ShapeDtypeStruct((B, S, D), F32),
        compiler_params=_cparams(2),
        name="moe_out",
    )(x, shared, yg, wt, ada_l, ln_g.reshape(1, D), ln_b.reshape(1, D))


def _routing_layout(expert_idx, n_experts):
    T, K = expert_idx.shape
    BM = MOE_ROW_BLOCK
    TK = T * K
    e_flat = expert_idx.reshape(TK)
    order = jnp.argsort(e_flat)
    e_sorted = e_flat[order]
    counts = jnp.bincount(e_flat, length=n_experts)
    padded = (counts + BM - 1) // BM * BM
    pad_end = jnp.cumsum(padded)
    dest_sorted = (pad_end - padded)[e_sorted] + jnp.arange(TK) - (jnp.cumsum(counts) - counts)[e_sorted]
    n_blocks = (TK + n_experts * (BM - 1) + BM - 1) // BM
    row_tok = jnp.zeros((n_blocks * BM,), jnp.int32).at[dest_sorted].set((order // K).astype(jnp.int32))
    dest = jnp.zeros((TK,), jnp.int32).at[order].set(dest_sorted.astype(jnp.int32))
    block_expert = jnp.minimum(jnp.searchsorted(pad_end, jnp.arange(n_blocks) * BM, side='right'),
                               n_experts - 1).astype(jnp.int32)
    n_used = (pad_end[-1] // BM).astype(jnp.int32).reshape(1)
    return row_tok, dest.reshape(T, K), block_expert, n_used


def kernel(x, c, positions, ada_w, ada_b, w_in, gdn_conv_w, gdn_a_log, gdn_dt_bias, gdn_norm_w, w_gdn_branch, nsa_cmp_pos_k, nsa_cmp_pos_v, nsa_cmp_k_w1, nsa_cmp_k_w2, nsa_cmp_v_w1, nsa_cmp_v_w2, w_nsa_branch, w_out, ln1_g, ln1_b, router_w, router_bias, w_sh_gate, w_sh_up, w_sh_down, w_e_gate, w_e_up, w_e_down, ln2_g, ln2_b):
    B, S, D = x.shape
    L = ada_w.shape[0]
    T = B * S
    G = NSA_GROUPS
    nh_gdn = gdn_a_log.shape[1]
    gdn_w = nh_gdn * HEAD_DIM
    nsa_w = w_nsa_branch.shape[1]
    kvw = G * HEAD_DIM
    E = router_w.shape[2]
    alpha = (2.0 * L) ** 0.25
    assert gdn_w == D and nsa_w == D and S % NSA_SEL_LEN == 0

    splits = (3 * gdn_w, gdn_w, nh_gdn, nh_gdn, nsa_w, 6 * kvw, 3 * (nsa_w // HEAD_DIM), D, D)
    offs = [0]
    for s_ in splits:
        offs.append(offs[-1] + s_)
    seg = lambda i: slice(offs[i], offs[i + 1])
    w_big = jnp.concatenate([w_in[:, :, seg(0)], w_in[:, :, seg(1)], w_in[:, :, seg(4)],
                             w_in[:, :, seg(7)], w_in[:, :, seg(8)], w_in[:, :, seg(5)]], axis=-1).astype(BF16)
    n_small = splits[2] + splits[3] + splits[6]
    w_small = jnp.concatenate([w_in[:, :, seg(2)], w_in[:, :, seg(3)], w_in[:, :, seg(6)],
                               jnp.zeros((L, D, 128 - n_small), w_in.dtype)], axis=-1).astype(BF16)
    Z_BLK, Q_BLK, MA_BLK, MB_BLK = 3, 4, 5, 6
    kv0 = 7 * D
    n_big = w_big.shape[2]
    tn_big = n_big // 4 if (n_big // 4) % 128 == 0 else 128

    wgb = w_gdn_branch.astype(BF16)
    wnb = w_nsa_branch.astype(BF16)
    wob = w_out.astype(BF16)
    wsg = w_sh_gate.astype(BF16)
    wsu = w_sh_up.astype(BF16)
    wsd = w_sh_down.astype(BF16)
    weg = w_e_gate.astype(BF16)
    weu = w_e_up.astype(BF16)
    wed = w_e_down.astype(BF16)
    cmp_w1 = jnp.stack([nsa_cmp_k_w1, nsa_cmp_v_w1], axis=1).astype(BF16)
    cmp_w2 = jnp.stack([nsa_cmp_k_w2, nsa_cmp_v_w2], axis=1).astype(BF16)
    cmp_pos = jnp.stack([nsa_cmp_pos_k, nsa_cmp_pos_v], axis=1).reshape(L, 2, 1, NSA_CMP_LEN * HEAD_DIM)

    ada = _ada(c, ada_w, ada_b)
    cosf, sinf = _rope_tables(positions)
    NC = S // NSA_CMP_STRIDE
    last = jnp.minimum(jnp.arange(NC) * NSA_CMP_STRIDE + NSA_CMP_LEN - 1, S - 1)
    cosc = cosf[:, last]
    sinc = sinf[:, last]
    n_sel = S // NSA_SEL_LEN

    for l in range(L):
        ada_l = ada[l].reshape(B, 1, 6 * D)
        big = _proj(x, ada_l, 0, 1, w_big[l], BF16, tn_big)
        small = _proj(x, ada_l, 0, 1, w_small[l], F32, 128)
        small_t = jnp.swapaxes(small[:, :, nh_gdn:2 * nh_gdn].reshape(B, S // GDN_CHUNK, GDN_CHUNK, nh_gdn), 2, 3)
        q_a, k_a, v_a = _gdn_prep(big, gdn_conv_w[l], gdn_w)
        o_a = _gdn(q_a, k_a, v_a, big, Z_BLK, small, small_t, gdn_a_log[l], gdn_dt_bias[l], gdn_norm_w[l])
        q_r, ksel_r, kwin_r = _rope_apply(big, Q_BLK, (kv0 + 2 * kvw) // kvw, (kv0 + 4 * kvw) // kvw,
                                          nsa_w, kvw, cosf, sinf)
        hb = big[:, :, kv0:kv0 + 2 * kvw].reshape(B, NC, NSA_CMP_STRIDE, 2 * G, HEAD_DIM)
        hb = hb.transpose(0, 3, 1, 2, 4).reshape(B, 2 * G, NC, NSA_CMP_STRIDE * HEAD_DIM)
        kvc = _compress(hb, cmp_pos[l], cmp_w1[l], cmp_w2[l], cosc, sinc)
        o_c, sel = _cmp_attn(q_r, kvc, small, nh_gdn, n_sel)
        o_s = _sel_attn(q_r, ksel_r, big, (kv0 + 3 * kvw) // HEAD_DIM, sel, small, nh_gdn)
        o_w = _win_attn(q_r, kwin_r, big, (kv0 + 5 * kvw) // HEAD_DIM, small, nh_gdn)
        x = _mixer_out(o_a, o_c, o_s, o_w, big, MA_BLK, MB_BLK, x, ada_l, 2, ln1_g[l], ln1_b[l],
                       wgb[l], wnb[l], wob[l], alpha)
        h2, eidx, ewt, shared = _moe_router(x, ada_l, 3, 4, router_w[l], router_bias[l], wsg[l], wsu[l], wsd[l])
        row_tok, dest, block_expert, n_used = _routing_layout(eidx.reshape(T, -1)[:, :TOP_K], E)
        xs = jnp.take(h2.reshape(T, D), row_tok, axis=0)
        yb = _moe_experts(block_expert, n_used, xs, weg[l], weu[l], wed[l])
        yg = jnp.take(yb, dest.reshape(T * TOP_K), axis=0).reshape(B, S, TOP_K * D)
        x = _moe_out(x, shared, yg, ewt, ada_l, 5, ln2_g[l], ln2_b[l], alpha)
    return x
```

```python
import functools
import math

import jax
import jax.numpy as jnp
from jax import lax
from jax.experimental import pallas as pl
from jax.experimental.pallas import tpu as pltpu

F32 = jnp.float32
BF16 = jnp.bfloat16

HEAD_DIM = 128
GDN_CONV = 4
GDN_CHUNK = 64
NSA_GROUPS = 2
NSA_CMP_LEN = 32
NSA_CMP_STRIDE = 16
NSA_SEL_LEN = 64
NSA_SEL_TOP = 16
NSA_WINDOW = 512
ROPE_THETA = 10000.0
N_EXPERTS = 64
N_EXPERT_GROUPS = 8
TOPK_GROUPS = 4
TOP_K = 8
ROUTED_SCALE = 2.5
LN_EPS = 1e-5
NEG_INF = -1e30
SEL_FORCE = 1e6
MOE_ROW_BLOCK = 256
Q_SCALE_LOG2E = (HEAD_DIM ** -0.5) * math.log2(math.e)
MASK_BIG = 2.0 ** 100
N_ROW_PARTS = 2

_ARB = "arbitrary"


def _cparams(n_axes):
    return pltpu.CompilerParams(dimension_semantics=(_ARB,) * n_axes)


def _sigmoid(x):
    return 1.0 / (1.0 + jnp.exp(-x))


def _silu(x):
    return x * _sigmoid(x)


def _dot(a, b):
    return jnp.dot(a, b, preferred_element_type=F32)


def _dot_nt(a, b):
    return lax.dot_general(a, b, (((1,), (1,)), ((), ())), preferred_element_type=F32)


def _dot_tn(a, b):
    return lax.dot_general(a, b, (((0,), (0,)), ((), ())), preferred_element_type=F32)


def _split3(x):
    x0 = x.astype(BF16)
    r1 = x - x0.astype(F32)
    x1 = r1.astype(BF16)
    x2 = (r1 - x1.astype(F32)).astype(BF16)
    return x0, x1, x2


def _dot_sel_right(x, sel_bf16):
    x0, x1, x2 = _split3(x)
    return _dot(x0, sel_bf16) + _dot(x1, sel_bf16) + _dot(x2, sel_bf16)


def _dot_sel_left(sel_bf16, x):
    x0, x1, x2 = _split3(x)
    return _dot(sel_bf16, x0) + _dot(sel_bf16, x1) + _dot(sel_bf16, x2)


def _normalize_rows(x):
    mu = jnp.mean(x, axis=-1, keepdims=True)
    xc = x - mu
    var = jnp.mean(xc * xc, axis=-1, keepdims=True)
    return xc * lax.rsqrt(var + LN_EPS)


def _ada_kernel(c_ref, w_ref, b_ref, o_ref):
    cond = _silu(c_ref[...])
    o_ref[0] = jnp.dot(cond, w_ref[0], preferred_element_type=F32,
                       precision=lax.Precision.HIGHEST) + b_ref[0]


def _ada(c, ada_w, ada_b):
    L, D, N = ada_w.shape
    B = c.shape[0]
    tn = min(N, 1536)
    return pl.pallas_call(
        _ada_kernel,
        grid=(L, N // tn),
        in_specs=[pl.BlockSpec((B, D), lambda l, j: (0, 0)),
                  pl.BlockSpec((1, D, tn), lambda l, j: (l, 0, j)),
                  pl.BlockSpec((1, 1, tn), lambda l, j: (l, 0, j))],
        out_specs=pl.BlockSpec((1, B, tn), lambda l, j: (l, 0, j)),
        out_shape=jax.ShapeDtypeStruct((L, B, N), F32),
        compiler_params=_cparams(2),
        name="ada",
    )(c, ada_w, ada_b.reshape(L, 1, N))


def _proj_kernel(x_ref, sh_ref, sc_ref, w_ref, o_ref, h_ref):
    @pl.when(pl.program_id(2) == 0)
    def _():
        h = _normalize_rows(x_ref[0]) * (1.0 + sc_ref[0]) + sh_ref[0]
        h_ref[...] = h.astype(BF16)

    o_ref[0] = _dot(h_ref[...], w_ref[...]).astype(o_ref.dtype)


def _proj(x, ada_l, sh_col, sc_col, w, out_dtype, tn):
    B, S, D = x.shape
    N = w.shape[1]
    tm = min(S, 1024)
    return pl.pallas_call(
        _proj_kernel,
        grid=(B, S // tm, N // tn),
        in_specs=[pl.BlockSpec((1, tm, D), lambda b, i, j: (b, i, 0)),
                  pl.BlockSpec((1, 1, D), lambda b, i, j: (b, 0, sh_col)),
                  pl.BlockSpec((1, 1, D), lambda b, i, j: (b, 0, sc_col)),
                  pl.BlockSpec((D, tn), lambda b, i, j: (0, j))],
        out_specs=pl.BlockSpec((1, tm, tn), lambda b, i, j: (b, i, j)),
        out_shape=jax.ShapeDtypeStruct((B, S, N), out_dtype),
        scratch_shapes=[pltpu.VMEM((tm, D), BF16)],
        compiler_params=_cparams(3),
        name="proj",
    )(x, ada_l, ada_l, w)


def _gdn_prep_kernel(x_ref, w_ref, q_ref, k_ref, v_ref, carry_ref):
    ts = x_ref.shape[1]
    width = q_ref.shape[2]
    nh = width // HEAD_DIM

    @pl.when(pl.program_id(1) == 0)
    def _():
        carry_ref[...] = jnp.zeros_like(carry_ref)

    for part, o_ref in enumerate((q_ref, k_ref, v_ref)):
        for h in range(nh):
            c0 = part * width + h * HEAD_DIM
            cols = slice(c0, c0 + HEAD_DIM)
            xx = jnp.concatenate([carry_ref[:, cols], x_ref[0, :, cols].astype(F32)], axis=0)
            w = w_ref[:, cols]
            y = xx[8:8 + ts] * w[3:4]
            for kk in range(GDN_CONV - 1):
                off = 8 - (GDN_CONV - 1) + kk
                y = y + xx[off:off + ts] * w[kk:kk + 1]
            y = _silu(y)
            if part < 2:
                y = y * lax.rsqrt(jnp.sum(y * y, axis=-1, keepdims=True) + 1e-6)
            if part == 0:
                y = y * (HEAD_DIM ** -0.5)
            o_ref[0, :, h * HEAD_DIM:(h + 1) * HEAD_DIM] = y.astype(o_ref.dtype)
    carry_ref[...] = x_ref[0, ts - 8:ts, :].astype(F32)


def _gdn_prep(big, conv_w, width):
    B, S, _ = big.shape
    ts = min(S, 512)
    out = jax.ShapeDtypeStruct((B, S, width), BF16)
    ospec = pl.BlockSpec((1, ts, width), lambda b, i: (b, i, 0))
    return pl.pallas_call(
        _gdn_prep_kernel,
        grid=(B, S // ts),
        in_specs=[pl.BlockSpec((1, ts, 3 * width), lambda b, i: (b, i, 0)),
                  pl.BlockSpec((GDN_CONV, 3 * width), lambda b, i: (0, 0))],
        out_specs=[ospec, ospec, ospec],
        out_shape=[out, out, out],
        scratch_shapes=[pltpu.VMEM((8, 3 * width), F32)],
        compiler_params=_cparams(2),
        name="gdn_prep",
    )(big, conv_w)


def _softplus(x):
    return jnp.maximum(x, 0.0) + jnp.log(1.0 + jnp.exp(-jnp.abs(x)))


def _gdn_kernel(q_ref, k_ref, v_ref, z_ref, sm_ref, smt_ref, alog_ref, alogt_ref, dtb_ref, dtbt_ref,
                nw_ref, o_ref, state_ref, u_s, wq_s, attn_s, kd_s, dec_s):
    ts = q_ref.shape[1]
    nh = q_ref.shape[2] // HEAD_DIM
    C = GDN_CHUNK
    R = 2 * C
    npair = nh // 2
    nchunks = ts // C

    @pl.when(pl.program_id(1) == 0)
    def _():
        state_ref[...] = jnp.zeros_like(state_ref)

    ci = lax.broadcasted_iota(jnp.int32, (C, C), 0)
    cj = lax.broadcasted_iota(jnp.int32, (C, C), 1)
    tril = jnp.where(ci >= cj, 1.0, 0.0).astype(BF16)
    triu = jnp.where(cj >= ci, 1.0, 0.0).astype(BF16)
    ii = lax.broadcasted_iota(jnp.int32, (R, R), 0)
    jj = lax.broadcasted_iota(jnp.int32, (R, R), 1)
    same_head = (ii // C) == (jj // C)
    incl = same_head & (ii >= jj)
    strict = same_head & (ii > jj)
    eye = jnp.where(ii == jj, 1.0, 0.0)
    nw = nw_ref[...]

    def pair_rows(ref, rows, p):
        return jnp.concatenate([ref[0, rows, (2 * p) * HEAD_DIM:(2 * p + 1) * HEAD_DIM],
                                ref[0, rows, (2 * p + 1) * HEAD_DIM:(2 * p + 2) * HEAD_DIM]], axis=0)

    def pair_col(x, p):
        return jnp.concatenate([x[:, 2 * p:2 * p + 1], x[:, 2 * p + 1:2 * p + 2]], axis=0)

    def phase1(c, carry):
        rows = pl.ds(pl.multiple_of(c * C, C), C)
        sm = sm_ref[0, rows, :]
        beta = _sigmoid(sm[:, 0:nh])
        g = -jnp.exp(alog_ref[...]) * _softplus(sm[:, nh:2 * nh] + dtb_ref[...])
        gt = -jnp.exp(alogt_ref[...]) * _softplus(smt_ref[0, c] + dtbt_ref[...])
        gc = _dot_sel_left(tril, g)
        gcr = _dot_sel_right(gt, triu)
        g_last = gc[C - 1:C, :]
        dec_s[c] = jnp.exp(g_last)
        gl_b = jnp.broadcast_to(g_last, (C, nh))
        pairs = range(npair)
        q2 = [pair_rows(q_ref, rows, p).astype(F32) for p in pairs]
        k2 = [pair_rows(k_ref, rows, p).astype(F32) for p in pairs]
        b2 = [pair_col(beta, p) for p in pairs]
        g_col = [pair_col(gc, p) for p in pairs]
        kb = [k2[p] * b2[p] for p in pairs]
        kq = [_dot_nt(jnp.concatenate([kb[p], q2[p]], axis=0).astype(BF16), k2[p].astype(BF16))
              for p in pairs]
        decay = []
        for p in pairs:
            g_row = jnp.concatenate([gcr[2 * p:2 * p + 1, :], gcr[2 * p + 1:2 * p + 2, :]], axis=1)
            decay.append(jnp.where(incl, jnp.exp(jnp.where(incl, g_col[p] - g_row, 0.0)), 0.0))
        a = [jnp.where(strict, kq[p][:R] * decay[p], 0.0) for p in pairs]
        for p in pairs:
            attn_s[c, p] = (kq[p][R:] * decay[p]).astype(BF16)
        x = [eye - a[p] for p in pairs]
        pw = a
        n = 2
        while n < C:
            pw16 = [pw[p].astype(BF16) for p in pairs]
            pw = [_dot(pw16[p], pw16[p]) for p in pairs]
            x = [x[p] + _dot(x[p].astype(BF16), pw[p].astype(BF16)) for p in pairs]
            n *= 2
        eg = [jnp.exp(g_col[p]) for p in pairs]
        sol = []
        for p in pairs:
            v2 = pair_rows(v_ref, rows, p).astype(F32)
            r = jnp.concatenate([v2 * b2[p], kb[p] * eg[p]], axis=1)
            sol.append(_dot(x[p].astype(BF16), r.astype(BF16)))
        for p in pairs:
            u_s[c, p] = sol[p][:, :HEAD_DIM]
            w = sol[p][:, HEAD_DIM:]
            qg = q2[p] * eg[p]
            for e in range(2):
                wq_s[c, 2 * p + e] = jnp.concatenate([w[e * C:(e + 1) * C], qg[e * C:(e + 1) * C]],
                                                     axis=0).astype(BF16)
            kd_s[c, p] = (k2[p] * jnp.exp(pair_col(gl_b, p) - g_col[p])).astype(BF16)
        return carry

    lax.fori_loop(0, nchunks, phase1, 0)

    def phase2(c, carry):
        rows = pl.ds(pl.multiple_of(c * C, C), C)
        dec = dec_s[c]
        res = [_dot(wq_s[c, h], state_ref[h].astype(BF16)) for h in range(nh)]
        for p in range(npair):
            ws = jnp.concatenate([res[2 * p][:C], res[2 * p + 1][:C]], axis=0)
            qs = jnp.concatenate([res[2 * p][C:], res[2 * p + 1][C:]], axis=0)
            v_new = u_s[c, p] - ws
            v16 = v_new.astype(BF16)
            o2 = qs + _dot(attn_s[c, p], v16)
            kd = kd_s[c, p]
            for e in range(2):
                h = 2 * p + e
                part = slice(e * C, (e + 1) * C)
                state_ref[h] = state_ref[h] * dec[:, h:h + 1] + _dot_tn(kd[part], v16[part])
                o = o2[part]
                o = o * lax.rsqrt(jnp.mean(o * o, axis=-1, keepdims=True) + 1e-6) * nw
                cols = slice(h * HEAD_DIM, (h + 1) * HEAD_DIM)
                z = z_ref[0, rows, cols].astype(F32)
                o_ref[0, rows, cols] = (o * _silu(z)).astype(o_ref.dtype)
        return carry

    lax.fori_loop(0, nchunks, phase2, 0)


def _gdn(q, k, v, big, z_blk, small, small_t, a_log, dt_bias, norm_w):
    B, S, W = q.shape
    nh = W // HEAD_DIM
    assert nh % 2 == 0 and 2 * GDN_CHUNK == HEAD_DIM
    ts = min(S, 512)
    nc = ts // GDN_CHUNK
    spec = pl.BlockSpec((1, ts, W), lambda b, i: (b, i, 0))
    full = lambda shape: pl.BlockSpec(shape, lambda b, i: (0,) * len(shape))
    return pl.pallas_call(
        _gdn_kernel,
        grid=(B, S // ts),
        in_specs=[spec, spec, spec,
                  pl.BlockSpec((1, ts, W), lambda b, i: (b, i, z_blk)),
                  pl.BlockSpec((1, ts, small.shape[2]), lambda b, i: (b, i, 0)),
                  pl.BlockSpec((1, ts // GDN_CHUNK, nh, GDN_CHUNK), lambda b, i: (b, i, 0, 0)),
                  full((1, nh)), full((nh, 1)), full((1, nh)), full((nh, 1)),
                  full((1, HEAD_DIM))],
        out_specs=spec,
        out_shape=jax.ShapeDtypeStruct((B, S, W), BF16),
        scratch_shapes=[pltpu.VMEM((nh, HEAD_DIM, HEAD_DIM), F32),
                        pltpu.VMEM((nc, nh // 2, 2 * GDN_CHUNK, HEAD_DIM), F32),
                        pltpu.VMEM((nc, nh, 2 * GDN_CHUNK, HEAD_DIM), BF16),
                        pltpu.VMEM((nc, nh // 2, 2 * GDN_CHUNK, 2 * GDN_CHUNK), BF16),
                        pltpu.VMEM((nc, nh // 2, 2 * GDN_CHUNK, HEAD_DIM), BF16),
                        pltpu.VMEM((nc, 1, nh), F32)],
        compiler_params=_cparams(2),
        name="gdn",
    )(q, k, v, big, small, small_t, a_log.reshape(1, nh), a_log.reshape(nh, 1),
      dt_bias.reshape(1, nh), dt_bias.reshape(nh, 1), norm_w.reshape(1, HEAD_DIM))


def _rope_table_kernel(pos_ref, invf_ref, sign_ref, cos_ref, sin_ref):
    ang = pos_ref[0].astype(F32) * invf_ref[...]
    cos_ref[0] = jnp.cos(ang)
    sin_ref[0] = jnp.sin(ang) * sign_ref[...]


def _rope_tables(positions):
    B, S = positions.shape
    half = HEAD_DIM // 2
    inv = ROPE_THETA ** (-jnp.arange(half, dtype=F32) / half)
    invf = jnp.concatenate([inv, inv]).reshape(1, HEAD_DIM)
    sign = jnp.concatenate([-jnp.ones((half,), F32), jnp.ones((half,), F32)]).reshape(1, HEAD_DIM)
    ts = min(S, 1024)
    out = jax.ShapeDtypeStruct((B, S, HEAD_DIM), F32)
    ospec = pl.BlockSpec((1, ts, HEAD_DIM), lambda b, i: (b, i, 0))
    return pl.pallas_call(
        _rope_table_kernel,
        grid=(B, S // ts),
        in_specs=[pl.BlockSpec((1, ts, 1), lambda b, i: (b, i, 0)),
                  pl.BlockSpec((1, HEAD_DIM), lambda b, i: (0, 0)),
                  pl.BlockSpec((1, HEAD_DIM), lambda b, i: (0, 0))],
        out_specs=[ospec, ospec],
        out_shape=[out, out],
        compiler_params=_cparams(2),
        name="rope_table",
    )(positions.reshape(B, S, 1), invf, sign)


def _rope_rows(x, cosf, sinf):
    return x * cosf + pltpu.roll(x, HEAD_DIM // 2, 1) * sinf


def _rope_apply_kernel(q_ref, ks_ref, kw_ref, cos_ref, sin_ref, qo_ref, kso_ref, kwo_ref):
    cosf = cos_ref[0]
    sinf = sin_ref[0]
    for src, dst, mult in ((q_ref, qo_ref, Q_SCALE_LOG2E), (ks_ref, kso_ref, None), (kw_ref, kwo_ref, None)):
        for h in range(src.shape[2] // HEAD_DIM):
            cols = slice(h * HEAD_DIM, (h + 1) * HEAD_DIM)
            r = _rope_rows(src[0, :, cols].astype(F32), cosf, sinf)
            if mult is not None:
                r = r * mult
            dst[0, :, cols] = r.astype(dst.dtype)


def _rope_apply(big, q_blk, ksel_blk, kwin_blk, wq, wkv, cosf, sinf):
    B, S, _ = big.shape
    ts = min(S, 512)
    tab = pl.BlockSpec((1, ts, HEAD_DIM), lambda b, i: (b, i, 0))
    return pl.pallas_call(
        _rope_apply_kernel,
        grid=(B, S // ts),
        in_specs=[pl.BlockSpec((1, ts, wq), lambda b, i: (b, i, q_blk)),
                  pl.BlockSpec((1, ts, wkv), lambda b, i: (b, i, ksel_blk)),
                  pl.BlockSpec((1, ts, wkv), lambda b, i: (b, i, kwin_blk)),
                  tab, tab],
        out_specs=[pl.BlockSpec((1, ts, wq), lambda b, i: (b, i, 0)),
                   pl.BlockSpec((1, ts, wkv), lambda b, i: (b, i, 0)),
                   pl.BlockSpec((1, ts, wkv), lambda b, i: (b, i, 0))],
        out_shape=[jax.ShapeDtypeStruct((B, S, wq), BF16),
                   jax.ShapeDtypeStruct((B, S, wkv), BF16),
                   jax.ShapeDtypeStruct((B, S, wkv), BF16)],
        compiler_params=_cparams(2),
        name="rope_apply",
    )(big, big, big, cosf, sinf)


def _compress_kernel(hb_ref, pos_ref, w1_ref, w2_ref, cos_ref, sin_ref, o_ref):
    hb = hb_ref[0, 0]
    w1 = w1_ref[0]
    half = hb.shape[1]
    p0 = _dot(hb, w1[:half])
    p1 = _dot(hb, w1[half:])
    nc = p0.shape[0]
    pos8 = jnp.broadcast_to(pos_ref[0], (8, 2 * half)).astype(BF16)
    pb = _dot(pos8, w1)[0:1]
    pre = p0 + pltpu.roll(p1, nc - 1, 0) + pb
    out = _dot(_silu(pre).astype(BF16), w2_ref[0])
    roped = _rope_rows(out, cos_ref[0], sin_ref[0])
    is_key = pl.program_id(1) < NSA_GROUPS
    o_ref[0, 0] = jnp.where(is_key, roped, out).astype(o_ref.dtype)


def _compress(hb, pos_flat, w1, w2, cosc, sinc):
    B, J, NC, HW = hb.shape
    G = NSA_GROUPS
    return pl.pallas_call(
        _compress_kernel,
        grid=(B, J),
        in_specs=[pl.BlockSpec((1, 1, NC, HW), lambda b, j: (b, j, 0, 0)),
                  pl.BlockSpec((1, 1, 2 * HW), lambda b, j: (j // G, 0, 0)),
                  pl.BlockSpec((1, 2 * HW, HEAD_DIM), lambda b, j: (j // G, 0, 0)),
                  pl.BlockSpec((1, HEAD_DIM, HEAD_DIM), lambda b, j: (j // G, 0, 0)),
                  pl.BlockSpec((1, NC, HEAD_DIM), lambda b, j: (b, 0, 0)),
                  pl.BlockSpec((1, NC, HEAD_DIM), lambda b, j: (b, 0, 0))],
        out_specs=pl.BlockSpec((1, 1, NC, HEAD_DIM), lambda b, j: (b, j, 0, 0)),
        out_shape=jax.ShapeDtypeStruct((B, J, NC, HEAD_DIM), BF16),
        compiler_params=_cparams(2),
        name="nsa_compress",
    )(hb, pos_flat, w1, w2, cosc, sinc)


def _gate_col(g_ref, nh_gdn, group, hpg, h, branch):
    col = None
    for gi in range(NSA_GROUPS):
        c = 2 * nh_gdn + (gi * hpg + h) * 3 + branch
        cand = g_ref[0, :, c:c + 1]
        col = cand if col is None else jnp.where(group == gi, cand, col)
    return _sigmoid(col)


def _cmp_attn_kernel(q_ref, kc_ref, vc_ref, sm_ref, o_ref, sel_ref, *, nh_gdn, hpg):
    tq = q_ref.shape[1]
    nc = kc_ref.shape[2]
    nb = sel_ref.shape[3]
    g = pl.program_id(1)
    t0 = pl.program_id(2) * tq
    t_nc = t0 + lax.broadcasted_iota(jnp.int32, (tq, nc), 0)
    n_nc = lax.broadcasted_iota(jnp.int32, (tq, nc), 1)
    valid = (n_nc * NSA_CMP_STRIDE + (NSA_CMP_LEN - 1)) <= t_nc
    kc = kc_ref[0, 0]
    vc = vc_ref[0, 0]
    psum = jnp.zeros((tq, nc), F32)
    for h in range(hpg):
        cols = slice(h * HEAD_DIM, (h + 1) * HEAD_DIM)
        s = jnp.where(valid, _dot_nt(q_ref[0, :, cols], kc), NEG_INF)
        e = jnp.exp2(s - jnp.max(s, axis=-1, keepdims=True))
        p = e / jnp.sum(e, axis=-1, keepdims=True)
        p = jnp.where(valid, p, 0.0)
        gate = _gate_col(sm_ref, nh_gdn, g, hpg, h, 0)
        o_ref[0, :, cols] = (_dot(p.astype(BF16), vc) * gate).astype(o_ref.dtype)
        psum = psum + p
    sj = lax.broadcasted_iota(jnp.int32, (nb, nc), 0) * NSA_SEL_LEN
    cn = lax.broadcasted_iota(jnp.int32, (nb, nc), 1) * NSA_CMP_STRIDE
    overlap_t = jnp.where((cn <= sj + (NSA_SEL_LEN - 1)) & (cn + (NSA_CMP_LEN - 1) >= sj), 1.0, 0.0).astype(BF16)
    p0, p1, p2 = _split3(psum)
    imp = _dot_nt(overlap_t, p0) + _dot_nt(overlap_t, p1) + _dot_nt(overlap_t, p2)
    t_nb = t0 + lax.broadcasted_iota(jnp.int32, (nb, tq), 1)
    blk = lax.broadcasted_iota(jnp.int32, (nb, tq), 0)
    cur = t_nb // NSA_SEL_LEN
    forced = (blk == 0) | (blk == cur) | (blk == cur - 1)
    score = jnp.where(forced, SEL_FORCE, jnp.where(blk * NSA_SEL_LEN <= t_nb, imp, -1.0))
    sel = jnp.zeros((nb, tq), F32)
    blk_f = blk.astype(F32)
    for _ in range(min(NSA_SEL_TOP, nb)):
        m = jnp.max(score, axis=0, keepdims=True)
        first = jnp.min(jnp.where(score == m, blk_f, float(nb)), axis=0, keepdims=True)
        pick = blk_f == first
        sel = jnp.where(pick, 1.0, sel)
        score = jnp.where(pick, -jnp.inf, score)
    sel_ref[0, 0] = sel.T.astype(sel_ref.dtype)


def _cmp_attn(q_r, kvc, small, nh_gdn, n_sel):
    B, S, WQ = q_r.shape
    G = NSA_GROUPS
    hpg = WQ // HEAD_DIM // G
    NC = kvc.shape[2]
    tq = min(S, 512)
    return pl.pallas_call(
        functools.partial(_cmp_attn_kernel, nh_gdn=nh_gdn, hpg=hpg),
        grid=(B, G, S // tq),
        in_specs=[pl.BlockSpec((1, tq, hpg * HEAD_DIM), lambda b, g, i: (b, i, g)),
                  pl.BlockSpec((1, 1, NC, HEAD_DIM), lambda b, g, i: (b, g, 0, 0)),
                  pl.BlockSpec((1, 1, NC, HEAD_DIM), lambda b, g, i: (b, G + g, 0, 0)),
                  pl.BlockSpec((1, tq, small.shape[2]), lambda b, g, i: (b, i, 0))],
        out_specs=[pl.BlockSpec((1, tq, hpg * HEAD_DIM), lambda b, g, i: (b, i, g)),
                   pl.BlockSpec((1, 1, tq, n_sel), lambda b, g, i: (b, g, i, 0))],
        out_shape=[jax.ShapeDtypeStruct((B, S, WQ), BF16),
                   jax.ShapeDtypeStruct((B, G, S, n_sel), BF16)],
        compiler_params=_cparams(3),
        name="nsa_cmp_attn",
    )(q_r, kvc, kvc, small)


def _sel_attn_kernel(q_ref, k_ref, ext_ref, v_ref, sel_ref, sm_ref, o_ref, qx_ref, m_ref, acc_ref,
                     *, nh_gdn, hpg, tk):
    tq = q_ref.shape[1]
    nb = sel_ref.shape[3]
    g = pl.program_id(1)
    t0 = pl.program_id(2) * tq
    assert tk % tq == 0
    unsel = (sel_ref[0, 0].astype(F32) - 1.0).astype(BF16)
    for h in range(hpg):
        qx_ref[h * tq:(h + 1) * tq, 0:HEAD_DIM] = q_ref[0, :, h * HEAD_DIM:(h + 1) * HEAD_DIM]
        qx_ref[h * tq:(h + 1) * tq, HEAD_DIM:HEAD_DIM + nb] = unsel
    m_ref[...] = jnp.full_like(m_ref, NEG_INF)
    acc_ref[...] = jnp.zeros_like(acc_ref)
    ones = jnp.ones((tk, HEAD_DIM), BF16)

    def step(kt, causal):
        k0 = pl.multiple_of(kt * tk, tk)
        kx = jnp.concatenate([k_ref[0, pl.ds(k0, tk), :], ext_ref[pl.ds(k0, tk), :]], axis=1)
        vx = jnp.concatenate([v_ref[0, pl.ds(k0, tk), :], ones], axis=1)
        if causal:
            t_row = t0 + lax.broadcasted_iota(jnp.int32, (tq, tk), 0)
            key = k0 + lax.broadcasted_iota(jnp.int32, (tq, tk), 1)
            keep = jnp.concatenate([key <= t_row] * (hpg // N_ROW_PARTS), axis=0)
        pr = hpg * tq // N_ROW_PARTS
        for part in range(N_ROW_PARTS):
            rows = slice(part * pr, (part + 1) * pr)
            s = _dot_nt(qx_ref[rows, :], kx)
            if causal:
                s = jnp.where(keep, s, NEG_INF)
            m_old = m_ref[rows, :]
            m_new = jnp.maximum(m_old, jnp.max(s, axis=-1, keepdims=True))
            alpha = jnp.exp2(m_old - m_new)
            p = jnp.exp2(s - pltpu.repeat(m_new, tk // HEAD_DIM, 1))
            acc_ref[rows, :] = pltpu.repeat(alpha, 2, 1) * acc_ref[rows, :] + _dot(p.astype(BF16), vx)
            m_ref[rows, :] = m_new

    kt_diag = t0 // tk

    def body(kt, carry):
        step(kt, False)
        return carry

    lax.fori_loop(0, kt_diag, body, 0)
    step(kt_diag, True)
    for h in range(hpg):
        rows = slice(h * tq, (h + 1) * tq)
        gate = _gate_col(sm_ref, nh_gdn, g, hpg, h, 1)
        out = acc_ref[rows, 0:HEAD_DIM] / acc_ref[rows, HEAD_DIM:2 * HEAD_DIM]
        o_ref[0, :, h * HEAD_DIM:(h + 1) * HEAD_DIM] = (out * gate).astype(o_ref.dtype)


def _sel_attn(q_r, ksel_r, big, vsel_blk, sel, small, nh_gdn):
    B, S, WQ = q_r.shape
    G = NSA_GROUPS
    hpg = WQ // HEAD_DIM // G
    nbp = sel.shape[3]
    tq = min(S, 512)
    tk = min(S, 512)
    rows = hpg * tq
    ext = jnp.where(jnp.arange(S)[:, None] // NSA_SEL_LEN == jnp.arange(nbp)[None, :], MASK_BIG, 0.0).astype(BF16)
    return pl.pallas_call(
        functools.partial(_sel_attn_kernel, nh_gdn=nh_gdn, hpg=hpg, tk=tk),
        grid=(B, G, S // tq),
        in_specs=[pl.BlockSpec((1, tq, hpg * HEAD_DIM), lambda b, g, i: (b, i, g)),
                  pl.BlockSpec((1, S, HEAD_DIM), lambda b, g, i: (b, 0, g)),
                  pl.BlockSpec((S, nbp), lambda b, g, i: (0, 0)),
                  pl.BlockSpec((1, S, HEAD_DIM), lambda b, g, i: (b, 0, vsel_blk + g)),
                  pl.BlockSpec((1, 1, tq, nbp), lambda b, g, i: (b, g, i, 0)),
                  pl.BlockSpec((1, tq, small.shape[2]), lambda b, g, i: (b, i, 0))],
        out_specs=pl.BlockSpec((1, tq, hpg * HEAD_DIM), lambda b, g, i: (b, i, g)),
        out_shape=jax.ShapeDtypeStruct((B, S, WQ), BF16),
        scratch_shapes=[pltpu.VMEM((rows, HEAD_DIM + nbp), BF16),
                        pltpu.VMEM((rows, HEAD_DIM), F32),
                        pltpu.VMEM((rows, 2 * HEAD_DIM), F32)],
        compiler_params=_cparams(3),
        name="nsa_sel_attn",
    )(q_r, ksel_r, ext, big, sel, small)


def _win_attn_kernel(q_ref, k_ref, v_ref, sm_ref, o_ref, *, nh_gdn, hpg):
    tq = q_ref.shape[1]
    span = tq + NSA_WINDOW
    g = pl.program_id(1)
    t0 = pl.program_id(2) * tq
    k0 = pl.multiple_of(jnp.maximum(t0 - NSA_WINDOW, 0), tq)
    k = k_ref[0, pl.ds(k0, span), :]
    v = v_ref[0, pl.ds(k0, span), :]
    t_row = t0 + lax.broadcasted_iota(jnp.int32, (tq, span), 0)
    spos = k0 + lax.broadcasted_iota(jnp.int32, (tq, span), 1)
    mask = (spos <= t_row) & (spos > t_row - NSA_WINDOW)
    for h in range(hpg):
        cols = slice(h * HEAD_DIM, (h + 1) * HEAD_DIM)
        s = jnp.where(mask, _dot_nt(q_ref[0, :, cols], k), NEG_INF)
        e = jnp.exp2(s - jnp.max(s, axis=-1, keepdims=True))
        p = e / jnp.sum(e, axis=-1, keepdims=True)
        gate = _gate_col(sm_ref, nh_gdn, g, hpg, h, 2)
        o_ref[0, :, cols] = (_dot(p.astype(BF16), v) * gate).astype(o_ref.dtype)


def _win_attn(q_r, kwin_r, big, vwin_blk, small, nh_gdn):
    B, S, WQ = q_r.shape
    G = NSA_GROUPS
    hpg = WQ // HEAD_DIM // G
    tq = min(S, 512)
    assert S >= tq + NSA_WINDOW
    return pl.pallas_call(
        functools.partial(_win_attn_kernel, nh_gdn=nh_gdn, hpg=hpg),
        grid=(B, G, S // tq),
        in_specs=[pl.BlockSpec((1, tq, hpg * HEAD_DIM), lambda b, g, i: (b, i, g)),
                  pl.BlockSpec((1, S, HEAD_DIM), lambda b, g, i: (b, 0, g)),
                  pl.BlockSpec((1, S, HEAD_DIM), lambda b, g, i: (b, 0, vwin_blk + g)),
                  pl.BlockSpec((1, tq, small.shape[2]), lambda b, g, i: (b, i, 0))],
        out_specs=pl.BlockSpec((1, tq, hpg * HEAD_DIM), lambda b, g, i: (b, i, g)),
        out_shape=jax.ShapeDtypeStruct((B, S, WQ), BF16),
        compiler_params=_cparams(3),
        name="nsa_win_attn",
    )(q_r, kwin_r, big, small)


def _post_norm(x, y, gt, g, b, alpha):
    r = alpha * x + (1.0 + gt) * y
    return _normalize_rows(r) * g + b


def _mixer_out_kernel(oa_ref, oc_ref, os_ref, ow_ref, ma_ref, mb_ref, x_ref, gt_ref, lg_ref, lb_ref,
                      wg_ref, wn_ref, wo_ref, o_ref, *, alpha):
    y_a = _dot(oa_ref[0], wg_ref[...])
    o_b = oc_ref[0].astype(F32) + os_ref[0].astype(F32) + ow_ref[0].astype(F32)
    y_b = _dot(o_b.astype(BF16), wn_ref[...])
    mixed = _sigmoid(ma_ref[0].astype(F32)) * y_a + _sigmoid(mb_ref[0].astype(F32)) * y_b
    y = _dot(mixed.astype(BF16), wo_ref[...])
    o_ref[0] = _post_norm(x_ref[0], y, gt_ref[0], lg_ref[...], lb_ref[...], alpha)


def _mixer_out(o_a, o_c, o_s, o_w, big, ma_blk, mb_blk, x, ada_l, gt_col, ln_g, ln_b, wg, wn, wo, alpha):
    B, S, D = x.shape
    tm = min(S, 512)
    row = lambda blk: pl.BlockSpec((1, tm, D), lambda b, i: (b, i, blk))
    full = lambda shape: pl.BlockSpec(shape, lambda b, i: (0,) * len(shape))
    return pl.pallas_call(
        functools.partial(_mixer_out_kernel, alpha=alpha),
        grid=(B, S // tm),
        in_specs=[row(0), row(0), row(0), row(0), row(ma_blk), row(mb_blk), row(0),
                  pl.BlockSpec((1, 1, D), lambda b, i: (b, 0, gt_col)),
                  full((1, D)), full((1, D)), full((D, D)), full((D, D)), full((D, D))],
        out_specs=row(0),
        out_shape=jax.ShapeDtypeStruct((B, S, D), F32),
        compiler_params=_cparams(2),
        name="mixer_out",
    )(o_a, o_c, o_s, o_w, big, big, x, ada_l, ln_g.reshape(1, D), ln_b.reshape(1, D), wg, wn, wo)


def _first_argmax(vals, lane, big):
    m = jnp.max(vals, axis=-1, keepdims=True)
    first = jnp.min(jnp.where(vals == m, lane, big), axis=-1, keepdims=True)
    return m, first


def _moe_router_kernel(x_ref, sh_ref, sc_ref, rw_ref, rb_ref, wsg_ref, wsu_ref, wsd_ref,
                       h_ref, idx_ref, wt_ref, shared_ref):
    h = _normalize_rows(x_ref[0]) * (1.0 + sc_ref[0]) + sh_ref[0]
    h16 = h.astype(BF16)
    h_ref[0] = h16
    h_lo = (h - h16.astype(F32)).astype(BF16)
    rw = rw_ref[...]
    rw_hi = rw.astype(BF16)
    rw_lo = (rw - rw_hi.astype(F32)).astype(BF16)
    logits = _dot(h16, rw_hi) + _dot(h16, rw_lo) + _dot(h_lo, rw_hi)
    scores = _sigmoid(logits)
    biased = scores + rb_ref[...]
    tm, E = scores.shape
    per = E // N_EXPERT_GROUPS
    lane_i = lax.broadcasted_iota(jnp.int32, (tm, E), 1)
    lane = lane_i.astype(F32)
    grp_i = lane_i // per
    grp = grp_i.astype(F32)
    gscore = jnp.zeros((tm, E), F32)
    for gi in range(N_EXPERT_GROUPS):
        in_g = grp_i == gi
        vals = jnp.where(in_g, biased, -jnp.inf)
        m1, first = _first_argmax(vals, lane, float(E))
        m2 = jnp.max(jnp.where(lane == first, -jnp.inf, vals), axis=-1, keepdims=True)
        gscore = jnp.where(in_g, m1 + m2, gscore)
    cand = jnp.full((tm, E), NEG_INF, F32)
    for _ in range(TOPK_GROUPS):
        _, first_g = _first_argmax(gscore, grp, float(N_EXPERT_GROUPS))
        pick = grp == first_g
        cand = jnp.where(pick, biased, cand)
        gscore = jnp.where(pick, -jnp.inf, gscore)
    out_lane = lax.broadcasted_iota(jnp.int32, (tm, idx_ref.shape[2]), 1)
    idx_out = jnp.zeros((tm, idx_ref.shape[2]), F32)
    wt_out = jnp.zeros((tm, idx_ref.shape[2]), F32)
    wsum = jnp.zeros((tm, 1), F32)
    for kk in range(TOP_K):
        _, first = _first_argmax(cand, lane, float(E))
        pick = lane == first
        w = jnp.sum(jnp.where(pick, scores, 0.0), axis=-1, keepdims=True)
        cand = jnp.where(pick, -jnp.inf, cand)
        idx_out = jnp.where(out_lane == kk, first, idx_out)
        wt_out = jnp.where(out_lane == kk, w, wt_out)
        wsum = wsum + w
    idx_ref[0] = idx_out.astype(jnp.int32)
    wt_ref[0] = wt_out / wsum * ROUTED_SCALE
    act = _silu(_dot(h16, wsg_ref[...])) * _dot(h16, wsu_ref[...])
    shared_ref[0] = _dot(act.astype(BF16), wsd_ref[...])


def _moe_router(x, ada_l, sh_col, sc_col, router_w, router_b, wsg, wsu, wsd):
    B, S, D = x.shape
    E = router_w.shape[1]
    FF = wsg.shape[1]
    tm = min(S, 512)
    KP = 128
    row = lambda w: pl.BlockSpec((1, tm, w), lambda b, i: (b, i, 0))
    full = lambda shape: pl.BlockSpec(shape, lambda b, i: (0,) * len(shape))
    return pl.pallas_call(
        _moe_router_kernel,
        grid=(B, S // tm),
        in_specs=[row(D),
                  pl.BlockSpec((1, 1, D), lambda b, i: (b, 0, sh_col)),
                  pl.BlockSpec((1, 1, D), lambda b, i: (b, 0, sc_col)),
                  full((D, E)), full((1, E)), full((D, FF)), full((D, FF)), full((FF, D))],
        out_specs=[row(D), row(KP), row(KP), row(D)],
        out_shape=[jax.ShapeDtypeStruct((B, S, D), BF16),
                   jax.ShapeDtypeStruct((B, S, KP), jnp.int32),
                   jax.ShapeDtypeStruct((B, S, KP), F32),
                   jax.ShapeDtypeStruct((B, S, D), F32)],
        compiler_params=_cparams(2),
        name="moe_router",
    )(x, ada_l, ada_l, router_w, router_b.reshape(1, E), wsg, wsu, wsd)


def _moe_experts_kernel(be_ref, nu_ref, xs_ref, wg_ref, wu_ref, wd_ref, o_ref):
    @pl.when(pl.program_id(0) < nu_ref[0])
    def _():
        xb = xs_ref[...]
        act = _silu(_dot(xb, wg_ref[0, 0].astype(BF16))) * _dot(xb, wu_ref[0, 0].astype(BF16))
        o_ref[...] = _dot(act.astype(BF16), wd_ref[0, 0].astype(BF16)).astype(o_ref.dtype)


def _moe_experts(block_expert, n_used, xs, layer, wg, wu, wd):
    rows, D = xs.shape
    BM = MOE_ROW_BLOCK
    FF = wg.shape[3]
    grid_spec = pltpu.PrefetchScalarGridSpec(
        num_scalar_prefetch=2,
        grid=(rows // BM,),
        in_specs=[pl.BlockSpec((BM, D), lambda i, be, nu: (i, 0)),
                  pl.BlockSpec((1, 1, D, FF), lambda i, be, nu: (layer, be[i], 0, 0)),
                  pl.BlockSpec((1, 1, D, FF), lambda i, be, nu: (layer, be[i], 0, 0)),
                  pl.BlockSpec((1, 1, FF, D), lambda i, be, nu: (layer, be[i], 0, 0))],
        out_specs=pl.BlockSpec((BM, D), lambda i, be, nu: (i, 0)),
    )
    return pl.pallas_call(
        _moe_experts_kernel,
        grid_spec=grid_spec,
        out_shape=jax.ShapeDtypeStruct((rows, D), BF16),
        compiler_params=_cparams(1),
        name="moe_experts",
    )(block_expert, n_used, xs, wg, wu, wd)


def _moe_out_kernel(x_ref, shared_ref, yg_ref, wt_ref, gt_ref, lg_ref, lb_ref, o_ref, *, alpha):
    D = x_ref.shape[2]
    y = shared_ref[0]
    wt = wt_ref[0]
    for kk in range(TOP_K):
        y = y + wt[:, kk:kk + 1] * yg_ref[kk, 0].astype(F32)
    o_ref[0] = _post_norm(x_ref[0], y, gt_ref[0], lg_ref[...], lb_ref[...], alpha)


def _moe_out(x, shared, yg, wt, ada_l, gt_col, ln_g, ln_b, alpha):
    B, S, D = x.shape
    tm = min(S, 256)
    row = lambda w: pl.BlockSpec((1, tm, w), lambda b, i: (b, i, 0))
    full = lambda shape: pl.BlockSpec(shape, lambda b, i: (0,) * len(shape))
    return pl.pallas_call(
        functools.partial(_moe_out_kernel, alpha=alpha),
        grid=(B, S // tm),
        in_specs=[row(D), row(D),
                  pl.BlockSpec((TOP_K, 1, tm, D), lambda b, i: (0, b, i, 0)),
                  row(wt.shape[2]),
                  pl.BlockSpec((1, 1, D), lambda b, i: (b, 0, gt_col)),
                  full((1, D)), full((1, D))],
        out_specs=row(D),
        out_shape=jax.ShapeDtypeStruct((B, S, D), F32),
        compiler_params=_cparams(2),
        name="moe_out",
    )(x, shared, yg, wt, ada_l, ln_g.reshape(1, D), ln_b.reshape(1, D))


def _routing_layout(expert_idx, n_experts):
    T, K = expert_idx.shape
    BM = MOE_ROW_BLOCK
    TK = T * K
    e_flat = expert_idx.reshape(TK)
    order = jnp.argsort(e_flat)
    e_sorted = e_flat[order]
    counts = jnp.bincount(e_flat, length=n_experts)
    padded = (counts + BM - 1) // BM * BM
    pad_end = jnp.cumsum(padded)
    dest_sorted = (pad_end - padded)[e_sorted] + jnp.arange(TK) - (jnp.cumsum(counts) - counts)[e_sorted]
    n_blocks = (TK + n_experts * (BM - 1) + BM - 1) // BM
    dest_sorted = dest_sorted.astype(jnp.int32)
    row_tok = jnp.zeros((n_blocks * BM,), jnp.int32).at[dest_sorted].set(
        (order // K).astype(jnp.int32), unique_indices=True, indices_are_sorted=True, mode='promise_in_bounds')
    dest = jnp.zeros((TK,), jnp.int32).at[order].set(dest_sorted, unique_indices=True, mode='promise_in_bounds')
    block_start = jnp.arange(n_blocks, dtype=pad_end.dtype) * BM
    block_expert = jnp.minimum(jnp.sum(pad_end[None, :] <= block_start[:, None], axis=1),
                               n_experts - 1).astype(jnp.int32)
    n_used = (pad_end[-1] // BM).astype(jnp.int32).reshape(1)
    return row_tok, dest.reshape(T, K), block_expert, n_used


def kernel(x, c, positions, ada_w, ada_b, w_in, gdn_conv_w, gdn_a_log, gdn_dt_bias, gdn_norm_w, w_gdn_branch, nsa_cmp_pos_k, nsa_cmp_pos_v, nsa_cmp_k_w1, nsa_cmp_k_w2, nsa_cmp_v_w1, nsa_cmp_v_w2, w_nsa_branch, w_out, ln1_g, ln1_b, router_w, router_bias, w_sh_gate, w_sh_up, w_sh_down, w_e_gate, w_e_up, w_e_down, ln2_g, ln2_b):
    B, S, D = x.shape
    L = ada_w.shape[0]
    T = B * S
    G = NSA_GROUPS
    nh_gdn = gdn_a_log.shape[1]
    gdn_w = nh_gdn * HEAD_DIM
    nsa_w = w_nsa_branch.shape[1]
    kvw = G * HEAD_DIM
    E = router_w.shape[2]
    alpha = (2.0 * L) ** 0.25
    assert gdn_w == D and nsa_w == D and S % NSA_SEL_LEN == 0

    splits = (3 * gdn_w, gdn_w, nh_gdn, nh_gdn, nsa_w, 6 * kvw, 3 * (nsa_w // HEAD_DIM), D, D)
    offs = [0]
    for s_ in splits:
        offs.append(offs[-1] + s_)
    seg = lambda i: slice(offs[i], offs[i + 1])
    w_big = jnp.concatenate([w_in[:, :, seg(0)], w_in[:, :, seg(1)], w_in[:, :, seg(4)],
                             w_in[:, :, seg(7)], w_in[:, :, seg(8)], w_in[:, :, seg(5)]], axis=-1).astype(BF16)
    n_small = splits[2] + splits[3] + splits[6]
    w_small = jnp.concatenate([w_in[:, :, seg(2)], w_in[:, :, seg(3)], w_in[:, :, seg(6)],
                               jnp.zeros((L, D, 128 - n_small), w_in.dtype)], axis=-1).astype(BF16)
    Z_BLK, Q_BLK, MA_BLK, MB_BLK = 3, 4, 5, 6
    kv0 = 7 * D
    n_big = w_big.shape[2]
    tn_big = n_big // 4 if (n_big // 4) % 128 == 0 else 128

    wgb = w_gdn_branch.astype(BF16)
    wnb = w_nsa_branch.astype(BF16)
    wob = w_out.astype(BF16)
    wsg = w_sh_gate.astype(BF16)
    wsu = w_sh_up.astype(BF16)
    wsd = w_sh_down.astype(BF16)
    cmp_w1 = jnp.stack([nsa_cmp_k_w1, nsa_cmp_v_w1], axis=1).astype(BF16)
    cmp_w2 = jnp.stack([nsa_cmp_k_w2, nsa_cmp_v_w2], axis=1).astype(BF16)
    cmp_pos = jnp.stack([nsa_cmp_pos_k, nsa_cmp_pos_v], axis=1).reshape(L, 2, 1, NSA_CMP_LEN * HEAD_DIM)

    ada = _ada(c, ada_w, ada_b)
    cosf, sinf = _rope_tables(positions)
    NC = S // NSA_CMP_STRIDE
    last = jnp.minimum(jnp.arange(NC) * NSA_CMP_STRIDE + NSA_CMP_LEN - 1, S - 1)
    cosc = cosf[:, last]
    sinc = sinf[:, last]
    n_sel = -(-(S // NSA_SEL_LEN) // 128) * 128

    for l in range(L):
        ada_l = ada[l].reshape(B, 1, 6 * D)
        big = _proj(x, ada_l, 0, 1, w_big[l], BF16, tn_big)
        small = _proj(x, ada_l, 0, 1, w_small[l], F32, 128)
        small_t = jnp.swapaxes(small[:, :, nh_gdn:2 * nh_gdn].reshape(B, S // GDN_CHUNK, GDN_CHUNK, nh_gdn), 2, 3)
        q_a, k_a, v_a = _gdn_prep(big, gdn_conv_w[l], gdn_w)
        o_a = _gdn(q_a, k_a, v_a, big, Z_BLK, small, small_t, gdn_a_log[l], gdn_dt_bias[l], gdn_norm_w[l])
        q_r, ksel_r, kwin_r = _rope_apply(big, Q_BLK, (kv0 + 2 * kvw) // kvw, (kv0 + 4 * kvw) // kvw,
                                          nsa_w, kvw, cosf, sinf)
        hb = big[:, :, kv0:kv0 + 2 * kvw].reshape(B, NC, NSA_CMP_STRIDE, 2 * G, HEAD_DIM)
        hb = hb.transpose(0, 3, 1, 2, 4).reshape(B, 2 * G, NC, NSA_CMP_STRIDE * HEAD_DIM)
        kvc = _compress(hb, cmp_pos[l], cmp_w1[l], cmp_w2[l], cosc, sinc)
        o_c, sel = _cmp_attn(q_r, kvc, small, nh_gdn, n_sel)
        o_s = _sel_attn(q_r, ksel_r, big, (kv0 + 3 * kvw) // HEAD_DIM, sel, small, nh_gdn)
        o_w = _win_attn(q_r, kwin_r, big, (kv0 + 5 * kvw) // HEAD_DIM, small, nh_gdn)
        x = _mixer_out(o_a, o_c, o_s, o_w, big, MA_BLK, MB_BLK, x, ada_l, 2, ln1_g[l], ln1_b[l],
                       wgb[l], wnb[l], wob[l], alpha)
        h2, eidx, ewt, shared = _moe_router(x, ada_l, 3, 4, router_w[l], router_bias[l], wsg[l], wsu[l], wsd[l])
        row_tok, dest, block_expert, n_used = _routing_layout(eidx.reshape(T, -1)[:, :TOP_K], E)
        xs = h2.reshape(T, D).at[row_tok].get(mode='promise_in_bounds')
        yb = _moe_experts(block_expert, n_used, xs, l, w_e_gate, w_e_up, w_e_down)
        yg = yb.at[dest.T.reshape(TOP_K * T)].get(mode='promise_in_bounds').reshape(TOP_K, B, S, D)
        x = _moe_out(x, shared, yg, ewt, ada_l, 5, ln2_g[l], ln2_b[l], alpha)
    return x
```

```python
import functools
import math

import jax
import jax.numpy as jnp
from jax import lax
from jax.experimental import pallas as pl
from jax.experimental.pallas import tpu as pltpu

F32 = jnp.float32
BF16 = jnp.bfloat16

HEAD_DIM = 128
GDN_CONV = 4
GDN_CHUNK = 64
NSA_GROUPS = 2
NSA_CMP_LEN = 32
NSA_CMP_STRIDE = 16
NSA_SEL_LEN = 64
NSA_SEL_TOP = 16
NSA_WINDOW = 512
ROPE_THETA = 10000.0
N_EXPERTS = 64
N_EXPERT_GROUPS = 8
TOPK_GROUPS = 4
TOP_K = 8
ROUTED_SCALE = 2.5
LN_EPS = 1e-5
NEG_INF = -1e30
SEL_FORCE = 1e6
MOE_ROW_BLOCK = 512
Q_SCALE_LOG2E = (HEAD_DIM ** -0.5) * math.log2(math.e)
MASK_BIG = 2.0 ** 100
N_ROW_PARTS = 2
GDN_PHASE1_CHUNKS = 4

_ARB = "arbitrary"


def _cparams(n_axes):
    return pltpu.CompilerParams(dimension_semantics=(_ARB,) * n_axes)


def _sigmoid(x):
    return 1.0 / (1.0 + jnp.exp(-x))


def _silu(x):
    return x * _sigmoid(x)


def _dot(a, b):
    return jnp.dot(a, b, preferred_element_type=F32)


def _dot_nt(a, b):
    return lax.dot_general(a, b, (((1,), (1,)), ((), ())), preferred_element_type=F32)


def _dot_tn(a, b):
    return lax.dot_general(a, b, (((0,), (0,)), ((), ())), preferred_element_type=F32)


def _split3(x):
    x0 = x.astype(BF16)
    r1 = x - x0.astype(F32)
    x1 = r1.astype(BF16)
    x2 = (r1 - x1.astype(F32)).astype(BF16)
    return x0, x1, x2


def _dot_sel_right(x, sel_bf16):
    x0, x1, x2 = _split3(x)
    return _dot(x0, sel_bf16) + _dot(x1, sel_bf16) + _dot(x2, sel_bf16)


def _dot_sel_left(sel_bf16, x):
    x0, x1, x2 = _split3(x)
    return _dot(sel_bf16, x0) + _dot(sel_bf16, x1) + _dot(sel_bf16, x2)


def _normalize_rows(x):
    mu = jnp.mean(x, axis=-1, keepdims=True)
    xc = x - mu
    var = jnp.mean(xc * xc, axis=-1, keepdims=True)
    return xc * lax.rsqrt(var + LN_EPS)


def _ada_kernel(c_ref, w_ref, b_ref, o_ref):
    cond = _silu(c_ref[...])
    o_ref[0] = jnp.dot(cond, w_ref[0], preferred_element_type=F32,
                       precision=lax.Precision.HIGHEST) + b_ref[0]


def _ada(c, ada_w, ada_b):
    L, D, N = ada_w.shape
    B = c.shape[0]
    tn = min(N, 1536)
    return pl.pallas_call(
        _ada_kernel,
        grid=(L, N // tn),
        in_specs=[pl.BlockSpec((B, D), lambda l, j: (0, 0)),
                  pl.BlockSpec((1, D, tn), lambda l, j: (l, 0, j)),
                  pl.BlockSpec((1, 1, tn), lambda l, j: (l, 0, j))],
        out_specs=pl.BlockSpec((1, B, tn), lambda l, j: (l, 0, j)),
        out_shape=jax.ShapeDtypeStruct((L, B, N), F32),
        compiler_params=_cparams(2),
        name="ada",
    )(c, ada_w, ada_b.reshape(L, 1, N))


def _proj_kernel(x_ref, sh_ref, sc_ref, w_ref, ws_ref, o_ref, os_ref, h_ref):
    @pl.when(pl.program_id(2) == 0)
    def _():
        h = _normalize_rows(x_ref[0]) * (1.0 + sc_ref[0]) + sh_ref[0]
        h_ref[...] = h.astype(BF16)
        os_ref[0] = _dot(h_ref[...], ws_ref[...])

    o_ref[0] = _dot(h_ref[...], w_ref[...]).astype(o_ref.dtype)


def _proj(x, ada_l, sh_col, sc_col, w, w_small, tn):
    B, S, D = x.shape
    N = w.shape[1]
    NS = w_small.shape[1]
    tm = min(S, 1024)
    return pl.pallas_call(
        _proj_kernel,
        grid=(B, S // tm, N // tn),
        in_specs=[pl.BlockSpec((1, tm, D), lambda b, i, j: (b, i, 0)),
                  pl.BlockSpec((1, 1, D), lambda b, i, j: (b, 0, sh_col)),
                  pl.BlockSpec((1, 1, D), lambda b, i, j: (b, 0, sc_col)),
                  pl.BlockSpec((D, tn), lambda b, i, j: (0, j)),
                  pl.BlockSpec((D, NS), lambda b, i, j: (0, 0))],
        out_specs=[pl.BlockSpec((1, tm, tn), lambda b, i, j: (b, i, j)),
                   pl.BlockSpec((1, tm, NS), lambda b, i, j: (b, i, 0))],
        out_shape=[jax.ShapeDtypeStruct((B, S, N), BF16),
                   jax.ShapeDtypeStruct((B, S, NS), F32)],
        scratch_shapes=[pltpu.VMEM((tm, D), BF16)],
        compiler_params=_cparams(3),
        name="proj",
    )(x, ada_l, ada_l, w, w_small)


def _gdn_prep_kernel(x_ref, w_ref, q_ref, k_ref, v_ref, carry_ref):
    ts = x_ref.shape[1]
    width = q_ref.shape[2]
    nh = width // HEAD_DIM

    @pl.when(pl.program_id(1) == 0)
    def _():
        carry_ref[...] = jnp.zeros_like(carry_ref)

    for part, o_ref in enumerate((q_ref, k_ref, v_ref)):
        for h in range(nh):
            c0 = part * width + h * HEAD_DIM
            cols = slice(c0, c0 + HEAD_DIM)
            xx = jnp.concatenate([carry_ref[:, cols], x_ref[0, :, cols].astype(F32)], axis=0)
            w = w_ref[:, cols]
            y = xx[8:8 + ts] * w[3:4]
            for kk in range(GDN_CONV - 1):
                off = 8 - (GDN_CONV - 1) + kk
                y = y + xx[off:off + ts] * w[kk:kk + 1]
            y = _silu(y)
            if part < 2:
                y = y * lax.rsqrt(jnp.sum(y * y, axis=-1, keepdims=True) + 1e-6)
            if part == 0:
                y = y * (HEAD_DIM ** -0.5)
            o_ref[0, :, h * HEAD_DIM:(h + 1) * HEAD_DIM] = y.astype(o_ref.dtype)
    carry_ref[...] = x_ref[0, ts - 8:ts, :].astype(F32)


def _gdn_prep(big, conv_w, width):
    B, S, _ = big.shape
    ts = min(S, 512)
    out = jax.ShapeDtypeStruct((B, S, width), BF16)
    ospec = pl.BlockSpec((1, ts, width), lambda b, i: (b, i, 0))
    return pl.pallas_call(
        _gdn_prep_kernel,
        grid=(B, S // ts),
        in_specs=[pl.BlockSpec((1, ts, 3 * width), lambda b, i: (b, i, 0)),
                  pl.BlockSpec((GDN_CONV, 3 * width), lambda b, i: (0, 0))],
        out_specs=[ospec, ospec, ospec],
        out_shape=[out, out, out],
        scratch_shapes=[pltpu.VMEM((8, 3 * width), F32)],
        compiler_params=_cparams(2),
        name="gdn_prep",
    )(big, conv_w)


def _softplus(x):
    return jnp.maximum(x, 0.0) + jnp.log(1.0 + jnp.exp(-jnp.abs(x)))


def _gdn_kernel(q_ref, k_ref, v_ref, z_ref, sm_ref, smt_ref, alog_ref, alogt_ref, dtb_ref, dtbt_ref,
                nw_ref, o_ref, state_ref, u_s, wq_s, attn_s, kd_s, dec_s):
    ts = q_ref.shape[1]
    nh = q_ref.shape[2] // HEAD_DIM
    C = GDN_CHUNK
    R = 2 * C
    npair = nh // 2
    nchunks = ts // C

    @pl.when(pl.program_id(1) == 0)
    def _():
        state_ref[...] = jnp.zeros_like(state_ref)

    ci = lax.broadcasted_iota(jnp.int32, (C, C), 0)
    cj = lax.broadcasted_iota(jnp.int32, (C, C), 1)
    tril = jnp.where(ci >= cj, 1.0, 0.0).astype(BF16)
    triu = jnp.where(cj >= ci, 1.0, 0.0).astype(BF16)
    ii = lax.broadcasted_iota(jnp.int32, (R, R), 0)
    jj = lax.broadcasted_iota(jnp.int32, (R, R), 1)
    same_head = (ii // C) == (jj // C)
    incl = same_head & (ii >= jj)
    strict = same_head & (ii > jj)
    eye = jnp.where(ii == jj, 1.0, 0.0)
    nw = nw_ref[...]

    def pair_rows(ref, rows, p):
        return jnp.concatenate([ref[0, rows, (2 * p) * HEAD_DIM:(2 * p + 1) * HEAD_DIM],
                                ref[0, rows, (2 * p + 1) * HEAD_DIM:(2 * p + 2) * HEAD_DIM]], axis=0)

    def pair_col(x, p):
        return jnp.concatenate([x[:, 2 * p:2 * p + 1], x[:, 2 * p + 1:2 * p + 2]], axis=0)

    def phase1(j, carry):
        cs = [j * GDN_PHASE1_CHUNKS + d for d in range(GDN_PHASE1_CHUNKS)]
        rows_c, beta_c, gc_c, gcr_c, glb_c = [], [], [], [], []
        for c in cs:
            rows = pl.ds(pl.multiple_of(c * C, C), C)
            sm = sm_ref[0, rows, :]
            g = -jnp.exp(alog_ref[...]) * _softplus(sm[:, nh:2 * nh] + dtb_ref[...])
            gt = -jnp.exp(alogt_ref[...]) * _softplus(smt_ref[0, c] + dtbt_ref[...])
            gc = _dot_sel_left(tril, g)
            g_last = gc[C - 1:C, :]
            dec_s[c] = jnp.exp(g_last)
            rows_c.append(rows)
            beta_c.append(_sigmoid(sm[:, 0:nh]))
            gc_c.append(gc)
            gcr_c.append(_dot_sel_right(gt, triu))
            glb_c.append(jnp.broadcast_to(g_last, (C, nh)))
        items = [(ci, p) for ci in range(len(cs)) for p in range(npair)]
        idx = range(len(items))
        q2 = [pair_rows(q_ref, rows_c[ci], p).astype(F32) for ci, p in items]
        k2 = [pair_rows(k_ref, rows_c[ci], p).astype(F32) for ci, p in items]
        b2 = [pair_col(beta_c[ci], p) for ci, p in items]
        g_col = [pair_col(gc_c[ci], p) for ci, p in items]
        kb = [k2[i] * b2[i] for i in idx]
        kq = [_dot_nt(jnp.concatenate([kb[i], q2[i]], axis=0).astype(BF16), k2[i].astype(BF16))
              for i in idx]
        decay = []
        for i, (ci, p) in enumerate(items):
            gcr = gcr_c[ci]
            g_row = jnp.concatenate([gcr[2 * p:2 * p + 1, :], gcr[2 * p + 1:2 * p + 2, :]], axis=1)
            decay.append(jnp.where(incl, jnp.exp(jnp.where(incl, g_col[i] - g_row, 0.0)), 0.0))
        a = [jnp.where(strict, kq[i][:R] * decay[i], 0.0) for i in idx]
        for i, (ci, p) in enumerate(items):
            attn_s[cs[ci], p] = (kq[i][R:] * decay[i]).astype(BF16)
        x = [eye - a[i] for i in idx]
        pw = a
        n = 2
        while n < C:
            pw16 = [pw[i].astype(BF16) for i in idx]
            pw = [_dot(pw16[i], pw16[i]) for i in idx]
            x = [x[i] + _dot(x[i].astype(BF16), pw[i].astype(BF16)) for i in idx]
            n *= 2
        eg = [jnp.exp(g_col[i]) for i in idx]
        sol = []
        for i, (ci, p) in enumerate(items):
            v2 = pair_rows(v_ref, rows_c[ci], p).astype(F32)
            r = jnp.concatenate([v2 * b2[i], kb[i] * eg[i]], axis=1)
            sol.append(_dot(x[i].astype(BF16), r.astype(BF16)))
        for i, (ci, p) in enumerate(items):
            c = cs[ci]
            u_s[c, p] = sol[i][:, :HEAD_DIM]
            w = sol[i][:, HEAD_DIM:]
            qg = q2[i] * eg[i]
            for e in range(2):
                wq_s[c, 2 * p + e] = jnp.concatenate([w[e * C:(e + 1) * C], qg[e * C:(e + 1) * C]],
                                                     axis=0).astype(BF16)
            kd_s[c, p] = (k2[i] * jnp.exp(pair_col(glb_c[ci], p) - g_col[i])).astype(BF16)
        return carry

    assert nchunks % GDN_PHASE1_CHUNKS == 0
    lax.fori_loop(0, nchunks // GDN_PHASE1_CHUNKS, phase1, 0)

    def phase2(c, carry):
        rows = pl.ds(pl.multiple_of(c * C, C), C)
        dec = dec_s[c]
        res = [_dot(wq_s[c, h], state_ref[h].astype(BF16)) for h in range(nh)]
        for p in range(npair):
            ws = jnp.concatenate([res[2 * p][:C], res[2 * p + 1][:C]], axis=0)
            qs = jnp.concatenate([res[2 * p][C:], res[2 * p + 1][C:]], axis=0)
            v_new = u_s[c, p] - ws
            v16 = v_new.astype(BF16)
            o2 = qs + _dot(attn_s[c, p], v16)
            kd = kd_s[c, p]
            for e in range(2):
                h = 2 * p + e
                part = slice(e * C, (e + 1) * C)
                state_ref[h] = state_ref[h] * dec[:, h:h + 1] + _dot_tn(kd[part], v16[part])
                o = o2[part]
                o = o * lax.rsqrt(jnp.mean(o * o, axis=-1, keepdims=True) + 1e-6) * nw
                cols = slice(h * HEAD_DIM, (h + 1) * HEAD_DIM)
                z = z_ref[0, rows, cols].astype(F32)
                o_ref[0, rows, cols] = (o * _silu(z)).astype(o_ref.dtype)
        return carry

    lax.fori_loop(0, nchunks, phase2, 0)


def _gdn(q, k, v, big, z_blk, small, small_t, a_log, dt_bias, norm_w):
    B, S, W = q.shape
    nh = W // HEAD_DIM
    assert nh % 2 == 0 and 2 * GDN_CHUNK == HEAD_DIM
    ts = min(S, 512)
    nc = ts // GDN_CHUNK
    spec = pl.BlockSpec((1, ts, W), lambda b, i: (b, i, 0))
    full = lambda shape: pl.BlockSpec(shape, lambda b, i: (0,) * len(shape))
    return pl.pallas_call(
        _gdn_kernel,
        grid=(B, S // ts),
        in_specs=[spec, spec, spec,
                  pl.BlockSpec((1, ts, W), lambda b, i: (b, i, z_blk)),
                  pl.BlockSpec((1, ts, small.shape[2]), lambda b, i: (b, i, 0)),
                  pl.BlockSpec((1, ts // GDN_CHUNK, nh, GDN_CHUNK), lambda b, i: (b, i, 0, 0)),
                  full((1, nh)), full((nh, 1)), full((1, nh)), full((nh, 1)),
                  full((1, HEAD_DIM))],
        out_specs=spec,
        out_shape=jax.ShapeDtypeStruct((B, S, W), BF16),
        scratch_shapes=[pltpu.VMEM((nh, HEAD_DIM, HEAD_DIM), F32),
                        pltpu.VMEM((nc, nh // 2, 2 * GDN_CHUNK, HEAD_DIM), F32),
                        pltpu.VMEM((nc, nh, 2 * GDN_CHUNK, HEAD_DIM), BF16),
                        pltpu.VMEM((nc, nh // 2, 2 * GDN_CHUNK, 2 * GDN_CHUNK), BF16),
                        pltpu.VMEM((nc, nh // 2, 2 * GDN_CHUNK, HEAD_DIM), BF16),
                        pltpu.VMEM((nc, 1, nh), F32)],
        compiler_params=_cparams(2),
        name="gdn",
    )(q, k, v, big, small, small_t, a_log.reshape(1, nh), a_log.reshape(nh, 1),
      dt_bias.reshape(1, nh), dt_bias.reshape(nh, 1), norm_w.reshape(1, HEAD_DIM))


def _rope_table_kernel(pos_ref, invf_ref, sign_ref, cos_ref, sin_ref):
    ang = pos_ref[0].astype(F32) * invf_ref[...]
    cos_ref[0] = jnp.cos(ang)
    sin_ref[0] = jnp.sin(ang) * sign_ref[...]


def _rope_tables(positions):
    B, S = positions.shape
    half = HEAD_DIM // 2
    inv = ROPE_THETA ** (-jnp.arange(half, dtype=F32) / half)
    invf = jnp.concatenate([inv, inv]).reshape(1, HEAD_DIM)
    sign = jnp.concatenate([-jnp.ones((half,), F32), jnp.ones((half,), F32)]).reshape(1, HEAD_DIM)
    ts = min(S, 1024)
    out = jax.ShapeDtypeStruct((B, S, HEAD_DIM), F32)
    ospec = pl.BlockSpec((1, ts, HEAD_DIM), lambda b, i: (b, i, 0))
    return pl.pallas_call(
        _rope_table_kernel,
        grid=(B, S // ts),
        in_specs=[pl.BlockSpec((1, ts, 1), lambda b, i: (b, i, 0)),
                  pl.BlockSpec((1, HEAD_DIM), lambda b, i: (0, 0)),
                  pl.BlockSpec((1, HEAD_DIM), lambda b, i: (0, 0))],
        out_specs=[ospec, ospec],
        out_shape=[out, out],
        compiler_params=_cparams(2),
        name="rope_table",
    )(positions.reshape(B, S, 1), invf, sign)


def _rope_rows(x, cosf, sinf):
    return x * cosf + pltpu.roll(x, HEAD_DIM // 2, 1) * sinf


def _rope_apply_kernel(q_ref, ks_ref, kw_ref, cos_ref, sin_ref, qo_ref, kso_ref, kwo_ref):
    cosf = cos_ref[0]
    sinf = sin_ref[0]
    for src, dst, mult in ((q_ref, qo_ref, Q_SCALE_LOG2E), (ks_ref, kso_ref, None), (kw_ref, kwo_ref, None)):
        for h in range(src.shape[2] // HEAD_DIM):
            cols = slice(h * HEAD_DIM, (h + 1) * HEAD_DIM)
            r = _rope_rows(src[0, :, cols].astype(F32), cosf, sinf)
            if mult is not None:
                r = r * mult
            dst[0, :, cols] = r.astype(dst.dtype)


def _rope_apply(big, q_blk, ksel_blk, kwin_blk, wq, wkv, cosf, sinf):
    B, S, _ = big.shape
    ts = min(S, 512)
    tab = pl.BlockSpec((1, ts, HEAD_DIM), lambda b, i: (b, i, 0))
    return pl.pallas_call(
        _rope_apply_kernel,
        grid=(B, S // ts),
        in_specs=[pl.BlockSpec((1, ts, wq), lambda b, i: (b, i, q_blk)),
                  pl.BlockSpec((1, ts, wkv), lambda b, i: (b, i, ksel_blk)),
                  pl.BlockSpec((1, ts, wkv), lambda b, i: (b, i, kwin_blk)),
                  tab, tab],
        out_specs=[pl.BlockSpec((1, ts, wq), lambda b, i: (b, i, 0)),
                   pl.BlockSpec((1, ts, wkv), lambda b, i: (b, i, 0)),
                   pl.BlockSpec((1, ts, wkv), lambda b, i: (b, i, 0))],
        out_shape=[jax.ShapeDtypeStruct((B, S, wq), BF16),
                   jax.ShapeDtypeStruct((B, S, wkv), BF16),
                   jax.ShapeDtypeStruct((B, S, wkv), BF16)],
        compiler_params=_cparams(2),
        name="rope_apply",
    )(big, big, big, cosf, sinf)


def _compress_kernel(hb_ref, pos_ref, w1_ref, w2_ref, cos_ref, sin_ref, o_ref):
    hb = hb_ref[0, 0]
    w1 = w1_ref[0]
    half = hb.shape[1]
    p0 = _dot(hb, w1[:half])
    p1 = _dot(hb, w1[half:])
    nc = p0.shape[0]
    pos8 = jnp.broadcast_to(pos_ref[0], (8, 2 * half)).astype(BF16)
    pb = _dot(pos8, w1)[0:1]
    pre = p0 + pltpu.roll(p1, nc - 1, 0) + pb
    out = _dot(_silu(pre).astype(BF16), w2_ref[0])
    roped = _rope_rows(out, cos_ref[0], sin_ref[0])
    is_key = pl.program_id(1) < NSA_GROUPS
    o_ref[0, 0] = jnp.where(is_key, roped, out).astype(o_ref.dtype)


def _compress(hb, pos_flat, w1, w2, cosc, sinc):
    B, J, NC, HW = hb.shape
    G = NSA_GROUPS
    return pl.pallas_call(
        _compress_kernel,
        grid=(B, J),
        in_specs=[pl.BlockSpec((1, 1, NC, HW), lambda b, j: (b, j, 0, 0)),
                  pl.BlockSpec((1, 1, 2 * HW), lambda b, j: (j // G, 0, 0)),
                  pl.BlockSpec((1, 2 * HW, HEAD_DIM), lambda b, j: (j // G, 0, 0)),
                  pl.BlockSpec((1, HEAD_DIM, HEAD_DIM), lambda b, j: (j // G, 0, 0)),
                  pl.BlockSpec((1, NC, HEAD_DIM), lambda b, j: (b, 0, 0)),
                  pl.BlockSpec((1, NC, HEAD_DIM), lambda b, j: (b, 0, 0))],
        out_specs=pl.BlockSpec((1, 1, NC, HEAD_DIM), lambda b, j: (b, j, 0, 0)),
        out_shape=jax.ShapeDtypeStruct((B, J, NC, HEAD_DIM), BF16),
        compiler_params=_cparams(2),
        name="nsa_compress",
    )(hb, pos_flat, w1, w2, cosc, sinc)


def _gate_col(g_ref, nh_gdn, group, hpg, h, branch):
    col = None
    for gi in range(NSA_GROUPS):
        c = 2 * nh_gdn + (gi * hpg + h) * 3 + branch
        cand = g_ref[0, :, c:c + 1]
        col = cand if col is None else jnp.where(group == gi, cand, col)
    return _sigmoid(col)


def _cmp_attn_kernel(q_ref, kc_ref, vc_ref, sm_ref, o_ref, sel_ref, *, nh_gdn, hpg):
    tq = q_ref.shape[1]
    nc = kc_ref.shape[2]
    nb = sel_ref.shape[3]
    g = pl.program_id(1)
    t0 = pl.program_id(2) * tq
    t_nc = t0 + lax.broadcasted_iota(jnp.int32, (tq, nc), 0)
    n_nc = lax.broadcasted_iota(jnp.int32, (tq, nc), 1)
    valid = (n_nc * NSA_CMP_STRIDE + (NSA_CMP_LEN - 1)) <= t_nc
    kc = kc_ref[0, 0]
    vc = vc_ref[0, 0]
    psum = jnp.zeros((tq, nc), F32)
    for h in range(hpg):
        cols = slice(h * HEAD_DIM, (h + 1) * HEAD_DIM)
        s = jnp.where(valid, _dot_nt(q_ref[0, :, cols], kc), NEG_INF)
        e = jnp.exp2(s - jnp.max(s, axis=-1, keepdims=True))
        p = e / jnp.sum(e, axis=-1, keepdims=True)
        p = jnp.where(valid, p, 0.0)
        gate = _gate_col(sm_ref, nh_gdn, g, hpg, h, 0)
        o_ref[0, :, cols] = (_dot(p.astype(BF16), vc) * gate).astype(o_ref.dtype)
        psum = psum + p
    sj = lax.broadcasted_iota(jnp.int32, (nb, nc), 0) * NSA_SEL_LEN
    cn = lax.broadcasted_iota(jnp.int32, (nb, nc), 1) * NSA_CMP_STRIDE
    overlap_t = jnp.where((cn <= sj + (NSA_SEL_LEN - 1)) & (cn + (NSA_CMP_LEN - 1) >= sj), 1.0, 0.0).astype(BF16)
    p0, p1, p2 = _split3(psum)
    imp = _dot_nt(overlap_t, p0) + _dot_nt(overlap_t, p1) + _dot_nt(overlap_t, p2)
    t_nb = t0 + lax.broadcasted_iota(jnp.int32, (nb, tq), 1)
    blk = lax.broadcasted_iota(jnp.int32, (nb, tq), 0)
    cur = t_nb // NSA_SEL_LEN
    forced = (blk == 0) | (blk == cur) | (blk == cur - 1)
    score = jnp.where(forced, SEL_FORCE, jnp.where(blk * NSA_SEL_LEN <= t_nb, imp, -1.0))
    sel = jnp.zeros((nb, tq), F32)
    blk_f = blk.astype(F32)
    for _ in range(min(NSA_SEL_TOP, nb)):
        m = jnp.max(score, axis=0, keepdims=True)
        first = jnp.min(jnp.where(score == m, blk_f, float(nb)), axis=0, keepdims=True)
        pick = blk_f == first
        sel = jnp.where(pick, 1.0, sel)
        score = jnp.where(pick, -jnp.inf, score)
    sel_ref[0, 0] = sel.T.astype(sel_ref.dtype)


def _cmp_attn(q_r, kvc, small, nh_gdn, n_sel):
    B, S, WQ = q_r.shape
    G = NSA_GROUPS
    hpg = WQ // HEAD_DIM // G
    NC = kvc.shape[2]
    tq = min(S, 512)
    return pl.pallas_call(
        functools.partial(_cmp_attn_kernel, nh_gdn=nh_gdn, hpg=hpg),
        grid=(B, G, S // tq),
        in_specs=[pl.BlockSpec((1, tq, hpg * HEAD_DIM), lambda b, g, i: (b, i, g)),
                  pl.BlockSpec((1, 1, NC, HEAD_DIM), lambda b, g, i: (b, g, 0, 0)),
                  pl.BlockSpec((1, 1, NC, HEAD_DIM), lambda b, g, i: (b, G + g, 0, 0)),
                  pl.BlockSpec((1, tq, small.shape[2]), lambda b, g, i: (b, i, 0))],
        out_specs=[pl.BlockSpec((1, tq, hpg * HEAD_DIM), lambda b, g, i: (b, i, g)),
                   pl.BlockSpec((1, 1, tq, n_sel), lambda b, g, i: (b, g, i, 0))],
        out_shape=[jax.ShapeDtypeStruct((B, S, WQ), BF16),
                   jax.ShapeDtypeStruct((B, G, S, n_sel), BF16)],
        compiler_params=_cparams(3),
        name="nsa_cmp_attn",
    )(q_r, kvc, kvc, small)


def _sel_attn_kernel(q_ref, k_ref, ext_ref, v_ref, sel_ref, sm_ref, o_ref, qx_ref, m_ref, acc_ref, p_ref, a_ref,
                     *, nh_gdn, hpg, tk):
    tq = q_ref.shape[1]
    nb = sel_ref.shape[3]
    g = pl.program_id(1)
    t0 = pl.program_id(2) * tq
    assert tk % tq == 0
    unsel = (sel_ref[0, 0].astype(F32) - 1.0).astype(BF16)
    for h in range(hpg):
        qx_ref[h * tq:(h + 1) * tq, 0:HEAD_DIM] = q_ref[0, :, h * HEAD_DIM:(h + 1) * HEAD_DIM]
        qx_ref[h * tq:(h + 1) * tq, HEAD_DIM:HEAD_DIM + nb] = unsel
    m_ref[...] = jnp.full_like(m_ref, NEG_INF)
    acc_ref[...] = jnp.zeros_like(acc_ref)
    p_ref[...] = jnp.zeros_like(p_ref)
    a_ref[...] = jnp.ones_like(a_ref)
    ones = jnp.ones((tk, HEAD_DIM), BF16)
    pr = hpg * tq // N_ROW_PARTS
    parts = [slice(part * pr, (part + 1) * pr) for part in range(N_ROW_PARTS)]

    def apply_values(kt):
        k0 = pl.multiple_of(kt * tk, tk)
        vx = jnp.concatenate([v_ref[0, pl.ds(k0, tk), :], ones], axis=1)
        for rows in parts:
            acc_ref[rows, :] = pltpu.repeat(a_ref[rows, :], 2, 1) * acc_ref[rows, :] + _dot(p_ref[rows, :], vx)

    def scores(kt, causal):
        k0 = pl.multiple_of(kt * tk, tk)
        kx = jnp.concatenate([k_ref[0, pl.ds(k0, tk), :], ext_ref[pl.ds(k0, tk), :]], axis=1)
        if causal:
            t_row = t0 + lax.broadcasted_iota(jnp.int32, (tq, tk), 0)
            key = k0 + lax.broadcasted_iota(jnp.int32, (tq, tk), 1)
            keep = jnp.concatenate([key <= t_row] * (hpg // N_ROW_PARTS), axis=0)
        for rows in parts:
            s = _dot_nt(qx_ref[rows, :], kx)
            if causal:
                s = jnp.where(keep, s, NEG_INF)
            m_old = m_ref[rows, :]
            m_new = jnp.maximum(m_old, jnp.max(s, axis=-1, keepdims=True))
            a_ref[rows, :] = jnp.exp2(m_old - m_new)
            p_ref[rows, :] = jnp.exp2(s - pltpu.repeat(m_new, tk // HEAD_DIM, 1)).astype(BF16)
            m_ref[rows, :] = m_new

    kt_diag = t0 // tk

    def body(kt, carry):
        apply_values(jnp.maximum(kt - 1, 0))
        scores(kt, False)
        return carry

    lax.fori_loop(0, kt_diag, body, 0)
    apply_values(jnp.maximum(kt_diag - 1, 0))
    scores(kt_diag, True)
    apply_values(kt_diag)
    for h in range(hpg):
        rows = slice(h * tq, (h + 1) * tq)
        gate = _gate_col(sm_ref, nh_gdn, g, hpg, h, 1)
        out = acc_ref[rows, 0:HEAD_DIM] / acc_ref[rows, HEAD_DIM:2 * HEAD_DIM]
        o_ref[0, :, h * HEAD_DIM:(h + 1) * HEAD_DIM] = (out * gate).astype(o_ref.dtype)


def _sel_attn(q_r, ksel_r, big, vsel_blk, sel, small, nh_gdn):
    B, S, WQ = q_r.shape
    G = NSA_GROUPS
    hpg = WQ // HEAD_DIM // G
    nbp = sel.shape[3]
    tq = min(S, 512)
    tk = min(S, 512)
    rows = hpg * tq
    ext = jnp.where(jnp.arange(S)[:, None] // NSA_SEL_LEN == jnp.arange(nbp)[None, :], MASK_BIG, 0.0).astype(BF16)
    return pl.pallas_call(
        functools.partial(_sel_attn_kernel, nh_gdn=nh_gdn, hpg=hpg, tk=tk),
        grid=(B, G, S // tq),
        in_specs=[pl.BlockSpec((1, tq, hpg * HEAD_DIM), lambda b, g, i: (b, i, g)),
                  pl.BlockSpec((1, S, HEAD_DIM), lambda b, g, i: (b, 0, g)),
                  pl.BlockSpec((S, nbp), lambda b, g, i: (0, 0)),
                  pl.BlockSpec((1, S, HEAD_DIM), lambda b, g, i: (b, 0, vsel_blk + g)),
                  pl.BlockSpec((1, 1, tq, nbp), lambda b, g, i: (b, g, i, 0)),
                  pl.BlockSpec((1, tq, small.shape[2]), lambda b, g, i: (b, i, 0))],
        out_specs=pl.BlockSpec((1, tq, hpg * HEAD_DIM), lambda b, g, i: (b, i, g)),
        out_shape=jax.ShapeDtypeStruct((B, S, WQ), BF16),
        scratch_shapes=[pltpu.VMEM((rows, HEAD_DIM + nbp), BF16),
                        pltpu.VMEM((rows, HEAD_DIM), F32),
                        pltpu.VMEM((rows, 2 * HEAD_DIM), F32),
                        pltpu.VMEM((rows, tk), BF16),
                        pltpu.VMEM((rows, HEAD_DIM), F32)],
        compiler_params=_cparams(3),
        name="nsa_sel_attn",
    )(q_r, ksel_r, ext, big, sel, small)


def _win_attn_kernel(q_ref, k_ref, v_ref, sm_ref, o_ref, *, nh_gdn, hpg):
    tq = q_ref.shape[1]
    span = tq + NSA_WINDOW
    g = pl.program_id(1)
    t0 = pl.program_id(2) * tq
    k0 = pl.multiple_of(jnp.maximum(t0 - NSA_WINDOW, 0), tq)
    k = k_ref[0, pl.ds(k0, span), :]
    v = v_ref[0, pl.ds(k0, span), :]
    t_row = t0 + lax.broadcasted_iota(jnp.int32, (tq, span), 0)
    spos = k0 + lax.broadcasted_iota(jnp.int32, (tq, span), 1)
    mask = (spos <= t_row) & (spos > t_row - NSA_WINDOW)
    for h in range(hpg):
        cols = slice(h * HEAD_DIM, (h + 1) * HEAD_DIM)
        s = jnp.where(mask, _dot_nt(q_ref[0, :, cols], k), NEG_INF)
        e = jnp.exp2(s - jnp.max(s, axis=-1, keepdims=True))
        p = e / jnp.sum(e, axis=-1, keepdims=True)
        gate = _gate_col(sm_ref, nh_gdn, g, hpg, h, 2)
        o_ref[0, :, cols] = (_dot(p.astype(BF16), v) * gate).astype(o_ref.dtype)


def _win_attn(q_r, kwin_r, big, vwin_blk, small, nh_gdn):
    B, S, WQ = q_r.shape
    G = NSA_GROUPS
    hpg = WQ // HEAD_DIM // G
    tq = min(S, 512)
    assert S >= tq + NSA_WINDOW
    return pl.pallas_call(
        functools.partial(_win_attn_kernel, nh_gdn=nh_gdn, hpg=hpg),
        grid=(B, G, S // tq),
        in_specs=[pl.BlockSpec((1, tq, hpg * HEAD_DIM), lambda b, g, i: (b, i, g)),
                  pl.BlockSpec((1, S, HEAD_DIM), lambda b, g, i: (b, 0, g)),
                  pl.BlockSpec((1, S, HEAD_DIM), lambda b, g, i: (b, 0, vwin_blk + g)),
                  pl.BlockSpec((1, tq, small.shape[2]), lambda b, g, i: (b, i, 0))],
        out_specs=pl.BlockSpec((1, tq, hpg * HEAD_DIM), lambda b, g, i: (b, i, g)),
        out_shape=jax.ShapeDtypeStruct((B, S, WQ), BF16),
        compiler_params=_cparams(3),
        name="nsa_win_attn",
    )(q_r, kwin_r, big, small)


def _post_norm(x, y, gt, g, b, alpha):
    r = alpha * x + (1.0 + gt) * y
    return _normalize_rows(r) * g + b


def _mixer_out_kernel(oa_ref, oc_ref, os_ref, ow_ref, ma_ref, mb_ref, x_ref, gt_ref, lg_ref, lb_ref,
                      wg_ref, wn_ref, wo_ref, o_ref, *, alpha):
    y_a = _dot(oa_ref[0], wg_ref[...])
    o_b = oc_ref[0].astype(F32) + os_ref[0].astype(F32) + ow_ref[0].astype(F32)
    y_b = _dot(o_b.astype(BF16), wn_ref[...])
    mixed = _sigmoid(ma_ref[0].astype(F32)) * y_a + _sigmoid(mb_ref[0].astype(F32)) * y_b
    y = _dot(mixed.astype(BF16), wo_ref[...])
    o_ref[0] = _post_norm(x_ref[0], y, gt_ref[0], lg_ref[...], lb_ref[...], alpha)


def _mixer_out(o_a, o_c, o_s, o_w, big, ma_blk, mb_blk, x, ada_l, gt_col, ln_g, ln_b, wg, wn, wo, alpha):
    B, S, D = x.shape
    tm = min(S, 512)
    row = lambda blk: pl.BlockSpec((1, tm, D), lambda b, i: (b, i, blk))
    full = lambda shape: pl.BlockSpec(shape, lambda b, i: (0,) * len(shape))
    return pl.pallas_call(
        functools.partial(_mixer_out_kernel, alpha=alpha),
        grid=(B, S // tm),
        in_specs=[row(0), row(0), row(0), row(0), row(ma_blk), row(mb_blk), row(0),
                  pl.BlockSpec((1, 1, D), lambda b, i: (b, 0, gt_col)),
                  full((1, D)), full((1, D)), full((D, D)), full((D, D)), full((D, D))],
        out_specs=row(0),
        out_shape=jax.ShapeDtypeStruct((B, S, D), F32),
        compiler_params=_cparams(2),
        name="mixer_out",
    )(o_a, o_c, o_s, o_w, big, big, x, ada_l, ln_g.reshape(1, D), ln_b.reshape(1, D), wg, wn, wo)


def _first_argmax(vals, lane, big):
    m = jnp.max(vals, axis=-1, keepdims=True)
    first = jnp.min(jnp.where(vals == m, lane, big), axis=-1, keepdims=True)
    return m, first


def _moe_router_kernel(x_ref, sh_ref, sc_ref, rw_ref, rb_ref, wsg_ref, wsu_ref, wsd_ref,
                       h_ref, idx_ref, wt_ref, shared_ref):
    h = _normalize_rows(x_ref[0]) * (1.0 + sc_ref[0]) + sh_ref[0]
    h16 = h.astype(BF16)
    h_ref[0] = h16
    h_lo = (h - h16.astype(F32)).astype(BF16)
    rw = rw_ref[...]
    rw_hi = rw.astype(BF16)
    rw_lo = (rw - rw_hi.astype(F32)).astype(BF16)
    logits = _dot(h16, rw_hi) + _dot(h16, rw_lo) + _dot(h_lo, rw_hi)
    scores = _sigmoid(logits)
    biased = scores + rb_ref[...]
    tm, E = scores.shape
    per = E // N_EXPERT_GROUPS
    lane_i = lax.broadcasted_iota(jnp.int32, (tm, E), 1)
    lane = lane_i.astype(F32)
    grp_i = lane_i // per
    grp = grp_i.astype(F32)
    gscore = jnp.zeros((tm, E), F32)
    for gi in range(N_EXPERT_GROUPS):
        in_g = grp_i == gi
        vals = jnp.where(in_g, biased, -jnp.inf)
        m1, first = _first_argmax(vals, lane, float(E))
        m2 = jnp.max(jnp.where(lane == first, -jnp.inf, vals), axis=-1, keepdims=True)
        gscore = jnp.where(in_g, m1 + m2, gscore)
    cand = jnp.full((tm, E), NEG_INF, F32)
    for _ in range(TOPK_GROUPS):
        _, first_g = _first_argmax(gscore, grp, float(N_EXPERT_GROUPS))
        pick = grp == first_g
        cand = jnp.where(pick, biased, cand)
        gscore = jnp.where(pick, -jnp.inf, gscore)
    out_lane = lax.broadcasted_iota(jnp.int32, (tm, idx_ref.shape[2]), 1)
    idx_out = jnp.zeros((tm, idx_ref.shape[2]), F32)
    wt_out = jnp.zeros((tm, idx_ref.shape[2]), F32)
    wsum = jnp.zeros((tm, 1), F32)
    for kk in range(TOP_K):
        _, first = _first_argmax(cand, lane, float(E))
        pick = lane == first
        w = jnp.sum(jnp.where(pick, scores, 0.0), axis=-1, keepdims=True)
        cand = jnp.where(pick, -jnp.inf, cand)
        idx_out = jnp.where(out_lane == kk, first, idx_out)
        wt_out = jnp.where(out_lane == kk, w, wt_out)
        wsum = wsum + w
    idx_ref[0] = idx_out.astype(jnp.int32)
    wt_ref[0] = wt_out / wsum * ROUTED_SCALE
    act = _silu(_dot(h16, wsg_ref[...])) * _dot(h16, wsu_ref[...])
    shared_ref[0] = _dot(act.astype(BF16), wsd_ref[...])


def _moe_router(x, ada_l, sh_col, sc_col, router_w, router_b, wsg, wsu, wsd):
    B, S, D = x.shape
    E = router_w.shape[1]
    FF = wsg.shape[1]
    tm = min(S, 512)
    KP = 128
    row = lambda w: pl.BlockSpec((1, tm, w), lambda b, i: (b, i, 0))
    full = lambda shape: pl.BlockSpec(shape, lambda b, i: (0,) * len(shape))
    return pl.pallas_call(
        _moe_router_kernel,
        grid=(B, S // tm),
        in_specs=[row(D),
                  pl.BlockSpec((1, 1, D), lambda b, i: (b, 0, sh_col)),
                  pl.BlockSpec((1, 1, D), lambda b, i: (b, 0, sc_col)),
                  full((D, E)), full((1, E)), full((D, FF)), full((D, FF)), full((FF, D))],
        out_specs=[row(D), row(KP), row(KP), row(D)],
        out_shape=[jax.ShapeDtypeStruct((B, S, D), BF16),
                   jax.ShapeDtypeStruct((B, S, KP), jnp.int32),
                   jax.ShapeDtypeStruct((B, S, KP), F32),
                   jax.ShapeDtypeStruct((B, S, D), F32)],
        compiler_params=_cparams(2),
        name="moe_router",
    )(x, ada_l, ada_l, router_w, router_b.reshape(1, E), wsg, wsu, wsd)


def _moe_experts_kernel(be_ref, nu_ref, xs_ref, wg_ref, wu_ref, wd_ref, o_ref):
    @pl.when(pl.program_id(0) < nu_ref[0])
    def _():
        xb = xs_ref[...]
        act = _silu(_dot(xb, wg_ref[0, 0].astype(BF16))) * _dot(xb, wu_ref[0, 0].astype(BF16))
        o_ref[...] = _dot(act.astype(BF16), wd_ref[0, 0].astype(BF16)).astype(o_ref.dtype)


def _moe_experts(block_expert, n_used, xs, layer, wg, wu, wd):
    rows, D = xs.shape
    BM = MOE_ROW_BLOCK
    FF = wg.shape[3]
    grid_spec = pltpu.PrefetchScalarGridSpec(
        num_scalar_prefetch=2,
        grid=(rows // BM,),
        in_specs=[pl.BlockSpec((BM, D), lambda i, be, nu: (i, 0)),
                  pl.BlockSpec((1, 1, D, FF), lambda i, be, nu: (layer, be[i], 0, 0)),
                  pl.BlockSpec((1, 1, D, FF), lambda i, be, nu: (layer, be[i], 0, 0)),
                  pl.BlockSpec((1, 1, FF, D), lambda i, be, nu: (layer, be[i], 0, 0))],
        out_specs=pl.BlockSpec((BM, D), lambda i, be, nu: (i, 0)),
    )
    return pl.pallas_call(
        _moe_experts_kernel,
        grid_spec=grid_spec,
        out_shape=jax.ShapeDtypeStruct((rows, D), BF16),
        compiler_params=_cparams(1),
        name="moe_experts",
    )(block_expert, n_used, xs, wg, wu, wd)


def _moe_out_kernel(x_ref, shared_ref, yg_ref, wt_ref, gt_ref, lg_ref, lb_ref, o_ref, *, alpha):
    D = x_ref.shape[2]
    y = shared_ref[0]
    wt = wt_ref[0]
    for kk in range(TOP_K):
        y = y + wt[:, kk:kk + 1] * yg_ref[kk, 0].astype(F32)
    o_ref[0] = _post_norm(x_ref[0], y, gt_ref[0], lg_ref[...], lb_ref[...], alpha)


def _moe_out(x, shared, yg, wt, ada_l, gt_col, ln_g, ln_b, alpha):
    B, S, D = x.shape
    tm = min(S, 256)
    row = lambda w: pl.BlockSpec((1, tm, w), lambda b, i: (b, i, 0))
    full = lambda shape: pl.BlockSpec(shape, lambda b, i: (0,) * len(shape))
    return pl.pallas_call(
        functools.partial(_moe_out_kernel, alpha=alpha),
        grid=(B, S // tm),
        in_specs=[row(D), row(D),
                  pl.BlockSpec((TOP_K, 1, tm, D), lambda b, i: (0, b, i, 0)),
                  row(wt.shape[2]),
                  pl.BlockSpec((1, 1, D), lambda b, i: (b, 0, gt_col)),
                  full((1, D)), full((1, D))],
        out_specs=row(D),
        out_shape=jax.ShapeDtypeStruct((B, S, D), F32),
        compiler_params=_cparams(2),
        name="moe_out",
    )(x, shared, yg, wt, ada_l, ln_g.reshape(1, D), ln_b.reshape(1, D))


def _routing_layout(expert_idx, n_experts):
    T, K = expert_idx.shape
    BM = MOE_ROW_BLOCK
    TK = T * K
    e_flat = expert_idx.reshape(TK)
    iota = jnp.arange(TK, dtype=jnp.int32)
    _, order = lax.sort((e_flat, iota), num_keys=1)
    _, inv = lax.sort((order, iota), num_keys=1)
    counts = jnp.bincount(e_flat, length=n_experts).astype(jnp.int32)
    padded = (counts + BM - 1) // BM * BM
    pad_end = jnp.cumsum(padded)
    pad_start = pad_end - padded
    delta = pad_start - (jnp.cumsum(counts) - counts)
    dest = inv + delta[e_flat]
    n_blocks = (TK + n_experts * (BM - 1) + BM - 1) // BM
    block_start = jnp.arange(n_blocks, dtype=jnp.int32) * BM
    block_expert = jnp.minimum(jnp.sum(pad_end[None, :] <= block_start[:, None], axis=1),
                               n_experts - 1).astype(jnp.int32)
    row = block_start[:, None] + jnp.arange(BM, dtype=jnp.int32)[None, :]
    in_expert = (row - pad_start[block_expert][:, None]) < counts[block_expert][:, None]
    src = jnp.clip(row - delta[block_expert][:, None], 0, TK - 1)
    row_tok = jnp.where(in_expert, order[src.reshape(-1)].reshape(n_blocks, BM) // K, 0).reshape(-1)
    n_used = (pad_end[-1] // BM).astype(jnp.int32).reshape(1)
    return row_tok.astype(jnp.int32), dest.reshape(T, K).astype(jnp.int32), block_expert, n_used


def kernel(x, c, positions, ada_w, ada_b, w_in, gdn_conv_w, gdn_a_log, gdn_dt_bias, gdn_norm_w, w_gdn_branch, nsa_cmp_pos_k, nsa_cmp_pos_v, nsa_cmp_k_w1, nsa_cmp_k_w2, nsa_cmp_v_w1, nsa_cmp_v_w2, w_nsa_branch, w_out, ln1_g, ln1_b, router_w, router_bias, w_sh_gate, w_sh_up, w_sh_down, w_e_gate, w_e_up, w_e_down, ln2_g, ln2_b):
    B, S, D = x.shape
    L = ada_w.shape[0]
    T = B * S
    G = NSA_GROUPS
    nh_gdn = gdn_a_log.shape[1]
    gdn_w = nh_gdn * HEAD_DIM
    nsa_w = w_nsa_branch.shape[1]
    kvw = G * HEAD_DIM
    E = router_w.shape[2]
    alpha = (2.0 * L) ** 0.25
    assert gdn_w == D and nsa_w == D and S % NSA_SEL_LEN == 0

    splits = (3 * gdn_w, gdn_w, nh_gdn, nh_gdn, nsa_w, 6 * kvw, 3 * (nsa_w // HEAD_DIM), D, D)
    offs = [0]
    for s_ in splits:
        offs.append(offs[-1] + s_)
    seg = lambda i: slice(offs[i], offs[i + 1])
    w_big = jnp.concatenate([w_in[:, :, seg(0)], w_in[:, :, seg(1)], w_in[:, :, seg(4)],
                             w_in[:, :, seg(7)], w_in[:, :, seg(8)], w_in[:, :, seg(5)]], axis=-1).astype(BF16)
    n_small = splits[2] + splits[3] + splits[6]
    w_small = jnp.concatenate([w_in[:, :, seg(2)], w_in[:, :, seg(3)], w_in[:, :, seg(6)],
                               jnp.zeros((L, D, 128 - n_small), w_in.dtype)], axis=-1).astype(BF16)
    Z_BLK, Q_BLK, MA_BLK, MB_BLK = 3, 4, 5, 6
    kv0 = 7 * D
    n_big = w_big.shape[2]
    tn_big = n_big // 4 if (n_big // 4) % 128 == 0 else 128

    wgb = w_gdn_branch.astype(BF16)
    wnb = w_nsa_branch.astype(BF16)
    wob = w_out.astype(BF16)
    wsg = w_sh_gate.astype(BF16)
    wsu = w_sh_up.astype(BF16)
    wsd = w_sh_down.astype(BF16)
    cmp_w1 = jnp.stack([nsa_cmp_k_w1, nsa_cmp_v_w1], axis=1).astype(BF16)
    cmp_w2 = jnp.stack([nsa_cmp_k_w2, nsa_cmp_v_w2], axis=1).astype(BF16)
    cmp_pos = jnp.stack([nsa_cmp_pos_k, nsa_cmp_pos_v], axis=1).reshape(L, 2, 1, NSA_CMP_LEN * HEAD_DIM)

    ada = _ada(c, ada_w, ada_b)
    cosf, sinf = _rope_tables(positions)
    NC = S // NSA_CMP_STRIDE
    last = jnp.minimum(jnp.arange(NC) * NSA_CMP_STRIDE + NSA_CMP_LEN - 1, S - 1)
    cosc = cosf[:, last]
    sinc = sinf[:, last]
    n_sel = -(-(S // NSA_SEL_LEN) // 128) * 128

    for l in range(L):
        ada_l = ada[l].reshape(B, 1, 6 * D)
        big, small = _proj(x, ada_l, 0, 1, w_big[l], w_small[l], tn_big)
        small_t = jnp.swapaxes(small[:, :, nh_gdn:2 * nh_gdn].reshape(B, S // GDN_CHUNK, GDN_CHUNK, nh_gdn), 2, 3)
        q_a, k_a, v_a = _gdn_prep(big, gdn_conv_w[l], gdn_w)
        o_a = _gdn(q_a, k_a, v_a, big, Z_BLK, small, small_t, gdn_a_log[l], gdn_dt_bias[l], gdn_norm_w[l])
        q_r, ksel_r, kwin_r = _rope_apply(big, Q_BLK, (kv0 + 2 * kvw) // kvw, (kv0 + 4 * kvw) // kvw,
                                          nsa_w, kvw, cosf, sinf)
        hb = big[:, :, kv0:kv0 + 2 * kvw].reshape(B, NC, NSA_CMP_STRIDE, 2 * G, HEAD_DIM)
        hb = hb.transpose(0, 3, 1, 2, 4).reshape(B, 2 * G, NC, NSA_CMP_STRIDE * HEAD_DIM)
        kvc = _compress(hb, cmp_pos[l], cmp_w1[l], cmp_w2[l], cosc, sinc)
        o_c, sel = _cmp_attn(q_r, kvc, small, nh_gdn, n_sel)
        o_s = _sel_attn(q_r, ksel_r, big, (kv0 + 3 * kvw) // HEAD_DIM, sel, small, nh_gdn)
        o_w = _win_attn(q_r, kwin_r, big, (kv0 + 5 * kvw) // HEAD_DIM, small, nh_gdn)
        x = _mixer_out(o_a, o_c, o_s, o_w, big, MA_BLK, MB_BLK, x, ada_l, 2, ln1_g[l], ln1_b[l],
                       wgb[l], wnb[l], wob[l], alpha)
        h2, eidx, ewt, shared = _moe_router(x, ada_l, 3, 4, router_w[l], router_bias[l], wsg[l], wsu[l], wsd[l])
        row_tok, dest, block_expert, n_used = _routing_layout(eidx.reshape(T, -1)[:, :TOP_K], E)
        xs = h2.reshape(T, D).at[row_tok].get(mode='promise_in_bounds')
        yb = _moe_experts(block_expert, n_used, xs, l, w_e_gate, w_e_up, w_e_down)
        yg = yb.at[dest.T.reshape(TOP_K * T)].get(mode='promise_in_bounds').reshape(TOP_K, B, S, D)
        x = _moe_out(x, shared, yg, ewt, ada_l, 5, ln2_g[l], ln2_b[l], alpha)
    return x
```

```python
import functools
import math

import jax
import jax.numpy as jnp
from jax import lax
from jax.experimental import pallas as pl
from jax.experimental.pallas import tpu as pltpu

F32 = jnp.float32
BF16 = jnp.bfloat16

HEAD_DIM = 128
GDN_CONV = 4
GDN_CHUNK = 64
NSA_GROUPS = 2
NSA_CMP_LEN = 32
NSA_CMP_STRIDE = 16
NSA_SEL_LEN = 64
NSA_SEL_TOP = 16
NSA_WINDOW = 512
ROPE_THETA = 10000.0
N_EXPERTS = 64
N_EXPERT_GROUPS = 8
TOPK_GROUPS = 4
TOP_K = 8
ROUTED_SCALE = 2.5
LN_EPS = 1e-5
NEG_INF = -1e30
SEL_FORCE = 1e6
MOE_ROW_BLOCK = 512
Q_SCALE_LOG2E = (HEAD_DIM ** -0.5) * math.log2(math.e)
MASK_BIG = 2.0 ** 100
N_ROW_PARTS = 2
GDN_PHASE1_CHUNKS = 4
WIN_SUB_TILE = 128

_ARB = "arbitrary"


def _cparams(n_axes):
    return pltpu.CompilerParams(dimension_semantics=(_ARB,) * n_axes)


def _sigmoid(x):
    return 1.0 / (1.0 + jnp.exp(-x))


def _silu(x):
    return x * _sigmoid(x)


def _dot(a, b):
    return jnp.dot(a, b, preferred_element_type=F32)


def _dot_nt(a, b):
    return lax.dot_general(a, b, (((1,), (1,)), ((), ())), preferred_element_type=F32)


def _dot_tn(a, b):
    return lax.dot_general(a, b, (((0,), (0,)), ((), ())), preferred_element_type=F32)


def _tile_lanes(x, n):
    return jnp.concatenate([x] * n, axis=1)


def _split3(x):
    x0 = x.astype(BF16)
    r1 = x - x0.astype(F32)
    x1 = r1.astype(BF16)
    x2 = (r1 - x1.astype(F32)).astype(BF16)
    return x0, x1, x2


def _dot_sel_right(x, sel_bf16):
    x0, x1, x2 = _split3(x)
    return _dot(x0, sel_bf16) + _dot(x1, sel_bf16) + _dot(x2, sel_bf16)


def _dot_sel_left(sel_bf16, x):
    x0, x1, x2 = _split3(x)
    return _dot(sel_bf16, x0) + _dot(sel_bf16, x1) + _dot(sel_bf16, x2)


def _normalize_rows(x):
    mu = jnp.mean(x, axis=-1, keepdims=True)
    xc = x - mu
    var = jnp.mean(xc * xc, axis=-1, keepdims=True)
    return xc * lax.rsqrt(var + LN_EPS)


def _ada_kernel(c_ref, w_ref, b_ref, o_ref):
    cond = _silu(c_ref[...])
    o_ref[0] = jnp.dot(cond, w_ref[0], preferred_element_type=F32,
                       precision=lax.Precision.HIGHEST) + b_ref[0]


def _ada(c, ada_w, ada_b):
    L, D, N = ada_w.shape
    B = c.shape[0]
    tn = min(N, 1536)
    return pl.pallas_call(
        _ada_kernel,
        grid=(L, N // tn),
        in_specs=[pl.BlockSpec((B, D), lambda l, j: (0, 0)),
                  pl.BlockSpec((1, D, tn), lambda l, j: (l, 0, j)),
                  pl.BlockSpec((1, 1, tn), lambda l, j: (l, 0, j))],
        out_specs=pl.BlockSpec((1, B, tn), lambda l, j: (l, 0, j)),
        out_shape=jax.ShapeDtypeStruct((L, B, N), F32),
        compiler_params=_cparams(2),
        name="ada",
    )(c, ada_w, ada_b.reshape(L, 1, N))


def _proj_kernel(x_ref, sh_ref, sc_ref, w_ref, ws_ref, o_ref, os_ref, h_ref):
    @pl.when(pl.program_id(2) == 0)
    def _():
        h = _normalize_rows(x_ref[0]) * (1.0 + sc_ref[0]) + sh_ref[0]
        h_ref[...] = h.astype(BF16)
        os_ref[0] = _dot(h_ref[...], ws_ref[...])

    o_ref[0] = _dot(h_ref[...], w_ref[...]).astype(o_ref.dtype)


def _proj(x, ada_l, sh_col, sc_col, w, w_small, tn):
    B, S, D = x.shape
    N = w.shape[1]
    NS = w_small.shape[1]
    tm = min(S, 1024)
    return pl.pallas_call(
        _proj_kernel,
        grid=(B, S // tm, N // tn),
        in_specs=[pl.BlockSpec((1, tm, D), lambda b, i, j: (b, i, 0)),
                  pl.BlockSpec((1, 1, D), lambda b, i, j: (b, 0, sh_col)),
                  pl.BlockSpec((1, 1, D), lambda b, i, j: (b, 0, sc_col)),
                  pl.BlockSpec((D, tn), lambda b, i, j: (0, j)),
                  pl.BlockSpec((D, NS), lambda b, i, j: (0, 0))],
        out_specs=[pl.BlockSpec((1, tm, tn), lambda b, i, j: (b, i, j)),
                   pl.BlockSpec((1, tm, NS), lambda b, i, j: (b, i, 0))],
        out_shape=[jax.ShapeDtypeStruct((B, S, N), BF16),
                   jax.ShapeDtypeStruct((B, S, NS), F32)],
        scratch_shapes=[pltpu.VMEM((tm, D), BF16)],
        compiler_params=_cparams(3),
        name="proj",
    )(x, ada_l, ada_l, w, w_small)


def _gdn_prep_kernel(x_ref, w_ref, q_ref, k_ref, v_ref, carry_ref):
    ts = x_ref.shape[1]
    width = q_ref.shape[2]
    nh = width // HEAD_DIM

    @pl.when(pl.program_id(1) == 0)
    def _():
        carry_ref[...] = jnp.zeros_like(carry_ref)

    for part, o_ref in enumerate((q_ref, k_ref, v_ref)):
        for h in range(nh):
            c0 = part * width + h * HEAD_DIM
            cols = slice(c0, c0 + HEAD_DIM)
            xx = jnp.concatenate([carry_ref[:, cols], x_ref[0, :, cols].astype(F32)], axis=0)
            w = w_ref[:, cols]
            y = xx[8:8 + ts] * w[3:4]
            for kk in range(GDN_CONV - 1):
                off = 8 - (GDN_CONV - 1) + kk
                y = y + xx[off:off + ts] * w[kk:kk + 1]
            y = _silu(y)
            if part < 2:
                y = y * lax.rsqrt(jnp.sum(y * y, axis=-1, keepdims=True) + 1e-6)
            if part == 0:
                y = y * (HEAD_DIM ** -0.5)
            o_ref[0, :, h * HEAD_DIM:(h + 1) * HEAD_DIM] = y.astype(o_ref.dtype)
    carry_ref[...] = x_ref[0, ts - 8:ts, :].astype(F32)


def _gdn_prep(big, conv_w, width):
    B, S, _ = big.shape
    ts = min(S, 512)
    out = jax.ShapeDtypeStruct((B, S, width), BF16)
    ospec = pl.BlockSpec((1, ts, width), lambda b, i: (b, i, 0))
    return pl.pallas_call(
        _gdn_prep_kernel,
        grid=(B, S // ts),
        in_specs=[pl.BlockSpec((1, ts, 3 * width), lambda b, i: (b, i, 0)),
                  pl.BlockSpec((GDN_CONV, 3 * width), lambda b, i: (0, 0))],
        out_specs=[ospec, ospec, ospec],
        out_shape=[out, out, out],
        scratch_shapes=[pltpu.VMEM((8, 3 * width), F32)],
        compiler_params=_cparams(2),
        name="gdn_prep",
    )(big, conv_w)


def _softplus(x):
    return jnp.maximum(x, 0.0) + jnp.log(1.0 + jnp.exp(-jnp.abs(x)))


def _gdn_kernel(q_ref, k_ref, v_ref, z_ref, sm_ref, smt_ref, alog_ref, alogt_ref, dtb_ref, dtbt_ref,
                nw_ref, o_ref, state_ref, u_s, wq_s, attn_s, kd_s, dec_s):
    ts = q_ref.shape[1]
    nh = q_ref.shape[2] // HEAD_DIM
    C = GDN_CHUNK
    R = 2 * C
    npair = nh // 2
    nchunks = ts // C

    @pl.when(pl.program_id(1) == 0)
    def _():
        state_ref[...] = jnp.zeros_like(state_ref)

    ci = lax.broadcasted_iota(jnp.int32, (C, C), 0)
    cj = lax.broadcasted_iota(jnp.int32, (C, C), 1)
    tril = jnp.where(ci >= cj, 1.0, 0.0).astype(BF16)
    triu = jnp.where(cj >= ci, 1.0, 0.0).astype(BF16)
    ii = lax.broadcasted_iota(jnp.int32, (R, R), 0)
    jj = lax.broadcasted_iota(jnp.int32, (R, R), 1)
    same_head = (ii // C) == (jj // C)
    incl = same_head & (ii >= jj)
    strict = same_head & (ii > jj)
    eye = jnp.where(ii == jj, 1.0, 0.0)
    nw = nw_ref[...]

    def pair_rows(ref, rows, p):
        return jnp.concatenate([ref[0, rows, (2 * p) * HEAD_DIM:(2 * p + 1) * HEAD_DIM],
                                ref[0, rows, (2 * p + 1) * HEAD_DIM:(2 * p + 2) * HEAD_DIM]], axis=0)

    def pair_col(x, p):
        return jnp.concatenate([x[:, 2 * p:2 * p + 1], x[:, 2 * p + 1:2 * p + 2]], axis=0)

    def phase1(j, carry):
        cs = [j * GDN_PHASE1_CHUNKS + d for d in range(GDN_PHASE1_CHUNKS)]
        rows_c, beta_c, gc_c, gcr_c, glb_c = [], [], [], [], []
        for c in cs:
            rows = pl.ds(pl.multiple_of(c * C, C), C)
            sm = sm_ref[0, rows, :]
            g = -jnp.exp(alog_ref[...]) * _softplus(sm[:, nh:2 * nh] + dtb_ref[...])
            gt = -jnp.exp(alogt_ref[...]) * _softplus(smt_ref[0, c] + dtbt_ref[...])
            gc = _dot_sel_left(tril, g)
            g_last = gc[C - 1:C, :]
            dec_s[c] = jnp.exp(g_last)
            rows_c.append(rows)
            beta_c.append(_sigmoid(sm[:, 0:nh]))
            gc_c.append(gc)
            gcr_c.append(_dot_sel_right(gt, triu))
            glb_c.append(jnp.broadcast_to(g_last, (C, nh)))
        items = [(ci, p) for ci in range(len(cs)) for p in range(npair)]
        idx = range(len(items))
        q2 = [pair_rows(q_ref, rows_c[ci], p).astype(F32) for ci, p in items]
        k2 = [pair_rows(k_ref, rows_c[ci], p).astype(F32) for ci, p in items]
        b2 = [pair_col(beta_c[ci], p) for ci, p in items]
        g_col = [pair_col(gc_c[ci], p) for ci, p in items]
        kb = [k2[i] * b2[i] for i in idx]
        kq = [_dot_nt(jnp.concatenate([kb[i], q2[i]], axis=0).astype(BF16), k2[i].astype(BF16))
              for i in idx]
        decay = []
        for i, (ci, p) in enumerate(items):
            gcr = gcr_c[ci]
            g_row = jnp.concatenate([gcr[2 * p:2 * p + 1, :], gcr[2 * p + 1:2 * p + 2, :]], axis=1)
            decay.append(jnp.where(incl, jnp.exp(jnp.where(incl, g_col[i] - g_row, 0.0)), 0.0))
        a = [jnp.where(strict, kq[i][:R] * decay[i], 0.0) for i in idx]
        for i, (ci, p) in enumerate(items):
            attn_s[cs[ci], p] = (kq[i][R:] * decay[i]).astype(BF16)
        x = [eye - a[i] for i in idx]
        pw = a
        n = 2
        while n < C:
            pw16 = [pw[i].astype(BF16) for i in idx]
            pw = [_dot(pw16[i], pw16[i]) for i in idx]
            x = [x[i] + _dot(x[i].astype(BF16), pw[i].astype(BF16)) for i in idx]
            n *= 2
        eg = [jnp.exp(g_col[i]) for i in idx]
        sol = []
        for i, (ci, p) in enumerate(items):
            v2 = pair_rows(v_ref, rows_c[ci], p).astype(F32)
            r = jnp.concatenate([v2 * b2[i], kb[i] * eg[i]], axis=1)
            sol.append(_dot(x[i].astype(BF16), r.astype(BF16)))
        for i, (ci, p) in enumerate(items):
            c = cs[ci]
            u_s[c, p] = sol[i][:, :HEAD_DIM]
            w = sol[i][:, HEAD_DIM:]
            qg = q2[i] * eg[i]
            for e in range(2):
                wq_s[c, 2 * p + e] = jnp.concatenate([w[e * C:(e + 1) * C], qg[e * C:(e + 1) * C]],
                                                     axis=0).astype(BF16)
            kd_s[c, p] = (k2[i] * jnp.exp(pair_col(glb_c[ci], p) - g_col[i])).astype(BF16)
        return carry

    assert nchunks % GDN_PHASE1_CHUNKS == 0
    lax.fori_loop(0, nchunks // GDN_PHASE1_CHUNKS, phase1, 0)

    def phase2(c, carry):
        rows = pl.ds(pl.multiple_of(c * C, C), C)
        dec = dec_s[c]
        res = [_dot(wq_s[c, h], state_ref[h].astype(BF16)) for h in range(nh)]
        for p in range(npair):
            ws = jnp.concatenate([res[2 * p][:C], res[2 * p + 1][:C]], axis=0)
            qs = jnp.concatenate([res[2 * p][C:], res[2 * p + 1][C:]], axis=0)
            v_new = u_s[c, p] - ws
            v16 = v_new.astype(BF16)
            o2 = qs + _dot(attn_s[c, p], v16)
            kd = kd_s[c, p]
            for e in range(2):
                h = 2 * p + e
                part = slice(e * C, (e + 1) * C)
                state_ref[h] = state_ref[h] * dec[:, h:h + 1] + _dot_tn(kd[part], v16[part])
                o = o2[part]
                o = o * lax.rsqrt(jnp.mean(o * o, axis=-1, keepdims=True) + 1e-6) * nw
                cols = slice(h * HEAD_DIM, (h + 1) * HEAD_DIM)
                z = z_ref[0, rows, cols].astype(F32)
                o_ref[0, rows, cols] = (o * _silu(z)).astype(o_ref.dtype)
        return carry

    lax.fori_loop(0, nchunks, phase2, 0)


def _gdn(q, k, v, big, z_blk, small, small_t, a_log, dt_bias, norm_w):
    B, S, W = q.shape
    nh = W // HEAD_DIM
    assert nh % 2 == 0 and 2 * GDN_CHUNK == HEAD_DIM
    ts = min(S, 512)
    nc = ts // GDN_CHUNK
    spec = pl.BlockSpec((1, ts, W), lambda b, i: (b, i, 0))
    full = lambda shape: pl.BlockSpec(shape, lambda b, i: (0,) * len(shape))
    return pl.pallas_call(
        _gdn_kernel,
        grid=(B, S // ts),
        in_specs=[spec, spec, spec,
                  pl.BlockSpec((1, ts, W), lambda b, i: (b, i, z_blk)),
                  pl.BlockSpec((1, ts, small.shape[2]), lambda b, i: (b, i, 0)),
                  pl.BlockSpec((1, ts // GDN_CHUNK, nh, GDN_CHUNK), lambda b, i: (b, i, 0, 0)),
                  full((1, nh)), full((nh, 1)), full((1, nh)), full((nh, 1)),
                  full((1, HEAD_DIM))],
        out_specs=spec,
        out_shape=jax.ShapeDtypeStruct((B, S, W), BF16),
        scratch_shapes=[pltpu.VMEM((nh, HEAD_DIM, HEAD_DIM), F32),
                        pltpu.VMEM((nc, nh // 2, 2 * GDN_CHUNK, HEAD_DIM), F32),
                        pltpu.VMEM((nc, nh, 2 * GDN_CHUNK, HEAD_DIM), BF16),
                        pltpu.VMEM((nc, nh // 2, 2 * GDN_CHUNK, 2 * GDN_CHUNK), BF16),
                        pltpu.VMEM((nc, nh // 2, 2 * GDN_CHUNK, HEAD_DIM), BF16),
                        pltpu.VMEM((nc, 1, nh), F32)],
        compiler_params=_cparams(2),
        name="gdn",
    )(q, k, v, big, small, small_t, a_log.reshape(1, nh), a_log.reshape(nh, 1),
      dt_bias.reshape(1, nh), dt_bias.reshape(nh, 1), norm_w.reshape(1, HEAD_DIM))


def _rope_table_kernel(pos_ref, invf_ref, sign_ref, cos_ref, sin_ref):
    ang = pos_ref[0].astype(F32) * invf_ref[...]
    cos_ref[0] = jnp.cos(ang)
    sin_ref[0] = jnp.sin(ang) * sign_ref[...]


def _rope_tables(positions):
    B, S = positions.shape
    half = HEAD_DIM // 2
    inv = ROPE_THETA ** (-jnp.arange(half, dtype=F32) / half)
    invf = jnp.concatenate([inv, inv]).reshape(1, HEAD_DIM)
    sign = jnp.concatenate([-jnp.ones((half,), F32), jnp.ones((half,), F32)]).reshape(1, HEAD_DIM)
    ts = min(S, 1024)
    out = jax.ShapeDtypeStruct((B, S, HEAD_DIM), F32)
    ospec = pl.BlockSpec((1, ts, HEAD_DIM), lambda b, i: (b, i, 0))
    return pl.pallas_call(
        _rope_table_kernel,
        grid=(B, S // ts),
        in_specs=[pl.BlockSpec((1, ts, 1), lambda b, i: (b, i, 0)),
                  pl.BlockSpec((1, HEAD_DIM), lambda b, i: (0, 0)),
                  pl.BlockSpec((1, HEAD_DIM), lambda b, i: (0, 0))],
        out_specs=[ospec, ospec],
        out_shape=[out, out],
        compiler_params=_cparams(2),
        name="rope_table",
    )(positions.reshape(B, S, 1), invf, sign)


def _rope_rows(x, cosf, sinf):
    return x * cosf + pltpu.roll(x, HEAD_DIM // 2, 1) * sinf


def _rope_apply_kernel(q_ref, ks_ref, kw_ref, cos_ref, sin_ref, qo_ref, kso_ref, kwo_ref):
    cosf = cos_ref[0]
    sinf = sin_ref[0]
    for src, dst, mult in ((q_ref, qo_ref, Q_SCALE_LOG2E), (ks_ref, kso_ref, None), (kw_ref, kwo_ref, None)):
        for h in range(src.shape[2] // HEAD_DIM):
            cols = slice(h * HEAD_DIM, (h + 1) * HEAD_DIM)
            r = _rope_rows(src[0, :, cols].astype(F32), cosf, sinf)
            if mult is not None:
                r = r * mult
            dst[0, :, cols] = r.astype(dst.dtype)


def _rope_apply(big, q_blk, ksel_blk, kwin_blk, wq, wkv, cosf, sinf):
    B, S, _ = big.shape
    ts = min(S, 512)
    tab = pl.BlockSpec((1, ts, HEAD_DIM), lambda b, i: (b, i, 0))
    return pl.pallas_call(
        _rope_apply_kernel,
        grid=(B, S // ts),
        in_specs=[pl.BlockSpec((1, ts, wq), lambda b, i: (b, i, q_blk)),
                  pl.BlockSpec((1, ts, wkv), lambda b, i: (b, i, ksel_blk)),
                  pl.BlockSpec((1, ts, wkv), lambda b, i: (b, i, kwin_blk)),
                  tab, tab],
        out_specs=[pl.BlockSpec((1, ts, wq), lambda b, i: (b, i, 0)),
                   pl.BlockSpec((1, ts, wkv), lambda b, i: (b, i, 0)),
                   pl.BlockSpec((1, ts, wkv), lambda b, i: (b, i, 0))],
        out_shape=[jax.ShapeDtypeStruct((B, S, wq), BF16),
                   jax.ShapeDtypeStruct((B, S, wkv), BF16),
                   jax.ShapeDtypeStruct((B, S, wkv), BF16)],
        compiler_params=_cparams(2),
        name="rope_apply",
    )(big, big, big, cosf, sinf)


def _compress_kernel(hb_ref, pos_ref, w1_ref, w2_ref, cos_ref, sin_ref, o_ref):
    hb = hb_ref[0, 0]
    w1 = w1_ref[0]
    half = hb.shape[1]
    p0 = _dot(hb, w1[:half])
    p1 = _dot(hb, w1[half:])
    nc = p0.shape[0]
    pos8 = jnp.broadcast_to(pos_ref[0], (8, 2 * half)).astype(BF16)
    pb = _dot(pos8, w1)[0:1]
    pre = p0 + pltpu.roll(p1, nc - 1, 0) + pb
    out = _dot(_silu(pre).astype(BF16), w2_ref[0])
    roped = _rope_rows(out, cos_ref[0], sin_ref[0])
    is_key = pl.program_id(1) < NSA_GROUPS
    o_ref[0, 0] = jnp.where(is_key, roped, out).astype(o_ref.dtype)


def _compress(hb, pos_flat, w1, w2, cosc, sinc):
    B, J, NC, HW = hb.shape
    G = NSA_GROUPS
    return pl.pallas_call(
        _compress_kernel,
        grid=(B, J),
        in_specs=[pl.BlockSpec((1, 1, NC, HW), lambda b, j: (b, j, 0, 0)),
                  pl.BlockSpec((1, 1, 2 * HW), lambda b, j: (j // G, 0, 0)),
                  pl.BlockSpec((1, 2 * HW, HEAD_DIM), lambda b, j: (j // G, 0, 0)),
                  pl.BlockSpec((1, HEAD_DIM, HEAD_DIM), lambda b, j: (j // G, 0, 0)),
                  pl.BlockSpec((1, NC, HEAD_DIM), lambda b, j: (b, 0, 0)),
                  pl.BlockSpec((1, NC, HEAD_DIM), lambda b, j: (b, 0, 0))],
        out_specs=pl.BlockSpec((1, 1, NC, HEAD_DIM), lambda b, j: (b, j, 0, 0)),
        out_shape=jax.ShapeDtypeStruct((B, J, NC, HEAD_DIM), BF16),
        compiler_params=_cparams(2),
        name="nsa_compress",
    )(hb, pos_flat, w1, w2, cosc, sinc)


def _gate_col(g_ref, nh_gdn, group, hpg, h, branch):
    col = None
    for gi in range(NSA_GROUPS):
        c = 2 * nh_gdn + (gi * hpg + h) * 3 + branch
        cand = g_ref[0, :, c:c + 1]
        col = cand if col is None else jnp.where(group == gi, cand, col)
    return _sigmoid(col)


def _cmp_attn_kernel(q_ref, kc_ref, vc_ref, sm_ref, o_ref, sel_ref, *, nh_gdn, hpg):
    tq = q_ref.shape[1]
    nc = kc_ref.shape[2]
    nb = sel_ref.shape[3]
    g = pl.program_id(1)
    t0 = pl.program_id(2) * tq
    t_nc = t0 + lax.broadcasted_iota(jnp.int32, (tq, nc), 0)
    n_nc = lax.broadcasted_iota(jnp.int32, (tq, nc), 1)
    valid = (n_nc * NSA_CMP_STRIDE + (NSA_CMP_LEN - 1)) <= t_nc
    t_col = t0 + lax.broadcasted_iota(jnp.int32, (tq, 1), 0)
    has_valid = jnp.where(t_col >= NSA_CMP_LEN - 1, 1.0, 0.0)
    kc = kc_ref[0, 0]
    vc = vc_ref[0, 0]
    psum = jnp.zeros((tq, nc), F32)
    for h in range(hpg):
        cols = slice(h * HEAD_DIM, (h + 1) * HEAD_DIM)
        s = jnp.where(valid, _dot_nt(q_ref[0, :, cols], kc), NEG_INF)
        e = jnp.exp2(s - jnp.max(s, axis=-1, keepdims=True))
        p = e * (has_valid / jnp.sum(e, axis=-1, keepdims=True))
        gate = _gate_col(sm_ref, nh_gdn, g, hpg, h, 0)
        o_ref[0, :, cols] = (_dot(p.astype(BF16), vc) * gate).astype(o_ref.dtype)
        psum = psum + p
    sj = lax.broadcasted_iota(jnp.int32, (nb, nc), 0) * NSA_SEL_LEN
    cn = lax.broadcasted_iota(jnp.int32, (nb, nc), 1) * NSA_CMP_STRIDE
    overlap_t = jnp.where((cn <= sj + (NSA_SEL_LEN - 1)) & (cn + (NSA_CMP_LEN - 1) >= sj), 1.0, 0.0).astype(BF16)
    p0, p1, p2 = _split3(psum)
    imp = _dot_nt(overlap_t, p0) + _dot_nt(overlap_t, p1) + _dot_nt(overlap_t, p2)
    t_nb = t0 + lax.broadcasted_iota(jnp.int32, (nb, tq), 1)
    blk = lax.broadcasted_iota(jnp.int32, (nb, tq), 0)
    cur = t_nb // NSA_SEL_LEN
    forced = (blk == 0) | (blk == cur) | (blk == cur - 1)
    score = jnp.where(forced, SEL_FORCE, jnp.where(blk * NSA_SEL_LEN <= t_nb, imp, -1.0))
    sel = jnp.zeros((nb, tq), F32)
    blk_f = blk.astype(F32)
    for _ in range(min(NSA_SEL_TOP, nb)):
        m = jnp.max(score, axis=0, keepdims=True)
        first = jnp.min(jnp.where(score == m, blk_f, float(nb)), axis=0, keepdims=True)
        pick = blk_f == first
        sel = jnp.where(pick, 1.0, sel)
        score = jnp.where(pick, -jnp.inf, score)
    sel_ref[0, 0] = sel.T.astype(sel_ref.dtype)


def _cmp_attn(q_r, kvc, small, nh_gdn, n_sel):
    B, S, WQ = q_r.shape
    G = NSA_GROUPS
    hpg = WQ // HEAD_DIM // G
    NC = kvc.shape[2]
    tq = min(S, 512)
    return pl.pallas_call(
        functools.partial(_cmp_attn_kernel, nh_gdn=nh_gdn, hpg=hpg),
        grid=(B, G, S // tq),
        in_specs=[pl.BlockSpec((1, tq, hpg * HEAD_DIM), lambda b, g, i: (b, i, g)),
                  pl.BlockSpec((1, 1, NC, HEAD_DIM), lambda b, g, i: (b, g, 0, 0)),
                  pl.BlockSpec((1, 1, NC, HEAD_DIM), lambda b, g, i: (b, G + g, 0, 0)),
                  pl.BlockSpec((1, tq, small.shape[2]), lambda b, g, i: (b, i, 0))],
        out_specs=[pl.BlockSpec((1, tq, hpg * HEAD_DIM), lambda b, g, i: (b, i, g)),
                   pl.BlockSpec((1, 1, tq, n_sel), lambda b, g, i: (b, g, i, 0))],
        out_shape=[jax.ShapeDtypeStruct((B, S, WQ), BF16),
                   jax.ShapeDtypeStruct((B, G, S, n_sel), BF16)],
        compiler_params=_cparams(3),
        name="nsa_cmp_attn",
    )(q_r, kvc, kvc, small)


def _sel_attn_kernel(q_ref, k_ref, ext_ref, v_ref, sel_ref, sm_ref, o_ref, qx_ref, m_ref, acc_ref, p_ref, a_ref,
                     *, nh_gdn, hpg, tk):
    tq = q_ref.shape[1]
    nb = sel_ref.shape[3]
    g = pl.program_id(1)
    t0 = pl.program_id(2) * tq
    assert tk % tq == 0
    unsel = (sel_ref[0, 0].astype(F32) - 1.0).astype(BF16)
    for h in range(hpg):
        qx_ref[h * tq:(h + 1) * tq, 0:HEAD_DIM] = q_ref[0, :, h * HEAD_DIM:(h + 1) * HEAD_DIM]
        qx_ref[h * tq:(h + 1) * tq, HEAD_DIM:HEAD_DIM + nb] = unsel
    m_ref[...] = jnp.full_like(m_ref, NEG_INF)
    acc_ref[...] = jnp.zeros_like(acc_ref)
    p_ref[...] = jnp.zeros_like(p_ref)
    a_ref[...] = jnp.ones_like(a_ref)
    ones = jnp.ones((tk, HEAD_DIM), BF16)
    pr = hpg * tq // N_ROW_PARTS
    parts = [slice(part * pr, (part + 1) * pr) for part in range(N_ROW_PARTS)]

    def apply_values(kt):
        k0 = pl.multiple_of(kt * tk, tk)
        vx = jnp.concatenate([v_ref[0, pl.ds(k0, tk), :], ones], axis=1)
        for rows in parts:
            acc_ref[rows, :] = _tile_lanes(a_ref[rows, :], 2) * acc_ref[rows, :] + _dot(p_ref[rows, :], vx)

    def scores(kt, causal):
        k0 = pl.multiple_of(kt * tk, tk)
        kx = jnp.concatenate([k_ref[0, pl.ds(k0, tk), :], ext_ref[pl.ds(k0, tk), :]], axis=1)
        if causal:
            t_row = t0 + lax.broadcasted_iota(jnp.int32, (tq, tk), 0)
            key = k0 + lax.broadcasted_iota(jnp.int32, (tq, tk), 1)
            keep = jnp.concatenate([key <= t_row] * (hpg // N_ROW_PARTS), axis=0)
        for rows in parts:
            s = _dot_nt(qx_ref[rows, :], kx)
            if causal:
                s = jnp.where(keep, s, NEG_INF)
            m_old = m_ref[rows, :]
            m_new = jnp.maximum(m_old, jnp.max(s, axis=-1, keepdims=True))
            a_ref[rows, :] = jnp.exp2(m_old - m_new)
            p_ref[rows, :] = jnp.exp2(s - _tile_lanes(m_new, tk // HEAD_DIM)).astype(BF16)
            m_ref[rows, :] = m_new

    kt_diag = t0 // tk

    def body(kt, carry):
        apply_values(jnp.maximum(kt - 1, 0))
        scores(kt, False)
        return carry

    lax.fori_loop(0, kt_diag, body, 0)
    apply_values(jnp.maximum(kt_diag - 1, 0))
    scores(kt_diag, True)
    apply_values(kt_diag)
    for h in range(hpg):
        rows = slice(h * tq, (h + 1) * tq)
        gate = _gate_col(sm_ref, nh_gdn, g, hpg, h, 1)
        out = acc_ref[rows, 0:HEAD_DIM] / acc_ref[rows, HEAD_DIM:2 * HEAD_DIM]
        o_ref[0, :, h * HEAD_DIM:(h + 1) * HEAD_DIM] = (out * gate).astype(o_ref.dtype)


def _sel_attn(q_r, ksel_r, big, vsel_blk, sel, small, nh_gdn):
    B, S, WQ = q_r.shape
    G = NSA_GROUPS
    hpg = WQ // HEAD_DIM // G
    nbp = sel.shape[3]
    tq = min(S, 512)
    tk = min(S, 512)
    rows = hpg * tq
    ext = jnp.where(jnp.arange(S)[:, None] // NSA_SEL_LEN == jnp.arange(nbp)[None, :], MASK_BIG, 0.0).astype(BF16)
    return pl.pallas_call(
        functools.partial(_sel_attn_kernel, nh_gdn=nh_gdn, hpg=hpg, tk=tk),
        grid=(B, G, S // tq),
        in_specs=[pl.BlockSpec((1, tq, hpg * HEAD_DIM), lambda b, g, i: (b, i, g)),
                  pl.BlockSpec((1, S, HEAD_DIM), lambda b, g, i: (b, 0, g)),
                  pl.BlockSpec((S, nbp), lambda b, g, i: (0, 0)),
                  pl.BlockSpec((1, S, HEAD_DIM), lambda b, g, i: (b, 0, vsel_blk + g)),
                  pl.BlockSpec((1, 1, tq, nbp), lambda b, g, i: (b, g, i, 0)),
                  pl.BlockSpec((1, tq, small.shape[2]), lambda b, g, i: (b, i, 0))],
        out_specs=pl.BlockSpec((1, tq, hpg * HEAD_DIM), lambda b, g, i: (b, i, g)),
        out_shape=jax.ShapeDtypeStruct((B, S, WQ), BF16),
        scratch_shapes=[pltpu.VMEM((rows, HEAD_DIM + nbp), BF16),
                        pltpu.VMEM((rows, HEAD_DIM), F32),
                        pltpu.VMEM((rows, 2 * HEAD_DIM), F32),
                        pltpu.VMEM((rows, tk), BF16),
                        pltpu.VMEM((rows, HEAD_DIM), F32)],
        compiler_params=_cparams(3),
        name="nsa_sel_attn",
    )(q_r, ksel_r, ext, big, sel, small)


def _win_attn_kernel(q_ref, k_ref, v_ref, sm_ref, o_ref, *, nh_gdn, hpg):
    tq = q_ref.shape[1]
    ts = min(tq, WIN_SUB_TILE)
    span = ts + NSA_WINDOW
    g = pl.program_id(1)
    for sub in range(tq // ts):
        qrows = slice(sub * ts, (sub + 1) * ts)
        t0 = pl.program_id(2) * tq + sub * ts
        k0 = pl.multiple_of(jnp.maximum(t0 - NSA_WINDOW, 0), ts)
        k = k_ref[0, pl.ds(k0, span), :]
        v = v_ref[0, pl.ds(k0, span), :]
        t_row = t0 + lax.broadcasted_iota(jnp.int32, (ts, span), 0)
        spos = k0 + lax.broadcasted_iota(jnp.int32, (ts, span), 1)
        mask = (spos <= t_row) & (spos > t_row - NSA_WINDOW)
        vx = jnp.concatenate([v, jnp.ones((span, HEAD_DIM), BF16)], axis=1)
        q4 = jnp.concatenate([q_ref[0, qrows, h * HEAD_DIM:(h + 1) * HEAD_DIM] for h in range(hpg)], axis=0)
        s = jnp.where(jnp.concatenate([mask] * hpg, axis=0), _dot_nt(q4, k), NEG_INF)
        e = jnp.exp2(s - jnp.max(s, axis=-1, keepdims=True))
        acc = _dot(e.astype(BF16), vx)
        out = acc[:, :HEAD_DIM] / acc[:, HEAD_DIM:]
        for h in range(hpg):
            gate = _gate_col(sm_ref, nh_gdn, g, hpg, h, 2)[qrows]
            o_ref[0, qrows, h * HEAD_DIM:(h + 1) * HEAD_DIM] = (out[h * ts:(h + 1) * ts] * gate).astype(o_ref.dtype)


def _win_attn(q_r, kwin_r, big, vwin_blk, small, nh_gdn):
    B, S, WQ = q_r.shape
    G = NSA_GROUPS
    hpg = WQ // HEAD_DIM // G
    tq = min(S, 512)
    assert S >= tq + NSA_WINDOW
    return pl.pallas_call(
        functools.partial(_win_attn_kernel, nh_gdn=nh_gdn, hpg=hpg),
        grid=(B, G, S // tq),
        in_specs=[pl.BlockSpec((1, tq, hpg * HEAD_DIM), lambda b, g, i: (b, i, g)),
                  pl.BlockSpec((1, S, HEAD_DIM), lambda b, g, i: (b, 0, g)),
                  pl.BlockSpec((1, S, HEAD_DIM), lambda b, g, i: (b, 0, vwin_blk + g)),
                  pl.BlockSpec((1, tq, small.shape[2]), lambda b, g, i: (b, i, 0))],
        out_specs=pl.BlockSpec((1, tq, hpg * HEAD_DIM), lambda b, g, i: (b, i, g)),
        out_shape=jax.ShapeDtypeStruct((B, S, WQ), BF16),
        compiler_params=_cparams(3),
        name="nsa_win_attn",
    )(q_r, kwin_r, big, small)


def _post_norm(x, y, gt, g, b, alpha):
    r = alpha * x + (1.0 + gt) * y
    return _normalize_rows(r) * g + b


def _mixer_out_kernel(oa_ref, oc_ref, os_ref, ow_ref, ma_ref, mb_ref, x_ref, gt_ref, lg_ref, lb_ref,
                      wg_ref, wn_ref, wo_ref, o_ref, *, alpha):
    y_a = _dot(oa_ref[0], wg_ref[...])
    o_b = oc_ref[0].astype(F32) + os_ref[0].astype(F32) + ow_ref[0].astype(F32)
    y_b = _dot(o_b.astype(BF16), wn_ref[...])
    mixed = _sigmoid(ma_ref[0].astype(F32)) * y_a + _sigmoid(mb_ref[0].astype(F32)) * y_b
    y = _dot(mixed.astype(BF16), wo_ref[...])
    o_ref[0] = _post_norm(x_ref[0], y, gt_ref[0], lg_ref[...], lb_ref[...], alpha)


def _mixer_out(o_a, o_c, o_s, o_w, big, ma_blk, mb_blk, x, ada_l, gt_col, ln_g, ln_b, wg, wn, wo, alpha):
    B, S, D = x.shape
    tm = min(S, 512)
    row = lambda blk: pl.BlockSpec((1, tm, D), lambda b, i: (b, i, blk))
    full = lambda shape: pl.BlockSpec(shape, lambda b, i: (0,) * len(shape))
    return pl.pallas_call(
        functools.partial(_mixer_out_kernel, alpha=alpha),
        grid=(B, S // tm),
        in_specs=[row(0), row(0), row(0), row(0), row(ma_blk), row(mb_blk), row(0),
                  pl.BlockSpec((1, 1, D), lambda b, i: (b, 0, gt_col)),
                  full((1, D)), full((1, D)), full((D, D)), full((D, D)), full((D, D))],
        out_specs=row(0),
        out_shape=jax.ShapeDtypeStruct((B, S, D), F32),
        compiler_params=_cparams(2),
        name="mixer_out",
    )(o_a, o_c, o_s, o_w, big, big, x, ada_l, ln_g.reshape(1, D), ln_b.reshape(1, D), wg, wn, wo)


def _first_argmax(vals, lane, big):
    m = jnp.max(vals, axis=-1, keepdims=True)
    first = jnp.min(jnp.where(vals == m, lane, big), axis=-1, keepdims=True)
    return m, first


def _moe_router_kernel(x_ref, sh_ref, sc_ref, rwh_ref, rwl_ref, rb_ref, wsg_ref, wsu_ref, wsd_ref,
                       h_ref, idx_ref, wt_ref, shared_ref):
    h = _normalize_rows(x_ref[0]) * (1.0 + sc_ref[0]) + sh_ref[0]
    h16 = h.astype(BF16)
    h_ref[0] = h16
    h_lo = (h - h16.astype(F32)).astype(BF16)
    rw_hi = rwh_ref[...]
    logits = _dot_nt(rw_hi, h16) + _dot_nt(rwl_ref[...], h16) + _dot_nt(rw_hi, h_lo)
    scores = _sigmoid(logits)
    biased = scores + rb_ref[...]
    E, tm = scores.shape
    NG = N_EXPERT_GROUPS
    per = E // NG
    b3 = biased.reshape(NG, per, tm)
    in_grp = lax.broadcasted_iota(jnp.int32, (NG, per, tm), 1).astype(F32)
    m1 = jnp.max(b3, axis=1, keepdims=True)
    first = jnp.min(jnp.where(b3 == m1, in_grp, float(per)), axis=1, keepdims=True)
    m2 = jnp.max(jnp.where(in_grp == first, -jnp.inf, b3), axis=1, keepdims=True)
    gscore = (m1 + m2).reshape(NG, tm)
    g_id = lax.broadcasted_iota(jnp.int32, (NG, tm), 0).astype(F32)
    allowed = jnp.zeros((NG, tm), F32)
    for _ in range(TOPK_GROUPS):
        m = jnp.max(gscore, axis=0, keepdims=True)
        first_g = jnp.min(jnp.where(gscore == m, g_id, float(NG)), axis=0, keepdims=True)
        pick = g_id == first_g
        allowed = jnp.where(pick, 1.0, allowed)
        gscore = jnp.where(pick, -jnp.inf, gscore)
    cand = jnp.where(allowed.reshape(NG, 1, tm) > 0.5, b3, NEG_INF).reshape(E, tm)
    e_id = lax.broadcasted_iota(jnp.int32, (E, tm), 0).astype(F32)
    idx_rows, wt_rows = [], []
    wsum = jnp.zeros((1, tm), F32)
    for kk in range(TOP_K):
        m = jnp.max(cand, axis=0, keepdims=True)
        first_e = jnp.min(jnp.where(cand == m, e_id, float(E)), axis=0, keepdims=True)
        pick = e_id == first_e
        w = jnp.sum(jnp.where(pick, scores, 0.0), axis=0, keepdims=True)
        cand = jnp.where(pick, -jnp.inf, cand)
        idx_rows.append(first_e)
        wt_rows.append(w)
        wsum = wsum + w
    idx_ref[0] = jnp.concatenate(idx_rows, axis=0).astype(jnp.int32)
    wt_ref[0] = jnp.concatenate(wt_rows, axis=0) / wsum * ROUTED_SCALE
    act = _silu(_dot(h16, wsg_ref[...])) * _dot(h16, wsu_ref[...])
    shared_ref[0] = _dot(act.astype(BF16), wsd_ref[...])


def _moe_router(x, ada_l, sh_col, sc_col, rwt_hi, rwt_lo, router_b, wsg, wsu, wsd):
    B, S, D = x.shape
    E = rwt_hi.shape[0]
    FF = wsg.shape[1]
    tm = min(S, 512)
    row = lambda w: pl.BlockSpec((1, tm, w), lambda b, i: (b, i, 0))
    krow = pl.BlockSpec((1, TOP_K, tm), lambda b, i: (b, 0, i))
    full = lambda shape: pl.BlockSpec(shape, lambda b, i: (0,) * len(shape))
    return pl.pallas_call(
        _moe_router_kernel,
        grid=(B, S // tm),
        in_specs=[row(D),
                  pl.BlockSpec((1, 1, D), lambda b, i: (b, 0, sh_col)),
                  pl.BlockSpec((1, 1, D), lambda b, i: (b, 0, sc_col)),
                  full((E, D)), full((E, D)), full((E, 1)), full((D, FF)), full((D, FF)), full((FF, D))],
        out_specs=[row(D), krow, krow, row(D)],
        out_shape=[jax.ShapeDtypeStruct((B, S, D), BF16),
                   jax.ShapeDtypeStruct((B, TOP_K, S), jnp.int32),
                   jax.ShapeDtypeStruct((B, TOP_K, S), F32),
                   jax.ShapeDtypeStruct((B, S, D), F32)],
        compiler_params=_cparams(2),
        name="moe_router",
    )(x, ada_l, ada_l, rwt_hi, rwt_lo, router_b.reshape(E, 1), wsg, wsu, wsd)


def _moe_experts_kernel(be_ref, nu_ref, xs_ref, wg_ref, wu_ref, wd_ref, o_ref):
    @pl.when(pl.program_id(0) < nu_ref[0])
    def _():
        xb = xs_ref[...]
        act = _silu(_dot(xb, wg_ref[0, 0].astype(BF16))) * _dot(xb, wu_ref[0, 0].astype(BF16))
        o_ref[...] = _dot(act.astype(BF16), wd_ref[0, 0].astype(BF16)).astype(o_ref.dtype)


def _moe_experts(block_expert, n_used, xs, layer, wg, wu, wd):
    rows, D = xs.shape
    BM = MOE_ROW_BLOCK
    FF = wg.shape[3]
    grid_spec = pltpu.PrefetchScalarGridSpec(
        num_scalar_prefetch=2,
        grid=(rows // BM,),
        in_specs=[pl.BlockSpec((BM, D), lambda i, be, nu: (i, 0)),
                  pl.BlockSpec((1, 1, D, FF), lambda i, be, nu: (layer, be[i], 0, 0)),
                  pl.BlockSpec((1, 1, D, FF), lambda i, be, nu: (layer, be[i], 0, 0)),
                  pl.BlockSpec((1, 1, FF, D), lambda i, be, nu: (layer, be[i], 0, 0))],
        out_specs=pl.BlockSpec((BM, D), lambda i, be, nu: (i, 0)),
    )
    return pl.pallas_call(
        _moe_experts_kernel,
        grid_spec=grid_spec,
        out_shape=jax.ShapeDtypeStruct((rows, D), BF16),
        compiler_params=_cparams(1),
        name="moe_experts",
    )(block_expert, n_used, xs, wg, wu, wd)


def _moe_out_kernel(x_ref, shared_ref, yg_ref, wt_ref, gt_ref, lg_ref, lb_ref, o_ref, *, alpha):
    D = x_ref.shape[2]
    y = shared_ref[0]
    wt = wt_ref[0]
    for kk in range(TOP_K):
        y = y + wt[:, kk:kk + 1] * yg_ref[kk, 0].astype(F32)
    o_ref[0] = _post_norm(x_ref[0], y, gt_ref[0], lg_ref[...], lb_ref[...], alpha)


def _moe_out(x, shared, yg, wt, ada_l, gt_col, ln_g, ln_b, alpha):
    B, S, D = x.shape
    tm = min(S, 256)
    row = lambda w: pl.BlockSpec((1, tm, w), lambda b, i: (b, i, 0))
    full = lambda shape: pl.BlockSpec(shape, lambda b, i: (0,) * len(shape))
    return pl.pallas_call(
        functools.partial(_moe_out_kernel, alpha=alpha),
        grid=(B, S // tm),
        in_specs=[row(D), row(D),
                  pl.BlockSpec((TOP_K, 1, tm, D), lambda b, i: (0, b, i, 0)),
                  row(wt.shape[2]),
                  pl.BlockSpec((1, 1, D), lambda b, i: (b, 0, gt_col)),
                  full((1, D)), full((1, D))],
        out_specs=row(D),
        out_shape=jax.ShapeDtypeStruct((B, S, D), F32),
        compiler_params=_cparams(2),
        name="moe_out",
    )(x, shared, yg, wt, ada_l, ln_g.reshape(1, D), ln_b.reshape(1, D))


def _routing_layout(expert_idx, n_experts):
    K, T = expert_idx.shape
    BM = MOE_ROW_BLOCK
    TK = T * K
    e_flat = expert_idx.reshape(TK)
    iota = jnp.arange(TK, dtype=jnp.int32)
    _, order = lax.sort((e_flat, iota), num_keys=1)
    _, inv = lax.sort((order, iota), num_keys=1)
    counts = jnp.bincount(e_flat, length=n_experts).astype(jnp.int32)
    padded = (counts + BM - 1) // BM * BM
    pad_end = jnp.cumsum(padded)
    pad_start = pad_end - padded
    delta = pad_start - (jnp.cumsum(counts) - counts)
    dest = inv + delta[e_flat]
    n_blocks = (TK + n_experts * (BM - 1) + BM - 1) // BM
    block_start = jnp.arange(n_blocks, dtype=jnp.int32) * BM
    block_expert = jnp.minimum(jnp.sum(pad_end[None, :] <= block_start[:, None], axis=1),
                               n_experts - 1).astype(jnp.int32)
    row = block_start[:, None] + jnp.arange(BM, dtype=jnp.int32)[None, :]
    in_expert = (row - pad_start[block_expert][:, None]) < counts[block_expert][:, None]
    src = jnp.clip(row - delta[block_expert][:, None], 0, TK - 1)
    row_tok = jnp.where(in_expert, order[src.reshape(-1)].reshape(n_blocks, BM) % T, row % T).reshape(-1)
    n_used = (pad_end[-1] // BM).astype(jnp.int32).reshape(1)
    return row_tok.astype(jnp.int32), dest.astype(jnp.int32), block_expert, n_used


def kernel(x, c, positions, ada_w, ada_b, w_in, gdn_conv_w, gdn_a_log, gdn_dt_bias, gdn_norm_w, w_gdn_branch, nsa_cmp_pos_k, nsa_cmp_pos_v, nsa_cmp_k_w1, nsa_cmp_k_w2, nsa_cmp_v_w1, nsa_cmp_v_w2, w_nsa_branch, w_out, ln1_g, ln1_b, router_w, router_bias, w_sh_gate, w_sh_up, w_sh_down, w_e_gate, w_e_up, w_e_down, ln2_g, ln2_b):
    B, S, D = x.shape
    L = ada_w.shape[0]
    T = B * S
    G = NSA_GROUPS
    nh_gdn = gdn_a_log.shape[1]
    gdn_w = nh_gdn * HEAD_DIM
    nsa_w = w_nsa_branch.shape[1]
    kvw = G * HEAD_DIM
    E = router_w.shape[2]
    alpha = (2.0 * L) ** 0.25
    assert gdn_w == D and nsa_w == D and S % NSA_SEL_LEN == 0

    splits = (3 * gdn_w, gdn_w, nh_gdn, nh_gdn, nsa_w, 6 * kvw, 3 * (nsa_w // HEAD_DIM), D, D)
    offs = [0]
    for s_ in splits:
        offs.append(offs[-1] + s_)
    seg = lambda i: slice(offs[i], offs[i + 1])
    w_big = jnp.concatenate([w_in[:, :, seg(0)], w_in[:, :, seg(1)], w_in[:, :, seg(4)],
                             w_in[:, :, seg(7)], w_in[:, :, seg(8)], w_in[:, :, seg(5)]], axis=-1).astype(BF16)
    n_small = splits[2] + splits[3] + splits[6]
    w_small = jnp.concatenate([w_in[:, :, seg(2)], w_in[:, :, seg(3)], w_in[:, :, seg(6)],
                               jnp.zeros((L, D, 128 - n_small), w_in.dtype)], axis=-1).astype(BF16)
    Z_BLK, Q_BLK, MA_BLK, MB_BLK = 3, 4, 5, 6
    kv0 = 7 * D
    n_big = w_big.shape[2]
    tn_big = n_big // 4 if (n_big // 4) % 128 == 0 else 128

    wgb = w_gdn_branch.astype(BF16)
    wnb = w_nsa_branch.astype(BF16)
    wob = w_out.astype(BF16)
    wsg = w_sh_gate.astype(BF16)
    wsu = w_sh_up.astype(BF16)
    wsd = w_sh_down.astype(BF16)
    rwt = jnp.swapaxes(router_w, 1, 2)
    rwt_hi = rwt.astype(BF16)
    rwt_lo = (rwt - rwt_hi.astype(F32)).astype(BF16)
    cmp_w1 = jnp.stack([nsa_cmp_k_w1, nsa_cmp_v_w1], axis=1).astype(BF16)
    cmp_w2 = jnp.stack([nsa_cmp_k_w2, nsa_cmp_v_w2], axis=1).astype(BF16)
    cmp_pos = jnp.stack([nsa_cmp_pos_k, nsa_cmp_pos_v], axis=1).reshape(L, 2, 1, NSA_CMP_LEN * HEAD_DIM)

    ada = _ada(c, ada_w, ada_b)
    cosf, sinf = _rope_tables(positions)
    NC = S // NSA_CMP_STRIDE
    last = jnp.minimum(jnp.arange(NC) * NSA_CMP_STRIDE + NSA_CMP_LEN - 1, S - 1)
    cosc = cosf[:, last]
    sinc = sinf[:, last]
    n_sel = -(-(S // NSA_SEL_LEN) // 128) * 128

    for l in range(L):
        ada_l = ada[l].reshape(B, 1, 6 * D)
        big, small = _proj(x, ada_l, 0, 1, w_big[l], w_small[l], tn_big)
        small_t = jnp.swapaxes(small[:, :, nh_gdn:2 * nh_gdn].reshape(B, S // GDN_CHUNK, GDN_CHUNK, nh_gdn), 2, 3)
        q_a, k_a, v_a = _gdn_prep(big, gdn_conv_w[l], gdn_w)
        o_a = _gdn(q_a, k_a, v_a, big, Z_BLK, small, small_t, gdn_a_log[l], gdn_dt_bias[l], gdn_norm_w[l])
        q_r, ksel_r, kwin_r = _rope_apply(big, Q_BLK, (kv0 + 2 * kvw) // kvw, (kv0 + 4 * kvw) // kvw,
                                          nsa_w, kvw, cosf, sinf)
        hb = big[:, :, kv0:kv0 + 2 * kvw].reshape(B, NC, NSA_CMP_STRIDE, 2 * G, HEAD_DIM)
        hb = hb.transpose(0, 3, 1, 2, 4).reshape(B, 2 * G, NC, NSA_CMP_STRIDE * HEAD_DIM)
        kvc = _compress(hb, cmp_pos[l], cmp_w1[l], cmp_w2[l], cosc, sinc)
        o_c, sel = _cmp_attn(q_r, kvc, small, nh_gdn, n_sel)
        o_s = _sel_attn(q_r, ksel_r, big, (kv0 + 3 * kvw) // HEAD_DIM, sel, small, nh_gdn)
        o_w = _win_attn(q_r, kwin_r, big, (kv0 + 5 * kvw) // HEAD_DIM, small, nh_gdn)
        x = _mixer_out(o_a, o_c, o_s, o_w, big, MA_BLK, MB_BLK, x, ada_l, 2, ln1_g[l], ln1_b[l],
                       wgb[l], wnb[l], wob[l], alpha)
        h2, eidx, ewt, shared = _moe_router(x, ada_l, 3, 4, rwt_hi[l], rwt_lo[l], router_bias[l],
                                            wsg[l], wsu[l], wsd[l])
        row_tok, dest, block_expert, n_used = _routing_layout(jnp.swapaxes(eidx, 0, 1).reshape(TOP_K, T), E)
        xs = h2.reshape(T, D).at[row_tok].get(mode='promise_in_bounds')
        yb = _moe_experts(block_expert, n_used, xs, l, w_e_gate, w_e_up, w_e_down)
        yg = yb.at[dest].get(mode='promise_in_bounds').reshape(TOP_K, B, S, D)
        x = _moe_out(x, shared, yg, jnp.swapaxes(ewt, 1, 2), ada_l, 5, ln2_g[l], ln2_b[l], alpha)
    return x
```

```python
import functools
import math

import jax
import jax.numpy as jnp
from jax import lax
from jax.experimental import pallas as pl
from jax.experimental.pallas import tpu as pltpu

F32 = jnp.float32
BF16 = jnp.bfloat16

HEAD_DIM = 128
GDN_CONV = 4
GDN_CHUNK = 64
NSA_GROUPS = 2
NSA_CMP_LEN = 32
NSA_CMP_STRIDE = 16
NSA_SEL_LEN = 64
NSA_SEL_TOP = 16
NSA_WINDOW = 512
ROPE_THETA = 10000.0
N_EXPERTS = 64
N_EXPERT_GROUPS = 8
TOPK_GROUPS = 4
TOP_K = 8
ROUTED_SCALE = 2.5
LN_EPS = 1e-5
NEG_INF = -1e30
SEL_FORCE = 1e6
MOE_ROW_BLOCK = 512
Q_SCALE_LOG2E = (HEAD_DIM ** -0.5) * math.log2(math.e)
MASK_BIG = 2.0 ** 100
N_ROW_PARTS = 2
GDN_PHASE1_CHUNKS = 4
WIN_SUB_TILE = 128
MOE_BATCH_PARTS = 2

_ARB = "arbitrary"


def _cparams(n_axes):
    return pltpu.CompilerParams(dimension_semantics=(_ARB,) * n_axes)


def _sigmoid(x):
    return 1.0 / (1.0 + jnp.exp(-x))


def _silu(x):
    return x * _sigmoid(x)


def _dot(a, b):
    return jnp.dot(a, b, preferred_element_type=F32)


def _dot_nt(a, b):
    return lax.dot_general(a, b, (((1,), (1,)), ((), ())), preferred_element_type=F32)


def _dot_tn(a, b):
    return lax.dot_general(a, b, (((0,), (0,)), ((), ())), preferred_element_type=F32)


def _tile_lanes(x, n):
    return jnp.concatenate([x] * n, axis=1)


def _split3(x):
    x0 = x.astype(BF16)
    r1 = x - x0.astype(F32)
    x1 = r1.astype(BF16)
    x2 = (r1 - x1.astype(F32)).astype(BF16)
    return x0, x1, x2


def _dot_sel_right(x, sel_bf16):
    x0, x1, x2 = _split3(x)
    return _dot(x0, sel_bf16) + _dot(x1, sel_bf16) + _dot(x2, sel_bf16)


def _dot_sel_left(sel_bf16, x):
    x0, x1, x2 = _split3(x)
    return _dot(sel_bf16, x0) + _dot(sel_bf16, x1) + _dot(sel_bf16, x2)


def _normalize_rows(x):
    mu = jnp.mean(x, axis=-1, keepdims=True)
    xc = x - mu
    var = jnp.mean(xc * xc, axis=-1, keepdims=True)
    return xc * lax.rsqrt(var + LN_EPS)


def _ada_kernel(c_ref, w_ref, b_ref, o_ref):
    cond = _silu(c_ref[...])
    o_ref[0] = jnp.dot(cond, w_ref[0], preferred_element_type=F32,
                       precision=lax.Precision.HIGHEST) + b_ref[0]


def _ada(c, ada_w, ada_b):
    L, D, N = ada_w.shape
    B = c.shape[0]
    tn = min(N, 1536)
    return pl.pallas_call(
        _ada_kernel,
        grid=(L, N // tn),
        in_specs=[pl.BlockSpec((B, D), lambda l, j: (0, 0)),
                  pl.BlockSpec((1, D, tn), lambda l, j: (l, 0, j)),
                  pl.BlockSpec((1, 1, tn), lambda l, j: (l, 0, j))],
        out_specs=pl.BlockSpec((1, B, tn), lambda l, j: (l, 0, j)),
        out_shape=jax.ShapeDtypeStruct((L, B, N), F32),
        compiler_params=_cparams(2),
        name="ada",
    )(c, ada_w, ada_b.reshape(L, 1, N))


def _proj_kernel(x_ref, sh_ref, sc_ref, w_ref, ws_ref, o_ref, os_ref, h_ref):
    @pl.when(pl.program_id(2) == 0)
    def _():
        h = _normalize_rows(x_ref[0]) * (1.0 + sc_ref[0]) + sh_ref[0]
        h_ref[...] = h.astype(BF16)
        os_ref[0] = _dot(h_ref[...], ws_ref[...])

    o_ref[0] = _dot(h_ref[...], w_ref[...]).astype(o_ref.dtype)


def _proj(x, ada_l, sh_col, sc_col, w, w_small, tn):
    B, S, D = x.shape
    N = w.shape[1]
    NS = w_small.shape[1]
    tm = min(S, 1024)
    return pl.pallas_call(
        _proj_kernel,
        grid=(B, S // tm, N // tn),
        in_specs=[pl.BlockSpec((1, tm, D), lambda b, i, j: (b, i, 0)),
                  pl.BlockSpec((1, 1, D), lambda b, i, j: (b, 0, sh_col)),
                  pl.BlockSpec((1, 1, D), lambda b, i, j: (b, 0, sc_col)),
                  pl.BlockSpec((D, tn), lambda b, i, j: (0, j)),
                  pl.BlockSpec((D, NS), lambda b, i, j: (0, 0))],
        out_specs=[pl.BlockSpec((1, tm, tn), lambda b, i, j: (b, i, j)),
                   pl.BlockSpec((1, tm, NS), lambda b, i, j: (b, i, 0))],
        out_shape=[jax.ShapeDtypeStruct((B, S, N), BF16),
                   jax.ShapeDtypeStruct((B, S, NS), F32)],
        scratch_shapes=[pltpu.VMEM((tm, D), BF16)],
        compiler_params=_cparams(3),
        name="proj",
    )(x, ada_l, ada_l, w, w_small)


def _gdn_prep_kernel(x_ref, w_ref, q_ref, k_ref, v_ref, carry_ref):
    ts = x_ref.shape[1]
    width = q_ref.shape[2]
    nh = width // HEAD_DIM

    @pl.when(pl.program_id(1) == 0)
    def _():
        carry_ref[...] = jnp.zeros_like(carry_ref)

    for part, o_ref in enumerate((q_ref, k_ref, v_ref)):
        for h in range(nh):
            c0 = part * width + h * HEAD_DIM
            cols = slice(c0, c0 + HEAD_DIM)
            xx = jnp.concatenate([carry_ref[:, cols], x_ref[0, :, cols].astype(F32)], axis=0)
            w = w_ref[:, cols]
            y = xx[8:8 + ts] * w[3:4]
            for kk in range(GDN_CONV - 1):
                off = 8 - (GDN_CONV - 1) + kk
                y = y + xx[off:off + ts] * w[kk:kk + 1]
            y = _silu(y)
            if part < 2:
                y = y * lax.rsqrt(jnp.sum(y * y, axis=-1, keepdims=True) + 1e-6)
            if part == 0:
                y = y * (HEAD_DIM ** -0.5)
            o_ref[0, :, h * HEAD_DIM:(h + 1) * HEAD_DIM] = y.astype(o_ref.dtype)
    carry_ref[...] = x_ref[0, ts - 8:ts, :].astype(F32)


def _gdn_prep(big, conv_w, width):
    B, S, _ = big.shape
    ts = min(S, 512)
    out = jax.ShapeDtypeStruct((B, S, width), BF16)
    ospec = pl.BlockSpec((1, ts, width), lambda b, i: (b, i, 0))
    return pl.pallas_call(
        _gdn_prep_kernel,
        grid=(B, S // ts),
        in_specs=[pl.BlockSpec((1, ts, 3 * width), lambda b, i: (b, i, 0)),
                  pl.BlockSpec((GDN_CONV, 3 * width), lambda b, i: (0, 0))],
        out_specs=[ospec, ospec, ospec],
        out_shape=[out, out, out],
        scratch_shapes=[pltpu.VMEM((8, 3 * width), F32)],
        compiler_params=_cparams(2),
        name="gdn_prep",
    )(big, conv_w)


def _softplus(x):
    return jnp.maximum(x, 0.0) + jnp.log(1.0 + jnp.exp(-jnp.abs(x)))


def _gdn_kernel(q_ref, k_ref, v_ref, z_ref, sm_ref, smt_ref, alog_ref, alogt_ref, dtb_ref, dtbt_ref,
                nw_ref, o_ref, state_ref, u_s, wq_s, attn_s, kd_s, dec_s):
    ts = q_ref.shape[1]
    nh = q_ref.shape[2] // HEAD_DIM
    C = GDN_CHUNK
    R = 2 * C
    npair = nh // 2
    nchunks = ts // C

    @pl.when(pl.program_id(1) == 0)
    def _():
        state_ref[...] = jnp.zeros_like(state_ref)

    ci = lax.broadcasted_iota(jnp.int32, (C, C), 0)
    cj = lax.broadcasted_iota(jnp.int32, (C, C), 1)
    tril = jnp.where(ci >= cj, 1.0, 0.0).astype(BF16)
    triu = jnp.where(cj >= ci, 1.0, 0.0).astype(BF16)
    ii = lax.broadcasted_iota(jnp.int32, (R, R), 0)
    jj = lax.broadcasted_iota(jnp.int32, (R, R), 1)
    same_head = (ii // C) == (jj // C)
    incl = same_head & (ii >= jj)
    strict = same_head & (ii > jj)
    eye = jnp.where(ii == jj, 1.0, 0.0)
    nw = nw_ref[...]

    def pair_rows(ref, rows, p):
        return jnp.concatenate([ref[0, rows, (2 * p) * HEAD_DIM:(2 * p + 1) * HEAD_DIM],
                                ref[0, rows, (2 * p + 1) * HEAD_DIM:(2 * p + 2) * HEAD_DIM]], axis=0)

    def pair_col(x, p):
        return jnp.concatenate([x[:, 2 * p:2 * p + 1], x[:, 2 * p + 1:2 * p + 2]], axis=0)

    def phase1(j, carry):
        cs = [j * GDN_PHASE1_CHUNKS + d for d in range(GDN_PHASE1_CHUNKS)]
        rows_c, beta_c, gc_c, gcr_c, glb_c = [], [], [], [], []
        for c in cs:
            rows = pl.ds(pl.multiple_of(c * C, C), C)
            sm = sm_ref[0, rows, :]
            g = -jnp.exp(alog_ref[...]) * _softplus(sm[:, nh:2 * nh] + dtb_ref[...])
            gt = -jnp.exp(alogt_ref[...]) * _softplus(smt_ref[0, c] + dtbt_ref[...])
            gc = _dot_sel_left(tril, g)
            g_last = gc[C - 1:C, :]
            dec_s[c] = jnp.exp(g_last)
            rows_c.append(rows)
            beta_c.append(_sigmoid(sm[:, 0:nh]))
            gc_c.append(gc)
            gcr_c.append(_dot_sel_right(gt, triu))
            glb_c.append(jnp.broadcast_to(g_last, (C, nh)))
        items = [(ci, p) for ci in range(len(cs)) for p in range(npair)]
        idx = range(len(items))
        q2 = [pair_rows(q_ref, rows_c[ci], p).astype(F32) for ci, p in items]
        k2 = [pair_rows(k_ref, rows_c[ci], p).astype(F32) for ci, p in items]
        b2 = [pair_col(beta_c[ci], p) for ci, p in items]
        g_col = [pair_col(gc_c[ci], p) for ci, p in items]
        kb = [k2[i] * b2[i] for i in idx]
        kq = [_dot_nt(jnp.concatenate([kb[i], q2[i]], axis=0).astype(BF16), k2[i].astype(BF16))
              for i in idx]
        decay = []
        for i, (ci, p) in enumerate(items):
            gcr = gcr_c[ci]
            g_row = jnp.concatenate([gcr[2 * p:2 * p + 1, :], gcr[2 * p + 1:2 * p + 2, :]], axis=1)
            decay.append(jnp.where(incl, jnp.exp(jnp.where(incl, g_col[i] - g_row, 0.0)), 0.0))
        a = [jnp.where(strict, kq[i][:R] * decay[i], 0.0) for i in idx]
        for i, (ci, p) in enumerate(items):
            attn_s[cs[ci], p] = (kq[i][R:] * decay[i]).astype(BF16)
        x = [eye - a[i] for i in idx]
        pw = a
        n = 2
        while n < C:
            pw16 = [pw[i].astype(BF16) for i in idx]
            pw = [_dot(pw16[i], pw16[i]) for i in idx]
            x = [x[i] + _dot(x[i].astype(BF16), pw[i].astype(BF16)) for i in idx]
            n *= 2
        eg = [jnp.exp(g_col[i]) for i in idx]
        sol = []
        for i, (ci, p) in enumerate(items):
            v2 = pair_rows(v_ref, rows_c[ci], p).astype(F32)
            r = jnp.concatenate([v2 * b2[i], kb[i] * eg[i]], axis=1)
            sol.append(_dot(x[i].astype(BF16), r.astype(BF16)))
        for i, (ci, p) in enumerate(items):
            c = cs[ci]
            u_s[c, p] = sol[i][:, :HEAD_DIM]
            w = sol[i][:, HEAD_DIM:]
            qg = q2[i] * eg[i]
            for e in range(2):
                wq_s[c, 2 * p + e] = jnp.concatenate([w[e * C:(e + 1) * C], qg[e * C:(e + 1) * C]],
                                                     axis=0).astype(BF16)
            kd_s[c, p] = (k2[i] * jnp.exp(pair_col(glb_c[ci], p) - g_col[i])).astype(BF16)
        return carry

    assert nchunks % GDN_PHASE1_CHUNKS == 0
    lax.fori_loop(0, nchunks // GDN_PHASE1_CHUNKS, phase1, 0)

    def phase2(c, carry):
        rows = pl.ds(pl.multiple_of(c * C, C), C)
        dec = dec_s[c]
        res = [_dot(wq_s[c, h], state_ref[h].astype(BF16)) for h in range(nh)]
        for p in range(npair):
            ws = jnp.concatenate([res[2 * p][:C], res[2 * p + 1][:C]], axis=0)
            qs = jnp.concatenate([res[2 * p][C:], res[2 * p + 1][C:]], axis=0)
            v_new = u_s[c, p] - ws
            v16 = v_new.astype(BF16)
            o2 = qs + _dot(attn_s[c, p], v16)
            kd = kd_s[c, p]
            for e in range(2):
                h = 2 * p + e
                part = slice(e * C, (e + 1) * C)
                state_ref[h] = state_ref[h] * dec[:, h:h + 1] + _dot_tn(kd[part], v16[part])
                o = o2[part]
                o = o * lax.rsqrt(jnp.mean(o * o, axis=-1, keepdims=True) + 1e-6) * nw
                cols = slice(h * HEAD_DIM, (h + 1) * HEAD_DIM)
                z = z_ref[0, rows, cols].astype(F32)
                o_ref[0, rows, cols] = (o * _silu(z)).astype(o_ref.dtype)
        return carry

    lax.fori_loop(0, nchunks, phase2, 0)


def _gdn(q, k, v, big, z_blk, small, small_t, a_log, dt_bias, norm_w):
    B, S, W = q.shape
    nh = W // HEAD_DIM
    assert nh % 2 == 0 and 2 * GDN_CHUNK == HEAD_DIM
    ts = min(S, 512)
    nc = ts // GDN_CHUNK
    spec = pl.BlockSpec((1, ts, W), lambda b, i: (b, i, 0))
    full = lambda shape: pl.BlockSpec(shape, lambda b, i: (0,) * len(shape))
    return pl.pallas_call(
        _gdn_kernel,
        grid=(B, S // ts),
        in_specs=[spec, spec, spec,
                  pl.BlockSpec((1, ts, W), lambda b, i: (b, i, z_blk)),
                  pl.BlockSpec((1, ts, small.shape[2]), lambda b, i: (b, i, 0)),
                  pl.BlockSpec((1, ts // GDN_CHUNK, nh, GDN_CHUNK), lambda b, i: (b, i, 0, 0)),
                  full((1, nh)), full((nh, 1)), full((1, nh)), full((nh, 1)),
                  full((1, HEAD_DIM))],
        out_specs=spec,
        out_shape=jax.ShapeDtypeStruct((B, S, W), BF16),
        scratch_shapes=[pltpu.VMEM((nh, HEAD_DIM, HEAD_DIM), F32),
                        pltpu.VMEM((nc, nh // 2, 2 * GDN_CHUNK, HEAD_DIM), F32),
                        pltpu.VMEM((nc, nh, 2 * GDN_CHUNK, HEAD_DIM), BF16),
                        pltpu.VMEM((nc, nh // 2, 2 * GDN_CHUNK, 2 * GDN_CHUNK), BF16),
                        pltpu.VMEM((nc, nh // 2, 2 * GDN_CHUNK, HEAD_DIM), BF16),
                        pltpu.VMEM((nc, 1, nh), F32)],
        compiler_params=_cparams(2),
        name="gdn",
    )(q, k, v, big, small, small_t, a_log.reshape(1, nh), a_log.reshape(nh, 1),
      dt_bias.reshape(1, nh), dt_bias.reshape(nh, 1), norm_w.reshape(1, HEAD_DIM))


def _rope_table_kernel(pos_ref, invf_ref, sign_ref, cos_ref, sin_ref):
    ang = pos_ref[0].astype(F32) * invf_ref[...]
    cos_ref[0] = jnp.cos(ang)
    sin_ref[0] = jnp.sin(ang) * sign_ref[...]


def _rope_tables(positions):
    B, S = positions.shape
    half = HEAD_DIM // 2
    inv = ROPE_THETA ** (-jnp.arange(half, dtype=F32) / half)
    invf = jnp.concatenate([inv, inv]).reshape(1, HEAD_DIM)
    sign = jnp.concatenate([-jnp.ones((half,), F32), jnp.ones((half,), F32)]).reshape(1, HEAD_DIM)
    ts = min(S, 1024)
    out = jax.ShapeDtypeStruct((B, S, HEAD_DIM), F32)
    ospec = pl.BlockSpec((1, ts, HEAD_DIM), lambda b, i: (b, i, 0))
    return pl.pallas_call(
        _rope_table_kernel,
        grid=(B, S // ts),
        in_specs=[pl.BlockSpec((1, ts, 1), lambda b, i: (b, i, 0)),
                  pl.BlockSpec((1, HEAD_DIM), lambda b, i: (0, 0)),
                  pl.BlockSpec((1, HEAD_DIM), lambda b, i: (0, 0))],
        out_specs=[ospec, ospec],
        out_shape=[out, out],
        compiler_params=_cparams(2),
        name="rope_table",
    )(positions.reshape(B, S, 1), invf, sign)


def _rope_rows(x, cosf, sinf):
    return x * cosf + pltpu.roll(x, HEAD_DIM // 2, 1) * sinf


def _rope_apply_kernel(q_ref, ks_ref, kw_ref, cos_ref, sin_ref, qo_ref, kso_ref, kwo_ref):
    cosf = cos_ref[0]
    sinf = sin_ref[0]
    for src, dst, mult in ((q_ref, qo_ref, Q_SCALE_LOG2E), (ks_ref, kso_ref, None), (kw_ref, kwo_ref, None)):
        for h in range(src.shape[2] // HEAD_DIM):
            cols = slice(h * HEAD_DIM, (h + 1) * HEAD_DIM)
            r = _rope_rows(src[0, :, cols].astype(F32), cosf, sinf)
            if mult is not None:
                r = r * mult
            dst[0, :, cols] = r.astype(dst.dtype)


def _rope_apply(big, q_blk, ksel_blk, kwin_blk, wq, wkv, cosf, sinf):
    B, S, _ = big.shape
    ts = min(S, 512)
    tab = pl.BlockSpec((1, ts, HEAD_DIM), lambda b, i: (b, i, 0))
    return pl.pallas_call(
        _rope_apply_kernel,
        grid=(B, S // ts),
        in_specs=[pl.BlockSpec((1, ts, wq), lambda b, i: (b, i, q_blk)),
                  pl.BlockSpec((1, ts, wkv), lambda b, i: (b, i, ksel_blk)),
                  pl.BlockSpec((1, ts, wkv), lambda b, i: (b, i, kwin_blk)),
                  tab, tab],
        out_specs=[pl.BlockSpec((1, ts, wq), lambda b, i: (b, i, 0)),
                   pl.BlockSpec((1, ts, wkv), lambda b, i: (b, i, 0)),
                   pl.BlockSpec((1, ts, wkv), lambda b, i: (b, i, 0))],
        out_shape=[jax.ShapeDtypeStruct((B, S, wq), BF16),
                   jax.ShapeDtypeStruct((B, S, wkv), BF16),
                   jax.ShapeDtypeStruct((B, S, wkv), BF16)],
        compiler_params=_cparams(2),
        name="rope_apply",
    )(big, big, big, cosf, sinf)


def _compress_kernel(hb_ref, pos_ref, w1_ref, w2_ref, cos_ref, sin_ref, o_ref):
    hb = hb_ref[0, 0]
    w1 = w1_ref[0]
    half = hb.shape[1]
    p0 = _dot(hb, w1[:half])
    p1 = _dot(hb, w1[half:])
    nc = p0.shape[0]
    pos8 = jnp.broadcast_to(pos_ref[0], (8, 2 * half)).astype(BF16)
    pb = _dot(pos8, w1)[0:1]
    pre = p0 + pltpu.roll(p1, nc - 1, 0) + pb
    out = _dot(_silu(pre).astype(BF16), w2_ref[0])
    roped = _rope_rows(out, cos_ref[0], sin_ref[0])
    is_key = pl.program_id(1) < NSA_GROUPS
    o_ref[0, 0] = jnp.where(is_key, roped, out).astype(o_ref.dtype)


def _compress(hb, pos_flat, w1, w2, cosc, sinc):
    B, J, NC, HW = hb.shape
    G = NSA_GROUPS
    return pl.pallas_call(
        _compress_kernel,
        grid=(B, J),
        in_specs=[pl.BlockSpec((1, 1, NC, HW), lambda b, j: (b, j, 0, 0)),
                  pl.BlockSpec((1, 1, 2 * HW), lambda b, j: (j // G, 0, 0)),
                  pl.BlockSpec((1, 2 * HW, HEAD_DIM), lambda b, j: (j // G, 0, 0)),
                  pl.BlockSpec((1, HEAD_DIM, HEAD_DIM), lambda b, j: (j // G, 0, 0)),
                  pl.BlockSpec((1, NC, HEAD_DIM), lambda b, j: (b, 0, 0)),
                  pl.BlockSpec((1, NC, HEAD_DIM), lambda b, j: (b, 0, 0))],
        out_specs=pl.BlockSpec((1, 1, NC, HEAD_DIM), lambda b, j: (b, j, 0, 0)),
        out_shape=jax.ShapeDtypeStruct((B, J, NC, HEAD_DIM), BF16),
        compiler_params=_cparams(2),
        name="nsa_compress",
    )(hb, pos_flat, w1, w2, cosc, sinc)


def _gate_col(g_ref, nh_gdn, group, hpg, h, branch):
    col = None
    for gi in range(NSA_GROUPS):
        c = 2 * nh_gdn + (gi * hpg + h) * 3 + branch
        cand = g_ref[0, :, c:c + 1]
        col = cand if col is None else jnp.where(group == gi, cand, col)
    return _sigmoid(col)


def _cmp_attn_kernel(q_ref, kc_ref, vc_ref, sm_ref, o_ref, sel_ref, *, nh_gdn, hpg):
    tq = q_ref.shape[1]
    nc = kc_ref.shape[2]
    nb = sel_ref.shape[3]
    g = pl.program_id(1)
    t0 = pl.program_id(2) * tq
    t_nc = t0 + lax.broadcasted_iota(jnp.int32, (tq, nc), 0)
    n_nc = lax.broadcasted_iota(jnp.int32, (tq, nc), 1)
    valid = (n_nc * NSA_CMP_STRIDE + (NSA_CMP_LEN - 1)) <= t_nc
    t_col = t0 + lax.broadcasted_iota(jnp.int32, (tq, 1), 0)
    has_valid = jnp.where(t_col >= NSA_CMP_LEN - 1, 1.0, 0.0)
    kc = kc_ref[0, 0]
    vc = vc_ref[0, 0]
    psum = jnp.zeros((tq, nc), F32)
    for h in range(hpg):
        cols = slice(h * HEAD_DIM, (h + 1) * HEAD_DIM)
        s = jnp.where(valid, _dot_nt(q_ref[0, :, cols], kc), NEG_INF)
        e = jnp.exp2(s - jnp.max(s, axis=-1, keepdims=True))
        p = e * (has_valid / jnp.sum(e, axis=-1, keepdims=True))
        gate = _gate_col(sm_ref, nh_gdn, g, hpg, h, 0)
        o_ref[0, :, cols] = (_dot(p.astype(BF16), vc) * gate).astype(o_ref.dtype)
        psum = psum + p
    sj = lax.broadcasted_iota(jnp.int32, (nb, nc), 0) * NSA_SEL_LEN
    cn = lax.broadcasted_iota(jnp.int32, (nb, nc), 1) * NSA_CMP_STRIDE
    overlap_t = jnp.where((cn <= sj + (NSA_SEL_LEN - 1)) & (cn + (NSA_CMP_LEN - 1) >= sj), 1.0, 0.0).astype(BF16)
    p0, p1, p2 = _split3(psum)
    imp = _dot_nt(overlap_t, p0) + _dot_nt(overlap_t, p1) + _dot_nt(overlap_t, p2)
    t_nb = t0 + lax.broadcasted_iota(jnp.int32, (nb, tq), 1)
    blk = lax.broadcasted_iota(jnp.int32, (nb, tq), 0)
    cur = t_nb // NSA_SEL_LEN
    forced = (blk == 0) | (blk == cur) | (blk == cur - 1)
    score = jnp.where(forced, SEL_FORCE, jnp.where(blk * NSA_SEL_LEN <= t_nb, imp, -1.0))
    sel = jnp.zeros((nb, tq), F32)
    blk_f = blk.astype(F32)
    for _ in range(min(NSA_SEL_TOP, nb)):
        m = jnp.max(score, axis=0, keepdims=True)
        first = jnp.min(jnp.where(score == m, blk_f, float(nb)), axis=0, keepdims=True)
        pick = blk_f == first
        sel = jnp.where(pick, 1.0, sel)
        score = jnp.where(pick, -jnp.inf, score)
    sel_ref[0, 0] = sel.T.astype(sel_ref.dtype)


def _cmp_attn(q_r, kvc, small, nh_gdn, n_sel):
    B, S, WQ = q_r.shape
    G = NSA_GROUPS
    hpg = WQ // HEAD_DIM // G
    NC = kvc.shape[2]
    tq = min(S, 512)
    return pl.pallas_call(
        functools.partial(_cmp_attn_kernel, nh_gdn=nh_gdn, hpg=hpg),
        grid=(B, G, S // tq),
        in_specs=[pl.BlockSpec((1, tq, hpg * HEAD_DIM), lambda b, g, i: (b, i, g)),
                  pl.BlockSpec((1, 1, NC, HEAD_DIM), lambda b, g, i: (b, g, 0, 0)),
                  pl.BlockSpec((1, 1, NC, HEAD_DIM), lambda b, g, i: (b, G + g, 0, 0)),
                  pl.BlockSpec((1, tq, small.shape[2]), lambda b, g, i: (b, i, 0))],
        out_specs=[pl.BlockSpec((1, tq, hpg * HEAD_DIM), lambda b, g, i: (b, i, g)),
                   pl.BlockSpec((1, 1, tq, n_sel), lambda b, g, i: (b, g, i, 0))],
        out_shape=[jax.ShapeDtypeStruct((B, S, WQ), BF16),
                   jax.ShapeDtypeStruct((B, G, S, n_sel), BF16)],
        compiler_params=_cparams(3),
        name="nsa_cmp_attn",
    )(q_r, kvc, kvc, small)


def _sel_attn_kernel(q_ref, k_ref, ext_ref, v_ref, sel_ref, sm_ref, o_ref, qx_ref, m_ref, acc_ref, p_ref, a_ref,
                     *, nh_gdn, hpg, tk):
    tq = q_ref.shape[1]
    nb = sel_ref.shape[3]
    g = pl.program_id(1)
    t0 = pl.program_id(2) * tq
    assert tk % tq == 0
    unsel = (sel_ref[0, 0].astype(F32) - 1.0).astype(BF16)
    for h in range(hpg):
        qx_ref[h * tq:(h + 1) * tq, 0:HEAD_DIM] = q_ref[0, :, h * HEAD_DIM:(h + 1) * HEAD_DIM]
        qx_ref[h * tq:(h + 1) * tq, HEAD_DIM:HEAD_DIM + nb] = unsel
    m_ref[...] = jnp.full_like(m_ref, NEG_INF)
    acc_ref[...] = jnp.zeros_like(acc_ref)
    p_ref[...] = jnp.zeros_like(p_ref)
    a_ref[...] = jnp.ones_like(a_ref)
    ones = jnp.ones((tk, HEAD_DIM), BF16)
    pr = hpg * tq // N_ROW_PARTS
    parts = [slice(part * pr, (part + 1) * pr) for part in range(N_ROW_PARTS)]

    def apply_values(kt):
        k0 = pl.multiple_of(kt * tk, tk)
        vx = jnp.concatenate([v_ref[0, pl.ds(k0, tk), :], ones], axis=1)
        for rows in parts:
            acc_ref[rows, :] = _tile_lanes(a_ref[rows, :], 2) * acc_ref[rows, :] + _dot(p_ref[rows, :], vx)

    def scores(kt, causal):
        k0 = pl.multiple_of(kt * tk, tk)
        kx = jnp.concatenate([k_ref[0, pl.ds(k0, tk), :], ext_ref[pl.ds(k0, tk), :]], axis=1)
        if causal:
            t_row = t0 + lax.broadcasted_iota(jnp.int32, (tq, tk), 0)
            key = k0 + lax.broadcasted_iota(jnp.int32, (tq, tk), 1)
            keep = jnp.concatenate([key <= t_row] * (hpg // N_ROW_PARTS), axis=0)
        for rows in parts:
            s = _dot_nt(qx_ref[rows, :], kx)
            if causal:
                s = jnp.where(keep, s, NEG_INF)
            m_old = m_ref[rows, :]
            m_new = jnp.maximum(m_old, jnp.max(s, axis=-1, keepdims=True))
            a_ref[rows, :] = jnp.exp2(m_old - m_new)
            p_ref[rows, :] = jnp.exp2(s - _tile_lanes(m_new, tk // HEAD_DIM)).astype(BF16)
            m_ref[rows, :] = m_new

    kt_diag = t0 // tk

    def body(kt, carry):
        apply_values(jnp.maximum(kt - 1, 0))
        scores(kt, False)
        return carry

    lax.fori_loop(0, kt_diag, body, 0)
    apply_values(jnp.maximum(kt_diag - 1, 0))
    scores(kt_diag, True)
    apply_values(kt_diag)
    for h in range(hpg):
        rows = slice(h * tq, (h + 1) * tq)
        gate = _gate_col(sm_ref, nh_gdn, g, hpg, h, 1)
        out = acc_ref[rows, 0:HEAD_DIM] / acc_ref[rows, HEAD_DIM:2 * HEAD_DIM]
        o_ref[0, :, h * HEAD_DIM:(h + 1) * HEAD_DIM] = (out * gate).astype(o_ref.dtype)


def _sel_attn(q_r, ksel_r, big, vsel_blk, sel, small, nh_gdn):
    B, S, WQ = q_r.shape
    G = NSA_GROUPS
    hpg = WQ // HEAD_DIM // G
    nbp = sel.shape[3]
    tq = min(S, 512)
    tk = min(S, 512)
    rows = hpg * tq
    ext = jnp.where(jnp.arange(S)[:, None] // NSA_SEL_LEN == jnp.arange(nbp)[None, :], MASK_BIG, 0.0).astype(BF16)
    return pl.pallas_call(
        functools.partial(_sel_attn_kernel, nh_gdn=nh_gdn, hpg=hpg, tk=tk),
        grid=(B, G, S // tq),
        in_specs=[pl.BlockSpec((1, tq, hpg * HEAD_DIM), lambda b, g, i: (b, i, g)),
                  pl.BlockSpec((1, S, HEAD_DIM), lambda b, g, i: (b, 0, g)),
                  pl.BlockSpec((S, nbp), lambda b, g, i: (0, 0)),
                  pl.BlockSpec((1, S, HEAD_DIM), lambda b, g, i: (b, 0, vsel_blk + g)),
                  pl.BlockSpec((1, 1, tq, nbp), lambda b, g, i: (b, g, i, 0)),
                  pl.BlockSpec((1, tq, small.shape[2]), lambda b, g, i: (b, i, 0))],
        out_specs=pl.BlockSpec((1, tq, hpg * HEAD_DIM), lambda b, g, i: (b, i, g)),
        out_shape=jax.ShapeDtypeStruct((B, S, WQ), BF16),
        scratch_shapes=[pltpu.VMEM((rows, HEAD_DIM + nbp), BF16),
                        pltpu.VMEM((rows, HEAD_DIM), F32),
                        pltpu.VMEM((rows, 2 * HEAD_DIM), F32),
                        pltpu.VMEM((rows, tk), BF16),
                        pltpu.VMEM((rows, HEAD_DIM), F32)],
        compiler_params=_cparams(3),
        name="nsa_sel_attn",
    )(q_r, ksel_r, ext, big, sel, small)


def _win_attn_kernel(q_ref, k_ref, v_ref, sm_ref, o_ref, *, nh_gdn, hpg):
    tq = q_ref.shape[1]
    ts = min(tq, WIN_SUB_TILE)
    span = ts + NSA_WINDOW
    g = pl.program_id(1)
    for sub in range(tq // ts):
        qrows = slice(sub * ts, (sub + 1) * ts)
        t0 = pl.program_id(2) * tq + sub * ts
        k0 = pl.multiple_of(jnp.maximum(t0 - NSA_WINDOW, 0), ts)
        k = k_ref[0, pl.ds(k0, span), :]
        v = v_ref[0, pl.ds(k0, span), :]
        t_row = t0 + lax.broadcasted_iota(jnp.int32, (ts, span), 0)
        spos = k0 + lax.broadcasted_iota(jnp.int32, (ts, span), 1)
        mask = (spos <= t_row) & (spos > t_row - NSA_WINDOW)
        vx = jnp.concatenate([v, jnp.ones((span, HEAD_DIM), BF16)], axis=1)
        q4 = jnp.concatenate([q_ref[0, qrows, h * HEAD_DIM:(h + 1) * HEAD_DIM] for h in range(hpg)], axis=0)
        s = jnp.where(jnp.concatenate([mask] * hpg, axis=0), _dot_nt(q4, k), NEG_INF)
        e = jnp.exp2(s - jnp.max(s, axis=-1, keepdims=True))
        acc = _dot(e.astype(BF16), vx)
        out = acc[:, :HEAD_DIM] / acc[:, HEAD_DIM:]
        for h in range(hpg):
            gate = _gate_col(sm_ref, nh_gdn, g, hpg, h, 2)[qrows]
            o_ref[0, qrows, h * HEAD_DIM:(h + 1) * HEAD_DIM] = (out[h * ts:(h + 1) * ts] * gate).astype(o_ref.dtype)


def _win_attn(q_r, kwin_r, big, vwin_blk, small, nh_gdn):
    B, S, WQ = q_r.shape
    G = NSA_GROUPS
    hpg = WQ // HEAD_DIM // G
    tq = min(S, 512)
    assert S >= tq + NSA_WINDOW
    return pl.pallas_call(
        functools.partial(_win_attn_kernel, nh_gdn=nh_gdn, hpg=hpg),
        grid=(B, G, S // tq),
        in_specs=[pl.BlockSpec((1, tq, hpg * HEAD_DIM), lambda b, g, i: (b, i, g)),
                  pl.BlockSpec((1, S, HEAD_DIM), lambda b, g, i: (b, 0, g)),
                  pl.BlockSpec((1, S, HEAD_DIM), lambda b, g, i: (b, 0, vwin_blk + g)),
                  pl.BlockSpec((1, tq, small.shape[2]), lambda b, g, i: (b, i, 0))],
        out_specs=pl.BlockSpec((1, tq, hpg * HEAD_DIM), lambda b, g, i: (b, i, g)),
        out_shape=jax.ShapeDtypeStruct((B, S, WQ), BF16),
        compiler_params=_cparams(3),
        name="nsa_win_attn",
    )(q_r, kwin_r, big, small)


def _post_norm(x, y, gt, g, b, alpha):
    r = alpha * x + (1.0 + gt) * y
    return _normalize_rows(r) * g + b


def _mixer_out_kernel(oa_ref, oc_ref, os_ref, ow_ref, ma_ref, mb_ref, x_ref, gt_ref, lg_ref, lb_ref,
                      wg_ref, wn_ref, wo_ref, o_ref, *, alpha):
    y_a = _dot(oa_ref[0], wg_ref[...])
    o_b = oc_ref[0].astype(F32) + os_ref[0].astype(F32) + ow_ref[0].astype(F32)
    y_b = _dot(o_b.astype(BF16), wn_ref[...])
    mixed = _sigmoid(ma_ref[0].astype(F32)) * y_a + _sigmoid(mb_ref[0].astype(F32)) * y_b
    y = _dot(mixed.astype(BF16), wo_ref[...])
    o_ref[0] = _post_norm(x_ref[0], y, gt_ref[0], lg_ref[...], lb_ref[...], alpha)


def _mixer_out(o_a, o_c, o_s, o_w, big, ma_blk, mb_blk, x, ada_l, gt_col, ln_g, ln_b, wg, wn, wo, alpha):
    B, S, D = x.shape
    tm = min(S, 512)
    row = lambda blk: pl.BlockSpec((1, tm, D), lambda b, i: (b, i, blk))
    full = lambda shape: pl.BlockSpec(shape, lambda b, i: (0,) * len(shape))
    return pl.pallas_call(
        functools.partial(_mixer_out_kernel, alpha=alpha),
        grid=(B, S // tm),
        in_specs=[row(0), row(0), row(0), row(0), row(ma_blk), row(mb_blk), row(0),
                  pl.BlockSpec((1, 1, D), lambda b, i: (b, 0, gt_col)),
                  full((1, D)), full((1, D)), full((D, D)), full((D, D)), full((D, D))],
        out_specs=row(0),
        out_shape=jax.ShapeDtypeStruct((B, S, D), F32),
        compiler_params=_cparams(2),
        name="mixer_out",
    )(o_a, o_c, o_s, o_w, big, big, x, ada_l, ln_g.reshape(1, D), ln_b.reshape(1, D), wg, wn, wo)


def _first_argmax(vals, lane, big):
    m = jnp.max(vals, axis=-1, keepdims=True)
    first = jnp.min(jnp.where(vals == m, lane, big), axis=-1, keepdims=True)
    return m, first


def _moe_router_kernel(x_ref, sh_ref, sc_ref, rwh_ref, rwl_ref, rb_ref, wsg_ref, wsu_ref, wsd_ref,
                       h_ref, idx_ref, wt_ref, shared_ref, rank_ref, cnt_ref):
    h = _normalize_rows(x_ref[0]) * (1.0 + sc_ref[0]) + sh_ref[0]
    h16 = h.astype(BF16)
    h_ref[0] = h16
    h_lo = (h - h16.astype(F32)).astype(BF16)
    rw_hi = rwh_ref[...]
    logits = _dot_nt(rw_hi, h16) + _dot_nt(rwl_ref[...], h16) + _dot_nt(rw_hi, h_lo)
    scores = _sigmoid(logits)
    biased = scores + rb_ref[...]
    E, tm = scores.shape
    NG = N_EXPERT_GROUPS
    per = E // NG
    b3 = biased.reshape(NG, per, tm)
    in_grp = lax.broadcasted_iota(jnp.int32, (NG, per, tm), 1).astype(F32)
    m1 = jnp.max(b3, axis=1, keepdims=True)
    first = jnp.min(jnp.where(b3 == m1, in_grp, float(per)), axis=1, keepdims=True)
    m2 = jnp.max(jnp.where(in_grp == first, -jnp.inf, b3), axis=1, keepdims=True)
    gscore = (m1 + m2).reshape(NG, tm)
    g_id = lax.broadcasted_iota(jnp.int32, (NG, tm), 0).astype(F32)
    allowed = jnp.zeros((NG, tm), F32)
    for _ in range(TOPK_GROUPS):
        m = jnp.max(gscore, axis=0, keepdims=True)
        first_g = jnp.min(jnp.where(gscore == m, g_id, float(NG)), axis=0, keepdims=True)
        pick = g_id == first_g
        allowed = jnp.where(pick, 1.0, allowed)
        gscore = jnp.where(pick, -jnp.inf, gscore)
    cand = jnp.where(allowed.reshape(NG, 1, tm) > 0.5, b3, NEG_INF).reshape(E, tm)
    e_id = lax.broadcasted_iota(jnp.int32, (E, tm), 0).astype(F32)
    idx_rows, wt_rows, picks = [], [], []
    wsum = jnp.zeros((1, tm), F32)
    chosen = jnp.zeros((E, tm), F32)
    for kk in range(TOP_K):
        m = jnp.max(cand, axis=0, keepdims=True)
        first_e = jnp.min(jnp.where(cand == m, e_id, float(E)), axis=0, keepdims=True)
        pick = e_id == first_e
        w = jnp.sum(jnp.where(pick, scores, 0.0), axis=0, keepdims=True)
        cand = jnp.where(pick, -jnp.inf, cand)
        chosen = jnp.where(pick, 1.0, chosen)
        idx_rows.append(first_e)
        wt_rows.append(w)
        picks.append(pick)
        wsum = wsum + w
    idx_ref[0] = jnp.concatenate(idx_rows, axis=0).astype(jnp.int32)
    wt_ref[0] = jnp.concatenate(wt_rows, axis=0) / wsum * ROUTED_SCALE
    @pl.when((pl.program_id(0) == 0) & (pl.program_id(1) == 0))
    def _():
        cnt_ref[...] = jnp.zeros_like(cnt_ref)

    si = lax.broadcasted_iota(jnp.int32, (tm, tm), 0)
    ti = lax.broadcasted_iota(jnp.int32, (tm, tm), 1)
    before = jnp.where(si < ti, 1.0, 0.0).astype(BF16)
    prior = _dot(chosen.astype(BF16), before) + cnt_ref[...]
    rank_ref[0] = jnp.concatenate([jnp.sum(jnp.where(pk, prior, 0.0), axis=0, keepdims=True) for pk in picks],
                                  axis=0).astype(jnp.int32)
    cnt_ref[...] = cnt_ref[...] + jnp.sum(chosen, axis=1, keepdims=True)
    act = _silu(_dot(h16, wsg_ref[...])) * _dot(h16, wsu_ref[...])
    shared_ref[0] = _dot(act.astype(BF16), wsd_ref[...])


def _moe_router(x, ada_l, sh_col, sc_col, rwt_hi, rwt_lo, router_b, wsg, wsu, wsd):
    B, S, D = x.shape
    E = rwt_hi.shape[0]
    FF = wsg.shape[1]
    tm = min(S, 512)
    row = lambda w: pl.BlockSpec((1, tm, w), lambda b, i: (b, i, 0))
    krow = pl.BlockSpec((1, TOP_K, tm), lambda b, i: (b, 0, i))
    full = lambda shape: pl.BlockSpec(shape, lambda b, i: (0,) * len(shape))
    return pl.pallas_call(
        _moe_router_kernel,
        grid=(B, S // tm),
        in_specs=[row(D),
                  pl.BlockSpec((1, 1, D), lambda b, i: (b, 0, sh_col)),
                  pl.BlockSpec((1, 1, D), lambda b, i: (b, 0, sc_col)),
                  full((E, D)), full((E, D)), full((E, 1)), full((D, FF)), full((D, FF)), full((FF, D))],
        out_specs=[row(D), krow, krow, row(D), krow, full((E, 1))],
        out_shape=[jax.ShapeDtypeStruct((B, S, D), BF16),
                   jax.ShapeDtypeStruct((B, TOP_K, S), jnp.int32),
                   jax.ShapeDtypeStruct((B, TOP_K, S), F32),
                   jax.ShapeDtypeStruct((B, S, D), F32),
                   jax.ShapeDtypeStruct((B, TOP_K, S), jnp.int32),
                   jax.ShapeDtypeStruct((E, 1), F32)],
        compiler_params=_cparams(2),
        name="moe_router",
    )(x, ada_l, ada_l, rwt_hi, rwt_lo, router_b.reshape(E, 1), wsg, wsu, wsd)


def _moe_experts_kernel(be_ref, nu_ref, xs_ref, wg_ref, wu_ref, wd_ref, o_ref):
    @pl.when(pl.program_id(0) < nu_ref[0])
    def _():
        xb = xs_ref[...]
        act = _silu(_dot(xb, wg_ref[0, 0].astype(BF16))) * _dot(xb, wu_ref[0, 0].astype(BF16))
        o_ref[...] = _dot(act.astype(BF16), wd_ref[0, 0].astype(BF16)).astype(o_ref.dtype)


def _moe_experts(block_expert, n_used, xs, layer, wg, wu, wd):
    rows, D = xs.shape
    BM = MOE_ROW_BLOCK
    FF = wg.shape[3]
    grid_spec = pltpu.PrefetchScalarGridSpec(
        num_scalar_prefetch=2,
        grid=(rows // BM,),
        in_specs=[pl.BlockSpec((BM, D), lambda i, be, nu: (i, 0)),
                  pl.BlockSpec((1, 1, D, FF), lambda i, be, nu: (layer, be[i], 0, 0)),
                  pl.BlockSpec((1, 1, D, FF), lambda i, be, nu: (layer, be[i], 0, 0)),
                  pl.BlockSpec((1, 1, FF, D), lambda i, be, nu: (layer, be[i], 0, 0))],
        out_specs=pl.BlockSpec((BM, D), lambda i, be, nu: (i, 0)),
    )
    return pl.pallas_call(
        _moe_experts_kernel,
        grid_spec=grid_spec,
        out_shape=jax.ShapeDtypeStruct((rows, D), BF16),
        compiler_params=_cparams(1),
        name="moe_experts",
    )(block_expert, n_used, xs, wg, wu, wd)


def _moe_out_kernel(x_ref, shared_ref, yg_ref, wt_ref, gt_ref, lg_ref, lb_ref, o_ref, *, alpha):
    D = x_ref.shape[2]
    y = shared_ref[0]
    wt = wt_ref[0]
    for kk in range(TOP_K):
        y = y + wt[:, kk:kk + 1] * yg_ref[kk, 0].astype(F32)
    o_ref[0] = _post_norm(x_ref[0], y, gt_ref[0], lg_ref[...], lb_ref[...], alpha)


def _moe_out(x, shared, yg, wt, ada_l, gt_col, ln_g, ln_b, alpha):
    B, S, D = x.shape
    tm = min(S, 256)
    row = lambda w: pl.BlockSpec((1, tm, w), lambda b, i: (b, i, 0))
    full = lambda shape: pl.BlockSpec(shape, lambda b, i: (0,) * len(shape))
    return pl.pallas_call(
        functools.partial(_moe_out_kernel, alpha=alpha),
        grid=(B, S // tm),
        in_specs=[row(D), row(D),
                  pl.BlockSpec((TOP_K, 1, tm, D), lambda b, i: (0, b, i, 0)),
                  row(wt.shape[2]),
                  pl.BlockSpec((1, 1, D), lambda b, i: (b, 0, gt_col)),
                  full((1, D)), full((1, D))],
        out_specs=row(D),
        out_shape=jax.ShapeDtypeStruct((B, S, D), F32),
        compiler_params=_cparams(2),
        name="moe_out",
    )(x, shared, yg, wt, ada_l, ln_g.reshape(1, D), ln_b.reshape(1, D))


def _moe_dest_kernel(start_ref, e_ref, r_ref, o_ref):
    e = e_ref[...]
    acc = r_ref[...]
    for j in range(start_ref.shape[0]):
        acc = acc + jnp.where(e == j, start_ref[j], 0)
    o_ref[...] = acc


def _routing_layout(expert_idx, rank, counts):
    B, K, S = expert_idx.shape
    T = B * S
    n_experts = counts.shape[0]
    BM = MOE_ROW_BLOCK
    TK = T * K
    counts = counts.reshape(n_experts).astype(jnp.int32)
    padded = (counts + BM - 1) // BM * BM
    pad_end = jnp.cumsum(padded)
    pad_start = (pad_end - padded).astype(jnp.int32)
    ts = min(S, 2048)
    spec = pl.BlockSpec((1, K, ts), lambda b, i, tab: (b, 0, i))
    dest = pl.pallas_call(
        _moe_dest_kernel,
        grid_spec=pltpu.PrefetchScalarGridSpec(num_scalar_prefetch=1, grid=(B, S // ts),
                                               in_specs=[spec, spec], out_specs=spec),
        out_shape=jax.ShapeDtypeStruct((B, K, S), jnp.int32),
        compiler_params=_cparams(2),
        name="moe_dest",
    )(pad_start, expert_idx, rank)
    dest = jnp.swapaxes(dest, 0, 1).reshape(TK)
    n_blocks = (TK + n_experts * (BM - 1) + BM - 1) // BM
    rows = n_blocks * BM
    token = jnp.tile(jnp.arange(T, dtype=jnp.int32), K)
    row_tok = (jnp.arange(rows, dtype=jnp.int32) % T).at[dest].add(token - dest % T, mode='promise_in_bounds')
    block_start = jnp.arange(n_blocks, dtype=jnp.int32) * BM
    block_expert = jnp.minimum(jnp.sum(pad_end[None, :] <= block_start[:, None], axis=1),
                               n_experts - 1).astype(jnp.int32)
    n_used = (pad_end[-1] // BM).astype(jnp.int32).reshape(1)
    return row_tok, dest, block_expert, n_used


def kernel(x, c, positions, ada_w, ada_b, w_in, gdn_conv_w, gdn_a_log, gdn_dt_bias, gdn_norm_w, w_gdn_branch, nsa_cmp_pos_k, nsa_cmp_pos_v, nsa_cmp_k_w1, nsa_cmp_k_w2, nsa_cmp_v_w1, nsa_cmp_v_w2, w_nsa_branch, w_out, ln1_g, ln1_b, router_w, router_bias, w_sh_gate, w_sh_up, w_sh_down, w_e_gate, w_e_up, w_e_down, ln2_g, ln2_b):
    B, S, D = x.shape
    L = ada_w.shape[0]
    T = B * S
    G = NSA_GROUPS
    nh_gdn = gdn_a_log.shape[1]
    gdn_w = nh_gdn * HEAD_DIM
    nsa_w = w_nsa_branch.shape[1]
    kvw = G * HEAD_DIM
    E = router_w.shape[2]
    alpha = (2.0 * L) ** 0.25
    assert gdn_w == D and nsa_w == D and S % NSA_SEL_LEN == 0

    splits = (3 * gdn_w, gdn_w, nh_gdn, nh_gdn, nsa_w, 6 * kvw, 3 * (nsa_w // HEAD_DIM), D, D)
    offs = [0]
    for s_ in splits:
        offs.append(offs[-1] + s_)
    seg = lambda i: slice(offs[i], offs[i + 1])
    w_big = jnp.concatenate([w_in[:, :, seg(0)], w_in[:, :, seg(1)], w_in[:, :, seg(4)],
                             w_in[:, :, seg(7)], w_in[:, :, seg(8)], w_in[:, :, seg(5)]], axis=-1).astype(BF16)
    n_small = splits[2] + splits[3] + splits[6]
    w_small = jnp.concatenate([w_in[:, :, seg(2)], w_in[:, :, seg(3)], w_in[:, :, seg(6)],
                               jnp.zeros((L, D, 128 - n_small), w_in.dtype)], axis=-1).astype(BF16)
    Z_BLK, Q_BLK, MA_BLK, MB_BLK = 3, 4, 5, 6
    kv0 = 7 * D
    n_big = w_big.shape[2]
    tn_big = n_big // 4 if (n_big // 4) % 128 == 0 else 128

    wgb = w_gdn_branch.astype(BF16)
    wnb = w_nsa_branch.astype(BF16)
    wob = w_out.astype(BF16)
    wsg = w_sh_gate.astype(BF16)
    wsu = w_sh_up.astype(BF16)
    wsd = w_sh_down.astype(BF16)
    rwt = jnp.swapaxes(router_w, 1, 2)
    rwt_hi = rwt.astype(BF16)
    rwt_lo = (rwt - rwt_hi.astype(F32)).astype(BF16)
    cmp_w1 = jnp.stack([nsa_cmp_k_w1, nsa_cmp_v_w1], axis=1).astype(BF16)
    cmp_w2 = jnp.stack([nsa_cmp_k_w2, nsa_cmp_v_w2], axis=1).astype(BF16)
    cmp_pos = jnp.stack([nsa_cmp_pos_k, nsa_cmp_pos_v], axis=1).reshape(L, 2, 1, NSA_CMP_LEN * HEAD_DIM)

    ada = _ada(c, ada_w, ada_b)
    cosf, sinf = _rope_tables(positions)
    NC = S // NSA_CMP_STRIDE
    last = jnp.minimum(jnp.arange(NC) * NSA_CMP_STRIDE + NSA_CMP_LEN - 1, S - 1)
    cosc = cosf[:, last]
    sinc = sinf[:, last]
    n_sel = -(-(S // NSA_SEL_LEN) // 128) * 128

    for l in range(L):
        ada_l = ada[l].reshape(B, 1, 6 * D)
        big, small = _proj(x, ada_l, 0, 1, w_big[l], w_small[l], tn_big)
        small_t = jnp.swapaxes(small[:, :, nh_gdn:2 * nh_gdn].reshape(B, S // GDN_CHUNK, GDN_CHUNK, nh_gdn), 2, 3)
        q_a, k_a, v_a = _gdn_prep(big, gdn_conv_w[l], gdn_w)
        o_a = _gdn(q_a, k_a, v_a, big, Z_BLK, small, small_t, gdn_a_log[l], gdn_dt_bias[l], gdn_norm_w[l])
        q_r, ksel_r, kwin_r = _rope_apply(big, Q_BLK, (kv0 + 2 * kvw) // kvw, (kv0 + 4 * kvw) // kvw,
                                          nsa_w, kvw, cosf, sinf)
        hb = big[:, :, kv0:kv0 + 2 * kvw].reshape(B, NC, NSA_CMP_STRIDE, 2 * G, HEAD_DIM)
        hb = hb.transpose(0, 3, 1, 2, 4).reshape(B, 2 * G, NC, NSA_CMP_STRIDE * HEAD_DIM)
        kvc = _compress(hb, cmp_pos[l], cmp_w1[l], cmp_w2[l], cosc, sinc)
        o_c, sel = _cmp_attn(q_r, kvc, small, nh_gdn, n_sel)
        o_s = _sel_attn(q_r, ksel_r, big, (kv0 + 3 * kvw) // HEAD_DIM, sel, small, nh_gdn)
        o_w = _win_attn(q_r, kwin_r, big, (kv0 + 5 * kvw) // HEAD_DIM, small, nh_gdn)
        x = _mixer_out(o_a, o_c, o_s, o_w, big, MA_BLK, MB_BLK, x, ada_l, 2, ln1_g[l], ln1_b[l],
                       wgb[l], wnb[l], wob[l], alpha)
        parts = []
        n_parts = MOE_BATCH_PARTS if B % MOE_BATCH_PARTS == 0 else 1
        bp = B // n_parts
        for part in range(n_parts):
            xp = x[part * bp:(part + 1) * bp]
            ap = ada_l[part * bp:(part + 1) * bp]
            h2, eidx, ewt, shared, rank, counts = _moe_router(xp, ap, 3, 4, rwt_hi[l], rwt_lo[l], router_bias[l],
                                                              wsg[l], wsu[l], wsd[l])
            row_tok, dest, block_expert, n_used = _routing_layout(eidx, rank, counts)
            xs = h2.reshape(bp * S, D).at[row_tok].get(mode='promise_in_bounds')
            yb = _moe_experts(block_expert, n_used, xs, l, w_e_gate, w_e_up, w_e_down)
            yg = yb.at[dest].get(mode='promise_in_bounds').reshape(TOP_K, bp, S, D)
            parts.append(_moe_out(xp, shared, yg, jnp.swapaxes(ewt, 1, 2), ap, 5, ln2_g[l], ln2_b[l], alpha))
        x = jnp.concatenate(parts, axis=0) if len(parts) > 1 else parts[0]
    return x
```

```python
import functools
import math

import jax
import jax.numpy as jnp
from jax import lax
from jax.experimental import pallas as pl
from jax.experimental.pallas import tpu as pltpu

F32 = jnp.float32
BF16 = jnp.bfloat16

HEAD_DIM = 128
GDN_CONV = 4
GDN_CHUNK = 64
NSA_GROUPS = 2
NSA_CMP_LEN = 32
NSA_CMP_STRIDE = 16
NSA_SEL_LEN = 64
NSA_SEL_TOP = 16
NSA_WINDOW = 512
ROPE_THETA = 10000.0
N_EXPERTS = 64
N_EXPERT_GROUPS = 8
TOPK_GROUPS = 4
TOP_K = 8
ROUTED_SCALE = 2.5
LN_EPS = 1e-5
NEG_INF = -1e30
SEL_FORCE = 1e6
MOE_ROW_BLOCK = 512
Q_SCALE_LOG2E = (HEAD_DIM ** -0.5) * math.log2(math.e)
MASK_BIG = 2.0 ** 100
N_ROW_PARTS = 2
GDN_PHASE1_CHUNKS = 4
WIN_SUB_TILE = 128
_ARB = "arbitrary"


def _cparams(n_axes):
    return pltpu.CompilerParams(dimension_semantics=(_ARB,) * n_axes)


def _sigmoid(x):
    return 1.0 / (1.0 + jnp.exp(-x))


def _silu(x):
    return x * _sigmoid(x)


def _dot(a, b):
    return jnp.dot(a, b, preferred_element_type=F32)


def _dot_nt(a, b):
    return lax.dot_general(a, b, (((1,), (1,)), ((), ())), preferred_element_type=F32)


def _dot_tn(a, b):
    return lax.dot_general(a, b, (((0,), (0,)), ((), ())), preferred_element_type=F32)


def _tile_lanes(x, n):
    return jnp.concatenate([x] * n, axis=1)


def _split3(x):
    x0 = x.astype(BF16)
    r1 = x - x0.astype(F32)
    x1 = r1.astype(BF16)
    x2 = (r1 - x1.astype(F32)).astype(BF16)
    return x0, x1, x2


def _dot_sel_right(x, sel_bf16):
    x0, x1, x2 = _split3(x)
    return _dot(x0, sel_bf16) + _dot(x1, sel_bf16) + _dot(x2, sel_bf16)


def _dot_sel_left(sel_bf16, x):
    x0, x1, x2 = _split3(x)
    return _dot(sel_bf16, x0) + _dot(sel_bf16, x1) + _dot(sel_bf16, x2)


def _normalize_rows(x):
    mu = jnp.mean(x, axis=-1, keepdims=True)
    xc = x - mu
    var = jnp.mean(xc * xc, axis=-1, keepdims=True)
    return xc * lax.rsqrt(var + LN_EPS)


def _ada_kernel(c_ref, w_ref, b_ref, o_ref):
    cond = _silu(c_ref[...])
    o_ref[0] = jnp.dot(cond, w_ref[0], preferred_element_type=F32,
                       precision=lax.Precision.HIGHEST) + b_ref[0]


def _ada(c, ada_w, ada_b):
    L, D, N = ada_w.shape
    B = c.shape[0]
    tn = min(N, 1536)
    return pl.pallas_call(
        _ada_kernel,
        grid=(L, N // tn),
        in_specs=[pl.BlockSpec((B, D), lambda l, j: (0, 0)),
                  pl.BlockSpec((1, D, tn), lambda l, j: (l, 0, j)),
                  pl.BlockSpec((1, 1, tn), lambda l, j: (l, 0, j))],
        out_specs=pl.BlockSpec((1, B, tn), lambda l, j: (l, 0, j)),
        out_shape=jax.ShapeDtypeStruct((L, B, N), F32),
        compiler_params=_cparams(2),
        name="ada",
    )(c, ada_w, ada_b.reshape(L, 1, N))


def _proj_kernel(x_ref, sh_ref, sc_ref, w_ref, ws_ref, o_ref, os_ref, h_ref):
    @pl.when(pl.program_id(2) == 0)
    def _():
        h = _normalize_rows(x_ref[0]) * (1.0 + sc_ref[0]) + sh_ref[0]
        h_ref[...] = h.astype(BF16)
        os_ref[0] = _dot(h_ref[...], ws_ref[...])

    o_ref[0] = _dot(h_ref[...], w_ref[...]).astype(o_ref.dtype)


def _proj(x, ada_l, sh_col, sc_col, w, w_small, tn):
    B, S, D = x.shape
    N = w.shape[1]
    NS = w_small.shape[1]
    tm = min(S, 1024)
    return pl.pallas_call(
        _proj_kernel,
        grid=(B, S // tm, N // tn),
        in_specs=[pl.BlockSpec((1, tm, D), lambda b, i, j: (b, i, 0)),
                  pl.BlockSpec((1, 1, D), lambda b, i, j: (b, 0, sh_col)),
                  pl.BlockSpec((1, 1, D), lambda b, i, j: (b, 0, sc_col)),
                  pl.BlockSpec((D, tn), lambda b, i, j: (0, j)),
                  pl.BlockSpec((D, NS), lambda b, i, j: (0, 0))],
        out_specs=[pl.BlockSpec((1, tm, tn), lambda b, i, j: (b, i, j)),
                   pl.BlockSpec((1, tm, NS), lambda b, i, j: (b, i, 0))],
        out_shape=[jax.ShapeDtypeStruct((B, S, N), BF16),
                   jax.ShapeDtypeStruct((B, S, NS), F32)],
        scratch_shapes=[pltpu.VMEM((tm, D), BF16)],
        compiler_params=_cparams(3),
        name="proj",
    )(x, ada_l, ada_l, w, w_small)


def _gdn_prep_kernel(x_ref, w_ref, q_ref, k_ref, v_ref, carry_ref):
    ts = x_ref.shape[1]
    width = q_ref.shape[2]
    nh = width // HEAD_DIM

    @pl.when(pl.program_id(1) == 0)
    def _():
        carry_ref[...] = jnp.zeros_like(carry_ref)

    for part, o_ref in enumerate((q_ref, k_ref, v_ref)):
        for h in range(nh):
            c0 = part * width + h * HEAD_DIM
            cols = slice(c0, c0 + HEAD_DIM)
            xx = jnp.concatenate([carry_ref[:, cols], x_ref[0, :, cols].astype(F32)], axis=0)
            w = w_ref[:, cols]
            y = xx[8:8 + ts] * w[3:4]
            for kk in range(GDN_CONV - 1):
                off = 8 - (GDN_CONV - 1) + kk
                y = y + xx[off:off + ts] * w[kk:kk + 1]
            y = _silu(y)
            if part < 2:
                y = y * lax.rsqrt(jnp.sum(y * y, axis=-1, keepdims=True) + 1e-6)
            if part == 0:
                y = y * (HEAD_DIM ** -0.5)
            o_ref[0, :, h * HEAD_DIM:(h + 1) * HEAD_DIM] = y.astype(o_ref.dtype)
    carry_ref[...] = x_ref[0, ts - 8:ts, :].astype(F32)


def _gdn_prep(big, conv_w, width):
    B, S, _ = big.shape
    ts = min(S, 512)
    out = jax.ShapeDtypeStruct((B, S, width), BF16)
    ospec = pl.BlockSpec((1, ts, width), lambda b, i: (b, i, 0))
    return pl.pallas_call(
        _gdn_prep_kernel,
        grid=(B, S // ts),
        in_specs=[pl.BlockSpec((1, ts, 3 * width), lambda b, i: (b, i, 0)),
                  pl.BlockSpec((GDN_CONV, 3 * width), lambda b, i: (0, 0))],
        out_specs=[ospec, ospec, ospec],
        out_shape=[out, out, out],
        scratch_shapes=[pltpu.VMEM((8, 3 * width), F32)],
        compiler_params=_cparams(2),
        name="gdn_prep",
    )(big, conv_w)


def _softplus(x):
    return jnp.maximum(x, 0.0) + jnp.log(1.0 + jnp.exp(-jnp.abs(x)))


def _gdn_kernel(q_ref, k_ref, v_ref, z_ref, sm_ref, smt_ref, alog_ref, alogt_ref, dtb_ref, dtbt_ref,
                nw_ref, o_ref, state_ref, u_s, wq_s, attn_s, kd_s, dec_s):
    ts = q_ref.shape[1]
    nh = q_ref.shape[2] // HEAD_DIM
    C = GDN_CHUNK
    R = 2 * C
    npair = nh // 2
    nchunks = ts // C

    @pl.when(pl.program_id(1) == 0)
    def _():
        state_ref[...] = jnp.zeros_like(state_ref)

    ci = lax.broadcasted_iota(jnp.int32, (C, C), 0)
    cj = lax.broadcasted_iota(jnp.int32, (C, C), 1)
    tril = jnp.where(ci >= cj, 1.0, 0.0).astype(BF16)
    triu = jnp.where(cj >= ci, 1.0, 0.0).astype(BF16)
    ii = lax.broadcasted_iota(jnp.int32, (R, R), 0)
    jj = lax.broadcasted_iota(jnp.int32, (R, R), 1)
    same_head = (ii // C) == (jj // C)
    incl = same_head & (ii >= jj)
    strict = same_head & (ii > jj)
    eye = jnp.where(ii == jj, 1.0, 0.0)
    nw = nw_ref[...]

    def pair_rows(ref, rows, p):
        return jnp.concatenate([ref[0, rows, (2 * p) * HEAD_DIM:(2 * p + 1) * HEAD_DIM],
                                ref[0, rows, (2 * p + 1) * HEAD_DIM:(2 * p + 2) * HEAD_DIM]], axis=0)

    def pair_col(x, p):
        return jnp.concatenate([x[:, 2 * p:2 * p + 1], x[:, 2 * p + 1:2 * p + 2]], axis=0)

    def phase1(j, carry):
        cs = [j * GDN_PHASE1_CHUNKS + d for d in range(GDN_PHASE1_CHUNKS)]
        rows_c, beta_c, gc_c, gcr_c, glb_c = [], [], [], [], []
        for c in cs:
            rows = pl.ds(pl.multiple_of(c * C, C), C)
            sm = sm_ref[0, rows, :]
            g = -jnp.exp(alog_ref[...]) * _softplus(sm[:, nh:2 * nh] + dtb_ref[...])
            gt = -jnp.exp(alogt_ref[...]) * _softplus(smt_ref[0, c] + dtbt_ref[...])
            gc = _dot_sel_left(tril, g)
            g_last = gc[C - 1:C, :]
            dec_s[c] = jnp.exp(g_last)
            rows_c.append(rows)
            beta_c.append(_sigmoid(sm[:, 0:nh]))
            gc_c.append(gc)
            gcr_c.append(_dot_sel_right(gt, triu))
            glb_c.append(jnp.broadcast_to(g_last, (C, nh)))
        items = [(ci, p) for ci in range(len(cs)) for p in range(npair)]
        idx = range(len(items))
        q2 = [pair_rows(q_ref, rows_c[ci], p).astype(F32) for ci, p in items]
        k2 = [pair_rows(k_ref, rows_c[ci], p).astype(F32) for ci, p in items]
        b2 = [pair_col(beta_c[ci], p) for ci, p in items]
        g_col = [pair_col(gc_c[ci], p) for ci, p in items]
        kb = [k2[i] * b2[i] for i in idx]
        kq = [_dot_nt(jnp.concatenate([kb[i], q2[i]], axis=0).astype(BF16), k2[i].astype(BF16))
              for i in idx]
        decay = []
        for i, (ci, p) in enumerate(items):
            gcr = gcr_c[ci]
            g_row = jnp.concatenate([gcr[2 * p:2 * p + 1, :], gcr[2 * p + 1:2 * p + 2, :]], axis=1)
            decay.append(jnp.where(incl, jnp.exp(jnp.where(incl, g_col[i] - g_row, 0.0)), 0.0))
        a = [jnp.where(strict, kq[i][:R] * decay[i], 0.0) for i in idx]
        for i, (ci, p) in enumerate(items):
            attn_s[cs[ci], p] = (kq[i][R:] * decay[i]).astype(BF16)
        x = [eye - a[i] for i in idx]
        pw = a
        n = 2
        while n < C:
            pw16 = [pw[i].astype(BF16) for i in idx]
            pw = [_dot(pw16[i], pw16[i]) for i in idx]
            x = [x[i] + _dot(x[i].astype(BF16), pw[i].astype(BF16)) for i in idx]
            n *= 2
        eg = [jnp.exp(g_col[i]) for i in idx]
        sol = []
        for i, (ci, p) in enumerate(items):
            v2 = pair_rows(v_ref, rows_c[ci], p).astype(F32)
            r = jnp.concatenate([v2 * b2[i], kb[i] * eg[i]], axis=1)
            sol.append(_dot(x[i].astype(BF16), r.astype(BF16)))
        for i, (ci, p) in enumerate(items):
            c = cs[ci]
            u_s[c, p] = sol[i][:, :HEAD_DIM]
            w = sol[i][:, HEAD_DIM:]
            qg = q2[i] * eg[i]
            for e in range(2):
                wq_s[c, 2 * p + e] = jnp.concatenate([w[e * C:(e + 1) * C], qg[e * C:(e + 1) * C]],
                                                     axis=0).astype(BF16)
            kd_s[c, p] = (k2[i] * jnp.exp(pair_col(glb_c[ci], p) - g_col[i])).astype(BF16)
        return carry

    assert nchunks % GDN_PHASE1_CHUNKS == 0
    lax.fori_loop(0, nchunks // GDN_PHASE1_CHUNKS, phase1, 0)

    def phase2(c, carry):
        rows = pl.ds(pl.multiple_of(c * C, C), C)
        dec = dec_s[c]
        res = [_dot(wq_s[c, h], state_ref[h].astype(BF16)) for h in range(nh)]
        for p in range(npair):
            ws = jnp.concatenate([res[2 * p][:C], res[2 * p + 1][:C]], axis=0)
            qs = jnp.concatenate([res[2 * p][C:], res[2 * p + 1][C:]], axis=0)
            v_new = u_s[c, p] - ws
            v16 = v_new.astype(BF16)
            o2 = qs + _dot(attn_s[c, p], v16)
            kd = kd_s[c, p]
            for e in range(2):
                h = 2 * p + e
                part = slice(e * C, (e + 1) * C)
                state_ref[h] = state_ref[h] * dec[:, h:h + 1] + _dot_tn(kd[part], v16[part])
                o = o2[part]
                o = o * lax.rsqrt(jnp.mean(o * o, axis=-1, keepdims=True) + 1e-6) * nw
                cols = slice(h * HEAD_DIM, (h + 1) * HEAD_DIM)
                z = z_ref[0, rows, cols].astype(F32)
                o_ref[0, rows, cols] = (o * _silu(z)).astype(o_ref.dtype)
        return carry

    lax.fori_loop(0, nchunks, phase2, 0)


def _gdn(q, k, v, big, z_blk, small, small_t, a_log, dt_bias, norm_w):
    B, S, W = q.shape
    nh = W // HEAD_DIM
    assert nh % 2 == 0 and 2 * GDN_CHUNK == HEAD_DIM
    ts = min(S, 512)
    nc = ts // GDN_CHUNK
    spec = pl.BlockSpec((1, ts, W), lambda b, i: (b, i, 0))
    full = lambda shape: pl.BlockSpec(shape, lambda b, i: (0,) * len(shape))
    return pl.pallas_call(
        _gdn_kernel,
        grid=(B, S // ts),
        in_specs=[spec, spec, spec,
                  pl.BlockSpec((1, ts, W), lambda b, i: (b, i, z_blk)),
                  pl.BlockSpec((1, ts, small.shape[2]), lambda b, i: (b, i, 0)),
                  pl.BlockSpec((1, ts // GDN_CHUNK, nh, GDN_CHUNK), lambda b, i: (b, i, 0, 0)),
                  full((1, nh)), full((nh, 1)), full((1, nh)), full((nh, 1)),
                  full((1, HEAD_DIM))],
        out_specs=spec,
        out_shape=jax.ShapeDtypeStruct((B, S, W), BF16),
        scratch_shapes=[pltpu.VMEM((nh, HEAD_DIM, HEAD_DIM), F32),
                        pltpu.VMEM((nc, nh // 2, 2 * GDN_CHUNK, HEAD_DIM), F32),
                        pltpu.VMEM((nc, nh, 2 * GDN_CHUNK, HEAD_DIM), BF16),
                        pltpu.VMEM((nc, nh // 2, 2 * GDN_CHUNK, 2 * GDN_CHUNK), BF16),
                        pltpu.VMEM((nc, nh // 2, 2 * GDN_CHUNK, HEAD_DIM), BF16),
                        pltpu.VMEM((nc, 1, nh), F32)],
        compiler_params=_cparams(2),
        name="gdn",
    )(q, k, v, big, small, small_t, a_log.reshape(1, nh), a_log.reshape(nh, 1),
      dt_bias.reshape(1, nh), dt_bias.reshape(nh, 1), norm_w.reshape(1, HEAD_DIM))


def _rope_table_kernel(pos_ref, invf_ref, sign_ref, cos_ref, sin_ref):
    ang = pos_ref[0].astype(F32) * invf_ref[...]
    cos_ref[0] = jnp.cos(ang)
    sin_ref[0] = jnp.sin(ang) * sign_ref[...]


def _rope_tables(positions):
    B, S = positions.shape
    half = HEAD_DIM // 2
    inv = ROPE_THETA ** (-jnp.arange(half, dtype=F32) / half)
    invf = jnp.concatenate([inv, inv]).reshape(1, HEAD_DIM)
    sign = jnp.concatenate([-jnp.ones((half,), F32), jnp.ones((half,), F32)]).reshape(1, HEAD_DIM)
    ts = min(S, 1024)
    out = jax.ShapeDtypeStruct((B, S, HEAD_DIM), F32)
    ospec = pl.BlockSpec((1, ts, HEAD_DIM), lambda b, i: (b, i, 0))
    return pl.pallas_call(
        _rope_table_kernel,
        grid=(B, S // ts),
        in_specs=[pl.BlockSpec((1, ts, 1), lambda b, i: (b, i, 0)),
                  pl.BlockSpec((1, HEAD_DIM), lambda b, i: (0, 0)),
                  pl.BlockSpec((1, HEAD_DIM), lambda b, i: (0, 0))],
        out_specs=[ospec, ospec],
        out_shape=[out, out],
        compiler_params=_cparams(2),
        name="rope_table",
    )(positions.reshape(B, S, 1), invf, sign)


def _rope_rows(x, cosf, sinf):
    return x * cosf + pltpu.roll(x, HEAD_DIM // 2, 1) * sinf


def _rope_apply_kernel(q_ref, ks_ref, kw_ref, cos_ref, sin_ref, qo_ref, kso_ref, kwo_ref):
    cosf = cos_ref[0]
    sinf = sin_ref[0]
    for src, dst, mult in ((q_ref, qo_ref, Q_SCALE_LOG2E), (ks_ref, kso_ref, None), (kw_ref, kwo_ref, None)):
        for h in range(src.shape[2] // HEAD_DIM):
            cols = slice(h * HEAD_DIM, (h + 1) * HEAD_DIM)
            r = _rope_rows(src[0, :, cols].astype(F32), cosf, sinf)
            if mult is not None:
                r = r * mult
            dst[0, :, cols] = r.astype(dst.dtype)


def _rope_apply(big, q_blk, ksel_blk, kwin_blk, wq, wkv, cosf, sinf):
    B, S, _ = big.shape
    ts = min(S, 512)
    tab = pl.BlockSpec((1, ts, HEAD_DIM), lambda b, i: (b, i, 0))
    return pl.pallas_call(
        _rope_apply_kernel,
        grid=(B, S // ts),
        in_specs=[pl.BlockSpec((1, ts, wq), lambda b, i: (b, i, q_blk)),
                  pl.BlockSpec((1, ts, wkv), lambda b, i: (b, i, ksel_blk)),
                  pl.BlockSpec((1, ts, wkv), lambda b, i: (b, i, kwin_blk)),
                  tab, tab],
        out_specs=[pl.BlockSpec((1, ts, wq), lambda b, i: (b, i, 0)),
                   pl.BlockSpec((1, ts, wkv), lambda b, i: (b, i, 0)),
                   pl.BlockSpec((1, ts, wkv), lambda b, i: (b, i, 0))],
        out_shape=[jax.ShapeDtypeStruct((B, S, wq), BF16),
                   jax.ShapeDtypeStruct((B, S, wkv), BF16),
                   jax.ShapeDtypeStruct((B, S, wkv), BF16)],
        compiler_params=_cparams(2),
        name="rope_apply",
    )(big, big, big, cosf, sinf)


def _compress_kernel(hb_ref, pos_ref, w1_ref, w2_ref, cos_ref, sin_ref, o_ref):
    hb = hb_ref[0, 0]
    w1 = w1_ref[0]
    half = hb.shape[1]
    p0 = _dot(hb, w1[:half])
    p1 = _dot(hb, w1[half:])
    nc = p0.shape[0]
    pos8 = jnp.broadcast_to(pos_ref[0], (8, 2 * half)).astype(BF16)
    pb = _dot(pos8, w1)[0:1]
    pre = p0 + pltpu.roll(p1, nc - 1, 0) + pb
    out = _dot(_silu(pre).astype(BF16), w2_ref[0])
    roped = _rope_rows(out, cos_ref[0], sin_ref[0])
    is_key = pl.program_id(1) < NSA_GROUPS
    o_ref[0, 0] = jnp.where(is_key, roped, out).astype(o_ref.dtype)


def _compress(hb, pos_flat, w1, w2, cosc, sinc):
    B, J, NC, HW = hb.shape
    G = NSA_GROUPS
    return pl.pallas_call(
        _compress_kernel,
        grid=(B, J),
        in_specs=[pl.BlockSpec((1, 1, NC, HW), lambda b, j: (b, j, 0, 0)),
                  pl.BlockSpec((1, 1, 2 * HW), lambda b, j: (j // G, 0, 0)),
                  pl.BlockSpec((1, 2 * HW, HEAD_DIM), lambda b, j: (j // G, 0, 0)),
                  pl.BlockSpec((1, HEAD_DIM, HEAD_DIM), lambda b, j: (j // G, 0, 0)),
                  pl.BlockSpec((1, NC, HEAD_DIM), lambda b, j: (b, 0, 0)),
                  pl.BlockSpec((1, NC, HEAD_DIM), lambda b, j: (b, 0, 0))],
        out_specs=pl.BlockSpec((1, 1, NC, HEAD_DIM), lambda b, j: (b, j, 0, 0)),
        out_shape=jax.ShapeDtypeStruct((B, J, NC, HEAD_DIM), BF16),
        compiler_params=_cparams(2),
        name="nsa_compress",
    )(hb, pos_flat, w1, w2, cosc, sinc)


def _gate_col(g_ref, nh_gdn, group, hpg, h, branch):
    col = None
    for gi in range(NSA_GROUPS):
        c = 2 * nh_gdn + (gi * hpg + h) * 3 + branch
        cand = g_ref[0, :, c:c + 1]
        col = cand if col is None else jnp.where(group == gi, cand, col)
    return _sigmoid(col)


def _cmp_attn_kernel(q_ref, kc_ref, vc_ref, sm_ref, o_ref, sel_ref, *, nh_gdn, hpg):
    tq = q_ref.shape[1]
    nc = kc_ref.shape[2]
    nb = sel_ref.shape[3]
    g = pl.program_id(1)
    t0 = pl.program_id(2) * tq
    t_nc = t0 + lax.broadcasted_iota(jnp.int32, (tq, nc), 0)
    n_nc = lax.broadcasted_iota(jnp.int32, (tq, nc), 1)
    valid = (n_nc * NSA_CMP_STRIDE + (NSA_CMP_LEN - 1)) <= t_nc
    t_col = t0 + lax.broadcasted_iota(jnp.int32, (tq, 1), 0)
    has_valid = jnp.where(t_col >= NSA_CMP_LEN - 1, 1.0, 0.0)
    kc = kc_ref[0, 0]
    vc = vc_ref[0, 0]
    psum = jnp.zeros((tq, nc), F32)
    for h in range(hpg):
        cols = slice(h * HEAD_DIM, (h + 1) * HEAD_DIM)
        s = jnp.where(valid, _dot_nt(q_ref[0, :, cols], kc), NEG_INF)
        e = jnp.exp2(s - jnp.max(s, axis=-1, keepdims=True))
        p = e * (has_valid / jnp.sum(e, axis=-1, keepdims=True))
        gate = _gate_col(sm_ref, nh_gdn, g, hpg, h, 0)
        o_ref[0, :, cols] = (_dot(p.astype(BF16), vc) * gate).astype(o_ref.dtype)
        psum = psum + p
    sj = lax.broadcasted_iota(jnp.int32, (nb, nc), 0) * NSA_SEL_LEN
    cn = lax.broadcasted_iota(jnp.int32, (nb, nc), 1) * NSA_CMP_STRIDE
    overlap_t = jnp.where((cn <= sj + (NSA_SEL_LEN - 1)) & (cn + (NSA_CMP_LEN - 1) >= sj), 1.0, 0.0).astype(BF16)
    p0, p1, p2 = _split3(psum)
    imp = _dot_nt(overlap_t, p0) + _dot_nt(overlap_t, p1) + _dot_nt(overlap_t, p2)
    t_nb = t0 + lax.broadcasted_iota(jnp.int32, (nb, tq), 1)
    blk = lax.broadcasted_iota(jnp.int32, (nb, tq), 0)
    cur = t_nb // NSA_SEL_LEN
    forced = (blk == 0) | (blk == cur) | (blk == cur - 1)
    score = jnp.where(forced, SEL_FORCE, jnp.where(blk * NSA_SEL_LEN <= t_nb, imp, -1.0))
    sel = jnp.zeros((nb, tq), F32)
    blk_f = blk.astype(F32)
    for _ in range(min(NSA_SEL_TOP, nb)):
        m = jnp.max(score, axis=0, keepdims=True)
        first = jnp.min(jnp.where(score == m, blk_f, float(nb)), axis=0, keepdims=True)
        pick = blk_f == first
        sel = jnp.where(pick, 1.0, sel)
        score = jnp.where(pick, -jnp.inf, score)
    sel_ref[0, 0] = sel.T.astype(sel_ref.dtype)


def _cmp_attn(q_r, kvc, small, nh_gdn, n_sel):
    B, S, WQ = q_r.shape
    G = NSA_GROUPS
    hpg = WQ // HEAD_DIM // G
    NC = kvc.shape[2]
    tq = min(S, 512)
    return pl.pallas_call(
        functools.partial(_cmp_attn_kernel, nh_gdn=nh_gdn, hpg=hpg),
        grid=(B, G, S // tq),
        in_specs=[pl.BlockSpec((1, tq, hpg * HEAD_DIM), lambda b, g, i: (b, i, g)),
                  pl.BlockSpec((1, 1, NC, HEAD_DIM), lambda b, g, i: (b, g, 0, 0)),
                  pl.BlockSpec((1, 1, NC, HEAD_DIM), lambda b, g, i: (b, G + g, 0, 0)),
                  pl.BlockSpec((1, tq, small.shape[2]), lambda b, g, i: (b, i, 0))],
        out_specs=[pl.BlockSpec((1, tq, hpg * HEAD_DIM), lambda b, g, i: (b, i, g)),
                   pl.BlockSpec((1, 1, tq, n_sel), lambda b, g, i: (b, g, i, 0))],
        out_shape=[jax.ShapeDtypeStruct((B, S, WQ), BF16),
                   jax.ShapeDtypeStruct((B, G, S, n_sel), BF16)],
        compiler_params=_cparams(3),
        name="nsa_cmp_attn",
    )(q_r, kvc, kvc, small)


def _sel_attn_kernel(q_ref, k_ref, ext_ref, v_ref, sel_ref, sm_ref, o_ref, qx_ref, m_ref, acc_ref, p_ref, a_ref,
                     *, nh_gdn, hpg, tk):
    tq = q_ref.shape[1]
    nb = sel_ref.shape[3]
    g = pl.program_id(1)
    t0 = pl.program_id(2) * tq
    assert tk % tq == 0
    unsel = (sel_ref[0, 0].astype(F32) - 1.0).astype(BF16)
    for h in range(hpg):
        qx_ref[h * tq:(h + 1) * tq, 0:HEAD_DIM] = q_ref[0, :, h * HEAD_DIM:(h + 1) * HEAD_DIM]
        qx_ref[h * tq:(h + 1) * tq, HEAD_DIM:HEAD_DIM + nb] = unsel
    m_ref[...] = jnp.full_like(m_ref, NEG_INF)
    acc_ref[...] = jnp.zeros_like(acc_ref)
    p_ref[...] = jnp.zeros_like(p_ref)
    a_ref[...] = jnp.ones_like(a_ref)
    ones = jnp.ones((tk, HEAD_DIM), BF16)
    pr = hpg * tq // N_ROW_PARTS
    parts = [slice(part * pr, (part + 1) * pr) for part in range(N_ROW_PARTS)]

    def apply_values(kt):
        k0 = pl.multiple_of(kt * tk, tk)
        vx = jnp.concatenate([v_ref[0, pl.ds(k0, tk), :], ones], axis=1)
        for rows in parts:
            acc_ref[rows, :] = _tile_lanes(a_ref[rows, :], 2) * acc_ref[rows, :] + _dot(p_ref[rows, :], vx)

    def scores(kt, causal):
        k0 = pl.multiple_of(kt * tk, tk)
        kx = jnp.concatenate([k_ref[0, pl.ds(k0, tk), :], ext_ref[pl.ds(k0, tk), :]], axis=1)
        if causal:
            t_row = t0 + lax.broadcasted_iota(jnp.int32, (tq, tk), 0)
            key = k0 + lax.broadcasted_iota(jnp.int32, (tq, tk), 1)
            keep = jnp.concatenate([key <= t_row] * (hpg // N_ROW_PARTS), axis=0)
        for rows in parts:
            s = _dot_nt(qx_ref[rows, :], kx)
            if causal:
                s = jnp.where(keep, s, NEG_INF)
            m_old = m_ref[rows, :]
            m_new = jnp.maximum(m_old, jnp.max(s, axis=-1, keepdims=True))
            a_ref[rows, :] = jnp.exp2(m_old - m_new)
            p_ref[rows, :] = jnp.exp2(s - _tile_lanes(m_new, tk // HEAD_DIM)).astype(BF16)
            m_ref[rows, :] = m_new

    kt_diag = t0 // tk

    def body(kt, carry):
        apply_values(jnp.maximum(kt - 1, 0))
        scores(kt, False)
        return carry

    lax.fori_loop(0, kt_diag, body, 0)
    apply_values(jnp.maximum(kt_diag - 1, 0))
    scores(kt_diag, True)
    apply_values(kt_diag)
    for h in range(hpg):
        rows = slice(h * tq, (h + 1) * tq)
        gate = _gate_col(sm_ref, nh_gdn, g, hpg, h, 1)
        out = acc_ref[rows, 0:HEAD_DIM] / acc_ref[rows, HEAD_DIM:2 * HEAD_DIM]
        o_ref[0, :, h * HEAD_DIM:(h + 1) * HEAD_DIM] = (out * gate).astype(o_ref.dtype)


def _sel_attn(q_r, ksel_r, big, vsel_blk, sel, small, nh_gdn):
    B, S, WQ = q_r.shape
    G = NSA_GROUPS
    hpg = WQ // HEAD_DIM // G
    nbp = sel.shape[3]
    tq = min(S, 512)
    tk = min(S, 512)
    rows = hpg * tq
    ext = jnp.where(jnp.arange(S)[:, None] // NSA_SEL_LEN == jnp.arange(nbp)[None, :], MASK_BIG, 0.0).astype(BF16)
    return pl.pallas_call(
        functools.partial(_sel_attn_kernel, nh_gdn=nh_gdn, hpg=hpg, tk=tk),
        grid=(B, G, S // tq),
        in_specs=[pl.BlockSpec((1, tq, hpg * HEAD_DIM), lambda b, g, i: (b, i, g)),
                  pl.BlockSpec((1, S, HEAD_DIM), lambda b, g, i: (b, 0, g)),
                  pl.BlockSpec((S, nbp), lambda b, g, i: (0, 0)),
                  pl.BlockSpec((1, S, HEAD_DIM), lambda b, g, i: (b, 0, vsel_blk + g)),
                  pl.BlockSpec((1, 1, tq, nbp), lambda b, g, i: (b, g, i, 0)),
                  pl.BlockSpec((1, tq, small.shape[2]), lambda b, g, i: (b, i, 0))],
        out_specs=pl.BlockSpec((1, tq, hpg * HEAD_DIM), lambda b, g, i: (b, i, g)),
        out_shape=jax.ShapeDtypeStruct((B, S, WQ), BF16),
        scratch_shapes=[pltpu.VMEM((rows, HEAD_DIM + nbp), BF16),
                        pltpu.VMEM((rows, HEAD_DIM), F32),
                        pltpu.VMEM((rows, 2 * HEAD_DIM), F32),
                        pltpu.VMEM((rows, tk), BF16),
                        pltpu.VMEM((rows, HEAD_DIM), F32)],
        compiler_params=_cparams(3),
        name="nsa_sel_attn",
    )(q_r, ksel_r, ext, big, sel, small)


def _win_attn_kernel(q_ref, k_ref, v_ref, sm_ref, o_ref, *, nh_gdn, hpg):
    tq = q_ref.shape[1]
    ts = min(tq, WIN_SUB_TILE)
    span = ts + NSA_WINDOW
    g = pl.program_id(1)
    for sub in range(tq // ts):
        qrows = slice(sub * ts, (sub + 1) * ts)
        t0 = pl.program_id(2) * tq + sub * ts
        k0 = pl.multiple_of(jnp.maximum(t0 - NSA_WINDOW, 0), ts)
        k = k_ref[0, pl.ds(k0, span), :]
        v = v_ref[0, pl.ds(k0, span), :]
        t_row = t0 + lax.broadcasted_iota(jnp.int32, (ts, span), 0)
        spos = k0 + lax.broadcasted_iota(jnp.int32, (ts, span), 1)
        mask = (spos <= t_row) & (spos > t_row - NSA_WINDOW)
        vx = jnp.concatenate([v, jnp.ones((span, HEAD_DIM), BF16)], axis=1)
        q4 = jnp.concatenate([q_ref[0, qrows, h * HEAD_DIM:(h + 1) * HEAD_DIM] for h in range(hpg)], axis=0)
        s = jnp.where(jnp.concatenate([mask] * hpg, axis=0), _dot_nt(q4, k), NEG_INF)
        e = jnp.exp2(s - jnp.max(s, axis=-1, keepdims=True))
        acc = _dot(e.astype(BF16), vx)
        out = acc[:, :HEAD_DIM] / acc[:, HEAD_DIM:]
        for h in range(hpg):
            gate = _gate_col(sm_ref, nh_gdn, g, hpg, h, 2)[qrows]
            o_ref[0, qrows, h * HEAD_DIM:(h + 1) * HEAD_DIM] = (out[h * ts:(h + 1) * ts] * gate).astype(o_ref.dtype)


def _win_attn(q_r, kwin_r, big, vwin_blk, small, nh_gdn):
    B, S, WQ = q_r.shape
    G = NSA_GROUPS
    hpg = WQ // HEAD_DIM // G
    tq = min(S, 512)
    assert S >= tq + NSA_WINDOW
    return pl.pallas_call(
        functools.partial(_win_attn_kernel, nh_gdn=nh_gdn, hpg=hpg),
        grid=(B, G, S // tq),
        in_specs=[pl.BlockSpec((1, tq, hpg * HEAD_DIM), lambda b, g, i: (b, i, g)),
                  pl.BlockSpec((1, S, HEAD_DIM), lambda b, g, i: (b, 0, g)),
                  pl.BlockSpec((1, S, HEAD_DIM), lambda b, g, i: (b, 0, vwin_blk + g)),
                  pl.BlockSpec((1, tq, small.shape[2]), lambda b, g, i: (b, i, 0))],
        out_specs=pl.BlockSpec((1, tq, hpg * HEAD_DIM), lambda b, g, i: (b, i, g)),
        out_shape=jax.ShapeDtypeStruct((B, S, WQ), BF16),
        compiler_params=_cparams(3),
        name="nsa_win_attn",
    )(q_r, kwin_r, big, small)


def _post_norm(x, y, gt, g, b, alpha):
    r = alpha * x + (1.0 + gt) * y
    return _normalize_rows(r) * g + b


def _mixer_out_kernel(oa_ref, oc_ref, os_ref, ow_ref, ma_ref, mb_ref, x_ref, gt_ref, lg_ref, lb_ref,
                      wg_ref, wn_ref, wo_ref, o_ref, *, alpha):
    y_a = _dot(oa_ref[0], wg_ref[...])
    o_b = oc_ref[0].astype(F32) + os_ref[0].astype(F32) + ow_ref[0].astype(F32)
    y_b = _dot(o_b.astype(BF16), wn_ref[...])
    mixed = _sigmoid(ma_ref[0].astype(F32)) * y_a + _sigmoid(mb_ref[0].astype(F32)) * y_b
    y = _dot(mixed.astype(BF16), wo_ref[...])
    o_ref[0] = _post_norm(x_ref[0], y, gt_ref[0], lg_ref[...], lb_ref[...], alpha)


def _mixer_out(o_a, o_c, o_s, o_w, big, ma_blk, mb_blk, x, ada_l, gt_col, ln_g, ln_b, wg, wn, wo, alpha):
    B, S, D = x.shape
    tm = min(S, 512)
    row = lambda blk: pl.BlockSpec((1, tm, D), lambda b, i: (b, i, blk))
    full = lambda shape: pl.BlockSpec(shape, lambda b, i: (0,) * len(shape))
    return pl.pallas_call(
        functools.partial(_mixer_out_kernel, alpha=alpha),
        grid=(B, S // tm),
        in_specs=[row(0), row(0), row(0), row(0), row(ma_blk), row(mb_blk), row(0),
                  pl.BlockSpec((1, 1, D), lambda b, i: (b, 0, gt_col)),
                  full((1, D)), full((1, D)), full((D, D)), full((D, D)), full((D, D))],
        out_specs=row(0),
        out_shape=jax.ShapeDtypeStruct((B, S, D), F32),
        compiler_params=_cparams(2),
        name="mixer_out",
    )(o_a, o_c, o_s, o_w, big, big, x, ada_l, ln_g.reshape(1, D), ln_b.reshape(1, D), wg, wn, wo)


def _first_argmax(vals, lane, big):
    m = jnp.max(vals, axis=-1, keepdims=True)
    first = jnp.min(jnp.where(vals == m, lane, big), axis=-1, keepdims=True)
    return m, first


def _moe_router_kernel(x_ref, sh_ref, sc_ref, rwh_ref, rwl_ref, rb_ref, wsg_ref, wsu_ref, wsd_ref,
                       h_ref, idx_ref, wt_ref, shared_ref, rank_ref, cnt_ref):
    h = _normalize_rows(x_ref[0]) * (1.0 + sc_ref[0]) + sh_ref[0]
    h16 = h.astype(BF16)
    h_ref[0] = h16
    h_lo = (h - h16.astype(F32)).astype(BF16)
    rw_hi = rwh_ref[...]
    logits = _dot_nt(rw_hi, h16) + _dot_nt(rwl_ref[...], h16) + _dot_nt(rw_hi, h_lo)
    scores = _sigmoid(logits)
    biased = scores + rb_ref[...]
    E, tm = scores.shape
    NG = N_EXPERT_GROUPS
    per = E // NG
    b3 = biased.reshape(NG, per, tm)
    in_grp = lax.broadcasted_iota(jnp.int32, (NG, per, tm), 1).astype(F32)
    m1 = jnp.max(b3, axis=1, keepdims=True)
    first = jnp.min(jnp.where(b3 == m1, in_grp, float(per)), axis=1, keepdims=True)
    m2 = jnp.max(jnp.where(in_grp == first, -jnp.inf, b3), axis=1, keepdims=True)
    gscore = (m1 + m2).reshape(NG, tm)
    g_id = lax.broadcasted_iota(jnp.int32, (NG, tm), 0).astype(F32)
    allowed = jnp.zeros((NG, tm), F32)
    for _ in range(TOPK_GROUPS):
        m = jnp.max(gscore, axis=0, keepdims=True)
        first_g = jnp.min(jnp.where(gscore == m, g_id, float(NG)), axis=0, keepdims=True)
        pick = g_id == first_g
        allowed = jnp.where(pick, 1.0, allowed)
        gscore = jnp.where(pick, -jnp.inf, gscore)
    cand = jnp.where(allowed.reshape(NG, 1, tm) > 0.5, b3, NEG_INF).reshape(E, tm)
    e_id = lax.broadcasted_iota(jnp.int32, (E, tm), 0).astype(F32)
    idx_rows, wt_rows, picks = [], [], []
    wsum = jnp.zeros((1, tm), F32)
    chosen = jnp.zeros((E, tm), F32)
    for kk in range(TOP_K):
        m = jnp.max(cand, axis=0, keepdims=True)
        first_e = jnp.min(jnp.where(cand == m, e_id, float(E)), axis=0, keepdims=True)
        pick = e_id == first_e
        w = jnp.sum(jnp.where(pick, scores, 0.0), axis=0, keepdims=True)
        cand = jnp.where(pick, -jnp.inf, cand)
        chosen = jnp.where(pick, 1.0, chosen)
        idx_rows.append(first_e)
        wt_rows.append(w)
        picks.append(pick)
        wsum = wsum + w
    idx_ref[0] = jnp.concatenate(idx_rows, axis=0).astype(jnp.int32)
    wt_ref[0] = jnp.concatenate(wt_rows, axis=0) / wsum * ROUTED_SCALE
    @pl.when((pl.program_id(0) == 0) & (pl.program_id(1) == 0))
    def _():
        cnt_ref[...] = jnp.zeros_like(cnt_ref)

    si = lax.broadcasted_iota(jnp.int32, (tm, tm), 0)
    ti = lax.broadcasted_iota(jnp.int32, (tm, tm), 1)
    before = jnp.where(si < ti, 1.0, 0.0).astype(BF16)
    prior = _dot(chosen.astype(BF16), before) + cnt_ref[...]
    rank_ref[0] = jnp.concatenate([jnp.sum(jnp.where(pk, prior, 0.0), axis=0, keepdims=True) for pk in picks],
                                  axis=0).astype(jnp.int32)
    cnt_ref[...] = cnt_ref[...] + jnp.sum(chosen, axis=1, keepdims=True)
    act = _silu(_dot(h16, wsg_ref[...])) * _dot(h16, wsu_ref[...])
    shared_ref[0] = _dot(act.astype(BF16), wsd_ref[...])


def _moe_router(x, ada_l, sh_col, sc_col, rwt_hi, rwt_lo, router_b, wsg, wsu, wsd):
    B, S, D = x.shape
    E = rwt_hi.shape[0]
    FF = wsg.shape[1]
    tm = min(S, 512)
    row = lambda w: pl.BlockSpec((1, tm, w), lambda b, i: (b, i, 0))
    krow = pl.BlockSpec((1, TOP_K, tm), lambda b, i: (b, 0, i))
    full = lambda shape: pl.BlockSpec(shape, lambda b, i: (0,) * len(shape))
    return pl.pallas_call(
        _moe_router_kernel,
        grid=(B, S // tm),
        in_specs=[row(D),
                  pl.BlockSpec((1, 1, D), lambda b, i: (b, 0, sh_col)),
                  pl.BlockSpec((1, 1, D), lambda b, i: (b, 0, sc_col)),
                  full((E, D)), full((E, D)), full((E, 1)), full((D, FF)), full((D, FF)), full((FF, D))],
        out_specs=[row(D), krow, krow, row(D), krow, full((E, 1))],
        out_shape=[jax.ShapeDtypeStruct((B, S, D), BF16),
                   jax.ShapeDtypeStruct((B, TOP_K, S), jnp.int32),
                   jax.ShapeDtypeStruct((B, TOP_K, S), F32),
                   jax.ShapeDtypeStruct((B, S, D), F32),
                   jax.ShapeDtypeStruct((B, TOP_K, S), jnp.int32),
                   jax.ShapeDtypeStruct((E, 1), F32)],
        compiler_params=_cparams(2),
        name="moe_router",
    )(x, ada_l, ada_l, rwt_hi, rwt_lo, router_b.reshape(E, 1), wsg, wsu, wsd)


def _moe_experts_kernel(be_ref, nu_ref, xs_ref, rw_ref, wg_ref, wu_ref, wd_ref, o_ref):
    @pl.when(pl.program_id(0) < nu_ref[0])
    def _():
        xb = xs_ref[...]
        act = _silu(_dot(xb, wg_ref[0, 0].astype(BF16))) * _dot(xb, wu_ref[0, 0].astype(BF16))
        o_ref[...] = (_dot(act.astype(BF16), wd_ref[0, 0].astype(BF16)) * rw_ref[...]).astype(o_ref.dtype)


def _moe_experts(block_expert, n_used, xs, row_w, layer, wg, wu, wd):
    rows, D = xs.shape
    BM = MOE_ROW_BLOCK
    FF = wg.shape[3]
    grid_spec = pltpu.PrefetchScalarGridSpec(
        num_scalar_prefetch=2,
        grid=(rows // BM,),
        in_specs=[pl.BlockSpec((BM, D), lambda i, be, nu: (i, 0)),
                  pl.BlockSpec((BM, 1), lambda i, be, nu: (i, 0)),
                  pl.BlockSpec((1, 1, D, FF), lambda i, be, nu: (layer, be[i], 0, 0)),
                  pl.BlockSpec((1, 1, D, FF), lambda i, be, nu: (layer, be[i], 0, 0)),
                  pl.BlockSpec((1, 1, FF, D), lambda i, be, nu: (layer, be[i], 0, 0))],
        out_specs=pl.BlockSpec((BM, D), lambda i, be, nu: (i, 0)),
    )
    return pl.pallas_call(
        _moe_experts_kernel,
        grid_spec=grid_spec,
        out_shape=jax.ShapeDtypeStruct((rows, D), BF16),
        compiler_params=_cparams(1),
        name="moe_experts",
    )(block_expert, n_used, xs, row_w, wg, wu, wd)


def _moe_out_kernel(x_ref, shared_ref, yg_ref, gt_ref, lg_ref, lb_ref, o_ref, *, alpha):
    y = shared_ref[0]
    for kk in range(TOP_K):
        y = y + yg_ref[kk, 0].astype(F32)
    o_ref[0] = _post_norm(x_ref[0], y, gt_ref[0], lg_ref[...], lb_ref[...], alpha)


def _moe_out(x, shared, yg, ada_l, gt_col, ln_g, ln_b, alpha):
    B, S, D = x.shape
    tm = min(S, 256)
    row = lambda w: pl.BlockSpec((1, tm, w), lambda b, i: (b, i, 0))
    full = lambda shape: pl.BlockSpec(shape, lambda b, i: (0,) * len(shape))
    return pl.pallas_call(
        functools.partial(_moe_out_kernel, alpha=alpha),
        grid=(B, S // tm),
        in_specs=[row(D), row(D),
                  pl.BlockSpec((TOP_K, 1, tm, D), lambda b, i: (0, b, i, 0)),
                  pl.BlockSpec((1, 1, D), lambda b, i: (b, 0, gt_col)),
                  full((1, D)), full((1, D))],
        out_specs=row(D),
        out_shape=jax.ShapeDtypeStruct((B, S, D), F32),
        compiler_params=_cparams(2),
        name="moe_out",
    )(x, shared, yg, ada_l, ln_g.reshape(1, D), ln_b.reshape(1, D))


def _moe_dest_kernel(start_ref, e_ref, r_ref, o_ref):
    e = e_ref[...]
    acc = r_ref[...]
    for j in range(start_ref.shape[0]):
        acc = acc + jnp.where(e == j, start_ref[j], 0)
    o_ref[...] = acc


def _routing_layout(expert_idx, rank, weights, counts):
    B, K, S = expert_idx.shape
    T = B * S
    n_experts = counts.shape[0]
    BM = MOE_ROW_BLOCK
    TK = T * K
    counts = counts.reshape(n_experts).astype(jnp.int32)
    padded = (counts + BM - 1) // BM * BM
    pad_end = jnp.cumsum(padded)
    pad_start = (pad_end - padded).astype(jnp.int32)
    ts = min(S, 2048)
    spec = pl.BlockSpec((1, K, ts), lambda b, i, tab: (b, 0, i))
    dest = pl.pallas_call(
        _moe_dest_kernel,
        grid_spec=pltpu.PrefetchScalarGridSpec(num_scalar_prefetch=1, grid=(B, S // ts),
                                               in_specs=[spec, spec], out_specs=spec),
        out_shape=jax.ShapeDtypeStruct((B, K, S), jnp.int32),
        compiler_params=_cparams(2),
        name="moe_dest",
    )(pad_start, expert_idx, rank)
    dest = jnp.swapaxes(dest, 0, 1).reshape(TK)
    n_blocks = (TK + n_experts * (BM - 1) + BM - 1) // BM
    rows = n_blocks * BM
    token = jnp.tile(jnp.arange(T, dtype=jnp.int32), K)
    row_tok = (jnp.arange(rows, dtype=jnp.int32) % T).at[dest].add(token - dest % T, mode='promise_in_bounds')
    row_w = jnp.zeros((rows,), F32).at[dest].add(jnp.swapaxes(weights, 0, 1).reshape(TK), mode='promise_in_bounds')
    block_start = jnp.arange(n_blocks, dtype=jnp.int32) * BM
    block_expert = jnp.minimum(jnp.sum(pad_end[None, :] <= block_start[:, None], axis=1),
                               n_experts - 1).astype(jnp.int32)
    n_used = (pad_end[-1] // BM).astype(jnp.int32).reshape(1)
    return row_tok, row_w.reshape(rows, 1), dest, block_expert, n_used


def kernel(x, c, positions, ada_w, ada_b, w_in, gdn_conv_w, gdn_a_log, gdn_dt_bias, gdn_norm_w, w_gdn_branch, nsa_cmp_pos_k, nsa_cmp_pos_v, nsa_cmp_k_w1, nsa_cmp_k_w2, nsa_cmp_v_w1, nsa_cmp_v_w2, w_nsa_branch, w_out, ln1_g, ln1_b, router_w, router_bias, w_sh_gate, w_sh_up, w_sh_down, w_e_gate, w_e_up, w_e_down, ln2_g, ln2_b):
    B, S, D = x.shape
    L = ada_w.shape[0]
    T = B * S
    G = NSA_GROUPS
    nh_gdn = gdn_a_log.shape[1]
    gdn_w = nh_gdn * HEAD_DIM
    nsa_w = w_nsa_branch.shape[1]
    kvw = G * HEAD_DIM
    E = router_w.shape[2]
    alpha = (2.0 * L) ** 0.25
    assert gdn_w == D and nsa_w == D and S % NSA_SEL_LEN == 0

    splits = (3 * gdn_w, gdn_w, nh_gdn, nh_gdn, nsa_w, 6 * kvw, 3 * (nsa_w // HEAD_DIM), D, D)
    offs = [0]
    for s_ in splits:
        offs.append(offs[-1] + s_)
    seg = lambda i: slice(offs[i], offs[i + 1])
    w_big = jnp.concatenate([w_in[:, :, seg(0)], w_in[:, :, seg(1)], w_in[:, :, seg(4)],
                             w_in[:, :, seg(7)], w_in[:, :, seg(8)], w_in[:, :, seg(5)]], axis=-1).astype(BF16)
    n_small = splits[2] + splits[3] + splits[6]
    w_small = jnp.concatenate([w_in[:, :, seg(2)], w_in[:, :, seg(3)], w_in[:, :, seg(6)],
                               jnp.zeros((L, D, 128 - n_small), w_in.dtype)], axis=-1).astype(BF16)
    Z_BLK, Q_BLK, MA_BLK, MB_BLK = 3, 4, 5, 6
    kv0 = 7 * D
    n_big = w_big.shape[2]
    tn_big = n_big // 4 if (n_big // 4) % 128 == 0 else 128

    wgb = w_gdn_branch.astype(BF16)
    wnb = w_nsa_branch.astype(BF16)
    wob = w_out.astype(BF16)
    wsg = w_sh_gate.astype(BF16)
    wsu = w_sh_up.astype(BF16)
    wsd = w_sh_down.astype(BF16)
    rwt = jnp.swapaxes(router_w, 1, 2)
    rwt_hi = rwt.astype(BF16)
    rwt_lo = (rwt - rwt_hi.astype(F32)).astype(BF16)
    cmp_w1 = jnp.stack([nsa_cmp_k_w1, nsa_cmp_v_w1], axis=1).astype(BF16)
    cmp_w2 = jnp.stack([nsa_cmp_k_w2, nsa_cmp_v_w2], axis=1).astype(BF16)
    cmp_pos = jnp.stack([nsa_cmp_pos_k, nsa_cmp_pos_v], axis=1).reshape(L, 2, 1, NSA_CMP_LEN * HEAD_DIM)

    ada = _ada(c, ada_w, ada_b)
    cosf, sinf = _rope_tables(positions)
    NC = S // NSA_CMP_STRIDE
    last = jnp.minimum(jnp.arange(NC) * NSA_CMP_STRIDE + NSA_CMP_LEN - 1, S - 1)
    cosc = cosf[:, last]
    sinc = sinf[:, last]
    n_sel = -(-(S // NSA_SEL_LEN) // 128) * 128

    for l in range(L):
        ada_l = ada[l].reshape(B, 1, 6 * D)
        big, small = _proj(x, ada_l, 0, 1, w_big[l], w_small[l], tn_big)
        small_t = jnp.swapaxes(small[:, :, nh_gdn:2 * nh_gdn].reshape(B, S // GDN_CHUNK, GDN_CHUNK, nh_gdn), 2, 3)
        q_a, k_a, v_a = _gdn_prep(big, gdn_conv_w[l], gdn_w)
        o_a = _gdn(q_a, k_a, v_a, big, Z_BLK, small, small_t, gdn_a_log[l], gdn_dt_bias[l], gdn_norm_w[l])
        q_r, ksel_r, kwin_r = _rope_apply(big, Q_BLK, (kv0 + 2 * kvw) // kvw, (kv0 + 4 * kvw) // kvw,
                                          nsa_w, kvw, cosf, sinf)
        hb = big[:, :, kv0:kv0 + 2 * kvw].reshape(B, NC, NSA_CMP_STRIDE, 2 * G, HEAD_DIM)
        hb = hb.transpose(0, 3, 1, 2, 4).reshape(B, 2 * G, NC, NSA_CMP_STRIDE * HEAD_DIM)
        kvc = _compress(hb, cmp_pos[l], cmp_w1[l], cmp_w2[l], cosc, sinc)
        o_c, sel = _cmp_attn(q_r, kvc, small, nh_gdn, n_sel)
        o_s = _sel_attn(q_r, ksel_r, big, (kv0 + 3 * kvw) // HEAD_DIM, sel, small, nh_gdn)
        o_w = _win_attn(q_r, kwin_r, big, (kv0 + 5 * kvw) // HEAD_DIM, small, nh_gdn)
        x = _mixer_out(o_a, o_c, o_s, o_w, big, MA_BLK, MB_BLK, x, ada_l, 2, ln1_g[l], ln1_b[l],
                       wgb[l], wnb[l], wob[l], alpha)
        h2, eidx, ewt, shared, rank, counts = _moe_router(x, ada_l, 3, 4, rwt_hi[l], rwt_lo[l], router_bias[l],
                                                          wsg[l], wsu[l], wsd[l])
        row_tok, row_w, dest, block_expert, n_used = _routing_layout(eidx, rank, ewt, counts)
        xs = h2.reshape(T, D).at[row_tok].get(mode='promise_in_bounds')
        yb = _moe_experts(block_expert, n_used, xs, row_w, l, w_e_gate, w_e_up, w_e_down)
        yg = yb.at[dest].get(mode='promise_in_bounds').reshape(TOP_K, B, S, D)
        x = _moe_out(x, shared, yg, ada_l, 5, ln2_g[l], ln2_b[l], alpha)
    return x
```

```python
import functools
import math

import jax
import jax.numpy as jnp
from jax import lax
from jax.experimental import pallas as pl
from jax.experimental.pallas import tpu as pltpu

F32 = jnp.float32
BF16 = jnp.bfloat16

HEAD_DIM = 128
GDN_CONV = 4
GDN_CHUNK = 64
NSA_GROUPS = 2
NSA_CMP_LEN = 32
NSA_CMP_STRIDE = 16
NSA_SEL_LEN = 64
NSA_SEL_TOP = 16
NSA_WINDOW = 512
ROPE_THETA = 10000.0
N_EXPERTS = 64
N_EXPERT_GROUPS = 8
TOPK_GROUPS = 4
TOP_K = 8
ROUTED_SCALE = 2.5
LN_EPS = 1e-5
NEG_INF = -1e30
SEL_FORCE = 1e6
MOE_ROW_BLOCK = 512
Q_SCALE_LOG2E = (HEAD_DIM ** -0.5) * math.log2(math.e)
MASK_BIG = 2.0 ** 100
N_ROW_PARTS = 2
GDN_PHASE1_CHUNKS = 4
WIN_SUB_TILE = 128
_ARB = "arbitrary"


def _cparams(n_axes):
    return pltpu.CompilerParams(dimension_semantics=(_ARB,) * n_axes)


def _sigmoid(x):
    return 1.0 / (1.0 + jnp.exp(-x))


def _silu(x):
    return x * _sigmoid(x)


def _dot(a, b):
    return jnp.dot(a, b, preferred_element_type=F32)


def _dot_nt(a, b):
    return lax.dot_general(a, b, (((1,), (1,)), ((), ())), preferred_element_type=F32)


def _dot_tn(a, b):
    return lax.dot_general(a, b, (((0,), (0,)), ((), ())), preferred_element_type=F32)


def _tile_lanes(x, n):
    return jnp.concatenate([x] * n, axis=1)


def _split3(x):
    x0 = x.astype(BF16)
    r1 = x - x0.astype(F32)
    x1 = r1.astype(BF16)
    x2 = (r1 - x1.astype(F32)).astype(BF16)
    return x0, x1, x2


def _dot_sel_right(x, sel_bf16):
    x0, x1, x2 = _split3(x)
    return _dot(x0, sel_bf16) + _dot(x1, sel_bf16) + _dot(x2, sel_bf16)


def _dot_sel_left(sel_bf16, x):
    x0, x1, x2 = _split3(x)
    return _dot(sel_bf16, x0) + _dot(sel_bf16, x1) + _dot(sel_bf16, x2)


def _normalize_rows(x):
    mu = jnp.mean(x, axis=-1, keepdims=True)
    xc = x - mu
    var = jnp.mean(xc * xc, axis=-1, keepdims=True)
    return xc * lax.rsqrt(var + LN_EPS)


def _ada_kernel(c_ref, w_ref, b_ref, o_ref):
    cond = _silu(c_ref[...])
    o_ref[0] = jnp.dot(cond, w_ref[0], preferred_element_type=F32,
                       precision=lax.Precision.HIGHEST) + b_ref[0]


def _ada(c, ada_w, ada_b):
    L, D, N = ada_w.shape
    B = c.shape[0]
    tn = min(N, 1536)
    return pl.pallas_call(
        _ada_kernel,
        grid=(L, N // tn),
        in_specs=[pl.BlockSpec((B, D), lambda l, j: (0, 0)),
                  pl.BlockSpec((1, D, tn), lambda l, j: (l, 0, j)),
                  pl.BlockSpec((1, 1, tn), lambda l, j: (l, 0, j))],
        out_specs=pl.BlockSpec((1, B, tn), lambda l, j: (l, 0, j)),
        out_shape=jax.ShapeDtypeStruct((L, B, N), F32),
        compiler_params=_cparams(2),
        name="ada",
    )(c, ada_w, ada_b.reshape(L, 1, N))


def _proj_kernel(x_ref, sh_ref, sc_ref, w_ref, ws_ref, o_ref, os_ref, h_ref):
    @pl.when(pl.program_id(2) == 0)
    def _():
        h = _normalize_rows(x_ref[0]) * (1.0 + sc_ref[0]) + sh_ref[0]
        h_ref[...] = h.astype(BF16)
        os_ref[0] = _dot(h_ref[...], ws_ref[...])

    o_ref[0] = _dot(h_ref[...], w_ref[...]).astype(o_ref.dtype)


def _proj(x, ada_l, sh_col, sc_col, w, w_small, tn):
    B, S, D = x.shape
    N = w.shape[1]
    NS = w_small.shape[1]
    tm = min(S, 1024)
    return pl.pallas_call(
        _proj_kernel,
        grid=(B, S // tm, N // tn),
        in_specs=[pl.BlockSpec((1, tm, D), lambda b, i, j: (b, i, 0)),
                  pl.BlockSpec((1, 1, D), lambda b, i, j: (b, 0, sh_col)),
                  pl.BlockSpec((1, 1, D), lambda b, i, j: (b, 0, sc_col)),
                  pl.BlockSpec((D, tn), lambda b, i, j: (0, j)),
                  pl.BlockSpec((D, NS), lambda b, i, j: (0, 0))],
        out_specs=[pl.BlockSpec((1, tm, tn), lambda b, i, j: (b, i, j)),
                   pl.BlockSpec((1, tm, NS), lambda b, i, j: (b, i, 0))],
        out_shape=[jax.ShapeDtypeStruct((B, S, N), BF16),
                   jax.ShapeDtypeStruct((B, S, NS), F32)],
        scratch_shapes=[pltpu.VMEM((tm, D), BF16)],
        compiler_params=_cparams(3),
        name="proj",
    )(x, ada_l, ada_l, w, w_small)


def _gdn_prep_kernel(x_ref, w_ref, q_ref, k_ref, v_ref, carry_ref):
    ts = x_ref.shape[1]
    width = q_ref.shape[2]
    nh = width // HEAD_DIM

    @pl.when(pl.program_id(1) == 0)
    def _():
        carry_ref[...] = jnp.zeros_like(carry_ref)

    for part, o_ref in enumerate((q_ref, k_ref, v_ref)):
        for h in range(nh):
            c0 = part * width + h * HEAD_DIM
            cols = slice(c0, c0 + HEAD_DIM)
            xx = jnp.concatenate([carry_ref[:, cols], x_ref[0, :, cols].astype(F32)], axis=0)
            w = w_ref[:, cols]
            y = xx[8:8 + ts] * w[3:4]
            for kk in range(GDN_CONV - 1):
                off = 8 - (GDN_CONV - 1) + kk
                y = y + xx[off:off + ts] * w[kk:kk + 1]
            y = _silu(y)
            if part < 2:
                y = y * lax.rsqrt(jnp.sum(y * y, axis=-1, keepdims=True) + 1e-6)
            if part == 0:
                y = y * (HEAD_DIM ** -0.5)
            o_ref[0, :, h * HEAD_DIM:(h + 1) * HEAD_DIM] = y.astype(o_ref.dtype)
    carry_ref[...] = x_ref[0, ts - 8:ts, :].astype(F32)


def _gdn_prep(big, conv_w, width):
    B, S, _ = big.shape
    ts = min(S, 512)
    out = jax.ShapeDtypeStruct((B, S, width), BF16)
    ospec = pl.BlockSpec((1, ts, width), lambda b, i: (b, i, 0))
    return pl.pallas_call(
        _gdn_prep_kernel,
        grid=(B, S // ts),
        in_specs=[pl.BlockSpec((1, ts, 3 * width), lambda b, i: (b, i, 0)),
                  pl.BlockSpec((GDN_CONV, 3 * width), lambda b, i: (0, 0))],
        out_specs=[ospec, ospec, ospec],
        out_shape=[out, out, out],
        scratch_shapes=[pltpu.VMEM((8, 3 * width), F32)],
        compiler_params=_cparams(2),
        name="gdn_prep",
    )(big, conv_w)


def _softplus(x):
    return jnp.maximum(x, 0.0) + jnp.log(1.0 + jnp.exp(-jnp.abs(x)))


def _gdn_kernel(q_ref, k_ref, v_ref, z_ref, sm_ref, smt_ref, alog_ref, alogt_ref, dtb_ref, dtbt_ref,
                nw_ref, o_ref, state_ref, u_s, wq_s, attn_s, kd_s, dec_s):
    ts = q_ref.shape[1]
    nh = q_ref.shape[2] // HEAD_DIM
    C = GDN_CHUNK
    R = 2 * C
    npair = nh // 2
    nchunks = ts // C

    @pl.when(pl.program_id(1) == 0)
    def _():
        state_ref[...] = jnp.zeros_like(state_ref)

    ci = lax.broadcasted_iota(jnp.int32, (C, C), 0)
    cj = lax.broadcasted_iota(jnp.int32, (C, C), 1)
    tril = jnp.where(ci >= cj, 1.0, 0.0).astype(BF16)
    triu = jnp.where(cj >= ci, 1.0, 0.0).astype(BF16)
    ii = lax.broadcasted_iota(jnp.int32, (R, R), 0)
    jj = lax.broadcasted_iota(jnp.int32, (R, R), 1)
    same_head = (ii // C) == (jj // C)
    incl = same_head & (ii >= jj)
    strict = same_head & (ii > jj)
    eye = jnp.where(ii == jj, 1.0, 0.0)
    nw = nw_ref[...]

    def pair_rows(ref, rows, p):
        return jnp.concatenate([ref[0, rows, (2 * p) * HEAD_DIM:(2 * p + 1) * HEAD_DIM],
                                ref[0, rows, (2 * p + 1) * HEAD_DIM:(2 * p + 2) * HEAD_DIM]], axis=0)

    def pair_col(x, p):
        return jnp.concatenate([x[:, 2 * p:2 * p + 1], x[:, 2 * p + 1:2 * p + 2]], axis=0)

    def phase1(j, carry):
        cs = [j * GDN_PHASE1_CHUNKS + d for d in range(GDN_PHASE1_CHUNKS)]
        rows_c, beta_c, gc_c, gcr_c, glb_c = [], [], [], [], []
        for c in cs:
            rows = pl.ds(pl.multiple_of(c * C, C), C)
            sm = sm_ref[0, rows, :]
            g = -jnp.exp(alog_ref[...]) * _softplus(sm[:, nh:2 * nh] + dtb_ref[...])
            gt = -jnp.exp(alogt_ref[...]) * _softplus(smt_ref[0, c] + dtbt_ref[...])
            gc = _dot_sel_left(tril, g)
            g_last = gc[C - 1:C, :]
            dec_s[c] = jnp.exp(g_last)
            rows_c.append(rows)
            beta_c.append(_sigmoid(sm[:, 0:nh]))
            gc_c.append(gc)
            gcr_c.append(_dot_sel_right(gt, triu))
            glb_c.append(jnp.broadcast_to(g_last, (C, nh)))
        items = [(ci, p) for ci in range(len(cs)) for p in range(npair)]
        idx = range(len(items))
        q2 = [pair_rows(q_ref, rows_c[ci], p).astype(F32) for ci, p in items]
        k2 = [pair_rows(k_ref, rows_c[ci], p).astype(F32) for ci, p in items]
        b2 = [pair_col(beta_c[ci], p) for ci, p in items]
        g_col = [pair_col(gc_c[ci], p) for ci, p in items]
        kb = [k2[i] * b2[i] for i in idx]
        kq = [_dot_nt(jnp.concatenate([kb[i], q2[i]], axis=0).astype(BF16), k2[i].astype(BF16))
              for i in idx]
        decay = []
        for i, (ci, p) in enumerate(items):
            gcr = gcr_c[ci]
            g_row = jnp.concatenate([gcr[2 * p:2 * p + 1, :], gcr[2 * p + 1:2 * p + 2, :]], axis=1)
            decay.append(jnp.where(incl, jnp.exp(jnp.where(incl, g_col[i] - g_row, 0.0)), 0.0))
        a = [jnp.where(strict, kq[i][:R] * decay[i], 0.0) for i in idx]
        for i, (ci, p) in enumerate(items):
            attn_s[cs[ci], p] = (kq[i][R:] * decay[i]).astype(BF16)
        x = [eye - a[i] for i in idx]
        pw = a
        n = 2
        while n < C:
            pw16 = [pw[i].astype(BF16) for i in idx]
            pw = [_dot(pw16[i], pw16[i]) for i in idx]
            x = [x[i] + _dot(x[i].astype(BF16), pw[i].astype(BF16)) for i in idx]
            n *= 2
        eg = [jnp.exp(g_col[i]) for i in idx]
        sol = []
        for i, (ci, p) in enumerate(items):
            v2 = pair_rows(v_ref, rows_c[ci], p).astype(F32)
            r = jnp.concatenate([v2 * b2[i], kb[i] * eg[i]], axis=1)
            sol.append(_dot(x[i].astype(BF16), r.astype(BF16)))
        for i, (ci, p) in enumerate(items):
            c = cs[ci]
            u_s[c, p] = sol[i][:, :HEAD_DIM]
            w = sol[i][:, HEAD_DIM:]
            qg = q2[i] * eg[i]
            for e in range(2):
                wq_s[c, 2 * p + e] = jnp.concatenate([w[e * C:(e + 1) * C], qg[e * C:(e + 1) * C]],
                                                     axis=0).astype(BF16)
            kd_s[c, p] = (k2[i] * jnp.exp(pair_col(glb_c[ci], p) - g_col[i])).astype(BF16)
        return carry

    assert nchunks % GDN_PHASE1_CHUNKS == 0
    lax.fori_loop(0, nchunks // GDN_PHASE1_CHUNKS, phase1, 0)

    def phase2(c, carry):
        rows = pl.ds(pl.multiple_of(c * C, C), C)
        dec = dec_s[c]
        res = [_dot(wq_s[c, h], state_ref[h].astype(BF16)) for h in range(nh)]
        for p in range(npair):
            ws = jnp.concatenate([res[2 * p][:C], res[2 * p + 1][:C]], axis=0)
            qs = jnp.concatenate([res[2 * p][C:], res[2 * p + 1][C:]], axis=0)
            v_new = u_s[c, p] - ws
            v16 = v_new.astype(BF16)
            o2 = qs + _dot(attn_s[c, p], v16)
            kd = kd_s[c, p]
            for e in range(2):
                h = 2 * p + e
                part = slice(e * C, (e + 1) * C)
                state_ref[h] = state_ref[h] * dec[:, h:h + 1] + _dot_tn(kd[part], v16[part])
                o = o2[part]
                o = o * lax.rsqrt(jnp.mean(o * o, axis=-1, keepdims=True) + 1e-6) * nw
                cols = slice(h * HEAD_DIM, (h + 1) * HEAD_DIM)
                z = z_ref[0, rows, cols].astype(F32)
                o_ref[0, rows, cols] = (o * _silu(z)).astype(o_ref.dtype)
        return carry

    lax.fori_loop(0, nchunks, phase2, 0)


def _gdn(q, k, v, big, z_blk, small, small_t, a_log, dt_bias, norm_w):
    B, S, W = q.shape
    nh = W // HEAD_DIM
    assert nh % 2 == 0 and 2 * GDN_CHUNK == HEAD_DIM
    ts = min(S, 512)
    nc = ts // GDN_CHUNK
    spec = pl.BlockSpec((1, ts, W), lambda b, i: (b, i, 0))
    full = lambda shape: pl.BlockSpec(shape, lambda b, i: (0,) * len(shape))
    return pl.pallas_call(
        _gdn_kernel,
        grid=(B, S // ts),
        in_specs=[spec, spec, spec,
                  pl.BlockSpec((1, ts, W), lambda b, i: (b, i, z_blk)),
                  pl.BlockSpec((1, ts, small.shape[2]), lambda b, i: (b, i, 0)),
                  pl.BlockSpec((1, ts // GDN_CHUNK, nh, GDN_CHUNK), lambda b, i: (b, i, 0, 0)),
                  full((1, nh)), full((nh, 1)), full((1, nh)), full((nh, 1)),
                  full((1, HEAD_DIM))],
        out_specs=spec,
        out_shape=jax.ShapeDtypeStruct((B, S, W), BF16),
        scratch_shapes=[pltpu.VMEM((nh, HEAD_DIM, HEAD_DIM), F32),
                        pltpu.VMEM((nc, nh // 2, 2 * GDN_CHUNK, HEAD_DIM), F32),
                        pltpu.VMEM((nc, nh, 2 * GDN_CHUNK, HEAD_DIM), BF16),
                        pltpu.VMEM((nc, nh // 2, 2 * GDN_CHUNK, 2 * GDN_CHUNK), BF16),
                        pltpu.VMEM((nc, nh // 2, 2 * GDN_CHUNK, HEAD_DIM), BF16),
                        pltpu.VMEM((nc, 1, nh), F32)],
        compiler_params=_cparams(2),
        name="gdn",
    )(q, k, v, big, small, small_t, a_log.reshape(1, nh), a_log.reshape(nh, 1),
      dt_bias.reshape(1, nh), dt_bias.reshape(nh, 1), norm_w.reshape(1, HEAD_DIM))


def _rope_table_kernel(pos_ref, invf_ref, sign_ref, cos_ref, sin_ref):
    ang = pos_ref[0].astype(F32) * invf_ref[...]
    cos_ref[0] = jnp.cos(ang)
    sin_ref[0] = jnp.sin(ang) * sign_ref[...]


def _rope_tables(positions):
    B, S = positions.shape
    half = HEAD_DIM // 2
    inv = ROPE_THETA ** (-jnp.arange(half, dtype=F32) / half)
    invf = jnp.concatenate([inv, inv]).reshape(1, HEAD_DIM)
    sign = jnp.concatenate([-jnp.ones((half,), F32), jnp.ones((half,), F32)]).reshape(1, HEAD_DIM)
    ts = min(S, 1024)
    out = jax.ShapeDtypeStruct((B, S, HEAD_DIM), F32)
    ospec = pl.BlockSpec((1, ts, HEAD_DIM), lambda b, i: (b, i, 0))
    return pl.pallas_call(
        _rope_table_kernel,
        grid=(B, S // ts),
        in_specs=[pl.BlockSpec((1, ts, 1), lambda b, i: (b, i, 0)),
                  pl.BlockSpec((1, HEAD_DIM), lambda b, i: (0, 0)),
                  pl.BlockSpec((1, HEAD_DIM), lambda b, i: (0, 0))],
        out_specs=[ospec, ospec],
        out_shape=[out, out],
        compiler_params=_cparams(2),
        name="rope_table",
    )(positions.reshape(B, S, 1), invf, sign)


def _rope_rows(x, cosf, sinf):
    return x * cosf + pltpu.roll(x, HEAD_DIM // 2, 1) * sinf


def _rope_apply_kernel(q_ref, ks_ref, kw_ref, cos_ref, sin_ref, qo_ref, kso_ref, kwo_ref):
    cosf = cos_ref[0]
    sinf = sin_ref[0]
    for src, dst, mult in ((q_ref, qo_ref, Q_SCALE_LOG2E), (ks_ref, kso_ref, None), (kw_ref, kwo_ref, None)):
        for h in range(src.shape[2] // HEAD_DIM):
            cols = slice(h * HEAD_DIM, (h + 1) * HEAD_DIM)
            r = _rope_rows(src[0, :, cols].astype(F32), cosf, sinf)
            if mult is not None:
                r = r * mult
            dst[0, :, cols] = r.astype(dst.dtype)


def _rope_apply(big, q_blk, ksel_blk, kwin_blk, wq, wkv, cosf, sinf):
    B, S, _ = big.shape
    ts = min(S, 512)
    tab = pl.BlockSpec((1, ts, HEAD_DIM), lambda b, i: (b, i, 0))
    return pl.pallas_call(
        _rope_apply_kernel,
        grid=(B, S // ts),
        in_specs=[pl.BlockSpec((1, ts, wq), lambda b, i: (b, i, q_blk)),
                  pl.BlockSpec((1, ts, wkv), lambda b, i: (b, i, ksel_blk)),
                  pl.BlockSpec((1, ts, wkv), lambda b, i: (b, i, kwin_blk)),
                  tab, tab],
        out_specs=[pl.BlockSpec((1, ts, wq), lambda b, i: (b, i, 0)),
                   pl.BlockSpec((1, ts, wkv), lambda b, i: (b, i, 0)),
                   pl.BlockSpec((1, ts, wkv), lambda b, i: (b, i, 0))],
        out_shape=[jax.ShapeDtypeStruct((B, S, wq), BF16),
                   jax.ShapeDtypeStruct((B, S, wkv), BF16),
                   jax.ShapeDtypeStruct((B, S, wkv), BF16)],
        compiler_params=_cparams(2),
        name="rope_apply",
    )(big, big, big, cosf, sinf)


def _compress_kernel(hb_ref, pos_ref, w1_ref, w2_ref, cos_ref, sin_ref, o_ref):
    hb = hb_ref[0, 0]
    w1 = w1_ref[0]
    half = hb.shape[1]
    p0 = _dot(hb, w1[:half])
    p1 = _dot(hb, w1[half:])
    nc = p0.shape[0]
    pos8 = jnp.broadcast_to(pos_ref[0], (8, 2 * half)).astype(BF16)
    pb = _dot(pos8, w1)[0:1]
    pre = p0 + pltpu.roll(p1, nc - 1, 0) + pb
    out = _dot(_silu(pre).astype(BF16), w2_ref[0])
    roped = _rope_rows(out, cos_ref[0], sin_ref[0])
    is_key = pl.program_id(1) < NSA_GROUPS
    o_ref[0, 0] = jnp.where(is_key, roped, out).astype(o_ref.dtype)


def _compress(hb, pos_flat, w1, w2, cosc, sinc):
    B, J, NC, HW = hb.shape
    G = NSA_GROUPS
    return pl.pallas_call(
        _compress_kernel,
        grid=(B, J),
        in_specs=[pl.BlockSpec((1, 1, NC, HW), lambda b, j: (b, j, 0, 0)),
                  pl.BlockSpec((1, 1, 2 * HW), lambda b, j: (j // G, 0, 0)),
                  pl.BlockSpec((1, 2 * HW, HEAD_DIM), lambda b, j: (j // G, 0, 0)),
                  pl.BlockSpec((1, HEAD_DIM, HEAD_DIM), lambda b, j: (j // G, 0, 0)),
                  pl.BlockSpec((1, NC, HEAD_DIM), lambda b, j: (b, 0, 0)),
                  pl.BlockSpec((1, NC, HEAD_DIM), lambda b, j: (b, 0, 0))],
        out_specs=pl.BlockSpec((1, 1, NC, HEAD_DIM), lambda b, j: (b, j, 0, 0)),
        out_shape=jax.ShapeDtypeStruct((B, J, NC, HEAD_DIM), BF16),
        compiler_params=_cparams(2),
        name="nsa_compress",
    )(hb, pos_flat, w1, w2, cosc, sinc)


def _gate_col(g_ref, nh_gdn, group, hpg, h, branch):
    col = None
    for gi in range(NSA_GROUPS):
        c = 2 * nh_gdn + (gi * hpg + h) * 3 + branch
        cand = g_ref[0, :, c:c + 1]
        col = cand if col is None else jnp.where(group == gi, cand, col)
    return _sigmoid(col)


def _cmp_attn_kernel(q_ref, kc_ref, vc_ref, sm_ref, o_ref, sel_ref, *, nh_gdn, hpg):
    tq = q_ref.shape[1]
    nc = kc_ref.shape[2]
    nb = sel_ref.shape[3]
    g = pl.program_id(1)
    t0 = pl.program_id(2) * tq
    t_nc = t0 + lax.broadcasted_iota(jnp.int32, (tq, nc), 0)
    n_nc = lax.broadcasted_iota(jnp.int32, (tq, nc), 1)
    valid = (n_nc * NSA_CMP_STRIDE + (NSA_CMP_LEN - 1)) <= t_nc
    t_col = t0 + lax.broadcasted_iota(jnp.int32, (tq, 1), 0)
    has_valid = jnp.where(t_col >= NSA_CMP_LEN - 1, 1.0, 0.0)
    kc = kc_ref[0, 0]
    vc = vc_ref[0, 0]
    psum = jnp.zeros((tq, nc), F32)
    for h in range(hpg):
        cols = slice(h * HEAD_DIM, (h + 1) * HEAD_DIM)
        s = jnp.where(valid, _dot_nt(q_ref[0, :, cols], kc), NEG_INF)
        e = jnp.exp2(s - jnp.max(s, axis=-1, keepdims=True))
        p = e * (has_valid / jnp.sum(e, axis=-1, keepdims=True))
        gate = _gate_col(sm_ref, nh_gdn, g, hpg, h, 0)
        o_ref[0, :, cols] = (_dot(p.astype(BF16), vc) * gate).astype(o_ref.dtype)
        psum = psum + p
    sj = lax.broadcasted_iota(jnp.int32, (nb, nc), 0) * NSA_SEL_LEN
    cn = lax.broadcasted_iota(jnp.int32, (nb, nc), 1) * NSA_CMP_STRIDE
    overlap_t = jnp.where((cn <= sj + (NSA_SEL_LEN - 1)) & (cn + (NSA_CMP_LEN - 1) >= sj), 1.0, 0.0).astype(BF16)
    p0, p1, p2 = _split3(psum)
    imp = _dot_nt(overlap_t, p0) + _dot_nt(overlap_t, p1) + _dot_nt(overlap_t, p2)
    t_nb = t0 + lax.broadcasted_iota(jnp.int32, (nb, tq), 1)
    blk = lax.broadcasted_iota(jnp.int32, (nb, tq), 0)
    cur = t_nb // NSA_SEL_LEN
    forced = (blk == 0) | (blk == cur) | (blk == cur - 1)
    score = jnp.where(forced, SEL_FORCE, jnp.where(blk * NSA_SEL_LEN <= t_nb, imp, -1.0))
    sel = jnp.zeros((nb, tq), F32)
    blk_f = blk.astype(F32)
    for _ in range(min(NSA_SEL_TOP, nb)):
        m = jnp.max(score, axis=0, keepdims=True)
        first = jnp.min(jnp.where(score == m, blk_f, float(nb)), axis=0, keepdims=True)
        pick = blk_f == first
        sel = jnp.where(pick, 1.0, sel)
        score = jnp.where(pick, -jnp.inf, score)
    sel_ref[0, 0] = sel.T.astype(sel_ref.dtype)


def _cmp_attn(q_r, kvc, small, nh_gdn, n_sel):
    B, S, WQ = q_r.shape
    G = NSA_GROUPS
    hpg = WQ // HEAD_DIM // G
    NC = kvc.shape[2]
    tq = min(S, 512)
    return pl.pallas_call(
        functools.partial(_cmp_attn_kernel, nh_gdn=nh_gdn, hpg=hpg),
        grid=(B, G, S // tq),
        in_specs=[pl.BlockSpec((1, tq, hpg * HEAD_DIM), lambda b, g, i: (b, i, g)),
                  pl.BlockSpec((1, 1, NC, HEAD_DIM), lambda b, g, i: (b, g, 0, 0)),
                  pl.BlockSpec((1, 1, NC, HEAD_DIM), lambda b, g, i: (b, G + g, 0, 0)),
                  pl.BlockSpec((1, tq, small.shape[2]), lambda b, g, i: (b, i, 0))],
        out_specs=[pl.BlockSpec((1, tq, hpg * HEAD_DIM), lambda b, g, i: (b, i, g)),
                   pl.BlockSpec((1, 1, tq, n_sel), lambda b, g, i: (b, g, i, 0))],
        out_shape=[jax.ShapeDtypeStruct((B, S, WQ), BF16),
                   jax.ShapeDtypeStruct((B, G, S, n_sel), BF16)],
        compiler_params=_cparams(3),
        name="nsa_cmp_attn",
    )(q_r, kvc, kvc, small)


def _sel_attn_kernel(q_ref, k_ref, ext_ref, v_ref, sel_ref, sm_ref, o_ref, qx_ref, m_ref, acc_ref, p_ref, a_ref,
                     *, nh_gdn, hpg, tk):
    tq = q_ref.shape[1]
    nb = sel_ref.shape[3]
    g = pl.program_id(1)
    t0 = pl.program_id(2) * tq
    assert tk % tq == 0
    unsel = (sel_ref[0, 0].astype(F32) - 1.0).astype(BF16)
    for h in range(hpg):
        qx_ref[h * tq:(h + 1) * tq, 0:HEAD_DIM] = q_ref[0, :, h * HEAD_DIM:(h + 1) * HEAD_DIM]
        qx_ref[h * tq:(h + 1) * tq, HEAD_DIM:HEAD_DIM + nb] = unsel
    m_ref[...] = jnp.full_like(m_ref, NEG_INF)
    acc_ref[...] = jnp.zeros_like(acc_ref)
    p_ref[...] = jnp.zeros_like(p_ref)
    a_ref[...] = jnp.ones_like(a_ref)
    ones = jnp.ones((tk, HEAD_DIM), BF16)
    pr = hpg * tq // N_ROW_PARTS
    parts = [slice(part * pr, (part + 1) * pr) for part in range(N_ROW_PARTS)]

    def apply_values(kt):
        k0 = pl.multiple_of(kt * tk, tk)
        vx = jnp.concatenate([v_ref[0, pl.ds(k0, tk), :], ones], axis=1)
        for rows in parts:
            acc_ref[rows, :] = _tile_lanes(a_ref[rows, :], 2) * acc_ref[rows, :] + _dot(p_ref[rows, :], vx)

    def scores(kt, causal):
        k0 = pl.multiple_of(kt * tk, tk)
        kx = jnp.concatenate([k_ref[0, pl.ds(k0, tk), :], ext_ref[pl.ds(k0, tk), :]], axis=1)
        if causal:
            t_row = t0 + lax.broadcasted_iota(jnp.int32, (tq, tk), 0)
            key = k0 + lax.broadcasted_iota(jnp.int32, (tq, tk), 1)
            keep = jnp.concatenate([key <= t_row] * (hpg // N_ROW_PARTS), axis=0)
        for rows in parts:
            s = _dot_nt(qx_ref[rows, :], kx)
            if causal:
                s = jnp.where(keep, s, NEG_INF)
            m_old = m_ref[rows, :]
            m_new = jnp.maximum(m_old, jnp.max(s, axis=-1, keepdims=True))
            a_ref[rows, :] = jnp.exp2(m_old - m_new)
            p_ref[rows, :] = jnp.exp2(s - _tile_lanes(m_new, tk // HEAD_DIM)).astype(BF16)
            m_ref[rows, :] = m_new

    kt_diag = t0 // tk

    def body(kt, carry):
        apply_values(jnp.maximum(kt - 1, 0))
        scores(kt, False)
        return carry

    lax.fori_loop(0, kt_diag, body, 0)
    apply_values(jnp.maximum(kt_diag - 1, 0))
    scores(kt_diag, True)
    apply_values(kt_diag)
    for h in range(hpg):
        rows = slice(h * tq, (h + 1) * tq)
        gate = _gate_col(sm_ref, nh_gdn, g, hpg, h, 1)
        out = acc_ref[rows, 0:HEAD_DIM] / acc_ref[rows, HEAD_DIM:2 * HEAD_DIM]
        o_ref[0, :, h * HEAD_DIM:(h + 1) * HEAD_DIM] = (out * gate).astype(o_ref.dtype)


def _sel_attn(q_r, ksel_r, big, vsel_blk, sel, small, nh_gdn):
    B, S, WQ = q_r.shape
    G = NSA_GROUPS
    hpg = WQ // HEAD_DIM // G
    nbp = sel.shape[3]
    tq = min(S, 512)
    tk = min(S, 512)
    rows = hpg * tq
    ext = jnp.where(jnp.arange(S)[:, None] // NSA_SEL_LEN == jnp.arange(nbp)[None, :], MASK_BIG, 0.0).astype(BF16)
    return pl.pallas_call(
        functools.partial(_sel_attn_kernel, nh_gdn=nh_gdn, hpg=hpg, tk=tk),
        grid=(B, G, S // tq),
        in_specs=[pl.BlockSpec((1, tq, hpg * HEAD_DIM), lambda b, g, i: (b, i, g)),
                  pl.BlockSpec((1, S, HEAD_DIM), lambda b, g, i: (b, 0, g)),
                  pl.BlockSpec((S, nbp), lambda b, g, i: (0, 0)),
                  pl.BlockSpec((1, S, HEAD_DIM), lambda b, g, i: (b, 0, vsel_blk + g)),
                  pl.BlockSpec((1, 1, tq, nbp), lambda b, g, i: (b, g, i, 0)),
                  pl.BlockSpec((1, tq, small.shape[2]), lambda b, g, i: (b, i, 0))],
        out_specs=pl.BlockSpec((1, tq, hpg * HEAD_DIM), lambda b, g, i: (b, i, g)),
        out_shape=jax.ShapeDtypeStruct((B, S, WQ), BF16),
        scratch_shapes=[pltpu.VMEM((rows, HEAD_DIM + nbp), BF16),
                        pltpu.VMEM((rows, HEAD_DIM), F32),
                        pltpu.VMEM((rows, 2 * HEAD_DIM), F32),
                        pltpu.VMEM((rows, tk), BF16),
                        pltpu.VMEM((rows, HEAD_DIM), F32)],
        compiler_params=_cparams(3),
        name="nsa_sel_attn",
    )(q_r, ksel_r, ext, big, sel, small)


def _win_attn_kernel(q_ref, k_ref, v_ref, sm_ref, o_ref, *, nh_gdn, hpg):
    tq = q_ref.shape[1]
    ts = min(tq, WIN_SUB_TILE)
    span = ts + NSA_WINDOW
    g = pl.program_id(1)
    for sub in range(tq // ts):
        qrows = slice(sub * ts, (sub + 1) * ts)
        t0 = pl.program_id(2) * tq + sub * ts
        k0 = pl.multiple_of(jnp.maximum(t0 - NSA_WINDOW, 0), ts)
        k = k_ref[0, pl.ds(k0, span), :]
        v = v_ref[0, pl.ds(k0, span), :]
        t_row = t0 + lax.broadcasted_iota(jnp.int32, (ts, span), 0)
        spos = k0 + lax.broadcasted_iota(jnp.int32, (ts, span), 1)
        mask = (spos <= t_row) & (spos > t_row - NSA_WINDOW)
        vx = jnp.concatenate([v, jnp.ones((span, HEAD_DIM), BF16)], axis=1)
        q4 = jnp.concatenate([q_ref[0, qrows, h * HEAD_DIM:(h + 1) * HEAD_DIM] for h in range(hpg)], axis=0)
        s = jnp.where(jnp.concatenate([mask] * hpg, axis=0), _dot_nt(q4, k), NEG_INF)
        e = jnp.exp2(s - jnp.max(s, axis=-1, keepdims=True))
        acc = _dot(e.astype(BF16), vx)
        out = acc[:, :HEAD_DIM] / acc[:, HEAD_DIM:]
        for h in range(hpg):
            gate = _gate_col(sm_ref, nh_gdn, g, hpg, h, 2)[qrows]
            o_ref[0, qrows, h * HEAD_DIM:(h + 1) * HEAD_DIM] = (out[h * ts:(h + 1) * ts] * gate).astype(o_ref.dtype)


def _win_attn(q_r, kwin_r, big, vwin_blk, small, nh_gdn):
    B, S, WQ = q_r.shape
    G = NSA_GROUPS
    hpg = WQ // HEAD_DIM // G
    tq = min(S, 512)
    assert S >= tq + NSA_WINDOW
    return pl.pallas_call(
        functools.partial(_win_attn_kernel, nh_gdn=nh_gdn, hpg=hpg),
        grid=(B, G, S // tq),
        in_specs=[pl.BlockSpec((1, tq, hpg * HEAD_DIM), lambda b, g, i: (b, i, g)),
                  pl.BlockSpec((1, S, HEAD_DIM), lambda b, g, i: (b, 0, g)),
                  pl.BlockSpec((1, S, HEAD_DIM), lambda b, g, i: (b, 0, vwin_blk + g)),
                  pl.BlockSpec((1, tq, small.shape[2]), lambda b, g, i: (b, i, 0))],
        out_specs=pl.BlockSpec((1, tq, hpg * HEAD_DIM), lambda b, g, i: (b, i, g)),
        out_shape=jax.ShapeDtypeStruct((B, S, WQ), BF16),
        compiler_params=_cparams(3),
        name="nsa_win_attn",
    )(q_r, kwin_r, big, small)


def _post_norm(x, y, gt, g, b, alpha):
    r = alpha * x + (1.0 + gt) * y
    return _normalize_rows(r) * g + b


def _mixer_out_kernel(oa_ref, oc_ref, os_ref, ow_ref, ma_ref, mb_ref, x_ref, gt_ref, lg_ref, lb_ref,
                      wg_ref, wn_ref, wo_ref, o_ref, *, alpha):
    y_a = _dot(oa_ref[0], wg_ref[...])
    o_b = oc_ref[0].astype(F32) + os_ref[0].astype(F32) + ow_ref[0].astype(F32)
    y_b = _dot(o_b.astype(BF16), wn_ref[...])
    mixed = _sigmoid(ma_ref[0].astype(F32)) * y_a + _sigmoid(mb_ref[0].astype(F32)) * y_b
    y = _dot(mixed.astype(BF16), wo_ref[...])
    o_ref[0] = _post_norm(x_ref[0], y, gt_ref[0], lg_ref[...], lb_ref[...], alpha)


def _mixer_out(o_a, o_c, o_s, o_w, big, ma_blk, mb_blk, x, ada_l, gt_col, ln_g, ln_b, wg, wn, wo, alpha):
    B, S, D = x.shape
    tm = min(S, 512)
    row = lambda blk: pl.BlockSpec((1, tm, D), lambda b, i: (b, i, blk))
    full = lambda shape: pl.BlockSpec(shape, lambda b, i: (0,) * len(shape))
    return pl.pallas_call(
        functools.partial(_mixer_out_kernel, alpha=alpha),
        grid=(B, S // tm),
        in_specs=[row(0), row(0), row(0), row(0), row(ma_blk), row(mb_blk), row(0),
                  pl.BlockSpec((1, 1, D), lambda b, i: (b, 0, gt_col)),
                  full((1, D)), full((1, D)), full((D, D)), full((D, D)), full((D, D))],
        out_specs=row(0),
        out_shape=jax.ShapeDtypeStruct((B, S, D), F32),
        compiler_params=_cparams(2),
        name="mixer_out",
    )(o_a, o_c, o_s, o_w, big, big, x, ada_l, ln_g.reshape(1, D), ln_b.reshape(1, D), wg, wn, wo)


def _first_argmax(vals, lane, big):
    m = jnp.max(vals, axis=-1, keepdims=True)
    first = jnp.min(jnp.where(vals == m, lane, big), axis=-1, keepdims=True)
    return m, first


def _moe_router_kernel(x_ref, sh_ref, sc_ref, rwh_ref, rwl_ref, rb_ref, wsg_ref, wsu_ref, wsd_ref,
                       h_ref, idx_ref, wt_ref, shared_ref, rank_ref, cnt_ref):
    h = _normalize_rows(x_ref[0]) * (1.0 + sc_ref[0]) + sh_ref[0]
    h16 = h.astype(BF16)
    h_ref[0] = h16
    h_lo = (h - h16.astype(F32)).astype(BF16)
    rw_hi = rwh_ref[...]
    logits = _dot_nt(rw_hi, h16) + _dot_nt(rwl_ref[...], h16) + _dot_nt(rw_hi, h_lo)
    scores = _sigmoid(logits)
    biased = scores + rb_ref[...]
    E, tm = scores.shape
    NG = N_EXPERT_GROUPS
    per = E // NG
    b3 = biased.reshape(NG, per, tm)
    in_grp = lax.broadcasted_iota(jnp.int32, (NG, per, tm), 1).astype(F32)
    m1 = jnp.max(b3, axis=1, keepdims=True)
    first = jnp.min(jnp.where(b3 == m1, in_grp, float(per)), axis=1, keepdims=True)
    m2 = jnp.max(jnp.where(in_grp == first, -jnp.inf, b3), axis=1, keepdims=True)
    gscore = (m1 + m2).reshape(NG, tm)
    g_id = lax.broadcasted_iota(jnp.int32, (NG, tm), 0).astype(F32)
    allowed = jnp.zeros((NG, tm), F32)
    for _ in range(TOPK_GROUPS):
        m = jnp.max(gscore, axis=0, keepdims=True)
        first_g = jnp.min(jnp.where(gscore == m, g_id, float(NG)), axis=0, keepdims=True)
        pick = g_id == first_g
        allowed = jnp.where(pick, 1.0, allowed)
        gscore = jnp.where(pick, -jnp.inf, gscore)
    cand = jnp.where(allowed.reshape(NG, 1, tm) > 0.5, b3, NEG_INF).reshape(E, tm)
    e_id = lax.broadcasted_iota(jnp.int32, (E, tm), 0).astype(F32)
    idx_rows, wt_rows, picks = [], [], []
    wsum = jnp.zeros((1, tm), F32)
    chosen = jnp.zeros((E, tm), F32)
    for kk in range(TOP_K):
        m = jnp.max(cand, axis=0, keepdims=True)
        first_e = jnp.min(jnp.where(cand == m, e_id, float(E)), axis=0, keepdims=True)
        pick = e_id == first_e
        w = jnp.sum(jnp.where(pick, scores, 0.0), axis=0, keepdims=True)
        cand = jnp.where(pick, -jnp.inf, cand)
        chosen = jnp.where(pick, 1.0, chosen)
        idx_rows.append(first_e)
        wt_rows.append(w)
        picks.append(pick)
        wsum = wsum + w
    idx_ref[0] = jnp.concatenate(idx_rows, axis=0).astype(jnp.int32)
    wt_ref[0] = jnp.concatenate(wt_rows, axis=0) / wsum * ROUTED_SCALE
    @pl.when((pl.program_id(0) == 0) & (pl.program_id(1) == 0))
    def _():
        cnt_ref[...] = jnp.zeros_like(cnt_ref)

    si = lax.broadcasted_iota(jnp.int32, (tm, tm), 0)
    ti = lax.broadcasted_iota(jnp.int32, (tm, tm), 1)
    before = jnp.where(si < ti, 1.0, 0.0).astype(BF16)
    prior = _dot(chosen.astype(BF16), before) + cnt_ref[...]
    rank_ref[0] = jnp.concatenate([jnp.sum(jnp.where(pk, prior, 0.0), axis=0, keepdims=True) for pk in picks],
                                  axis=0).astype(jnp.int32)
    cnt_ref[...] = cnt_ref[...] + jnp.sum(chosen, axis=1, keepdims=True)
    act = _silu(_dot(h16, wsg_ref[...])) * _dot(h16, wsu_ref[...])
    shared_ref[0] = _dot(act.astype(BF16), wsd_ref[...])


def _moe_router(x, ada_l, sh_col, sc_col, rwt_hi, rwt_lo, router_b, wsg, wsu, wsd):
    B, S, D = x.shape
    E = rwt_hi.shape[0]
    FF = wsg.shape[1]
    tm = min(S, 512)
    row = lambda w: pl.BlockSpec((1, tm, w), lambda b, i: (b, i, 0))
    krow = pl.BlockSpec((1, TOP_K, tm), lambda b, i: (b, 0, i))
    full = lambda shape: pl.BlockSpec(shape, lambda b, i: (0,) * len(shape))
    return pl.pallas_call(
        _moe_router_kernel,
        grid=(B, S // tm),
        in_specs=[row(D),
                  pl.BlockSpec((1, 1, D), lambda b, i: (b, 0, sh_col)),
                  pl.BlockSpec((1, 1, D), lambda b, i: (b, 0, sc_col)),
                  full((E, D)), full((E, D)), full((E, 1)), full((D, FF)), full((D, FF)), full((FF, D))],
        out_specs=[row(D), krow, krow, row(D), krow, full((E, 1))],
        out_shape=[jax.ShapeDtypeStruct((B, S, D), BF16),
                   jax.ShapeDtypeStruct((B, TOP_K, S), jnp.int32),
                   jax.ShapeDtypeStruct((B, TOP_K, S), F32),
                   jax.ShapeDtypeStruct((B, S, D), F32),
                   jax.ShapeDtypeStruct((B, TOP_K, S), jnp.int32),
                   jax.ShapeDtypeStruct((E, 1), F32)],
        compiler_params=_cparams(2),
        name="moe_router",
    )(x, ada_l, ada_l, rwt_hi, rwt_lo, router_b.reshape(E, 1), wsg, wsu, wsd)


def _moe_experts_kernel(be_ref, nu_ref, xs_ref, rw_ref, wg_ref, wu_ref, wd_ref, o_ref):
    @pl.when(pl.program_id(0) < nu_ref[0])
    def _():
        xb = xs_ref[...]
        act = _silu(_dot(xb, wg_ref[0, 0].astype(BF16))) * _dot(xb, wu_ref[0, 0].astype(BF16))
        o_ref[...] = (_dot(act.astype(BF16), wd_ref[0, 0].astype(BF16)) * rw_ref[...]).astype(o_ref.dtype)


def _moe_experts(block_expert, n_used, xs, row_w, layer, wg, wu, wd):
    rows, D = xs.shape
    BM = MOE_ROW_BLOCK
    FF = wg.shape[3]
    grid_spec = pltpu.PrefetchScalarGridSpec(
        num_scalar_prefetch=2,
        grid=(rows // BM,),
        in_specs=[pl.BlockSpec((BM, D), lambda i, be, nu: (i, 0)),
                  pl.BlockSpec((BM, 1), lambda i, be, nu: (i, 0)),
                  pl.BlockSpec((1, 1, D, FF), lambda i, be, nu: (layer, be[i], 0, 0)),
                  pl.BlockSpec((1, 1, D, FF), lambda i, be, nu: (layer, be[i], 0, 0)),
                  pl.BlockSpec((1, 1, FF, D), lambda i, be, nu: (layer, be[i], 0, 0))],
        out_specs=pl.BlockSpec((BM, D), lambda i, be, nu: (i, 0)),
    )
    return pl.pallas_call(
        _moe_experts_kernel,
        grid_spec=grid_spec,
        out_shape=jax.ShapeDtypeStruct((rows, D), BF16),
        compiler_params=_cparams(1),
        name="moe_experts",
    )(block_expert, n_used, xs, row_w, wg, wu, wd)


def _moe_out_kernel(x_ref, shared_ref, yg_ref, gt_ref, lg_ref, lb_ref, o_ref, *, alpha):
    y = shared_ref[0]
    for kk in range(TOP_K):
        y = y + yg_ref[kk, 0].astype(F32)
    o_ref[0] = _post_norm(x_ref[0], y, gt_ref[0], lg_ref[...], lb_ref[...], alpha)


def _moe_out(x, shared, yg, ada_l, gt_col, ln_g, ln_b, alpha):
    B, S, D = x.shape
    tm = min(S, 256)
    row = lambda w: pl.BlockSpec((1, tm, w), lambda b, i: (b, i, 0))
    full = lambda shape: pl.BlockSpec(shape, lambda b, i: (0,) * len(shape))
    return pl.pallas_call(
        functools.partial(_moe_out_kernel, alpha=alpha),
        grid=(B, S // tm),
        in_specs=[row(D), row(D),
                  pl.BlockSpec((TOP_K, 1, tm, D), lambda b, i: (0, b, i, 0)),
                  pl.BlockSpec((1, 1, D), lambda b, i: (b, 0, gt_col)),
                  full((1, D)), full((1, D))],
        out_specs=row(D),
        out_shape=jax.ShapeDtypeStruct((B, S, D), F32),
        compiler_params=_cparams(2),
        name="moe_out",
    )(x, shared, yg, ada_l, ln_g.reshape(1, D), ln_b.reshape(1, D))


def _moe_dest_kernel(start_ref, e_ref, r_ref, o_ref):
    e = e_ref[...]
    acc = r_ref[...]
    for j in range(start_ref.shape[0]):
        acc = acc + jnp.where(e == j, start_ref[j], 0)
    o_ref[...] = acc


def _routing_layout(expert_idx, rank, weights, counts):
    B, K, S = expert_idx.shape
    T = B * S
    n_experts = counts.shape[0]
    BM = MOE_ROW_BLOCK
    TK = T * K
    counts = counts.reshape(n_experts).astype(jnp.int32)
    padded = (counts + BM - 1) // BM * BM
    pad_end = jnp.cumsum(padded)
    pad_start = (pad_end - padded).astype(jnp.int32)
    ts = min(S, 2048)
    spec = pl.BlockSpec((1, K, ts), lambda b, i, tab: (b, 0, i))
    dest = pl.pallas_call(
        _moe_dest_kernel,
        grid_spec=pltpu.PrefetchScalarGridSpec(num_scalar_prefetch=1, grid=(B, S // ts),
                                               in_specs=[spec, spec], out_specs=spec),
        out_shape=jax.ShapeDtypeStruct((B, K, S), jnp.int32),
        compiler_params=_cparams(2),
        name="moe_dest",
    )(pad_start, expert_idx, rank)
    dest = jnp.swapaxes(dest, 0, 1).reshape(TK)
    n_blocks = (TK + n_experts * (BM - 1) + BM - 1) // BM
    rows = n_blocks * BM
    assignment = jnp.arange(TK, dtype=jnp.int32)
    row_asg = (jnp.arange(rows, dtype=jnp.int32) % T).at[dest].add(assignment - dest % T, mode='promise_in_bounds')
    row_tok = row_asg % T
    row_w = jnp.swapaxes(weights, 0, 1).reshape(TK).at[row_asg].get(mode='promise_in_bounds')
    block_start = jnp.arange(n_blocks, dtype=jnp.int32) * BM
    block_expert = jnp.minimum(jnp.sum(pad_end[None, :] <= block_start[:, None], axis=1),
                               n_experts - 1).astype(jnp.int32)
    n_used = (pad_end[-1] // BM).astype(jnp.int32).reshape(1)
    return row_tok, row_w.reshape(rows, 1), dest, block_expert, n_used


def kernel(x, c, positions, ada_w, ada_b, w_in, gdn_conv_w, gdn_a_log, gdn_dt_bias, gdn_norm_w, w_gdn_branch, nsa_cmp_pos_k, nsa_cmp_pos_v, nsa_cmp_k_w1, nsa_cmp_k_w2, nsa_cmp_v_w1, nsa_cmp_v_w2, w_nsa_branch, w_out, ln1_g, ln1_b, router_w, router_bias, w_sh_gate, w_sh_up, w_sh_down, w_e_gate, w_e_up, w_e_down, ln2_g, ln2_b):
    B, S, D = x.shape
    L = ada_w.shape[0]
    T = B * S
    G = NSA_GROUPS
    nh_gdn = gdn_a_log.shape[1]
    gdn_w = nh_gdn * HEAD_DIM
    nsa_w = w_nsa_branch.shape[1]
    kvw = G * HEAD_DIM
    E = router_w.shape[2]
    alpha = (2.0 * L) ** 0.25
    assert gdn_w == D and nsa_w == D and S % NSA_SEL_LEN == 0

    splits = (3 * gdn_w, gdn_w, nh_gdn, nh_gdn, nsa_w, 6 * kvw, 3 * (nsa_w // HEAD_DIM), D, D)
    offs = [0]
    for s_ in splits:
        offs.append(offs[-1] + s_)
    seg = lambda i: slice(offs[i], offs[i + 1])
    w_big = jnp.concatenate([w_in[:, :, seg(0)], w_in[:, :, seg(1)], w_in[:, :, seg(4)],
                             w_in[:, :, seg(7)], w_in[:, :, seg(8)], w_in[:, :, seg(5)]], axis=-1).astype(BF16)
    n_small = splits[2] + splits[3] + splits[6]
    w_small = jnp.concatenate([w_in[:, :, seg(2)], w_in[:, :, seg(3)], w_in[:, :, seg(6)],
                               jnp.zeros((L, D, 128 - n_small), w_in.dtype)], axis=-1).astype(BF16)
    Z_BLK, Q_BLK, MA_BLK, MB_BLK = 3, 4, 5, 6
    kv0 = 7 * D
    n_big = w_big.shape[2]
    tn_big = n_big // 4 if (n_big // 4) % 128 == 0 else 128

    wgb = w_gdn_branch.astype(BF16)
    wnb = w_nsa_branch.astype(BF16)
    wob = w_out.astype(BF16)
    wsg = w_sh_gate.astype(BF16)
    wsu = w_sh_up.astype(BF16)
    wsd = w_sh_down.astype(BF16)
    rwt = jnp.swapaxes(router_w, 1, 2)
    rwt_hi = rwt.astype(BF16)
    rwt_lo = (rwt - rwt_hi.astype(F32)).astype(BF16)
    cmp_w1 = jnp.stack([nsa_cmp_k_w1, nsa_cmp_v_w1], axis=1).astype(BF16)
    cmp_w2 = jnp.stack([nsa_cmp_k_w2, nsa_cmp_v_w2], axis=1).astype(BF16)
    cmp_pos = jnp.stack([nsa_cmp_pos_k, nsa_cmp_pos_v], axis=1).reshape(L, 2, 1, NSA_CMP_LEN * HEAD_DIM)

    ada = _ada(c, ada_w, ada_b)
    cosf, sinf = _rope_tables(positions)
    NC = S // NSA_CMP_STRIDE
    last = jnp.minimum(jnp.arange(NC) * NSA_CMP_STRIDE + NSA_CMP_LEN - 1, S - 1)
    cosc = cosf[:, last]
    sinc = sinf[:, last]
    n_sel = -(-(S // NSA_SEL_LEN) // 128) * 128

    for l in range(L):
        ada_l = ada[l].reshape(B, 1, 6 * D)
        big, small = _proj(x, ada_l, 0, 1, w_big[l], w_small[l], tn_big)
        small_t = jnp.swapaxes(small[:, :, nh_gdn:2 * nh_gdn].reshape(B, S // GDN_CHUNK, GDN_CHUNK, nh_gdn), 2, 3)
        q_a, k_a, v_a = _gdn_prep(big, gdn_conv_w[l], gdn_w)
        o_a = _gdn(q_a, k_a, v_a, big, Z_BLK, small, small_t, gdn_a_log[l], gdn_dt_bias[l], gdn_norm_w[l])
        q_r, ksel_r, kwin_r = _rope_apply(big, Q_BLK, (kv0 + 2 * kvw) // kvw, (kv0 + 4 * kvw) // kvw,
                                          nsa_w, kvw, cosf, sinf)
        hb = big[:, :, kv0:kv0 + 2 * kvw].reshape(B, NC, NSA_CMP_STRIDE, 2 * G, HEAD_DIM)
        hb = hb.transpose(0, 3, 1, 2, 4).reshape(B, 2 * G, NC, NSA_CMP_STRIDE * HEAD_DIM)
        kvc = _compress(hb, cmp_pos[l], cmp_w1[l], cmp_w2[l], cosc, sinc)
        o_c, sel = _cmp_attn(q_r, kvc, small, nh_gdn, n_sel)
        o_s = _sel_attn(q_r, ksel_r, big, (kv0 + 3 * kvw) // HEAD_DIM, sel, small, nh_gdn)
        o_w = _win_attn(q_r, kwin_r, big, (kv0 + 5 * kvw) // HEAD_DIM, small, nh_gdn)
        x = _mixer_out(o_a, o_c, o_s, o_w, big, MA_BLK, MB_BLK, x, ada_l, 2, ln1_g[l], ln1_b[l],
                       wgb[l], wnb[l], wob[l], alpha)
        h2, eidx, ewt, shared, rank, counts = _moe_router(x, ada_l, 3, 4, rwt_hi[l], rwt_lo[l], router_bias[l],
                                                          wsg[l], wsu[l], wsd[l])
        row_tok, row_w, dest, block_expert, n_used = _routing_layout(eidx, rank, ewt, counts)
        xs = h2.reshape(T, D).at[row_tok].get(mode='promise_in_bounds')
        yb = _moe_experts(block_expert, n_used, xs, row_w, l, w_e_gate, w_e_up, w_e_down)
        yg = yb.at[dest].get(mode='promise_in_bounds').reshape(TOP_K, B, S, D)
        x = _moe_out(x, shared, yg, ada_l, 5, ln2_g[l], ln2_b[l], alpha)
    return x
```

```python
import functools
import math

import jax
import jax.numpy as jnp
from jax import lax
from jax.experimental import pallas as pl
from jax.experimental.pallas import tpu as pltpu

F32 = jnp.float32
BF16 = jnp.bfloat16

HEAD_DIM = 128
GDN_CONV = 4
GDN_CHUNK = 64
NSA_GROUPS = 2
NSA_CMP_LEN = 32
NSA_CMP_STRIDE = 16
NSA_SEL_LEN = 64
NSA_SEL_TOP = 16
NSA_WINDOW = 512
ROPE_THETA = 10000.0
N_EXPERTS = 64
N_EXPERT_GROUPS = 8
TOPK_GROUPS = 4
TOP_K = 8
ROUTED_SCALE = 2.5
LN_EPS = 1e-5
NEG_INF = -1e30
SEL_FORCE = 1e6
MOE_ROW_BLOCK = 512
Q_SCALE_LOG2E = (HEAD_DIM ** -0.5) * math.log2(math.e)
MASK_BIG = 2.0 ** 100
N_ROW_PARTS = 2
GDN_PHASE1_CHUNKS = 4
WIN_SUB_TILE = 128
_ARB = "arbitrary"


def _cparams(n_axes):
    return pltpu.CompilerParams(dimension_semantics=(_ARB,) * n_axes)


def _sigmoid(x):
    return 1.0 / (1.0 + jnp.exp(-x))


def _silu(x):
    return x * _sigmoid(x)


def _dot(a, b):
    return jnp.dot(a, b, preferred_element_type=F32)


def _dot_nt(a, b):
    return lax.dot_general(a, b, (((1,), (1,)), ((), ())), preferred_element_type=F32)


def _dot_tn(a, b):
    return lax.dot_general(a, b, (((0,), (0,)), ((), ())), preferred_element_type=F32)


def _tile_lanes(x, n):
    return jnp.concatenate([x] * n, axis=1)


def _split3(x):
    x0 = x.astype(BF16)
    r1 = x - x0.astype(F32)
    x1 = r1.astype(BF16)
    x2 = (r1 - x1.astype(F32)).astype(BF16)
    return x0, x1, x2


def _dot_sel_right(x, sel_bf16):
    x0, x1, x2 = _split3(x)
    return _dot(x0, sel_bf16) + _dot(x1, sel_bf16) + _dot(x2, sel_bf16)


def _dot_sel_left(sel_bf16, x):
    x0, x1, x2 = _split3(x)
    return _dot(sel_bf16, x0) + _dot(sel_bf16, x1) + _dot(sel_bf16, x2)


def _normalize_rows(x):
    mu = jnp.mean(x, axis=-1, keepdims=True)
    xc = x - mu
    var = jnp.mean(xc * xc, axis=-1, keepdims=True)
    return xc * lax.rsqrt(var + LN_EPS)


def _ada_kernel(c_ref, w_ref, b_ref, o_ref):
    cond = _silu(c_ref[...])
    o_ref[0] = jnp.dot(cond, w_ref[0], preferred_element_type=F32,
                       precision=lax.Precision.HIGHEST) + b_ref[0]


def _ada(c, ada_w, ada_b):
    L, D, N = ada_w.shape
    B = c.shape[0]
    tn = min(N, 1536)
    return pl.pallas_call(
        _ada_kernel,
        grid=(L, N // tn),
        in_specs=[pl.BlockSpec((B, D), lambda l, j: (0, 0)),
                  pl.BlockSpec((1, D, tn), lambda l, j: (l, 0, j)),
                  pl.BlockSpec((1, 1, tn), lambda l, j: (l, 0, j))],
        out_specs=pl.BlockSpec((1, B, tn), lambda l, j: (l, 0, j)),
        out_shape=jax.ShapeDtypeStruct((L, B, N), F32),
        compiler_params=_cparams(2),
        name="ada",
    )(c, ada_w, ada_b.reshape(L, 1, N))


def _proj_kernel(x_ref, sh_ref, sc_ref, w_ref, ws_ref, o_ref, os_ref, oc_ref, h_ref, *, head_tile, head_off):
    @pl.when(pl.program_id(2) == 0)
    def _():
        h = _normalize_rows(x_ref[0]) * (1.0 + sc_ref[0]) + sh_ref[0]
        h_ref[...] = h.astype(BF16)
        os_ref[0] = _dot(h_ref[...], ws_ref[...])

    res = _dot(h_ref[...], w_ref[...]).astype(o_ref.dtype)
    o_ref[0] = res

    @pl.when(pl.program_id(2) == head_tile)
    def _():
        for jj in range(oc_ref.shape[1]):
            oc_ref[0, jj] = res[:, head_off + jj * HEAD_DIM:head_off + (jj + 1) * HEAD_DIM]


def _proj(x, ada_l, sh_col, sc_col, w, w_small, tn, head_col, n_head_cols):
    B, S, D = x.shape
    N = w.shape[1]
    NS = w_small.shape[1]
    tm = min(S, 1024)
    head_tile, head_off = head_col // tn, head_col % tn
    assert head_off % HEAD_DIM == 0 and head_off + n_head_cols * HEAD_DIM <= tn
    return pl.pallas_call(
        functools.partial(_proj_kernel, head_tile=head_tile, head_off=head_off),
        grid=(B, S // tm, N // tn),
        in_specs=[pl.BlockSpec((1, tm, D), lambda b, i, j: (b, i, 0)),
                  pl.BlockSpec((1, 1, D), lambda b, i, j: (b, 0, sh_col)),
                  pl.BlockSpec((1, 1, D), lambda b, i, j: (b, 0, sc_col)),
                  pl.BlockSpec((D, tn), lambda b, i, j: (0, j)),
                  pl.BlockSpec((D, NS), lambda b, i, j: (0, 0))],
        out_specs=[pl.BlockSpec((1, tm, tn), lambda b, i, j: (b, i, j)),
                   pl.BlockSpec((1, tm, NS), lambda b, i, j: (b, i, 0)),
                   pl.BlockSpec((1, n_head_cols, tm, HEAD_DIM), lambda b, i, j: (b, 0, i, 0))],
        out_shape=[jax.ShapeDtypeStruct((B, S, N), BF16),
                   jax.ShapeDtypeStruct((B, S, NS), F32),
                   jax.ShapeDtypeStruct((B, n_head_cols, S, HEAD_DIM), BF16)],
        scratch_shapes=[pltpu.VMEM((tm, D), BF16)],
        compiler_params=_cparams(3),
        name="proj",
    )(x, ada_l, ada_l, w, w_small)


def _gdn_prep_kernel(x_ref, w_ref, q_ref, k_ref, v_ref, carry_ref):
    ts = x_ref.shape[1]
    width = q_ref.shape[2]
    nh = width // HEAD_DIM

    @pl.when(pl.program_id(1) == 0)
    def _():
        carry_ref[...] = jnp.zeros_like(carry_ref)

    for part, o_ref in enumerate((q_ref, k_ref, v_ref)):
        for h in range(nh):
            c0 = part * width + h * HEAD_DIM
            cols = slice(c0, c0 + HEAD_DIM)
            xx = jnp.concatenate([carry_ref[:, cols], x_ref[0, :, cols].astype(F32)], axis=0)
            w = w_ref[:, cols]
            y = xx[8:8 + ts] * w[3:4]
            for kk in range(GDN_CONV - 1):
                off = 8 - (GDN_CONV - 1) + kk
                y = y + xx[off:off + ts] * w[kk:kk + 1]
            y = _silu(y)
            if part < 2:
                y = y * lax.rsqrt(jnp.sum(y * y, axis=-1, keepdims=True) + 1e-6)
            if part == 0:
                y = y * (HEAD_DIM ** -0.5)
            o_ref[0, :, h * HEAD_DIM:(h + 1) * HEAD_DIM] = y.astype(o_ref.dtype)
    carry_ref[...] = x_ref[0, ts - 8:ts, :].astype(F32)


def _gdn_prep(big, conv_w, width):
    B, S, _ = big.shape
    ts = min(S, 512)
    out = jax.ShapeDtypeStruct((B, S, width), BF16)
    ospec = pl.BlockSpec((1, ts, width), lambda b, i: (b, i, 0))
    return pl.pallas_call(
        _gdn_prep_kernel,
        grid=(B, S // ts),
        in_specs=[pl.BlockSpec((1, ts, 3 * width), lambda b, i: (b, i, 0)),
                  pl.BlockSpec((GDN_CONV, 3 * width), lambda b, i: (0, 0))],
        out_specs=[ospec, ospec, ospec],
        out_shape=[out, out, out],
        scratch_shapes=[pltpu.VMEM((8, 3 * width), F32)],
        compiler_params=_cparams(2),
        name="gdn_prep",
    )(big, conv_w)


def _softplus(x):
    return jnp.maximum(x, 0.0) + jnp.log(1.0 + jnp.exp(-jnp.abs(x)))


def _gdn_kernel(q_ref, k_ref, v_ref, z_ref, sm_ref, smt_ref, alog_ref, alogt_ref, dtb_ref, dtbt_ref,
                nw_ref, o_ref, state_ref, u_s, wq_s, attn_s, kd_s, dec_s):
    ts = q_ref.shape[1]
    nh = q_ref.shape[2] // HEAD_DIM
    C = GDN_CHUNK
    R = 2 * C
    npair = nh // 2
    nchunks = ts // C

    @pl.when(pl.program_id(1) == 0)
    def _():
        state_ref[...] = jnp.zeros_like(state_ref)

    ci = lax.broadcasted_iota(jnp.int32, (C, C), 0)
    cj = lax.broadcasted_iota(jnp.int32, (C, C), 1)
    tril = jnp.where(ci >= cj, 1.0, 0.0).astype(BF16)
    triu = jnp.where(cj >= ci, 1.0, 0.0).astype(BF16)
    ii = lax.broadcasted_iota(jnp.int32, (R, R), 0)
    jj = lax.broadcasted_iota(jnp.int32, (R, R), 1)
    same_head = (ii // C) == (jj // C)
    incl = same_head & (ii >= jj)
    strict = same_head & (ii > jj)
    eye = jnp.where(ii == jj, 1.0, 0.0)
    nw = nw_ref[...]

    def pair_rows(ref, rows, p):
        return jnp.concatenate([ref[0, rows, (2 * p) * HEAD_DIM:(2 * p + 1) * HEAD_DIM],
                                ref[0, rows, (2 * p + 1) * HEAD_DIM:(2 * p + 2) * HEAD_DIM]], axis=0)

    def pair_col(x, p):
        return jnp.concatenate([x[:, 2 * p:2 * p + 1], x[:, 2 * p + 1:2 * p + 2]], axis=0)

    def phase1(j, carry):
        cs = [j * GDN_PHASE1_CHUNKS + d for d in range(GDN_PHASE1_CHUNKS)]
        rows_c, beta_c, gc_c, gcr_c, glb_c = [], [], [], [], []
        for c in cs:
            rows = slice(c * C, (c + 1) * C)
            sm = sm_ref[0, rows, :]
            g = -jnp.exp(alog_ref[...]) * _softplus(sm[:, nh:2 * nh] + dtb_ref[...])
            gt = -jnp.exp(alogt_ref[...]) * _softplus(smt_ref[0, c] + dtbt_ref[...])
            gc = _dot_sel_left(tril, g)
            g_last = gc[C - 1:C, :]
            dec_s[c] = jnp.exp(g_last)
            rows_c.append(rows)
            beta_c.append(_sigmoid(sm[:, 0:nh]))
            gc_c.append(gc)
            gcr_c.append(_dot_sel_right(gt, triu))
            glb_c.append(jnp.broadcast_to(g_last, (C, nh)))
        items = [(ci, p) for ci in range(len(cs)) for p in range(npair)]
        idx = range(len(items))
        q2 = [pair_rows(q_ref, rows_c[ci], p).astype(F32) for ci, p in items]
        k2 = [pair_rows(k_ref, rows_c[ci], p).astype(F32) for ci, p in items]
        b2 = [pair_col(beta_c[ci], p) for ci, p in items]
        g_col = [pair_col(gc_c[ci], p) for ci, p in items]
        kb = [k2[i] * b2[i] for i in idx]
        kq = [_dot_nt(jnp.concatenate([kb[i], q2[i]], axis=0).astype(BF16), k2[i].astype(BF16))
              for i in idx]
        decay = []
        for i, (ci, p) in enumerate(items):
            gcr = gcr_c[ci]
            g_row = jnp.concatenate([gcr[2 * p:2 * p + 1, :], gcr[2 * p + 1:2 * p + 2, :]], axis=1)
            decay.append(jnp.where(incl, jnp.exp(jnp.where(incl, g_col[i] - g_row, 0.0)), 0.0))
        a = [jnp.where(strict, kq[i][:R] * decay[i], 0.0) for i in idx]
        for i, (ci, p) in enumerate(items):
            attn_s[cs[ci], p] = (kq[i][R:] * decay[i]).astype(BF16)
        x = [eye - a[i] for i in idx]
        pw = a
        n = 2
        while n < C:
            pw16 = [pw[i].astype(BF16) for i in idx]
            pw = [_dot(pw16[i], pw16[i]) for i in idx]
            x = [x[i] + _dot(x[i].astype(BF16), pw[i].astype(BF16)) for i in idx]
            n *= 2
        eg = [jnp.exp(g_col[i]) for i in idx]
        sol = []
        for i, (ci, p) in enumerate(items):
            v2 = pair_rows(v_ref, rows_c[ci], p).astype(F32)
            r = jnp.concatenate([v2 * b2[i], kb[i] * eg[i]], axis=1)
            sol.append(_dot(x[i].astype(BF16), r.astype(BF16)))
        for i, (ci, p) in enumerate(items):
            c = cs[ci]
            u_s[c, p] = sol[i][:, :HEAD_DIM]
            w = sol[i][:, HEAD_DIM:]
            qg = q2[i] * eg[i]
            for e in range(2):
                wq_s[c, 2 * p + e] = jnp.concatenate([w[e * C:(e + 1) * C], qg[e * C:(e + 1) * C]],
                                                     axis=0).astype(BF16)
            kd_s[c, p] = (k2[i] * jnp.exp(pair_col(glb_c[ci], p) - g_col[i])).astype(BF16)
        return carry

    def phase2(c):
        rows = slice(c * C, (c + 1) * C)
        dec = dec_s[c]
        res = [_dot(wq_s[c, h], state_ref[h].astype(BF16)) for h in range(nh)]
        for p in range(npair):
            ws = jnp.concatenate([res[2 * p][:C], res[2 * p + 1][:C]], axis=0)
            qs = jnp.concatenate([res[2 * p][C:], res[2 * p + 1][C:]], axis=0)
            v_new = u_s[c, p] - ws
            v16 = v_new.astype(BF16)
            o2 = qs + _dot(attn_s[c, p], v16)
            kd = kd_s[c, p]
            for e in range(2):
                h = 2 * p + e
                part = slice(e * C, (e + 1) * C)
                state_ref[h] = state_ref[h] * dec[:, h:h + 1] + _dot_tn(kd[part], v16[part])
                o = o2[part]
                o = o * lax.rsqrt(jnp.mean(o * o, axis=-1, keepdims=True) + 1e-6) * nw
                cols = slice(h * HEAD_DIM, (h + 1) * HEAD_DIM)
                z = z_ref[0, rows, cols].astype(F32)
                o_ref[0, rows, cols] = (o * _silu(z)).astype(o_ref.dtype)

    def phase2_group(j):
        for d in range(GDN_PHASE1_CHUNKS):
            phase2(j * GDN_PHASE1_CHUNKS + d)

    assert nchunks % GDN_PHASE1_CHUNKS == 0
    ngroups = nchunks // GDN_PHASE1_CHUNKS
    phase1(0, 0)
    for j in range(1, ngroups):
        phase2_group(j - 1)
        phase1(j, 0)
    phase2_group(ngroups - 1)


def _gdn(q, k, v, big, z_blk, small, small_t, a_log, dt_bias, norm_w):
    B, S, W = q.shape
    nh = W // HEAD_DIM
    assert nh % 2 == 0 and 2 * GDN_CHUNK == HEAD_DIM
    ts = min(S, 512)
    nc = ts // GDN_CHUNK
    spec = pl.BlockSpec((1, ts, W), lambda b, i: (b, i, 0))
    full = lambda shape: pl.BlockSpec(shape, lambda b, i: (0,) * len(shape))
    return pl.pallas_call(
        _gdn_kernel,
        grid=(B, S // ts),
        in_specs=[spec, spec, spec,
                  pl.BlockSpec((1, ts, W), lambda b, i: (b, i, z_blk)),
                  pl.BlockSpec((1, ts, small.shape[2]), lambda b, i: (b, i, 0)),
                  pl.BlockSpec((1, ts // GDN_CHUNK, nh, GDN_CHUNK), lambda b, i: (b, i, 0, 0)),
                  full((1, nh)), full((nh, 1)), full((1, nh)), full((nh, 1)),
                  full((1, HEAD_DIM))],
        out_specs=spec,
        out_shape=jax.ShapeDtypeStruct((B, S, W), BF16),
        scratch_shapes=[pltpu.VMEM((nh, HEAD_DIM, HEAD_DIM), F32),
                        pltpu.VMEM((nc, nh // 2, 2 * GDN_CHUNK, HEAD_DIM), F32),
                        pltpu.VMEM((nc, nh, 2 * GDN_CHUNK, HEAD_DIM), BF16),
                        pltpu.VMEM((nc, nh // 2, 2 * GDN_CHUNK, 2 * GDN_CHUNK), BF16),
                        pltpu.VMEM((nc, nh // 2, 2 * GDN_CHUNK, HEAD_DIM), BF16),
                        pltpu.VMEM((nc, 1, nh), F32)],
        compiler_params=_cparams(2),
        name="gdn",
    )(q, k, v, big, small, small_t, a_log.reshape(1, nh), a_log.reshape(nh, 1),
      dt_bias.reshape(1, nh), dt_bias.reshape(nh, 1), norm_w.reshape(1, HEAD_DIM))


def _rope_table_kernel(pos_ref, invf_ref, sign_ref, cos_ref, sin_ref):
    ang = pos_ref[0].astype(F32) * invf_ref[...]
    cos_ref[0] = jnp.cos(ang)
    sin_ref[0] = jnp.sin(ang) * sign_ref[...]


def _rope_tables(positions):
    B, S = positions.shape
    half = HEAD_DIM // 2
    inv = ROPE_THETA ** (-jnp.arange(half, dtype=F32) / half)
    invf = jnp.concatenate([inv, inv]).reshape(1, HEAD_DIM)
    sign = jnp.concatenate([-jnp.ones((half,), F32), jnp.ones((half,), F32)]).reshape(1, HEAD_DIM)
    ts = min(S, 1024)
    out = jax.ShapeDtypeStruct((B, S, HEAD_DIM), F32)
    ospec = pl.BlockSpec((1, ts, HEAD_DIM), lambda b, i: (b, i, 0))
    return pl.pallas_call(
        _rope_table_kernel,
        grid=(B, S // ts),
        in_specs=[pl.BlockSpec((1, ts, 1), lambda b, i: (b, i, 0)),
                  pl.BlockSpec((1, HEAD_DIM), lambda b, i: (0, 0)),
                  pl.BlockSpec((1, HEAD_DIM), lambda b, i: (0, 0))],
        out_specs=[ospec, ospec],
        out_shape=[out, out],
        compiler_params=_cparams(2),
        name="rope_table",
    )(positions.reshape(B, S, 1), invf, sign)


def _rope_rows(x, cosf, sinf):
    return x * cosf + pltpu.roll(x, HEAD_DIM // 2, 1) * sinf


def _rope_apply_kernel(q_ref, ks_ref, kw_ref, cos_ref, sin_ref, qo_ref, kso_ref, kwo_ref):
    cosf = cos_ref[0]
    sinf = sin_ref[0]
    for src, dst, mult in ((q_ref, qo_ref, Q_SCALE_LOG2E), (ks_ref, kso_ref, None), (kw_ref, kwo_ref, None)):
        for h in range(src.shape[2] // HEAD_DIM):
            cols = slice(h * HEAD_DIM, (h + 1) * HEAD_DIM)
            r = _rope_rows(src[0, :, cols].astype(F32), cosf, sinf)
            if mult is not None:
                r = r * mult
            dst[0, :, cols] = r.astype(dst.dtype)


def _rope_apply(big, q_blk, ksel_blk, kwin_blk, wq, wkv, cosf, sinf):
    B, S, _ = big.shape
    ts = min(S, 512)
    tab = pl.BlockSpec((1, ts, HEAD_DIM), lambda b, i: (b, i, 0))
    return pl.pallas_call(
        _rope_apply_kernel,
        grid=(B, S // ts),
        in_specs=[pl.BlockSpec((1, ts, wq), lambda b, i: (b, i, q_blk)),
                  pl.BlockSpec((1, ts, wkv), lambda b, i: (b, i, ksel_blk)),
                  pl.BlockSpec((1, ts, wkv), lambda b, i: (b, i, kwin_blk)),
                  tab, tab],
        out_specs=[pl.BlockSpec((1, ts, wq), lambda b, i: (b, i, 0)),
                   pl.BlockSpec((1, ts, wkv), lambda b, i: (b, i, 0)),
                   pl.BlockSpec((1, ts, wkv), lambda b, i: (b, i, 0))],
        out_shape=[jax.ShapeDtypeStruct((B, S, wq), BF16),
                   jax.ShapeDtypeStruct((B, S, wkv), BF16),
                   jax.ShapeDtypeStruct((B, S, wkv), BF16)],
        compiler_params=_cparams(2),
        name="rope_apply",
    )(big, big, big, cosf, sinf)


def _compress_kernel(hb_ref, pos_ref, w1_ref, w2_ref, cos_ref, sin_ref, o_ref):
    hb = hb_ref[0, 0]
    w1 = w1_ref[0]
    half = hb.shape[1]
    p0 = _dot(hb, w1[:half])
    p1 = _dot(hb, w1[half:])
    nc = p0.shape[0]
    pos8 = jnp.broadcast_to(pos_ref[0], (8, 2 * half)).astype(BF16)
    pb = _dot(pos8, w1)[0:1]
    pre = p0 + pltpu.roll(p1, nc - 1, 0) + pb
    out = _dot(_silu(pre).astype(BF16), w2_ref[0])
    roped = _rope_rows(out, cos_ref[0], sin_ref[0])
    is_key = pl.program_id(1) < NSA_GROUPS
    o_ref[0, 0] = jnp.where(is_key, roped, out).astype(o_ref.dtype)


def _compress(hb, pos_flat, w1, w2, cosc, sinc):
    B, J, NC, HW = hb.shape
    G = NSA_GROUPS
    return pl.pallas_call(
        _compress_kernel,
        grid=(B, J),
        in_specs=[pl.BlockSpec((1, 1, NC, HW), lambda b, j: (b, j, 0, 0)),
                  pl.BlockSpec((1, 1, 2 * HW), lambda b, j: (j // G, 0, 0)),
                  pl.BlockSpec((1, 2 * HW, HEAD_DIM), lambda b, j: (j // G, 0, 0)),
                  pl.BlockSpec((1, HEAD_DIM, HEAD_DIM), lambda b, j: (j // G, 0, 0)),
                  pl.BlockSpec((1, NC, HEAD_DIM), lambda b, j: (b, 0, 0)),
                  pl.BlockSpec((1, NC, HEAD_DIM), lambda b, j: (b, 0, 0))],
        out_specs=pl.BlockSpec((1, 1, NC, HEAD_DIM), lambda b, j: (b, j, 0, 0)),
        out_shape=jax.ShapeDtypeStruct((B, J, NC, HEAD_DIM), BF16),
        compiler_params=_cparams(2),
        name="nsa_compress",
    )(hb, pos_flat, w1, w2, cosc, sinc)


def _gate_col(g_ref, nh_gdn, group, hpg, h, branch):
    col = None
    for gi in range(NSA_GROUPS):
        c = 2 * nh_gdn + (gi * hpg + h) * 3 + branch
        cand = g_ref[0, :, c:c + 1]
        col = cand if col is None else jnp.where(group == gi, cand, col)
    return _sigmoid(col)


def _cmp_attn_kernel(q_ref, kc_ref, vc_ref, sm_ref, o_ref, sel_ref, *, nh_gdn, hpg):
    tq = q_ref.shape[1]
    nc = kc_ref.shape[2]
    nb = sel_ref.shape[3]
    g = pl.program_id(1)
    t0 = pl.program_id(2) * tq
    t_nc = t0 + lax.broadcasted_iota(jnp.int32, (tq, nc), 0)
    n_nc = lax.broadcasted_iota(jnp.int32, (tq, nc), 1)
    valid = (n_nc * NSA_CMP_STRIDE + (NSA_CMP_LEN - 1)) <= t_nc
    t_col = t0 + lax.broadcasted_iota(jnp.int32, (tq, 1), 0)
    has_valid = jnp.where(t_col >= NSA_CMP_LEN - 1, 1.0, 0.0)
    kc = kc_ref[0, 0]
    vc = vc_ref[0, 0]
    psum = jnp.zeros((tq, nc), F32)
    for h in range(hpg):
        cols = slice(h * HEAD_DIM, (h + 1) * HEAD_DIM)
        s = jnp.where(valid, _dot_nt(q_ref[0, :, cols], kc), NEG_INF)
        e = jnp.exp2(s - jnp.max(s, axis=-1, keepdims=True))
        p = e * (has_valid / jnp.sum(e, axis=-1, keepdims=True))
        gate = _gate_col(sm_ref, nh_gdn, g, hpg, h, 0)
        o_ref[0, :, cols] = (_dot(p.astype(BF16), vc) * gate).astype(o_ref.dtype)
        psum = psum + p
    sj = lax.broadcasted_iota(jnp.int32, (nb, nc), 0) * NSA_SEL_LEN
    cn = lax.broadcasted_iota(jnp.int32, (nb, nc), 1) * NSA_CMP_STRIDE
    overlap_t = jnp.where((cn <= sj + (NSA_SEL_LEN - 1)) & (cn + (NSA_CMP_LEN - 1) >= sj), 1.0, 0.0).astype(BF16)
    p0, p1, p2 = _split3(psum)
    imp = _dot_nt(overlap_t, p0) + _dot_nt(overlap_t, p1) + _dot_nt(overlap_t, p2)
    t_nb = t0 + lax.broadcasted_iota(jnp.int32, (nb, tq), 1)
    blk = lax.broadcasted_iota(jnp.int32, (nb, tq), 0)
    cur = t_nb // NSA_SEL_LEN
    forced = (blk == 0) | (blk == cur) | (blk == cur - 1)
    score = jnp.where(forced, SEL_FORCE, jnp.where(blk * NSA_SEL_LEN <= t_nb, imp, -1.0))
    sel = jnp.zeros((nb, tq), F32)
    blk_f = blk.astype(F32)
    for _ in range(min(NSA_SEL_TOP, nb)):
        m = jnp.max(score, axis=0, keepdims=True)
        first = jnp.min(jnp.where(score == m, blk_f, float(nb)), axis=0, keepdims=True)
        pick = blk_f == first
        sel = jnp.where(pick, 1.0, sel)
        score = jnp.where(pick, -jnp.inf, score)
    sel_ref[0, 0] = sel.T.astype(sel_ref.dtype)


def _cmp_attn(q_r, kvc, small, nh_gdn, n_sel):
    B, S, WQ = q_r.shape
    G = NSA_GROUPS
    hpg = WQ // HEAD_DIM // G
    NC = kvc.shape[2]
    tq = min(S, 512)
    return pl.pallas_call(
        functools.partial(_cmp_attn_kernel, nh_gdn=nh_gdn, hpg=hpg),
        grid=(B, G, S // tq),
        in_specs=[pl.BlockSpec((1, tq, hpg * HEAD_DIM), lambda b, g, i: (b, i, g)),
                  pl.BlockSpec((1, 1, NC, HEAD_DIM), lambda b, g, i: (b, g, 0, 0)),
                  pl.BlockSpec((1, 1, NC, HEAD_DIM), lambda b, g, i: (b, G + g, 0, 0)),
                  pl.BlockSpec((1, tq, small.shape[2]), lambda b, g, i: (b, i, 0))],
        out_specs=[pl.BlockSpec((1, tq, hpg * HEAD_DIM), lambda b, g, i: (b, i, g)),
                   pl.BlockSpec((1, 1, tq, n_sel), lambda b, g, i: (b, g, i, 0))],
        out_shape=[jax.ShapeDtypeStruct((B, S, WQ), BF16),
                   jax.ShapeDtypeStruct((B, G, S, n_sel), BF16)],
        compiler_params=_cparams(3),
        name="nsa_cmp_attn",
    )(q_r, kvc, kvc, small)


def _sel_attn_kernel(q_ref, k_ref, ext_ref, v_ref, sel_ref, sm_ref, o_ref, qx_ref, m_ref, acc_ref, p_ref, a_ref,
                     *, nh_gdn, hpg, tk):
    tq = q_ref.shape[1]
    nb = sel_ref.shape[3]
    g = pl.program_id(1)
    t0 = pl.program_id(2) * tq
    assert tk % tq == 0
    unsel = (sel_ref[0, 0].astype(F32) - 1.0).astype(BF16)
    for h in range(hpg):
        qx_ref[h * tq:(h + 1) * tq, 0:HEAD_DIM] = q_ref[0, :, h * HEAD_DIM:(h + 1) * HEAD_DIM]
        qx_ref[h * tq:(h + 1) * tq, HEAD_DIM:HEAD_DIM + nb] = unsel
    m_ref[...] = jnp.full_like(m_ref, NEG_INF)
    acc_ref[...] = jnp.zeros_like(acc_ref)
    p_ref[...] = jnp.zeros_like(p_ref)
    a_ref[...] = jnp.ones_like(a_ref)
    ones = jnp.ones((tk, HEAD_DIM), BF16)
    pr = hpg * tq // N_ROW_PARTS
    parts = [slice(part * pr, (part + 1) * pr) for part in range(N_ROW_PARTS)]

    def apply_values(kt):
        k0 = pl.multiple_of(kt * tk, tk)
        vx = jnp.concatenate([v_ref[0, pl.ds(k0, tk), :], ones], axis=1)
        for rows in parts:
            acc_ref[rows, :] = _tile_lanes(a_ref[rows, :], 2) * acc_ref[rows, :] + _dot(p_ref[rows, :], vx)

    def scores(kt, causal):
        k0 = pl.multiple_of(kt * tk, tk)
        kx = jnp.concatenate([k_ref[0, pl.ds(k0, tk), :], ext_ref[pl.ds(k0, tk), :]], axis=1)
        if causal:
            t_row = t0 + lax.broadcasted_iota(jnp.int32, (tq, tk), 0)
            key = k0 + lax.broadcasted_iota(jnp.int32, (tq, tk), 1)
            keep = jnp.concatenate([key <= t_row] * (hpg // N_ROW_PARTS), axis=0)
        for rows in parts:
            s = _dot_nt(qx_ref[rows, :], kx)
            if causal:
                s = jnp.where(keep, s, NEG_INF)
            m_old = m_ref[rows, :]
            m_new = jnp.maximum(m_old, jnp.max(s, axis=-1, keepdims=True))
            a_ref[rows, :] = jnp.exp2(m_old - m_new)
            p_ref[rows, :] = jnp.exp2(s - _tile_lanes(m_new, tk // HEAD_DIM)).astype(BF16)
            m_ref[rows, :] = m_new

    kt_diag = t0 // tk

    def body(kt, carry):
        apply_values(jnp.maximum(kt - 1, 0))
        scores(kt, False)
        return carry

    lax.fori_loop(0, kt_diag, body, 0)
    apply_values(jnp.maximum(kt_diag - 1, 0))
    scores(kt_diag, True)
    apply_values(kt_diag)
    for h in range(hpg):
        rows = slice(h * tq, (h + 1) * tq)
        gate = _gate_col(sm_ref, nh_gdn, g, hpg, h, 1)
        out = acc_ref[rows, 0:HEAD_DIM] / acc_ref[rows, HEAD_DIM:2 * HEAD_DIM]
        o_ref[0, :, h * HEAD_DIM:(h + 1) * HEAD_DIM] = (out * gate).astype(o_ref.dtype)


def _sel_attn(q_r, ksel_r, big, vsel_blk, sel, small, nh_gdn):
    B, S, WQ = q_r.shape
    G = NSA_GROUPS
    hpg = WQ // HEAD_DIM // G
    nbp = sel.shape[3]
    tq = min(S, 512)
    tk = min(S, 512)
    rows = hpg * tq
    ext = jnp.where(jnp.arange(S)[:, None] // NSA_SEL_LEN == jnp.arange(nbp)[None, :], MASK_BIG, 0.0).astype(BF16)
    return pl.pallas_call(
        functools.partial(_sel_attn_kernel, nh_gdn=nh_gdn, hpg=hpg, tk=tk),
        grid=(B, G, S // tq),
        in_specs=[pl.BlockSpec((1, tq, hpg * HEAD_DIM), lambda b, g, i: (b, i, g)),
                  pl.BlockSpec((1, S, HEAD_DIM), lambda b, g, i: (b, 0, g)),
                  pl.BlockSpec((S, nbp), lambda b, g, i: (0, 0)),
                  pl.BlockSpec((1, S, HEAD_DIM), lambda b, g, i: (b, 0, vsel_blk + g)),
                  pl.BlockSpec((1, 1, tq, nbp), lambda b, g, i: (b, g, i, 0)),
                  pl.BlockSpec((1, tq, small.shape[2]), lambda b, g, i: (b, i, 0))],
        out_specs=pl.BlockSpec((1, tq, hpg * HEAD_DIM), lambda b, g, i: (b, i, g)),
        out_shape=jax.ShapeDtypeStruct((B, S, WQ), BF16),
        scratch_shapes=[pltpu.VMEM((rows, HEAD_DIM + nbp), BF16),
                        pltpu.VMEM((rows, HEAD_DIM), F32),
                        pltpu.VMEM((rows, 2 * HEAD_DIM), F32),
                        pltpu.VMEM((rows, tk), BF16),
                        pltpu.VMEM((rows, HEAD_DIM), F32)],
        compiler_params=_cparams(3),
        name="nsa_sel_attn",
    )(q_r, ksel_r, ext, big, sel, small)


def _win_attn_kernel(q_ref, k_ref, v_ref, sm_ref, o_ref, *, nh_gdn, hpg):
    tq = q_ref.shape[1]
    ts = min(tq, WIN_SUB_TILE)
    span = ts + NSA_WINDOW
    g = pl.program_id(1)
    for sub in range(tq // ts):
        qrows = slice(sub * ts, (sub + 1) * ts)
        t0 = pl.program_id(2) * tq + sub * ts
        k0 = pl.multiple_of(jnp.maximum(t0 - NSA_WINDOW, 0), ts)
        k = k_ref[0, pl.ds(k0, span), :]
        v = v_ref[0, pl.ds(k0, span), :]
        t_row = t0 + lax.broadcasted_iota(jnp.int32, (ts, span), 0)
        spos = k0 + lax.broadcasted_iota(jnp.int32, (ts, span), 1)
        mask = (spos <= t_row) & (spos > t_row - NSA_WINDOW)
        vx = jnp.concatenate([v, jnp.ones((span, HEAD_DIM), BF16)], axis=1)
        q4 = jnp.concatenate([q_ref[0, qrows, h * HEAD_DIM:(h + 1) * HEAD_DIM] for h in range(hpg)], axis=0)
        s = jnp.where(jnp.concatenate([mask] * hpg, axis=0), _dot_nt(q4, k), NEG_INF)
        e = jnp.exp2(s - jnp.max(s, axis=-1, keepdims=True))
        acc = _dot(e.astype(BF16), vx)
        out = acc[:, :HEAD_DIM] / acc[:, HEAD_DIM:]
        for h in range(hpg):
            gate = _gate_col(sm_ref, nh_gdn, g, hpg, h, 2)[qrows]
            o_ref[0, qrows, h * HEAD_DIM:(h + 1) * HEAD_DIM] = (out[h * ts:(h + 1) * ts] * gate).astype(o_ref.dtype)


def _win_attn(q_r, kwin_r, big, vwin_blk, small, nh_gdn):
    B, S, WQ = q_r.shape
    G = NSA_GROUPS
    hpg = WQ // HEAD_DIM // G
    tq = min(S, 512)
    assert S >= tq + NSA_WINDOW
    return pl.pallas_call(
        functools.partial(_win_attn_kernel, nh_gdn=nh_gdn, hpg=hpg),
        grid=(B, G, S // tq),
        in_specs=[pl.BlockSpec((1, tq, hpg * HEAD_DIM), lambda b, g, i: (b, i, g)),
                  pl.BlockSpec((1, S, HEAD_DIM), lambda b, g, i: (b, 0, g)),
                  pl.BlockSpec((1, S, HEAD_DIM), lambda b, g, i: (b, 0, vwin_blk + g)),
                  pl.BlockSpec((1, tq, small.shape[2]), lambda b, g, i: (b, i, 0))],
        out_specs=pl.BlockSpec((1, tq, hpg * HEAD_DIM), lambda b, g, i: (b, i, g)),
        out_shape=jax.ShapeDtypeStruct((B, S, WQ), BF16),
        compiler_params=_cparams(3),
        name="nsa_win_attn",
    )(q_r, kwin_r, big, small)


def _post_norm(x, y, gt, g, b, alpha):
    r = alpha * x + (1.0 + gt) * y
    return _normalize_rows(r) * g + b


def _mixer_out_kernel(oa_ref, oc_ref, os_ref, ow_ref, ma_ref, mb_ref, x_ref, gt_ref, lg_ref, lb_ref,
                      wg_ref, wn_ref, wo_ref, o_ref, *, alpha):
    y_a = _dot(oa_ref[0], wg_ref[...])
    o_b = oc_ref[0].astype(F32) + os_ref[0].astype(F32) + ow_ref[0].astype(F32)
    y_b = _dot(o_b.astype(BF16), wn_ref[...])
    mixed = _sigmoid(ma_ref[0].astype(F32)) * y_a + _sigmoid(mb_ref[0].astype(F32)) * y_b
    y = _dot(mixed.astype(BF16), wo_ref[...])
    o_ref[0] = _post_norm(x_ref[0], y, gt_ref[0], lg_ref[...], lb_ref[...], alpha)


def _mixer_out(o_a, o_c, o_s, o_w, big, ma_blk, mb_blk, x, ada_l, gt_col, ln_g, ln_b, wg, wn, wo, alpha):
    B, S, D = x.shape
    tm = min(S, 512)
    row = lambda blk: pl.BlockSpec((1, tm, D), lambda b, i: (b, i, blk))
    full = lambda shape: pl.BlockSpec(shape, lambda b, i: (0,) * len(shape))
    return pl.pallas_call(
        functools.partial(_mixer_out_kernel, alpha=alpha),
        grid=(B, S // tm),
        in_specs=[row(0), row(0), row(0), row(0), row(ma_blk), row(mb_blk), row(0),
                  pl.BlockSpec((1, 1, D), lambda b, i: (b, 0, gt_col)),
                  full((1, D)), full((1, D)), full((D, D)), full((D, D)), full((D, D))],
        out_specs=row(0),
        out_shape=jax.ShapeDtypeStruct((B, S, D), F32),
        compiler_params=_cparams(2),
        name="mixer_out",
    )(o_a, o_c, o_s, o_w, big, big, x, ada_l, ln_g.reshape(1, D), ln_b.reshape(1, D), wg, wn, wo)


def _first_argmax(vals, lane, big):
    m = jnp.max(vals, axis=-1, keepdims=True)
    first = jnp.min(jnp.where(vals == m, lane, big), axis=-1, keepdims=True)
    return m, first


def _moe_router_kernel(x_ref, sh_ref, sc_ref, rwh_ref, rwl_ref, rb_ref, wsg_ref, wsu_ref, wsd_ref,
                       h_ref, idx_ref, wt_ref, shared_ref, rank_ref, cnt_ref):
    h = _normalize_rows(x_ref[0]) * (1.0 + sc_ref[0]) + sh_ref[0]
    h16 = h.astype(BF16)
    h_ref[0] = h16
    h_lo = (h - h16.astype(F32)).astype(BF16)
    rw_hi = rwh_ref[...]
    logits = _dot_nt(rw_hi, h16) + _dot_nt(rwl_ref[...], h16) + _dot_nt(rw_hi, h_lo)
    scores = _sigmoid(logits)
    biased = scores + rb_ref[...]
    E, tm = scores.shape
    NG = N_EXPERT_GROUPS
    per = E // NG
    b3 = biased.reshape(NG, per, tm)
    in_grp = lax.broadcasted_iota(jnp.int32, (NG, per, tm), 1).astype(F32)
    m1 = jnp.max(b3, axis=1, keepdims=True)
    first = jnp.min(jnp.where(b3 == m1, in_grp, float(per)), axis=1, keepdims=True)
    m2 = jnp.max(jnp.where(in_grp == first, -jnp.inf, b3), axis=1, keepdims=True)
    gscore = (m1 + m2).reshape(NG, tm)
    g_id = lax.broadcasted_iota(jnp.int32, (NG, tm), 0).astype(F32)
    allowed = jnp.zeros((NG, tm), F32)
    for _ in range(TOPK_GROUPS):
        m = jnp.max(gscore, axis=0, keepdims=True)
        first_g = jnp.min(jnp.where(gscore == m, g_id, float(NG)), axis=0, keepdims=True)
        pick = g_id == first_g
        allowed = jnp.where(pick, 1.0, allowed)
        gscore = jnp.where(pick, -jnp.inf, gscore)
    cand = jnp.where(allowed.reshape(NG, 1, tm) > 0.5, b3, NEG_INF).reshape(E, tm)
    e_id = lax.broadcasted_iota(jnp.int32, (E, tm), 0).astype(F32)
    idx_rows, wt_rows, picks = [], [], []
    wsum = jnp.zeros((1, tm), F32)
    chosen = jnp.zeros((E, tm), F32)
    for kk in range(TOP_K):
        m = jnp.max(cand, axis=0, keepdims=True)
        first_e = jnp.min(jnp.where(cand == m, e_id, float(E)), axis=0, keepdims=True)
        pick = e_id == first_e
        w = jnp.sum(jnp.where(pick, scores, 0.0), axis=0, keepdims=True)
        cand = jnp.where(pick, -jnp.inf, cand)
        chosen = jnp.where(pick, 1.0, chosen)
        idx_rows.append(first_e)
        wt_rows.append(w)
        picks.append(pick)
        wsum = wsum + w
    idx_ref[0] = jnp.concatenate(idx_rows, axis=0).astype(jnp.int32)
    wt_ref[0] = jnp.concatenate(wt_rows, axis=0) / wsum * ROUTED_SCALE
    @pl.when((pl.program_id(0) == 0) & (pl.program_id(1) == 0))
    def _():
        cnt_ref[...] = jnp.zeros_like(cnt_ref)

    si = lax.broadcasted_iota(jnp.int32, (tm, tm), 0)
    ti = lax.broadcasted_iota(jnp.int32, (tm, tm), 1)
    before = jnp.where(si < ti, 1.0, 0.0).astype(BF16)
    prior = _dot(chosen.astype(BF16), before) + cnt_ref[...]
    rank_ref[0] = jnp.concatenate([jnp.sum(jnp.where(pk, prior, 0.0), axis=0, keepdims=True) for pk in picks],
                                  axis=0).astype(jnp.int32)
    cnt_ref[...] = cnt_ref[...] + jnp.sum(chosen, axis=1, keepdims=True)
    act = _silu(_dot(h16, wsg_ref[...])) * _dot(h16, wsu_ref[...])
    shared_ref[0] = _dot(act.astype(BF16), wsd_ref[...])


def _moe_router(x, ada_l, sh_col, sc_col, rwt_hi, rwt_lo, router_b, wsg, wsu, wsd):
    B, S, D = x.shape
    E = rwt_hi.shape[0]
    FF = wsg.shape[1]
    tm = min(S, 512)
    row = lambda w: pl.BlockSpec((1, tm, w), lambda b, i: (b, i, 0))
    krow = pl.BlockSpec((1, TOP_K, tm), lambda b, i: (b, 0, i))
    full = lambda shape: pl.BlockSpec(shape, lambda b, i: (0,) * len(shape))
    return pl.pallas_call(
        _moe_router_kernel,
        grid=(B, S // tm),
        in_specs=[row(D),
                  pl.BlockSpec((1, 1, D), lambda b, i: (b, 0, sh_col)),
                  pl.BlockSpec((1, 1, D), lambda b, i: (b, 0, sc_col)),
                  full((E, D)), full((E, D)), full((E, 1)), full((D, FF)), full((D, FF)), full((FF, D))],
        out_specs=[row(D), krow, krow, row(D), krow, full((E, 1))],
        out_shape=[jax.ShapeDtypeStruct((B, S, D), BF16),
                   jax.ShapeDtypeStruct((B, TOP_K, S), jnp.int32),
                   jax.ShapeDtypeStruct((B, TOP_K, S), F32),
                   jax.ShapeDtypeStruct((B, S, D), F32),
                   jax.ShapeDtypeStruct((B, TOP_K, S), jnp.int32),
                   jax.ShapeDtypeStruct((E, 1), F32)],
        compiler_params=_cparams(2),
        name="moe_router",
    )(x, ada_l, ada_l, rwt_hi, rwt_lo, router_b.reshape(E, 1), wsg, wsu, wsd)


def _moe_experts_kernel(be_ref, nu_ref, xs_ref, wg_ref, wu_ref, wd_ref, o_ref):
    @pl.when(pl.program_id(0) < nu_ref[0])
    def _():
        xb = xs_ref[...]
        act = _silu(_dot(xb, wg_ref[0, 0].astype(BF16))) * _dot(xb, wu_ref[0, 0].astype(BF16))
        o_ref[...] = _dot(act.astype(BF16), wd_ref[0, 0].astype(BF16)).astype(o_ref.dtype)


def _moe_experts(block_expert, n_used, xs, layer, wg, wu, wd):
    rows, D = xs.shape
    BM = MOE_ROW_BLOCK
    FF = wg.shape[3]
    grid_spec = pltpu.PrefetchScalarGridSpec(
        num_scalar_prefetch=2,
        grid=(rows // BM,),
        in_specs=[pl.BlockSpec((BM, D), lambda i, be, nu: (i, 0)),
                  pl.BlockSpec((1, 1, D, FF), lambda i, be, nu: (layer, be[i], 0, 0)),
                  pl.BlockSpec((1, 1, D, FF), lambda i, be, nu: (layer, be[i], 0, 0)),
                  pl.BlockSpec((1, 1, FF, D), lambda i, be, nu: (layer, be[i], 0, 0))],
        out_specs=pl.BlockSpec((BM, D), lambda i, be, nu: (i, 0)),
    )
    return pl.pallas_call(
        _moe_experts_kernel,
        grid_spec=grid_spec,
        out_shape=jax.ShapeDtypeStruct((rows, D), BF16),
        compiler_params=_cparams(1),
        name="moe_experts",
    )(block_expert, n_used, xs, wg, wu, wd)


def _moe_out_kernel(x_ref, shared_ref, yg_ref, wt_ref, gt_ref, lg_ref, lb_ref, o_ref, *, alpha):
    y = shared_ref[0]
    wt = wt_ref[0]
    for kk in range(TOP_K):
        y = y + wt[:, kk:kk + 1] * yg_ref[kk, 0].astype(F32)
    o_ref[0] = _post_norm(x_ref[0], y, gt_ref[0], lg_ref[...], lb_ref[...], alpha)


def _moe_out(x, shared, yg, wt, ada_l, gt_col, ln_g, ln_b, alpha):
    B, S, D = x.shape
    tm = min(S, 256)
    row = lambda w: pl.BlockSpec((1, tm, w), lambda b, i: (b, i, 0))
    full = lambda shape: pl.BlockSpec(shape, lambda b, i: (0,) * len(shape))
    return pl.pallas_call(
        functools.partial(_moe_out_kernel, alpha=alpha),
        grid=(B, S // tm),
        in_specs=[row(D), row(D),
                  pl.BlockSpec((TOP_K, 1, tm, D), lambda b, i: (0, b, i, 0)),
                  row(wt.shape[2]),
                  pl.BlockSpec((1, 1, D), lambda b, i: (b, 0, gt_col)),
                  full((1, D)), full((1, D))],
        out_specs=row(D),
        out_shape=jax.ShapeDtypeStruct((B, S, D), F32),
        compiler_params=_cparams(2),
        name="moe_out",
    )(x, shared, yg, wt, ada_l, ln_g.reshape(1, D), ln_b.reshape(1, D))


def _moe_dest_kernel(start_ref, e_ref, r_ref, o_ref):
    e = e_ref[...]
    acc = r_ref[...]
    for j in range(start_ref.shape[0]):
        acc = acc + jnp.where(e == j, start_ref[j], 0)
    o_ref[...] = acc


def _routing_layout(expert_idx, rank, counts):
    B, K, S = expert_idx.shape
    T = B * S
    n_experts = counts.shape[0]
    BM = MOE_ROW_BLOCK
    TK = T * K
    counts = counts.reshape(n_experts).astype(jnp.int32)
    padded = (counts + BM - 1) // BM * BM
    pad_end = jnp.cumsum(padded)
    pad_start = (pad_end - padded).astype(jnp.int32)
    ts = min(S, 2048)
    spec = pl.BlockSpec((1, K, ts), lambda b, i, tab: (b, 0, i))
    dest = pl.pallas_call(
        _moe_dest_kernel,
        grid_spec=pltpu.PrefetchScalarGridSpec(num_scalar_prefetch=1, grid=(B, S // ts),
                                               in_specs=[spec, spec], out_specs=spec),
        out_shape=jax.ShapeDtypeStruct((B, K, S), jnp.int32),
        compiler_params=_cparams(2),
        name="moe_dest",
    )(pad_start, expert_idx, rank)
    dest = jnp.swapaxes(dest, 0, 1).reshape(TK)
    n_blocks = (TK + n_experts * (BM - 1) + BM - 1) // BM
    rows = n_blocks * BM
    token = jnp.tile(jnp.arange(T, dtype=jnp.int32), K)
    row_tok = (jnp.arange(rows, dtype=jnp.int32) % T).at[dest].add(token - dest % T, mode='promise_in_bounds')
    block_start = jnp.arange(n_blocks, dtype=jnp.int32) * BM
    block_expert = jnp.minimum(jnp.sum(pad_end[None, :] <= block_start[:, None], axis=1),
                               n_experts - 1).astype(jnp.int32)
    n_used = (pad_end[-1] // BM).astype(jnp.int32).reshape(1)
    return row_tok, dest, block_expert, n_used


def kernel(x, c, positions, ada_w, ada_b, w_in, gdn_conv_w, gdn_a_log, gdn_dt_bias, gdn_norm_w, w_gdn_branch, nsa_cmp_pos_k, nsa_cmp_pos_v, nsa_cmp_k_w1, nsa_cmp_k_w2, nsa_cmp_v_w1, nsa_cmp_v_w2, w_nsa_branch, w_out, ln1_g, ln1_b, router_w, router_bias, w_sh_gate, w_sh_up, w_sh_down, w_e_gate, w_e_up, w_e_down, ln2_g, ln2_b):
    B, S, D = x.shape
    L = ada_w.shape[0]
    T = B * S
    G = NSA_GROUPS
    nh_gdn = gdn_a_log.shape[1]
    gdn_w = nh_gdn * HEAD_DIM
    nsa_w = w_nsa_branch.shape[1]
    kvw = G * HEAD_DIM
    E = router_w.shape[2]
    alpha = (2.0 * L) ** 0.25
    assert gdn_w == D and nsa_w == D and S % NSA_SEL_LEN == 0

    splits = (3 * gdn_w, gdn_w, nh_gdn, nh_gdn, nsa_w, 6 * kvw, 3 * (nsa_w // HEAD_DIM), D, D)
    offs = [0]
    for s_ in splits:
        offs.append(offs[-1] + s_)
    seg = lambda i: slice(offs[i], offs[i + 1])
    w_big = jnp.concatenate([w_in[:, :, seg(0)], w_in[:, :, seg(1)], w_in[:, :, seg(4)],
                             w_in[:, :, seg(7)], w_in[:, :, seg(8)], w_in[:, :, seg(5)]], axis=-1).astype(BF16)
    n_small = splits[2] + splits[3] + splits[6]
    w_small = jnp.concatenate([w_in[:, :, seg(2)], w_in[:, :, seg(3)], w_in[:, :, seg(6)],
                               jnp.zeros((L, D, 128 - n_small), w_in.dtype)], axis=-1).astype(BF16)
    Z_BLK, Q_BLK, MA_BLK, MB_BLK = 3, 4, 5, 6
    kv0 = 7 * D
    n_big = w_big.shape[2]
    tn_big = n_big // 4 if (n_big // 4) % 128 == 0 else 128

    wgb = w_gdn_branch.astype(BF16)
    wnb = w_nsa_branch.astype(BF16)
    wob = w_out.astype(BF16)
    wsg = w_sh_gate.astype(BF16)
    wsu = w_sh_up.astype(BF16)
    wsd = w_sh_down.astype(BF16)
    rwt = jnp.swapaxes(router_w, 1, 2)
    rwt_hi = rwt.astype(BF16)
    rwt_lo = (rwt - rwt_hi.astype(F32)).astype(BF16)
    cmp_w1 = jnp.stack([nsa_cmp_k_w1, nsa_cmp_v_w1], axis=1).astype(BF16)
    cmp_w2 = jnp.stack([nsa_cmp_k_w2, nsa_cmp_v_w2], axis=1).astype(BF16)
    cmp_pos = jnp.stack([nsa_cmp_pos_k, nsa_cmp_pos_v], axis=1).reshape(L, 2, 1, NSA_CMP_LEN * HEAD_DIM)

    ada = _ada(c, ada_w, ada_b)
    cosf, sinf = _rope_tables(positions)
    NC = S // NSA_CMP_STRIDE
    last = jnp.minimum(jnp.arange(NC) * NSA_CMP_STRIDE + NSA_CMP_LEN - 1, S - 1)
    cosc = cosf[:, last]
    sinc = sinf[:, last]
    n_sel = -(-(S // NSA_SEL_LEN) // 128) * 128

    for l in range(L):
        ada_l = ada[l].reshape(B, 1, 6 * D)
        big, small, cmp_in = _proj(x, ada_l, 0, 1, w_big[l], w_small[l], tn_big, kv0, 2 * G)
        small_t = jnp.swapaxes(small[:, :, nh_gdn:2 * nh_gdn].reshape(B, S // GDN_CHUNK, GDN_CHUNK, nh_gdn), 2, 3)
        q_a, k_a, v_a = _gdn_prep(big, gdn_conv_w[l], gdn_w)
        o_a = _gdn(q_a, k_a, v_a, big, Z_BLK, small, small_t, gdn_a_log[l], gdn_dt_bias[l], gdn_norm_w[l])
        q_r, ksel_r, kwin_r = _rope_apply(big, Q_BLK, (kv0 + 2 * kvw) // kvw, (kv0 + 4 * kvw) // kvw,
                                          nsa_w, kvw, cosf, sinf)
        hb = cmp_in.reshape(B, 2 * G, NC, NSA_CMP_STRIDE * HEAD_DIM)
        kvc = _compress(hb, cmp_pos[l], cmp_w1[l], cmp_w2[l], cosc, sinc)
        o_c, sel = _cmp_attn(q_r, kvc, small, nh_gdn, n_sel)
        o_s = _sel_attn(q_r, ksel_r, big, (kv0 + 3 * kvw) // HEAD_DIM, sel, small, nh_gdn)
        o_w = _win_attn(q_r, kwin_r, big, (kv0 + 5 * kvw) // HEAD_DIM, small, nh_gdn)
        x = _mixer_out(o_a, o_c, o_s, o_w, big, MA_BLK, MB_BLK, x, ada_l, 2, ln1_g[l], ln1_b[l],
                       wgb[l], wnb[l], wob[l], alpha)
        h2, eidx, ewt, shared, rank, counts = _moe_router(x, ada_l, 3, 4, rwt_hi[l], rwt_lo[l], router_bias[l],
                                                          wsg[l], wsu[l], wsd[l])
        row_tok, dest, block_expert, n_used = _routing_layout(eidx, rank, counts)
        xs = h2.reshape(T, D).at[row_tok].get(mode='promise_in_bounds')
        yb = _moe_experts(block_expert, n_used, xs, l, w_e_gate, w_e_up, w_e_down)
        yg = yb.at[dest].get(mode='promise_in_bounds').reshape(TOP_K, B, S, D)
        x = _moe_out(x, shared, yg, jnp.swapaxes(ewt, 1, 2), ada_l, 5, ln2_g[l], ln2_b[l], alpha)
    return x
```

```python
import functools
import math

import jax
import jax.numpy as jnp
from jax import lax
from jax.experimental import pallas as pl
from jax.experimental.pallas import tpu as pltpu

F32 = jnp.float32
BF16 = jnp.bfloat16

HEAD_DIM = 128
GDN_CONV = 4
GDN_CHUNK = 64
NSA_GROUPS = 2
NSA_CMP_LEN = 32
NSA_CMP_STRIDE = 16
NSA_SEL_LEN = 64
NSA_SEL_TOP = 16
NSA_WINDOW = 512
ROPE_THETA = 10000.0
N_EXPERTS = 64
N_EXPERT_GROUPS = 8
TOPK_GROUPS = 4
TOP_K = 8
ROUTED_SCALE = 2.5
LN_EPS = 1e-5
NEG_INF = -1e30
SEL_FORCE = 1e6
MOE_ROW_BLOCK = 512
Q_SCALE_LOG2E = (HEAD_DIM ** -0.5) * math.log2(math.e)
MASK_BIG = 2.0 ** 100
N_ROW_PARTS = 2
GDN_PHASE1_CHUNKS = 4
WIN_SUB_TILE = 128
_ARB = "arbitrary"


def _cparams(n_axes):
    return pltpu.CompilerParams(dimension_semantics=(_ARB,) * n_axes)


def _sigmoid(x):
    return 1.0 / (1.0 + jnp.exp(-x))


def _silu(x):
    return x * _sigmoid(x)


def _dot(a, b):
    return jnp.dot(a, b, preferred_element_type=F32)


def _dot_nt(a, b):
    return lax.dot_general(a, b, (((1,), (1,)), ((), ())), preferred_element_type=F32)


def _dot_tn(a, b):
    return lax.dot_general(a, b, (((0,), (0,)), ((), ())), preferred_element_type=F32)


def _tile_lanes(x, n):
    return jnp.concatenate([x] * n, axis=1)


def _split3(x):
    x0 = x.astype(BF16)
    r1 = x - x0.astype(F32)
    x1 = r1.astype(BF16)
    x2 = (r1 - x1.astype(F32)).astype(BF16)
    return x0, x1, x2


def _dot_sel_right(x, sel_bf16):
    x0, x1, x2 = _split3(x)
    return _dot(x0, sel_bf16) + _dot(x1, sel_bf16) + _dot(x2, sel_bf16)


def _dot_sel_left(sel_bf16, x):
    x0, x1, x2 = _split3(x)
    return _dot(sel_bf16, x0) + _dot(sel_bf16, x1) + _dot(sel_bf16, x2)


def _normalize_rows(x):
    mu = jnp.mean(x, axis=-1, keepdims=True)
    xc = x - mu
    var = jnp.mean(xc * xc, axis=-1, keepdims=True)
    return xc * lax.rsqrt(var + LN_EPS)


def _ada_kernel(c_ref, w_ref, b_ref, o_ref):
    cond = _silu(c_ref[...])
    o_ref[0] = jnp.dot(cond, w_ref[0], preferred_element_type=F32,
                       precision=lax.Precision.HIGHEST) + b_ref[0]


def _ada(c, ada_w, ada_b):
    L, D, N = ada_w.shape
    B = c.shape[0]
    tn = min(N, 1536)
    return pl.pallas_call(
        _ada_kernel,
        grid=(L, N // tn),
        in_specs=[pl.BlockSpec((B, D), lambda l, j: (0, 0)),
                  pl.BlockSpec((1, D, tn), lambda l, j: (l, 0, j)),
                  pl.BlockSpec((1, 1, tn), lambda l, j: (l, 0, j))],
        out_specs=pl.BlockSpec((1, B, tn), lambda l, j: (l, 0, j)),
        out_shape=jax.ShapeDtypeStruct((L, B, N), F32),
        compiler_params=_cparams(2),
        name="ada",
    )(c, ada_w, ada_b.reshape(L, 1, N))


def _proj_kernel(x_ref, sh_ref, sc_ref, w_ref, ws_ref, o_ref, os_ref, oc_ref, h_ref, *, head_tile, head_off):
    @pl.when(pl.program_id(2) == 0)
    def _():
        h = _normalize_rows(x_ref[0]) * (1.0 + sc_ref[0]) + sh_ref[0]
        h_ref[...] = h.astype(BF16)
        os_ref[0] = _dot(h_ref[...], ws_ref[...])

    res = _dot(h_ref[...], w_ref[...]).astype(o_ref.dtype)
    o_ref[0] = res

    @pl.when(pl.program_id(2) == head_tile)
    def _():
        for jj in range(oc_ref.shape[1]):
            oc_ref[0, jj] = res[:, head_off + jj * HEAD_DIM:head_off + (jj + 1) * HEAD_DIM]


def _proj(x, ada_l, sh_col, sc_col, w, w_small, tn, head_col, n_head_cols):
    B, S, D = x.shape
    N = w.shape[1]
    NS = w_small.shape[1]
    tm = min(S, 1024)
    head_tile, head_off = head_col // tn, head_col % tn
    assert head_off % HEAD_DIM == 0 and head_off + n_head_cols * HEAD_DIM <= tn
    return pl.pallas_call(
        functools.partial(_proj_kernel, head_tile=head_tile, head_off=head_off),
        grid=(B, S // tm, N // tn),
        in_specs=[pl.BlockSpec((1, tm, D), lambda b, i, j: (b, i, 0)),
                  pl.BlockSpec((1, 1, D), lambda b, i, j: (b, 0, sh_col)),
                  pl.BlockSpec((1, 1, D), lambda b, i, j: (b, 0, sc_col)),
                  pl.BlockSpec((D, tn), lambda b, i, j: (0, j)),
                  pl.BlockSpec((D, NS), lambda b, i, j: (0, 0))],
        out_specs=[pl.BlockSpec((1, tm, tn), lambda b, i, j: (b, i, j)),
                   pl.BlockSpec((1, tm, NS), lambda b, i, j: (b, i, 0)),
                   pl.BlockSpec((1, n_head_cols, tm, HEAD_DIM), lambda b, i, j: (b, 0, i, 0))],
        out_shape=[jax.ShapeDtypeStruct((B, S, N), BF16),
                   jax.ShapeDtypeStruct((B, S, NS), F32),
                   jax.ShapeDtypeStruct((B, n_head_cols, S, HEAD_DIM), BF16)],
        scratch_shapes=[pltpu.VMEM((tm, D), BF16)],
        compiler_params=_cparams(3),
        name="proj",
    )(x, ada_l, ada_l, w, w_small)


def _gdn_prep_kernel(x_ref, w_ref, q_ref, k_ref, v_ref, carry_ref):
    ts = x_ref.shape[1]
    width = q_ref.shape[2]
    nh = width // HEAD_DIM

    @pl.when(pl.program_id(1) == 0)
    def _():
        carry_ref[...] = jnp.zeros_like(carry_ref)

    for part, o_ref in enumerate((q_ref, k_ref, v_ref)):
        for h in range(nh):
            c0 = part * width + h * HEAD_DIM
            cols = slice(c0, c0 + HEAD_DIM)
            xx = jnp.concatenate([carry_ref[:, cols], x_ref[0, :, cols].astype(F32)], axis=0)
            w = w_ref[:, cols]
            y = xx[8:8 + ts] * w[GDN_CONV - 1:GDN_CONV]
            for kk in range(GDN_CONV - 1):
                off = 8 - (GDN_CONV - 1) + kk
                y = y + xx[off:off + ts] * w[kk:kk + 1]
            y = _silu(y)
            if part < 2:
                norm = lax.rsqrt(jnp.sum(y * y, axis=-1, keepdims=True) + 1e-6)
                if part == 0:
                    norm = norm * (HEAD_DIM ** -0.5)
                y = y * norm
            o_ref[0, :, h * HEAD_DIM:(h + 1) * HEAD_DIM] = y.astype(o_ref.dtype)
    carry_ref[...] = x_ref[0, ts - 8:ts, :].astype(F32)


def _gdn_prep(big, conv_w, width):
    B, S, _ = big.shape
    ts = min(S, 512)
    out = jax.ShapeDtypeStruct((B, S, width), BF16)
    ospec = pl.BlockSpec((1, ts, width), lambda b, i: (b, i, 0))
    return pl.pallas_call(
        _gdn_prep_kernel,
        grid=(B, S // ts),
        in_specs=[pl.BlockSpec((1, ts, 3 * width), lambda b, i: (b, i, 0)),
                  pl.BlockSpec((GDN_CONV, 3 * width), lambda b, i: (0, 0))],
        out_specs=[ospec, ospec, ospec],
        out_shape=[out, out, out],
        scratch_shapes=[pltpu.VMEM((8, 3 * width), F32)],
        compiler_params=_cparams(2),
        name="gdn_prep",
    )(big, conv_w)


def _softplus(x):
    return jnp.maximum(x, 0.0) + jnp.log(1.0 + jnp.exp(-jnp.abs(x)))


def _gdn_kernel(q_ref, k_ref, v_ref, z_ref, sm_ref, smt_ref, alog_ref, alogt_ref, dtb_ref, dtbt_ref,
                nw_ref, o_ref, state_ref, u_s, wq_s, attn_s, kd_s, dec_s):
    ts = q_ref.shape[1]
    nh = q_ref.shape[2] // HEAD_DIM
    C = GDN_CHUNK
    R = 2 * C
    npair = nh // 2
    nchunks = ts // C

    @pl.when(pl.program_id(1) == 0)
    def _():
        state_ref[...] = jnp.zeros_like(state_ref)

    ci = lax.broadcasted_iota(jnp.int32, (C, C), 0)
    cj = lax.broadcasted_iota(jnp.int32, (C, C), 1)
    tril = jnp.where(ci >= cj, 1.0, 0.0).astype(BF16)
    triu = jnp.where(cj >= ci, 1.0, 0.0).astype(BF16)
    ii = lax.broadcasted_iota(jnp.int32, (R, R), 0)
    jj = lax.broadcasted_iota(jnp.int32, (R, R), 1)
    same_head = (ii // C) == (jj // C)
    incl = same_head & (ii >= jj)
    strict = same_head & (ii > jj)
    eye = jnp.where(ii == jj, 1.0, 0.0)
    nw = nw_ref[...]

    def pair_rows(ref, rows, p):
        return jnp.concatenate([ref[0, rows, (2 * p) * HEAD_DIM:(2 * p + 1) * HEAD_DIM],
                                ref[0, rows, (2 * p + 1) * HEAD_DIM:(2 * p + 2) * HEAD_DIM]], axis=0)

    def pair_col(x, p):
        return jnp.concatenate([x[:, 2 * p:2 * p + 1], x[:, 2 * p + 1:2 * p + 2]], axis=0)

    def phase1(j, carry):
        cs = [j * GDN_PHASE1_CHUNKS + d for d in range(GDN_PHASE1_CHUNKS)]
        rows_c, beta_c, gc_c, gcr_c, glb_c = [], [], [], [], []
        for c in cs:
            rows = slice(c * C, (c + 1) * C)
            sm = sm_ref[0, rows, :]
            g = -jnp.exp(alog_ref[...]) * _softplus(sm[:, nh:2 * nh] + dtb_ref[...])
            gt = -jnp.exp(alogt_ref[...]) * _softplus(smt_ref[0, c] + dtbt_ref[...])
            gc = _dot_sel_left(tril, g)
            g_last = gc[C - 1:C, :]
            dec_s[c] = jnp.exp(g_last)
            rows_c.append(rows)
            beta_c.append(_sigmoid(sm[:, 0:nh]))
            gc_c.append(gc)
            gcr_c.append(_dot_sel_right(gt, triu))
            glb_c.append(jnp.broadcast_to(g_last, (C, nh)))
        items = [(ci, p) for ci in range(len(cs)) for p in range(npair)]
        idx = range(len(items))
        q2 = [pair_rows(q_ref, rows_c[ci], p).astype(F32) for ci, p in items]
        k2 = [pair_rows(k_ref, rows_c[ci], p).astype(F32) for ci, p in items]
        b2 = [pair_col(beta_c[ci], p) for ci, p in items]
        g_col = [pair_col(gc_c[ci], p) for ci, p in items]
        kb = [k2[i] * b2[i] for i in idx]
        kq = [_dot_nt(jnp.concatenate([kb[i], q2[i]], axis=0).astype(BF16), k2[i].astype(BF16))
              for i in idx]
        decay = []
        for i, (ci, p) in enumerate(items):
            gcr = gcr_c[ci]
            g_row = jnp.concatenate([gcr[2 * p:2 * p + 1, :], gcr[2 * p + 1:2 * p + 2, :]], axis=1)
            decay.append(jnp.where(incl, jnp.exp(jnp.where(incl, g_col[i] - g_row, 0.0)), 0.0))
        a = [jnp.where(strict, kq[i][:R] * decay[i], 0.0) for i in idx]
        for i, (ci, p) in enumerate(items):
            attn_s[cs[ci], p] = (kq[i][R:] * decay[i]).astype(BF16)
        x = [eye - a[i] for i in idx]
        pw = a
        n = 2
        while n < C:
            pw16 = [pw[i].astype(BF16) for i in idx]
            pw = [_dot(pw16[i], pw16[i]) for i in idx]
            x = [x[i] + _dot(x[i].astype(BF16), pw[i].astype(BF16)) for i in idx]
            n *= 2
        eg = [jnp.exp(g_col[i]) for i in idx]
        sol = []
        for i, (ci, p) in enumerate(items):
            v2 = pair_rows(v_ref, rows_c[ci], p).astype(F32)
            r = jnp.concatenate([v2 * b2[i], kb[i] * eg[i]], axis=1)
            sol.append(_dot(x[i].astype(BF16), r.astype(BF16)))
        for i, (ci, p) in enumerate(items):
            c = cs[ci]
            u_s[c, p] = sol[i][:, :HEAD_DIM]
            w = sol[i][:, HEAD_DIM:]
            qg = q2[i] * eg[i]
            for e in range(2):
                wq_s[c, 2 * p + e] = jnp.concatenate([w[e * C:(e + 1) * C], qg[e * C:(e + 1) * C]],
                                                     axis=0).astype(BF16)
            kd_s[c, p] = (k2[i] * jnp.exp(pair_col(glb_c[ci], p) - g_col[i])).astype(BF16)
        return carry

    def phase2(c):
        rows = slice(c * C, (c + 1) * C)
        dec = dec_s[c]
        res = [_dot(wq_s[c, h], state_ref[h].astype(BF16)) for h in range(nh)]
        for p in range(npair):
            ws = jnp.concatenate([res[2 * p][:C], res[2 * p + 1][:C]], axis=0)
            qs = jnp.concatenate([res[2 * p][C:], res[2 * p + 1][C:]], axis=0)
            v_new = u_s[c, p] - ws
            v16 = v_new.astype(BF16)
            o2 = qs + _dot(attn_s[c, p], v16)
            kd = kd_s[c, p]
            for e in range(2):
                h = 2 * p + e
                part = slice(e * C, (e + 1) * C)
                state_ref[h] = state_ref[h] * dec[:, h:h + 1] + _dot_tn(kd[part], v16[part])
                o = o2[part]
                o = o * lax.rsqrt(jnp.mean(o * o, axis=-1, keepdims=True) + 1e-6) * nw
                cols = slice(h * HEAD_DIM, (h + 1) * HEAD_DIM)
                z = z_ref[0, rows, cols].astype(F32)
                o_ref[0, rows, cols] = (o * _silu(z)).astype(o_ref.dtype)

    def phase2_group(j):
        for d in range(GDN_PHASE1_CHUNKS):
            phase2(j * GDN_PHASE1_CHUNKS + d)

    assert nchunks % GDN_PHASE1_CHUNKS == 0
    ngroups = nchunks // GDN_PHASE1_CHUNKS
    phase1(0, 0)
    for j in range(1, ngroups):
        phase2_group(j - 1)
        phase1(j, 0)
    phase2_group(ngroups - 1)


def _gdn(q, k, v, big, z_blk, small, small_t, a_log, dt_bias, norm_w):
    B, S, W = q.shape
    nh = W // HEAD_DIM
    assert nh % 2 == 0 and 2 * GDN_CHUNK == HEAD_DIM
    ts = min(S, 512)
    nc = ts // GDN_CHUNK
    spec = pl.BlockSpec((1, ts, W), lambda b, i: (b, i, 0))
    full = lambda shape: pl.BlockSpec(shape, lambda b, i: (0,) * len(shape))
    return pl.pallas_call(
        _gdn_kernel,
        grid=(B, S // ts),
        in_specs=[spec, spec, spec,
                  pl.BlockSpec((1, ts, W), lambda b, i: (b, i, z_blk)),
                  pl.BlockSpec((1, ts, small.shape[2]), lambda b, i: (b, i, 0)),
                  pl.BlockSpec((1, ts // GDN_CHUNK, nh, GDN_CHUNK), lambda b, i: (b, i, 0, 0)),
                  full((1, nh)), full((nh, 1)), full((1, nh)), full((nh, 1)),
                  full((1, HEAD_DIM))],
        out_specs=spec,
        out_shape=jax.ShapeDtypeStruct((B, S, W), BF16),
        scratch_shapes=[pltpu.VMEM((nh, HEAD_DIM, HEAD_DIM), F32),
                        pltpu.VMEM((nc, nh // 2, 2 * GDN_CHUNK, HEAD_DIM), F32),
                        pltpu.VMEM((nc, nh, 2 * GDN_CHUNK, HEAD_DIM), BF16),
                        pltpu.VMEM((nc, nh // 2, 2 * GDN_CHUNK, 2 * GDN_CHUNK), BF16),
                        pltpu.VMEM((nc, nh // 2, 2 * GDN_CHUNK, HEAD_DIM), BF16),
                        pltpu.VMEM((nc, 1, nh), F32)],
        compiler_params=_cparams(2),
        name="gdn",
    )(q, k, v, big, small, small_t, a_log.reshape(1, nh), a_log.reshape(nh, 1),
      dt_bias.reshape(1, nh), dt_bias.reshape(nh, 1), norm_w.reshape(1, HEAD_DIM))


def _rope_table_kernel(pos_ref, invf_ref, sign_ref, cos_ref, sin_ref):
    ang = pos_ref[0].astype(F32) * invf_ref[...]
    cos_ref[0] = jnp.cos(ang)
    sin_ref[0] = jnp.sin(ang) * sign_ref[...]


def _rope_tables(positions):
    B, S = positions.shape
    half = HEAD_DIM // 2
    inv = ROPE_THETA ** (-jnp.arange(half, dtype=F32) / half)
    invf = jnp.concatenate([inv, inv]).reshape(1, HEAD_DIM)
    sign = jnp.concatenate([-jnp.ones((half,), F32), jnp.ones((half,), F32)]).reshape(1, HEAD_DIM)
    ts = min(S, 1024)
    out = jax.ShapeDtypeStruct((B, S, HEAD_DIM), F32)
    ospec = pl.BlockSpec((1, ts, HEAD_DIM), lambda b, i: (b, i, 0))
    return pl.pallas_call(
        _rope_table_kernel,
        grid=(B, S // ts),
        in_specs=[pl.BlockSpec((1, ts, 1), lambda b, i: (b, i, 0)),
                  pl.BlockSpec((1, HEAD_DIM), lambda b, i: (0, 0)),
                  pl.BlockSpec((1, HEAD_DIM), lambda b, i: (0, 0))],
        out_specs=[ospec, ospec],
        out_shape=[out, out],
        compiler_params=_cparams(2),
        name="rope_table",
    )(positions.reshape(B, S, 1), invf, sign)


def _rope_rows(x, cosf, sinf):
    return x * cosf + pltpu.roll(x, HEAD_DIM // 2, 1) * sinf


def _rope_apply_kernel(q_ref, ks_ref, kw_ref, cos_ref, sin_ref, qo_ref, kso_ref, kwo_ref):
    cosf = cos_ref[0]
    sinf = sin_ref[0]
    for src, dst, mult in ((q_ref, qo_ref, Q_SCALE_LOG2E), (ks_ref, kso_ref, None), (kw_ref, kwo_ref, None)):
        for h in range(src.shape[2] // HEAD_DIM):
            cols = slice(h * HEAD_DIM, (h + 1) * HEAD_DIM)
            r = _rope_rows(src[0, :, cols].astype(F32), cosf, sinf)
            if mult is not None:
                r = r * mult
            dst[0, :, cols] = r.astype(dst.dtype)


def _rope_apply(big, q_blk, ksel_blk, kwin_blk, wq, wkv, cosf, sinf):
    B, S, _ = big.shape
    ts = min(S, 512)
    tab = pl.BlockSpec((1, ts, HEAD_DIM), lambda b, i: (b, i, 0))
    return pl.pallas_call(
        _rope_apply_kernel,
        grid=(B, S // ts),
        in_specs=[pl.BlockSpec((1, ts, wq), lambda b, i: (b, i, q_blk)),
                  pl.BlockSpec((1, ts, wkv), lambda b, i: (b, i, ksel_blk)),
                  pl.BlockSpec((1, ts, wkv), lambda b, i: (b, i, kwin_blk)),
                  tab, tab],
        out_specs=[pl.BlockSpec((1, ts, wq), lambda b, i: (b, i, 0)),
                   pl.BlockSpec((1, ts, wkv), lambda b, i: (b, i, 0)),
                   pl.BlockSpec((1, ts, wkv), lambda b, i: (b, i, 0))],
        out_shape=[jax.ShapeDtypeStruct((B, S, wq), BF16),
                   jax.ShapeDtypeStruct((B, S, wkv), BF16),
                   jax.ShapeDtypeStruct((B, S, wkv), BF16)],
        compiler_params=_cparams(2),
        name="rope_apply",
    )(big, big, big, cosf, sinf)


def _compress_kernel(hb_ref, pos_ref, w1_ref, w2_ref, cos_ref, sin_ref, o_ref):
    hb = hb_ref[0, 0]
    w1 = w1_ref[0]
    half = hb.shape[1]
    p0 = _dot(hb, w1[:half])
    p1 = _dot(hb, w1[half:])
    nc = p0.shape[0]
    pos8 = jnp.broadcast_to(pos_ref[0], (8, 2 * half)).astype(BF16)
    pb = _dot(pos8, w1)[0:1]
    pre = p0 + pltpu.roll(p1, nc - 1, 0) + pb
    out = _dot(_silu(pre).astype(BF16), w2_ref[0])
    roped = _rope_rows(out, cos_ref[0], sin_ref[0])
    is_key = pl.program_id(1) < NSA_GROUPS
    o_ref[0, 0] = jnp.where(is_key, roped, out).astype(o_ref.dtype)


def _compress(hb, pos_flat, w1, w2, cosc, sinc):
    B, J, NC, HW = hb.shape
    G = NSA_GROUPS
    return pl.pallas_call(
        _compress_kernel,
        grid=(B, J),
        in_specs=[pl.BlockSpec((1, 1, NC, HW), lambda b, j: (b, j, 0, 0)),
                  pl.BlockSpec((1, 1, 2 * HW), lambda b, j: (j // G, 0, 0)),
                  pl.BlockSpec((1, 2 * HW, HEAD_DIM), lambda b, j: (j // G, 0, 0)),
                  pl.BlockSpec((1, HEAD_DIM, HEAD_DIM), lambda b, j: (j // G, 0, 0)),
                  pl.BlockSpec((1, NC, HEAD_DIM), lambda b, j: (b, 0, 0)),
                  pl.BlockSpec((1, NC, HEAD_DIM), lambda b, j: (b, 0, 0))],
        out_specs=pl.BlockSpec((1, 1, NC, HEAD_DIM), lambda b, j: (b, j, 0, 0)),
        out_shape=jax.ShapeDtypeStruct((B, J, NC, HEAD_DIM), BF16),
        compiler_params=_cparams(2),
        name="nsa_compress",
    )(hb, pos_flat, w1, w2, cosc, sinc)


def _gate_col(g_ref, nh_gdn, group, hpg, h, branch):
    col = None
    for gi in range(NSA_GROUPS):
        c = 2 * nh_gdn + (gi * hpg + h) * 3 + branch
        cand = g_ref[0, :, c:c + 1]
        col = cand if col is None else jnp.where(group == gi, cand, col)
    return _sigmoid(col)


def _cmp_attn_kernel(q_ref, kc_ref, vc_ref, sm_ref, o_ref, sel_ref, *, nh_gdn, hpg):
    tq = q_ref.shape[1]
    nc = kc_ref.shape[2]
    nb = sel_ref.shape[3]
    g = pl.program_id(1)
    t0 = pl.program_id(2) * tq
    t_nc = t0 + lax.broadcasted_iota(jnp.int32, (tq, nc), 0)
    n_nc = lax.broadcasted_iota(jnp.int32, (tq, nc), 1)
    valid = (n_nc * NSA_CMP_STRIDE + (NSA_CMP_LEN - 1)) <= t_nc
    t_col = t0 + lax.broadcasted_iota(jnp.int32, (tq, 1), 0)
    has_valid = jnp.where(t_col >= NSA_CMP_LEN - 1, 1.0, 0.0)
    kc = kc_ref[0, 0]
    vc = vc_ref[0, 0]
    psum = jnp.zeros((tq, nc), F32)
    for h in range(hpg):
        cols = slice(h * HEAD_DIM, (h + 1) * HEAD_DIM)
        s = jnp.where(valid, _dot_nt(q_ref[0, :, cols], kc), NEG_INF)
        e = jnp.exp2(s - jnp.max(s, axis=-1, keepdims=True))
        p = e * (has_valid / jnp.sum(e, axis=-1, keepdims=True))
        gate = _gate_col(sm_ref, nh_gdn, g, hpg, h, 0)
        o_ref[0, :, cols] = (_dot(p.astype(BF16), vc) * gate).astype(o_ref.dtype)
        psum = psum + p
    sj = lax.broadcasted_iota(jnp.int32, (nb, nc), 0) * NSA_SEL_LEN
    cn = lax.broadcasted_iota(jnp.int32, (nb, nc), 1) * NSA_CMP_STRIDE
    overlap_t = jnp.where((cn <= sj + (NSA_SEL_LEN - 1)) & (cn + (NSA_CMP_LEN - 1) >= sj), 1.0, 0.0).astype(BF16)
    p0, p1, p2 = _split3(psum)
    imp = _dot_nt(overlap_t, p0) + _dot_nt(overlap_t, p1) + _dot_nt(overlap_t, p2)
    t_nb = t0 + lax.broadcasted_iota(jnp.int32, (nb, tq), 1)
    blk = lax.broadcasted_iota(jnp.int32, (nb, tq), 0)
    cur = t_nb // NSA_SEL_LEN
    forced = (blk == 0) | (blk == cur) | (blk == cur - 1)
    score = jnp.where(forced, SEL_FORCE, jnp.where(blk * NSA_SEL_LEN <= t_nb, imp, -1.0))
    sel = jnp.zeros((nb, tq), F32)
    blk_f = blk.astype(F32)
    for _ in range(min(NSA_SEL_TOP, nb)):
        m = jnp.max(score, axis=0, keepdims=True)
        first = jnp.min(jnp.where(score == m, blk_f, float(nb)), axis=0, keepdims=True)
        pick = blk_f == first
        sel = jnp.where(pick, 1.0, sel)
        score = jnp.where(pick, -jnp.inf, score)
    sel_ref[0, 0] = sel.T.astype(sel_ref.dtype)


def _cmp_attn(q_r, kvc, small, nh_gdn, n_sel):
    B, S, WQ = q_r.shape
    G = NSA_GROUPS
    hpg = WQ // HEAD_DIM // G
    NC = kvc.shape[2]
    tq = min(S, 512)
    return pl.pallas_call(
        functools.partial(_cmp_attn_kernel, nh_gdn=nh_gdn, hpg=hpg),
        grid=(B, G, S // tq),
        in_specs=[pl.BlockSpec((1, tq, hpg * HEAD_DIM), lambda b, g, i: (b, i, g)),
                  pl.BlockSpec((1, 1, NC, HEAD_DIM), lambda b, g, i: (b, g, 0, 0)),
                  pl.BlockSpec((1, 1, NC, HEAD_DIM), lambda b, g, i: (b, G + g, 0, 0)),
                  pl.BlockSpec((1, tq, small.shape[2]), lambda b, g, i: (b, i, 0))],
        out_specs=[pl.BlockSpec((1, tq, hpg * HEAD_DIM), lambda b, g, i: (b, i, g)),
                   pl.BlockSpec((1, 1, tq, n_sel), lambda b, g, i: (b, g, i, 0))],
        out_shape=[jax.ShapeDtypeStruct((B, S, WQ), BF16),
                   jax.ShapeDtypeStruct((B, G, S, n_sel), BF16)],
        compiler_params=_cparams(3),
        name="nsa_cmp_attn",
    )(q_r, kvc, kvc, small)


def _sel_attn_kernel(q_ref, k_ref, ext_ref, v_ref, sel_ref, sm_ref, o_ref, qx_ref, m_ref, acc_ref, p_ref, a_ref,
                     *, nh_gdn, hpg, tk):
    tq = q_ref.shape[1]
    nb = sel_ref.shape[3]
    g = pl.program_id(1)
    t0 = pl.program_id(2) * tq
    assert tk % tq == 0
    unsel = (sel_ref[0, 0].astype(F32) - 1.0).astype(BF16)
    for h in range(hpg):
        qx_ref[h * tq:(h + 1) * tq, 0:HEAD_DIM] = q_ref[0, :, h * HEAD_DIM:(h + 1) * HEAD_DIM]
        qx_ref[h * tq:(h + 1) * tq, HEAD_DIM:HEAD_DIM + nb] = unsel
    m_ref[...] = jnp.full_like(m_ref, NEG_INF)
    acc_ref[...] = jnp.zeros_like(acc_ref)
    p_ref[...] = jnp.zeros_like(p_ref)
    a_ref[...] = jnp.ones_like(a_ref)
    ones = jnp.ones((tk, HEAD_DIM), BF16)
    pr = hpg * tq // N_ROW_PARTS
    parts = [slice(part * pr, (part + 1) * pr) for part in range(N_ROW_PARTS)]

    def apply_values(kt):
        k0 = pl.multiple_of(kt * tk, tk)
        vx = jnp.concatenate([v_ref[0, pl.ds(k0, tk), :], ones], axis=1)
        for rows in parts:
            acc_ref[rows, :] = _tile_lanes(a_ref[rows, :], 2) * acc_ref[rows, :] + _dot(p_ref[rows, :], vx)

    def scores(kt, causal):
        k0 = pl.multiple_of(kt * tk, tk)
        kx = jnp.concatenate([k_ref[0, pl.ds(k0, tk), :], ext_ref[pl.ds(k0, tk), :]], axis=1)
        if causal:
            t_row = t0 + lax.broadcasted_iota(jnp.int32, (tq, tk), 0)
            key = k0 + lax.broadcasted_iota(jnp.int32, (tq, tk), 1)
            keep = jnp.concatenate([key <= t_row] * (hpg // N_ROW_PARTS), axis=0)
        for rows in parts:
            s = _dot_nt(qx_ref[rows, :], kx)
            if causal:
                s = jnp.where(keep, s, NEG_INF)
            m_old = m_ref[rows, :]
            m_new = jnp.maximum(m_old, jnp.max(s, axis=-1, keepdims=True))
            a_ref[rows, :] = jnp.exp2(m_old - m_new)
            p_ref[rows, :] = jnp.exp2(s - _tile_lanes(m_new, tk // HEAD_DIM)).astype(BF16)
            m_ref[rows, :] = m_new

    kt_diag = t0 // tk

    def body(kt, carry):
        apply_values(jnp.maximum(kt - 1, 0))
        scores(kt, False)
        return carry

    lax.fori_loop(0, kt_diag, body, 0)
    apply_values(jnp.maximum(kt_diag - 1, 0))
    scores(kt_diag, True)
    apply_values(kt_diag)
    for h in range(hpg):
        rows = slice(h * tq, (h + 1) * tq)
        gate = _gate_col(sm_ref, nh_gdn, g, hpg, h, 1)
        out = acc_ref[rows, 0:HEAD_DIM] / acc_ref[rows, HEAD_DIM:2 * HEAD_DIM]
        o_ref[0, :, h * HEAD_DIM:(h + 1) * HEAD_DIM] = (out * gate).astype(o_ref.dtype)


def _sel_attn(q_r, ksel_r, big, vsel_blk, sel, small, nh_gdn):
    B, S, WQ = q_r.shape
    G = NSA_GROUPS
    hpg = WQ // HEAD_DIM // G
    nbp = sel.shape[3]
    tq = min(S, 512)
    tk = min(S, 512)
    rows = hpg * tq
    ext = jnp.where(jnp.arange(S)[:, None] // NSA_SEL_LEN == jnp.arange(nbp)[None, :], MASK_BIG, 0.0).astype(BF16)
    return pl.pallas_call(
        functools.partial(_sel_attn_kernel, nh_gdn=nh_gdn, hpg=hpg, tk=tk),
        grid=(B, G, S // tq),
        in_specs=[pl.BlockSpec((1, tq, hpg * HEAD_DIM), lambda b, g, i: (b, i, g)),
                  pl.BlockSpec((1, S, HEAD_DIM), lambda b, g, i: (b, 0, g)),
                  pl.BlockSpec((S, nbp), lambda b, g, i: (0, 0)),
                  pl.BlockSpec((1, S, HEAD_DIM), lambda b, g, i: (b, 0, vsel_blk + g)),
                  pl.BlockSpec((1, 1, tq, nbp), lambda b, g, i: (b, g, i, 0)),
                  pl.BlockSpec((1, tq, small.shape[2]), lambda b, g, i: (b, i, 0))],
        out_specs=pl.BlockSpec((1, tq, hpg * HEAD_DIM), lambda b, g, i: (b, i, g)),
        out_shape=jax.ShapeDtypeStruct((B, S, WQ), BF16),
        scratch_shapes=[pltpu.VMEM((rows, HEAD_DIM + nbp), BF16),
                        pltpu.VMEM((rows, HEAD_DIM), F32),
                        pltpu.VMEM((rows, 2 * HEAD_DIM), F32),
                        pltpu.VMEM((rows, tk), BF16),
                        pltpu.VMEM((rows, HEAD_DIM), F32)],
        compiler_params=_cparams(3),
        name="nsa_sel_attn",
    )(q_r, ksel_r, ext, big, sel, small)


def _win_attn_kernel(q_ref, k_ref, v_ref, sm_ref, o_ref, *, nh_gdn, hpg):
    tq = q_ref.shape[1]
    ts = min(tq, WIN_SUB_TILE)
    span = ts + NSA_WINDOW
    g = pl.program_id(1)
    for sub in range(tq // ts):
        qrows = slice(sub * ts, (sub + 1) * ts)
        t0 = pl.program_id(2) * tq + sub * ts
        k0 = pl.multiple_of(jnp.maximum(t0 - NSA_WINDOW, 0), ts)
        k = k_ref[0, pl.ds(k0, span), :]
        v = v_ref[0, pl.ds(k0, span), :]
        t_row = t0 + lax.broadcasted_iota(jnp.int32, (ts, span), 0)
        spos = k0 + lax.broadcasted_iota(jnp.int32, (ts, span), 1)
        mask = (spos <= t_row) & (spos > t_row - NSA_WINDOW)
        vx = jnp.concatenate([v, jnp.ones((span, HEAD_DIM), BF16)], axis=1)
        q4 = jnp.concatenate([q_ref[0, qrows, h * HEAD_DIM:(h + 1) * HEAD_DIM] for h in range(hpg)], axis=0)
        s = jnp.where(jnp.concatenate([mask] * hpg, axis=0), _dot_nt(q4, k), NEG_INF)
        e = jnp.exp2(s - jnp.max(s, axis=-1, keepdims=True))
        acc = _dot(e.astype(BF16), vx)
        out = acc[:, :HEAD_DIM] / acc[:, HEAD_DIM:]
        for h in range(hpg):
            gate = _gate_col(sm_ref, nh_gdn, g, hpg, h, 2)[qrows]
            o_ref[0, qrows, h * HEAD_DIM:(h + 1) * HEAD_DIM] = (out[h * ts:(h + 1) * ts] * gate).astype(o_ref.dtype)


def _win_attn(q_r, kwin_r, big, vwin_blk, small, nh_gdn):
    B, S, WQ = q_r.shape
    G = NSA_GROUPS
    hpg = WQ // HEAD_DIM // G
    tq = min(S, 512)
    assert S >= tq + NSA_WINDOW
    return pl.pallas_call(
        functools.partial(_win_attn_kernel, nh_gdn=nh_gdn, hpg=hpg),
        grid=(B, G, S // tq),
        in_specs=[pl.BlockSpec((1, tq, hpg * HEAD_DIM), lambda b, g, i: (b, i, g)),
                  pl.BlockSpec((1, S, HEAD_DIM), lambda b, g, i: (b, 0, g)),
                  pl.BlockSpec((1, S, HEAD_DIM), lambda b, g, i: (b, 0, vwin_blk + g)),
                  pl.BlockSpec((1, tq, small.shape[2]), lambda b, g, i: (b, i, 0))],
        out_specs=pl.BlockSpec((1, tq, hpg * HEAD_DIM), lambda b, g, i: (b, i, g)),
        out_shape=jax.ShapeDtypeStruct((B, S, WQ), BF16),
        compiler_params=_cparams(3),
        name="nsa_win_attn",
    )(q_r, kwin_r, big, small)


def _post_norm(x, y, gt, g, b, alpha):
    r = alpha * x + (1.0 + gt) * y
    return _normalize_rows(r) * g + b


def _mixer_out_kernel(oa_ref, oc_ref, os_ref, ow_ref, ma_ref, mb_ref, x_ref, gt_ref, lg_ref, lb_ref,
                      wg_ref, wn_ref, wo_ref, o_ref, *, alpha):
    y_a = _dot(oa_ref[0], wg_ref[...])
    o_b = oc_ref[0].astype(F32) + os_ref[0].astype(F32) + ow_ref[0].astype(F32)
    y_b = _dot(o_b.astype(BF16), wn_ref[...])
    mixed = _sigmoid(ma_ref[0].astype(F32)) * y_a + _sigmoid(mb_ref[0].astype(F32)) * y_b
    y = _dot(mixed.astype(BF16), wo_ref[...])
    o_ref[0] = _post_norm(x_ref[0], y, gt_ref[0], lg_ref[...], lb_ref[...], alpha)


def _mixer_out(o_a, o_c, o_s, o_w, big, ma_blk, mb_blk, x, ada_l, gt_col, ln_g, ln_b, wg, wn, wo, alpha):
    B, S, D = x.shape
    tm = min(S, 512)
    row = lambda blk: pl.BlockSpec((1, tm, D), lambda b, i: (b, i, blk))
    full = lambda shape: pl.BlockSpec(shape, lambda b, i: (0,) * len(shape))
    return pl.pallas_call(
        functools.partial(_mixer_out_kernel, alpha=alpha),
        grid=(B, S // tm),
        in_specs=[row(0), row(0), row(0), row(0), row(ma_blk), row(mb_blk), row(0),
                  pl.BlockSpec((1, 1, D), lambda b, i: (b, 0, gt_col)),
                  full((1, D)), full((1, D)), full((D, D)), full((D, D)), full((D, D))],
        out_specs=row(0),
        out_shape=jax.ShapeDtypeStruct((B, S, D), F32),
        compiler_params=_cparams(2),
        name="mixer_out",
    )(o_a, o_c, o_s, o_w, big, big, x, ada_l, ln_g.reshape(1, D), ln_b.reshape(1, D), wg, wn, wo)


def _first_argmax(vals, lane, big):
    m = jnp.max(vals, axis=-1, keepdims=True)
    first = jnp.min(jnp.where(vals == m, lane, big), axis=-1, keepdims=True)
    return m, first


def _moe_router_kernel(x_ref, sh_ref, sc_ref, rwh_ref, rwl_ref, rb_ref, wsg_ref, wsu_ref, wsd_ref,
                       h_ref, idx_ref, wt_ref, shared_ref, rank_ref, cnt_ref):
    h = _normalize_rows(x_ref[0]) * (1.0 + sc_ref[0]) + sh_ref[0]
    h16 = h.astype(BF16)
    h_ref[0] = h16
    h_lo = (h - h16.astype(F32)).astype(BF16)
    rw_hi = rwh_ref[...]
    logits = _dot_nt(rw_hi, h16) + _dot_nt(rwl_ref[...], h16) + _dot_nt(rw_hi, h_lo)
    scores = _sigmoid(logits)
    biased = scores + rb_ref[...]
    E, tm = scores.shape
    NG = N_EXPERT_GROUPS
    per = E // NG
    b3 = biased.reshape(NG, per, tm)
    in_grp = lax.broadcasted_iota(jnp.int32, (NG, per, tm), 1).astype(F32)
    m1 = jnp.max(b3, axis=1, keepdims=True)
    first = jnp.min(jnp.where(b3 == m1, in_grp, float(per)), axis=1, keepdims=True)
    m2 = jnp.max(jnp.where(in_grp == first, -jnp.inf, b3), axis=1, keepdims=True)
    gscore = (m1 + m2).reshape(NG, tm)
    g_id = lax.broadcasted_iota(jnp.int32, (NG, tm), 0).astype(F32)
    allowed = jnp.zeros((NG, tm), F32)
    for _ in range(TOPK_GROUPS):
        m = jnp.max(gscore, axis=0, keepdims=True)
        first_g = jnp.min(jnp.where(gscore == m, g_id, float(NG)), axis=0, keepdims=True)
        pick = g_id == first_g
        allowed = jnp.where(pick, 1.0, allowed)
        gscore = jnp.where(pick, -jnp.inf, gscore)
    cand = jnp.where(allowed.reshape(NG, 1, tm) > 0.5, b3, NEG_INF).reshape(E, tm)
    e_id = lax.broadcasted_iota(jnp.int32, (E, tm), 0).astype(F32)
    idx_rows, wt_rows, picks = [], [], []
    wsum = jnp.zeros((1, tm), F32)
    chosen = jnp.zeros((E, tm), F32)
    for kk in range(TOP_K):
        m = jnp.max(cand, axis=0, keepdims=True)
        first_e = jnp.min(jnp.where(cand == m, e_id, float(E)), axis=0, keepdims=True)
        pick = e_id == first_e
        w = jnp.sum(jnp.where(pick, scores, 0.0), axis=0, keepdims=True)
        cand = jnp.where(pick, -jnp.inf, cand)
        chosen = jnp.where(pick, 1.0, chosen)
        idx_rows.append(first_e)
        wt_rows.append(w)
        picks.append(pick)
        wsum = wsum + w
    idx_ref[0] = jnp.concatenate(idx_rows, axis=0).astype(jnp.int32)
    wt_ref[0] = jnp.concatenate(wt_rows, axis=0) / wsum * ROUTED_SCALE
    @pl.when((pl.program_id(0) == 0) & (pl.program_id(1) == 0))
    def _():
        cnt_ref[...] = jnp.zeros_like(cnt_ref)

    si = lax.broadcasted_iota(jnp.int32, (tm, tm), 0)
    ti = lax.broadcasted_iota(jnp.int32, (tm, tm), 1)
    before = jnp.where(si < ti, 1.0, 0.0).astype(BF16)
    prior = _dot(chosen.astype(BF16), before) + cnt_ref[...]
    rank_ref[0] = jnp.concatenate([jnp.sum(jnp.where(pk, prior, 0.0), axis=0, keepdims=True) for pk in picks],
                                  axis=0).astype(jnp.int32)
    cnt_ref[...] = cnt_ref[...] + jnp.sum(chosen, axis=1, keepdims=True)
    act = _silu(_dot(h16, wsg_ref[...])) * _dot(h16, wsu_ref[...])
    shared_ref[0] = _dot(act.astype(BF16), wsd_ref[...])


def _moe_router(x, ada_l, sh_col, sc_col, rwt_hi, rwt_lo, router_b, wsg, wsu, wsd):
    B, S, D = x.shape
    E = rwt_hi.shape[0]
    FF = wsg.shape[1]
    tm = min(S, 512)
    row = lambda w: pl.BlockSpec((1, tm, w), lambda b, i: (b, i, 0))
    krow = pl.BlockSpec((1, TOP_K, tm), lambda b, i: (b, 0, i))
    full = lambda shape: pl.BlockSpec(shape, lambda b, i: (0,) * len(shape))
    return pl.pallas_call(
        _moe_router_kernel,
        grid=(B, S // tm),
        in_specs=[row(D),
                  pl.BlockSpec((1, 1, D), lambda b, i: (b, 0, sh_col)),
                  pl.BlockSpec((1, 1, D), lambda b, i: (b, 0, sc_col)),
                  full((E, D)), full((E, D)), full((E, 1)), full((D, FF)), full((D, FF)), full((FF, D))],
        out_specs=[row(D), krow, krow, row(D), krow, full((E, 1))],
        out_shape=[jax.ShapeDtypeStruct((B, S, D), BF16),
                   jax.ShapeDtypeStruct((B, TOP_K, S), jnp.int32),
                   jax.ShapeDtypeStruct((B, TOP_K, S), F32),
                   jax.ShapeDtypeStruct((B, S, D), F32),
                   jax.ShapeDtypeStruct((B, TOP_K, S), jnp.int32),
                   jax.ShapeDtypeStruct((E, 1), F32)],
        compiler_params=_cparams(2),
        name="moe_router",
    )(x, ada_l, ada_l, rwt_hi, rwt_lo, router_b.reshape(E, 1), wsg, wsu, wsd)


def _moe_experts_kernel(be_ref, nu_ref, xs_ref, wg_ref, wu_ref, wd_ref, *rest, first_block):
    o_ref = rest[-1]

    @pl.when(pl.program_id(0) + first_block < nu_ref[0])
    def _():
        xb = xs_ref[...]
        act = _silu(_dot(xb, wg_ref[0, 0].astype(BF16))) * _dot(xb, wu_ref[0, 0].astype(BF16))
        o_ref[...] = _dot(act.astype(BF16), wd_ref[0, 0].astype(BF16)).astype(o_ref.dtype)


def _moe_experts(block_expert, n_used, xs, layer, wg, wu, wd, first_block, total_rows, prev=None):
    rows, D = xs.shape
    BM = MOE_ROW_BLOCK
    FF = wg.shape[3]
    fb = first_block
    in_specs = [pl.BlockSpec((BM, D), lambda i, be, nu: (i, 0)),
                pl.BlockSpec((1, 1, D, FF), lambda i, be, nu: (layer, be[i + fb], 0, 0)),
                pl.BlockSpec((1, 1, D, FF), lambda i, be, nu: (layer, be[i + fb], 0, 0)),
                pl.BlockSpec((1, 1, FF, D), lambda i, be, nu: (layer, be[i + fb], 0, 0))]
    args = [block_expert, n_used, xs, wg, wu, wd]
    aliases = {}
    if prev is not None:
        in_specs.append(pl.BlockSpec(memory_space=pl.ANY))
        args.append(prev)
        aliases = {len(args) - 1: 0}
    grid_spec = pltpu.PrefetchScalarGridSpec(
        num_scalar_prefetch=2,
        grid=(rows // BM,),
        in_specs=in_specs,
        out_specs=pl.BlockSpec((BM, D), lambda i, be, nu: (i + fb, 0)),
    )
    return pl.pallas_call(
        functools.partial(_moe_experts_kernel, first_block=fb),
        grid_spec=grid_spec,
        out_shape=jax.ShapeDtypeStruct((total_rows, D), BF16),
        input_output_aliases=aliases,
        compiler_params=_cparams(1),
        name="moe_experts",
    )(*args)


def _moe_out_kernel(x_ref, shared_ref, yg_ref, wt_ref, gt_ref, lg_ref, lb_ref, o_ref, *, alpha):
    y = shared_ref[0]
    wt = wt_ref[0]
    for kk in range(TOP_K):
        y = y + wt[:, kk:kk + 1] * yg_ref[kk, 0].astype(F32)
    o_ref[0] = _post_norm(x_ref[0], y, gt_ref[0], lg_ref[...], lb_ref[...], alpha)


def _moe_out(x, shared, yg, wt, ada_l, gt_col, ln_g, ln_b, alpha):
    B, S, D = x.shape
    tm = min(S, 256)
    row = lambda w: pl.BlockSpec((1, tm, w), lambda b, i: (b, i, 0))
    full = lambda shape: pl.BlockSpec(shape, lambda b, i: (0,) * len(shape))
    return pl.pallas_call(
        functools.partial(_moe_out_kernel, alpha=alpha),
        grid=(B, S // tm),
        in_specs=[row(D), row(D),
                  pl.BlockSpec((TOP_K, 1, tm, D), lambda b, i: (0, b, i, 0)),
                  row(wt.shape[2]),
                  pl.BlockSpec((1, 1, D), lambda b, i: (b, 0, gt_col)),
                  full((1, D)), full((1, D))],
        out_specs=row(D),
        out_shape=jax.ShapeDtypeStruct((B, S, D), F32),
        compiler_params=_cparams(2),
        name="moe_out",
    )(x, shared, yg, wt, ada_l, ln_g.reshape(1, D), ln_b.reshape(1, D))


def _moe_dest_kernel(start_ref, e_ref, r_ref, o_ref):
    e = e_ref[...]
    acc = r_ref[...]
    for j in range(start_ref.shape[0]):
        acc = acc + jnp.where(e == j, start_ref[j], 0)
    o_ref[...] = acc


def _routing_layout(expert_idx, rank, counts):
    B, K, S = expert_idx.shape
    T = B * S
    n_experts = counts.shape[0]
    BM = MOE_ROW_BLOCK
    TK = T * K
    counts = counts.reshape(n_experts).astype(jnp.int32)
    padded = (counts + BM - 1) // BM * BM
    pad_end = jnp.cumsum(padded)
    pad_start = (pad_end - padded).astype(jnp.int32)
    ts = min(S, 2048)
    spec = pl.BlockSpec((1, K, ts), lambda b, i, tab: (b, 0, i))
    dest = pl.pallas_call(
        _moe_dest_kernel,
        grid_spec=pltpu.PrefetchScalarGridSpec(num_scalar_prefetch=1, grid=(B, S // ts),
                                               in_specs=[spec, spec], out_specs=spec),
        out_shape=jax.ShapeDtypeStruct((B, K, S), jnp.int32),
        compiler_params=_cparams(2),
        name="moe_dest",
    )(pad_start, expert_idx, rank)
    dest = jnp.swapaxes(dest, 0, 1).reshape(TK)
    n_blocks = (TK + n_experts * (BM - 1) + BM - 1) // BM
    rows = n_blocks * BM
    token = jnp.tile(jnp.arange(T, dtype=jnp.int32), K)
    row_tok = (jnp.arange(rows, dtype=jnp.int32) % T).at[dest].add(token - dest % T, mode='promise_in_bounds')
    block_start = jnp.arange(n_blocks, dtype=jnp.int32) * BM
    block_expert = jnp.minimum(jnp.sum(pad_end[None, :] <= block_start[:, None], axis=1),
                               n_experts - 1).astype(jnp.int32)
    n_used = (pad_end[-1] // BM).astype(jnp.int32).reshape(1)
    return row_tok, dest, block_expert, n_used


def kernel(x, c, positions, ada_w, ada_b, w_in, gdn_conv_w, gdn_a_log, gdn_dt_bias, gdn_norm_w, w_gdn_branch, nsa_cmp_pos_k, nsa_cmp_pos_v, nsa_cmp_k_w1, nsa_cmp_k_w2, nsa_cmp_v_w1, nsa_cmp_v_w2, w_nsa_branch, w_out, ln1_g, ln1_b, router_w, router_bias, w_sh_gate, w_sh_up, w_sh_down, w_e_gate, w_e_up, w_e_down, ln2_g, ln2_b):
    B, S, D = x.shape
    L = ada_w.shape[0]
    T = B * S
    G = NSA_GROUPS
    nh_gdn = gdn_a_log.shape[1]
    gdn_w = nh_gdn * HEAD_DIM
    nsa_w = w_nsa_branch.shape[1]
    kvw = G * HEAD_DIM
    E = router_w.shape[2]
    alpha = (2.0 * L) ** 0.25
    assert gdn_w == D and nsa_w == D and S % NSA_SEL_LEN == 0

    splits = (3 * gdn_w, gdn_w, nh_gdn, nh_gdn, nsa_w, 6 * kvw, 3 * (nsa_w // HEAD_DIM), D, D)
    offs = [0]
    for s_ in splits:
        offs.append(offs[-1] + s_)
    seg = lambda i: slice(offs[i], offs[i + 1])
    w_big = jnp.concatenate([w_in[:, :, seg(0)], w_in[:, :, seg(1)], w_in[:, :, seg(4)],
                             w_in[:, :, seg(7)], w_in[:, :, seg(8)], w_in[:, :, seg(5)]], axis=-1).astype(BF16)
    n_small = splits[2] + splits[3] + splits[6]
    w_small = jnp.concatenate([w_in[:, :, seg(2)], w_in[:, :, seg(3)], w_in[:, :, seg(6)],
                               jnp.zeros((L, D, 128 - n_small), w_in.dtype)], axis=-1).astype(BF16)
    Z_BLK, Q_BLK, MA_BLK, MB_BLK = 3, 4, 5, 6
    kv0 = 7 * D
    n_big = w_big.shape[2]
    tn_big = n_big // 4 if (n_big // 4) % 128 == 0 else 128

    wgb = w_gdn_branch.astype(BF16)
    wnb = w_nsa_branch.astype(BF16)
    wob = w_out.astype(BF16)
    wsg = w_sh_gate.astype(BF16)
    wsu = w_sh_up.astype(BF16)
    wsd = w_sh_down.astype(BF16)
    rwt = jnp.swapaxes(router_w, 1, 2)
    rwt_hi = rwt.astype(BF16)
    rwt_lo = (rwt - rwt_hi.astype(F32)).astype(BF16)
    cmp_w1 = jnp.stack([nsa_cmp_k_w1, nsa_cmp_v_w1], axis=1).astype(BF16)
    cmp_w2 = jnp.stack([nsa_cmp_k_w2, nsa_cmp_v_w2], axis=1).astype(BF16)
    cmp_pos = jnp.stack([nsa_cmp_pos_k, nsa_cmp_pos_v], axis=1).reshape(L, 2, 1, NSA_CMP_LEN * HEAD_DIM)

    ada = _ada(c, ada_w, ada_b)
    cosf, sinf = _rope_tables(positions)
    NC = S // NSA_CMP_STRIDE
    last = jnp.minimum(jnp.arange(NC) * NSA_CMP_STRIDE + NSA_CMP_LEN - 1, S - 1)
    cosc = cosf[:, last]
    sinc = sinf[:, last]
    n_sel = -(-(S // NSA_SEL_LEN) // 128) * 128

    for l in range(L):
        ada_l = ada[l].reshape(B, 1, 6 * D)
        big, small, cmp_in = _proj(x, ada_l, 0, 1, w_big[l], w_small[l], tn_big, kv0, 2 * G)
        small_t = jnp.swapaxes(small[:, :, nh_gdn:2 * nh_gdn].reshape(B, S // GDN_CHUNK, GDN_CHUNK, nh_gdn), 2, 3)
        q_a, k_a, v_a = _gdn_prep(big, gdn_conv_w[l], gdn_w)
        o_a = _gdn(q_a, k_a, v_a, big, Z_BLK, small, small_t, gdn_a_log[l], gdn_dt_bias[l], gdn_norm_w[l])
        q_r, ksel_r, kwin_r = _rope_apply(big, Q_BLK, (kv0 + 2 * kvw) // kvw, (kv0 + 4 * kvw) // kvw,
                                          nsa_w, kvw, cosf, sinf)
        hb = cmp_in.reshape(B, 2 * G, NC, NSA_CMP_STRIDE * HEAD_DIM)
        kvc = _compress(hb, cmp_pos[l], cmp_w1[l], cmp_w2[l], cosc, sinc)
        o_c, sel = _cmp_attn(q_r, kvc, small, nh_gdn, n_sel)
        o_s = _sel_attn(q_r, ksel_r, big, (kv0 + 3 * kvw) // HEAD_DIM, sel, small, nh_gdn)
        o_w = _win_attn(q_r, kwin_r, big, (kv0 + 5 * kvw) // HEAD_DIM, small, nh_gdn)
        x = _mixer_out(o_a, o_c, o_s, o_w, big, MA_BLK, MB_BLK, x, ada_l, 2, ln1_g[l], ln1_b[l],
                       wgb[l], wnb[l], wob[l], alpha)
        h2, eidx, ewt, shared, rank, counts = _moe_router(x, ada_l, 3, 4, rwt_hi[l], rwt_lo[l], router_bias[l],
                                                          wsg[l], wsu[l], wsd[l])
        row_tok, dest, block_expert, n_used = _routing_layout(eidx, rank, counts)
        rows = row_tok.shape[0]
        nb_a = rows // MOE_ROW_BLOCK // 2
        h2f = h2.reshape(T, D)
        xs_a = h2f.at[row_tok[:nb_a * MOE_ROW_BLOCK]].get(mode='promise_in_bounds')
        xs_b = h2f.at[row_tok[nb_a * MOE_ROW_BLOCK:]].get(mode='promise_in_bounds')
        yb = _moe_experts(block_expert, n_used, xs_a, l, w_e_gate, w_e_up, w_e_down, 0, rows)
        yb = _moe_experts(block_expert, n_used, xs_b, l, w_e_gate, w_e_up, w_e_down, nb_a, rows, prev=yb)
        yg = yb.at[dest].get(mode='promise_in_bounds').reshape(TOP_K, B, S, D)
        x = _moe_out(x, shared, yg, jnp.swapaxes(ewt, 1, 2), ada_l, 5, ln2_g[l], ln2_b[l], alpha)
    return x
```

```python
import functools
import math

import jax
import jax.numpy as jnp
from jax import lax
from jax.experimental import pallas as pl
from jax.experimental.pallas import tpu as pltpu

F32 = jnp.float32
BF16 = jnp.bfloat16

HEAD_DIM = 128
GDN_CONV = 4
GDN_CHUNK = 64
NSA_GROUPS = 2
NSA_CMP_LEN = 32
NSA_CMP_STRIDE = 16
NSA_SEL_LEN = 64
NSA_SEL_TOP = 16
NSA_WINDOW = 512
ROPE_THETA = 10000.0
N_EXPERTS = 64
N_EXPERT_GROUPS = 8
TOPK_GROUPS = 4
TOP_K = 8
ROUTED_SCALE = 2.5
LN_EPS = 1e-5
NEG_INF = -1e30
SEL_FORCE = 1e6
MOE_ROW_BLOCK = 512
Q_SCALE_LOG2E = (HEAD_DIM ** -0.5) * math.log2(math.e)
MASK_BIG = 2.0 ** 100
N_ROW_PARTS = 2
GDN_PHASE1_CHUNKS = 4
WIN_SUB_TILE = 128
MOE_DISPATCH_PARTS = 4
MOE_COMBINE_PARTS = 2
_ARB = "arbitrary"


def _cparams(n_axes):
    return pltpu.CompilerParams(dimension_semantics=(_ARB,) * n_axes)


def _sigmoid(x):
    return 1.0 / (1.0 + jnp.exp(-x))


def _silu(x):
    return x * _sigmoid(x)


def _dot(a, b):
    return jnp.dot(a, b, preferred_element_type=F32)


def _dot_nt(a, b):
    return lax.dot_general(a, b, (((1,), (1,)), ((), ())), preferred_element_type=F32)


def _dot_tn(a, b):
    return lax.dot_general(a, b, (((0,), (0,)), ((), ())), preferred_element_type=F32)


def _tile_lanes(x, n):
    return jnp.concatenate([x] * n, axis=1)


def _split3(x):
    x0 = x.astype(BF16)
    r1 = x - x0.astype(F32)
    x1 = r1.astype(BF16)
    x2 = (r1 - x1.astype(F32)).astype(BF16)
    return x0, x1, x2


def _dot_sel_right(x, sel_bf16):
    x0, x1, x2 = _split3(x)
    return _dot(x0, sel_bf16) + _dot(x1, sel_bf16) + _dot(x2, sel_bf16)


def _dot_sel_left(sel_bf16, x):
    x0, x1, x2 = _split3(x)
    return _dot(sel_bf16, x0) + _dot(sel_bf16, x1) + _dot(sel_bf16, x2)


def _normalize_rows(x):
    mu = jnp.mean(x, axis=-1, keepdims=True)
    xc = x - mu
    var = jnp.mean(xc * xc, axis=-1, keepdims=True)
    return xc * lax.rsqrt(var + LN_EPS)


def _ada_kernel(c_ref, w_ref, b_ref, o_ref):
    cond = _silu(c_ref[...])
    o_ref[0] = jnp.dot(cond, w_ref[0], preferred_element_type=F32,
                       precision=lax.Precision.HIGHEST) + b_ref[0]


def _ada(c, ada_w, ada_b):
    L, D, N = ada_w.shape
    B = c.shape[0]
    tn = min(N, 1536)
    return pl.pallas_call(
        _ada_kernel,
        grid=(L, N // tn),
        in_specs=[pl.BlockSpec((B, D), lambda l, j: (0, 0)),
                  pl.BlockSpec((1, D, tn), lambda l, j: (l, 0, j)),
                  pl.BlockSpec((1, 1, tn), lambda l, j: (l, 0, j))],
        out_specs=pl.BlockSpec((1, B, tn), lambda l, j: (l, 0, j)),
        out_shape=jax.ShapeDtypeStruct((L, B, N), F32),
        compiler_params=_cparams(2),
        name="ada",
    )(c, ada_w, ada_b.reshape(L, 1, N))


def _proj_kernel(x_ref, sh_ref, sc_ref, w_ref, ws_ref, o_ref, os_ref, oc_ref, h_ref, *, head_tile, head_off):
    @pl.when(pl.program_id(2) == 0)
    def _():
        h = _normalize_rows(x_ref[0]) * (1.0 + sc_ref[0]) + sh_ref[0]
        h_ref[...] = h.astype(BF16)
        os_ref[0] = _dot(h_ref[...], ws_ref[...])

    res = _dot(h_ref[...], w_ref[...]).astype(o_ref.dtype)
    o_ref[0] = res

    @pl.when(pl.program_id(2) == head_tile)
    def _():
        for jj in range(oc_ref.shape[1]):
            oc_ref[0, jj] = res[:, head_off + jj * HEAD_DIM:head_off + (jj + 1) * HEAD_DIM]


def _proj(x, ada_l, sh_col, sc_col, w, w_small, tn, head_col, n_head_cols):
    B, S, D = x.shape
    N = w.shape[1]
    NS = w_small.shape[1]
    tm = min(S, 1024)
    head_tile, head_off = head_col // tn, head_col % tn
    assert head_off % HEAD_DIM == 0 and head_off + n_head_cols * HEAD_DIM <= tn
    return pl.pallas_call(
        functools.partial(_proj_kernel, head_tile=head_tile, head_off=head_off),
        grid=(B, S // tm, N // tn),
        in_specs=[pl.BlockSpec((1, tm, D), lambda b, i, j: (b, i, 0)),
                  pl.BlockSpec((1, 1, D), lambda b, i, j: (b, 0, sh_col)),
                  pl.BlockSpec((1, 1, D), lambda b, i, j: (b, 0, sc_col)),
                  pl.BlockSpec((D, tn), lambda b, i, j: (0, j)),
                  pl.BlockSpec((D, NS), lambda b, i, j: (0, 0))],
        out_specs=[pl.BlockSpec((1, tm, tn), lambda b, i, j: (b, i, j)),
                   pl.BlockSpec((1, tm, NS), lambda b, i, j: (b, i, 0)),
                   pl.BlockSpec((1, n_head_cols, tm, HEAD_DIM), lambda b, i, j: (b, 0, i, 0))],
        out_shape=[jax.ShapeDtypeStruct((B, S, N), BF16),
                   jax.ShapeDtypeStruct((B, S, NS), F32),
                   jax.ShapeDtypeStruct((B, n_head_cols, S, HEAD_DIM), BF16)],
        scratch_shapes=[pltpu.VMEM((tm, D), BF16)],
        compiler_params=_cparams(3),
        name="proj",
    )(x, ada_l, ada_l, w, w_small)


def _gdn_prep_kernel(x_ref, w_ref, q_ref, k_ref, v_ref, carry_ref):
    ts = x_ref.shape[1]
    width = q_ref.shape[2]
    nh = width // HEAD_DIM

    @pl.when(pl.program_id(1) == 0)
    def _():
        carry_ref[...] = jnp.zeros_like(carry_ref)

    for part, o_ref in enumerate((q_ref, k_ref, v_ref)):
        for h in range(nh):
            c0 = part * width + h * HEAD_DIM
            cols = slice(c0, c0 + HEAD_DIM)
            xx = jnp.concatenate([carry_ref[:, cols], x_ref[0, :, cols].astype(F32)], axis=0)
            w = w_ref[:, cols]
            y = xx[8:8 + ts] * w[GDN_CONV - 1:GDN_CONV]
            for kk in range(GDN_CONV - 1):
                off = 8 - (GDN_CONV - 1) + kk
                y = y + xx[off:off + ts] * w[kk:kk + 1]
            y = _silu(y)
            if part < 2:
                norm = lax.rsqrt(jnp.sum(y * y, axis=-1, keepdims=True) + 1e-6)
                if part == 0:
                    norm = norm * (HEAD_DIM ** -0.5)
                y = y * norm
            o_ref[0, :, h * HEAD_DIM:(h + 1) * HEAD_DIM] = y.astype(o_ref.dtype)
    carry_ref[...] = x_ref[0, ts - 8:ts, :].astype(F32)


def _gdn_prep(big, conv_w, width):
    B, S, _ = big.shape
    ts = min(S, 512)
    out = jax.ShapeDtypeStruct((B, S, width), BF16)
    ospec = pl.BlockSpec((1, ts, width), lambda b, i: (b, i, 0))
    return pl.pallas_call(
        _gdn_prep_kernel,
        grid=(B, S // ts),
        in_specs=[pl.BlockSpec((1, ts, 3 * width), lambda b, i: (b, i, 0)),
                  pl.BlockSpec((GDN_CONV, 3 * width), lambda b, i: (0, 0))],
        out_specs=[ospec, ospec, ospec],
        out_shape=[out, out, out],
        scratch_shapes=[pltpu.VMEM((8, 3 * width), F32)],
        compiler_params=_cparams(2),
        name="gdn_prep",
    )(big, conv_w)


def _softplus(x):
    return jnp.maximum(x, 0.0) + jnp.log(1.0 + jnp.exp(-jnp.abs(x)))


def _gdn_kernel(q_ref, k_ref, v_ref, z_ref, sm_ref, smt_ref, alog_ref, alogt_ref, dtb_ref, dtbt_ref,
                nw_ref, o_ref, state_ref, u_s, wq_s, attn_s, kd_s, dec_s):
    ts = q_ref.shape[1]
    nh = q_ref.shape[2] // HEAD_DIM
    C = GDN_CHUNK
    R = 2 * C
    npair = nh // 2
    nchunks = ts // C

    @pl.when(pl.program_id(1) == 0)
    def _():
        state_ref[...] = jnp.zeros_like(state_ref)

    ci = lax.broadcasted_iota(jnp.int32, (C, C), 0)
    cj = lax.broadcasted_iota(jnp.int32, (C, C), 1)
    tril = jnp.where(ci >= cj, 1.0, 0.0).astype(BF16)
    triu = jnp.where(cj >= ci, 1.0, 0.0).astype(BF16)
    ii = lax.broadcasted_iota(jnp.int32, (R, R), 0)
    jj = lax.broadcasted_iota(jnp.int32, (R, R), 1)
    same_head = (ii // C) == (jj // C)
    incl = same_head & (ii >= jj)
    strict = same_head & (ii > jj)
    eye = jnp.where(ii == jj, 1.0, 0.0)
    nw = nw_ref[...]

    def pair_rows(ref, rows, p):
        return jnp.concatenate([ref[0, rows, (2 * p) * HEAD_DIM:(2 * p + 1) * HEAD_DIM],
                                ref[0, rows, (2 * p + 1) * HEAD_DIM:(2 * p + 2) * HEAD_DIM]], axis=0)

    def pair_col(x, p):
        return jnp.concatenate([x[:, 2 * p:2 * p + 1], x[:, 2 * p + 1:2 * p + 2]], axis=0)

    def phase1(j, carry):
        cs = [j * GDN_PHASE1_CHUNKS + d for d in range(GDN_PHASE1_CHUNKS)]
        rows_c, beta_c, gc_c, gcr_c, glb_c = [], [], [], [], []
        for c in cs:
            rows = slice(c * C, (c + 1) * C)
            sm = sm_ref[0, rows, :]
            g = -jnp.exp(alog_ref[...]) * _softplus(sm[:, nh:2 * nh] + dtb_ref[...])
            gt = -jnp.exp(alogt_ref[...]) * _softplus(smt_ref[0, c] + dtbt_ref[...])
            gc = _dot_sel_left(tril, g)
            g_last = gc[C - 1:C, :]
            dec_s[c] = jnp.exp(g_last)
            rows_c.append(rows)
            beta_c.append(_sigmoid(sm[:, 0:nh]))
            gc_c.append(gc)
            gcr_c.append(_dot_sel_right(gt, triu))
            glb_c.append(jnp.broadcast_to(g_last, (C, nh)))
        items = [(ci, p) for ci in range(len(cs)) for p in range(npair)]
        idx = range(len(items))
        q2 = [pair_rows(q_ref, rows_c[ci], p).astype(F32) for ci, p in items]
        k2 = [pair_rows(k_ref, rows_c[ci], p).astype(F32) for ci, p in items]
        b2 = [pair_col(beta_c[ci], p) for ci, p in items]
        g_col = [pair_col(gc_c[ci], p) for ci, p in items]
        kb = [k2[i] * b2[i] for i in idx]
        kq = [_dot_nt(jnp.concatenate([kb[i], q2[i]], axis=0).astype(BF16), k2[i].astype(BF16))
              for i in idx]
        decay = []
        for i, (ci, p) in enumerate(items):
            gcr = gcr_c[ci]
            g_row = jnp.concatenate([gcr[2 * p:2 * p + 1, :], gcr[2 * p + 1:2 * p + 2, :]], axis=1)
            decay.append(jnp.where(incl, jnp.exp(jnp.where(incl, g_col[i] - g_row, 0.0)), 0.0))
        a = [jnp.where(strict, kq[i][:R] * decay[i], 0.0) for i in idx]
        for i, (ci, p) in enumerate(items):
            attn_s[cs[ci], p] = (kq[i][R:] * decay[i]).astype(BF16)
        x = [eye - a[i] for i in idx]
        pw = a
        n = 2
        while n < C:
            pw16 = [pw[i].astype(BF16) for i in idx]
            pw = [_dot(pw16[i], pw16[i]) for i in idx]
            x = [x[i] + _dot(x[i].astype(BF16), pw[i].astype(BF16)) for i in idx]
            n *= 2
        eg = [jnp.exp(g_col[i]) for i in idx]
        sol = []
        for i, (ci, p) in enumerate(items):
            v2 = pair_rows(v_ref, rows_c[ci], p).astype(F32)
            r = jnp.concatenate([v2 * b2[i], kb[i] * eg[i]], axis=1)
            sol.append(_dot(x[i].astype(BF16), r.astype(BF16)))
        for i, (ci, p) in enumerate(items):
            c = cs[ci]
            u_s[c, p] = sol[i][:, :HEAD_DIM]
            w = sol[i][:, HEAD_DIM:]
            qg = q2[i] * eg[i]
            for e in range(2):
                wq_s[c, 2 * p + e] = jnp.concatenate([w[e * C:(e + 1) * C], qg[e * C:(e + 1) * C]],
                                                     axis=0).astype(BF16)
            kd_s[c, p] = (k2[i] * jnp.exp(pair_col(glb_c[ci], p) - g_col[i])).astype(BF16)
        return carry

    def phase2(c):
        rows = slice(c * C, (c + 1) * C)
        dec = dec_s[c]
        res = [_dot(wq_s[c, h], state_ref[h].astype(BF16)) for h in range(nh)]
        for p in range(npair):
            ws = jnp.concatenate([res[2 * p][:C], res[2 * p + 1][:C]], axis=0)
            qs = jnp.concatenate([res[2 * p][C:], res[2 * p + 1][C:]], axis=0)
            v_new = u_s[c, p] - ws
            v16 = v_new.astype(BF16)
            o2 = qs + _dot(attn_s[c, p], v16)
            kd = kd_s[c, p]
            for e in range(2):
                h = 2 * p + e
                part = slice(e * C, (e + 1) * C)
                state_ref[h] = state_ref[h] * dec[:, h:h + 1] + _dot_tn(kd[part], v16[part])
                o = o2[part]
                o = o * lax.rsqrt(jnp.mean(o * o, axis=-1, keepdims=True) + 1e-6) * nw
                cols = slice(h * HEAD_DIM, (h + 1) * HEAD_DIM)
                z = z_ref[0, rows, cols].astype(F32)
                o_ref[0, rows, cols] = (o * _silu(z)).astype(o_ref.dtype)

    def phase2_group(j):
        for d in range(GDN_PHASE1_CHUNKS):
            phase2(j * GDN_PHASE1_CHUNKS + d)

    assert nchunks % GDN_PHASE1_CHUNKS == 0
    ngroups = nchunks // GDN_PHASE1_CHUNKS
    phase1(0, 0)
    for j in range(1, ngroups):
        phase2_group(j - 1)
        phase1(j, 0)
    phase2_group(ngroups - 1)


def _gdn(q, k, v, big, z_blk, small, small_t, a_log, dt_bias, norm_w):
    B, S, W = q.shape
    nh = W // HEAD_DIM
    assert nh % 2 == 0 and 2 * GDN_CHUNK == HEAD_DIM
    ts = min(S, 512)
    nc = ts // GDN_CHUNK
    spec = pl.BlockSpec((1, ts, W), lambda b, i: (b, i, 0))
    full = lambda shape: pl.BlockSpec(shape, lambda b, i: (0,) * len(shape))
    return pl.pallas_call(
        _gdn_kernel,
        grid=(B, S // ts),
        in_specs=[spec, spec, spec,
                  pl.BlockSpec((1, ts, W), lambda b, i: (b, i, z_blk)),
                  pl.BlockSpec((1, ts, small.shape[2]), lambda b, i: (b, i, 0)),
                  pl.BlockSpec((1, ts // GDN_CHUNK, nh, GDN_CHUNK), lambda b, i: (b, i, 0, 0)),
                  full((1, nh)), full((nh, 1)), full((1, nh)), full((nh, 1)),
                  full((1, HEAD_DIM))],
        out_specs=spec,
        out_shape=jax.ShapeDtypeStruct((B, S, W), BF16),
        scratch_shapes=[pltpu.VMEM((nh, HEAD_DIM, HEAD_DIM), F32),
                        pltpu.VMEM((nc, nh // 2, 2 * GDN_CHUNK, HEAD_DIM), F32),
                        pltpu.VMEM((nc, nh, 2 * GDN_CHUNK, HEAD_DIM), BF16),
                        pltpu.VMEM((nc, nh // 2, 2 * GDN_CHUNK, 2 * GDN_CHUNK), BF16),
                        pltpu.VMEM((nc, nh // 2, 2 * GDN_CHUNK, HEAD_DIM), BF16),
                        pltpu.VMEM((nc, 1, nh), F32)],
        compiler_params=_cparams(2),
        name="gdn",
    )(q, k, v, big, small, small_t, a_log.reshape(1, nh), a_log.reshape(nh, 1),
      dt_bias.reshape(1, nh), dt_bias.reshape(nh, 1), norm_w.reshape(1, HEAD_DIM))


def _rope_table_kernel(pos_ref, invf_ref, sign_ref, cos_ref, sin_ref):
    ang = pos_ref[0].astype(F32) * invf_ref[...]
    cos_ref[0] = jnp.cos(ang)
    sin_ref[0] = jnp.sin(ang) * sign_ref[...]


def _rope_tables(positions):
    B, S = positions.shape
    half = HEAD_DIM // 2
    inv = ROPE_THETA ** (-jnp.arange(half, dtype=F32) / half)
    invf = jnp.concatenate([inv, inv]).reshape(1, HEAD_DIM)
    sign = jnp.concatenate([-jnp.ones((half,), F32), jnp.ones((half,), F32)]).reshape(1, HEAD_DIM)
    ts = min(S, 1024)
    out = jax.ShapeDtypeStruct((B, S, HEAD_DIM), F32)
    ospec = pl.BlockSpec((1, ts, HEAD_DIM), lambda b, i: (b, i, 0))
    return pl.pallas_call(
        _rope_table_kernel,
        grid=(B, S // ts),
        in_specs=[pl.BlockSpec((1, ts, 1), lambda b, i: (b, i, 0)),
                  pl.BlockSpec((1, HEAD_DIM), lambda b, i: (0, 0)),
                  pl.BlockSpec((1, HEAD_DIM), lambda b, i: (0, 0))],
        out_specs=[ospec, ospec],
        out_shape=[out, out],
        compiler_params=_cparams(2),
        name="rope_table",
    )(positions.reshape(B, S, 1), invf, sign)


def _rope_rows(x, cosf, sinf):
    return x * cosf + pltpu.roll(x, HEAD_DIM // 2, 1) * sinf


def _rope_apply_kernel(q_ref, ks_ref, kw_ref, cos_ref, sin_ref, qo_ref, kso_ref, kwo_ref):
    cosf = cos_ref[0]
    sinf = sin_ref[0]
    for src, dst, mult in ((q_ref, qo_ref, Q_SCALE_LOG2E), (ks_ref, kso_ref, None), (kw_ref, kwo_ref, None)):
        for h in range(src.shape[2] // HEAD_DIM):
            cols = slice(h * HEAD_DIM, (h + 1) * HEAD_DIM)
            r = _rope_rows(src[0, :, cols].astype(F32), cosf, sinf)
            if mult is not None:
                r = r * mult
            dst[0, :, cols] = r.astype(dst.dtype)


def _rope_apply(big, q_blk, ksel_blk, kwin_blk, wq, wkv, cosf, sinf):
    B, S, _ = big.shape
    ts = min(S, 512)
    tab = pl.BlockSpec((1, ts, HEAD_DIM), lambda b, i: (b, i, 0))
    return pl.pallas_call(
        _rope_apply_kernel,
        grid=(B, S // ts),
        in_specs=[pl.BlockSpec((1, ts, wq), lambda b, i: (b, i, q_blk)),
                  pl.BlockSpec((1, ts, wkv), lambda b, i: (b, i, ksel_blk)),
                  pl.BlockSpec((1, ts, wkv), lambda b, i: (b, i, kwin_blk)),
                  tab, tab],
        out_specs=[pl.BlockSpec((1, ts, wq), lambda b, i: (b, i, 0)),
                   pl.BlockSpec((1, ts, wkv), lambda b, i: (b, i, 0)),
                   pl.BlockSpec((1, ts, wkv), lambda b, i: (b, i, 0))],
        out_shape=[jax.ShapeDtypeStruct((B, S, wq), BF16),
                   jax.ShapeDtypeStruct((B, S, wkv), BF16),
                   jax.ShapeDtypeStruct((B, S, wkv), BF16)],
        compiler_params=_cparams(2),
        name="rope_apply",
    )(big, big, big, cosf, sinf)


def _compress_kernel(hb_ref, pos_ref, w1_ref, w2_ref, cos_ref, sin_ref, o_ref):
    hb = hb_ref[0, 0]
    w1 = w1_ref[0]
    half = hb.shape[1]
    p0 = _dot(hb, w1[:half])
    p1 = _dot(hb, w1[half:])
    nc = p0.shape[0]
    pos8 = jnp.broadcast_to(pos_ref[0], (8, 2 * half)).astype(BF16)
    pb = _dot(pos8, w1)[0:1]
    pre = p0 + pltpu.roll(p1, nc - 1, 0) + pb
    out = _dot(_silu(pre).astype(BF16), w2_ref[0])
    roped = _rope_rows(out, cos_ref[0], sin_ref[0])
    is_key = pl.program_id(1) < NSA_GROUPS
    o_ref[0, 0] = jnp.where(is_key, roped, out).astype(o_ref.dtype)


def _compress(hb, pos_flat, w1, w2, cosc, sinc):
    B, J, NC, HW = hb.shape
    G = NSA_GROUPS
    return pl.pallas_call(
        _compress_kernel,
        grid=(B, J),
        in_specs=[pl.BlockSpec((1, 1, NC, HW), lambda b, j: (b, j, 0, 0)),
                  pl.BlockSpec((1, 1, 2 * HW), lambda b, j: (j // G, 0, 0)),
                  pl.BlockSpec((1, 2 * HW, HEAD_DIM), lambda b, j: (j // G, 0, 0)),
                  pl.BlockSpec((1, HEAD_DIM, HEAD_DIM), lambda b, j: (j // G, 0, 0)),
                  pl.BlockSpec((1, NC, HEAD_DIM), lambda b, j: (b, 0, 0)),
                  pl.BlockSpec((1, NC, HEAD_DIM), lambda b, j: (b, 0, 0))],
        out_specs=pl.BlockSpec((1, 1, NC, HEAD_DIM), lambda b, j: (b, j, 0, 0)),
        out_shape=jax.ShapeDtypeStruct((B, J, NC, HEAD_DIM), BF16),
        compiler_params=_cparams(2),
        name="nsa_compress",
    )(hb, pos_flat, w1, w2, cosc, sinc)


def _gate_col(g_ref, nh_gdn, group, hpg, h, branch):
    col = None
    for gi in range(NSA_GROUPS):
        c = 2 * nh_gdn + (gi * hpg + h) * 3 + branch
        cand = g_ref[0, :, c:c + 1]
        col = cand if col is None else jnp.where(group == gi, cand, col)
    return _sigmoid(col)


def _cmp_attn_kernel(q_ref, kc_ref, vc_ref, sm_ref, o_ref, sel_ref, *, nh_gdn, hpg):
    tq = q_ref.shape[1]
    nc = kc_ref.shape[2]
    nb = sel_ref.shape[3]
    g = pl.program_id(1)
    t0 = pl.program_id(2) * tq
    t_nc = t0 + lax.broadcasted_iota(jnp.int32, (tq, nc), 0)
    n_nc = lax.broadcasted_iota(jnp.int32, (tq, nc), 1)
    valid = (n_nc * NSA_CMP_STRIDE + (NSA_CMP_LEN - 1)) <= t_nc
    t_col = t0 + lax.broadcasted_iota(jnp.int32, (tq, 1), 0)
    has_valid = jnp.where(t_col >= NSA_CMP_LEN - 1, 1.0, 0.0)
    kc = kc_ref[0, 0]
    vc = vc_ref[0, 0]
    psum = jnp.zeros((tq, nc), F32)
    for h in range(hpg):
        cols = slice(h * HEAD_DIM, (h + 1) * HEAD_DIM)
        s = jnp.where(valid, _dot_nt(q_ref[0, :, cols], kc), NEG_INF)
        e = jnp.exp2(s - jnp.max(s, axis=-1, keepdims=True))
        p = e * (has_valid / jnp.sum(e, axis=-1, keepdims=True))
        gate = _gate_col(sm_ref, nh_gdn, g, hpg, h, 0)
        o_ref[0, :, cols] = (_dot(p.astype(BF16), vc) * gate).astype(o_ref.dtype)
        psum = psum + p
    sj = lax.broadcasted_iota(jnp.int32, (nb, nc), 0) * NSA_SEL_LEN
    cn = lax.broadcasted_iota(jnp.int32, (nb, nc), 1) * NSA_CMP_STRIDE
    overlap_t = jnp.where((cn <= sj + (NSA_SEL_LEN - 1)) & (cn + (NSA_CMP_LEN - 1) >= sj), 1.0, 0.0).astype(BF16)
    p0, p1, p2 = _split3(psum)
    imp = _dot_nt(overlap_t, p0) + _dot_nt(overlap_t, p1) + _dot_nt(overlap_t, p2)
    t_nb = t0 + lax.broadcasted_iota(jnp.int32, (nb, tq), 1)
    blk = lax.broadcasted_iota(jnp.int32, (nb, tq), 0)
    cur = t_nb // NSA_SEL_LEN
    forced = (blk == 0) | (blk == cur) | (blk == cur - 1)
    score = jnp.where(forced, SEL_FORCE, jnp.where(blk * NSA_SEL_LEN <= t_nb, imp, -1.0))
    sel = jnp.zeros((nb, tq), F32)
    blk_f = blk.astype(F32)
    for _ in range(min(NSA_SEL_TOP, nb)):
        m = jnp.max(score, axis=0, keepdims=True)
        first = jnp.min(jnp.where(score == m, blk_f, float(nb)), axis=0, keepdims=True)
        pick = blk_f == first
        sel = jnp.where(pick, 1.0, sel)
        score = jnp.where(pick, -jnp.inf, score)
    sel_ref[0, 0] = sel.T.astype(sel_ref.dtype)


def _cmp_attn(q_r, kvc, small, nh_gdn, n_sel):
    B, S, WQ = q_r.shape
    G = NSA_GROUPS
    hpg = WQ // HEAD_DIM // G
    NC = kvc.shape[2]
    tq = min(S, 512)
    return pl.pallas_call(
        functools.partial(_cmp_attn_kernel, nh_gdn=nh_gdn, hpg=hpg),
        grid=(B, G, S // tq),
        in_specs=[pl.BlockSpec((1, tq, hpg * HEAD_DIM), lambda b, g, i: (b, i, g)),
                  pl.BlockSpec((1, 1, NC, HEAD_DIM), lambda b, g, i: (b, g, 0, 0)),
                  pl.BlockSpec((1, 1, NC, HEAD_DIM), lambda b, g, i: (b, G + g, 0, 0)),
                  pl.BlockSpec((1, tq, small.shape[2]), lambda b, g, i: (b, i, 0))],
        out_specs=[pl.BlockSpec((1, tq, hpg * HEAD_DIM), lambda b, g, i: (b, i, g)),
                   pl.BlockSpec((1, 1, tq, n_sel), lambda b, g, i: (b, g, i, 0))],
        out_shape=[jax.ShapeDtypeStruct((B, S, WQ), BF16),
                   jax.ShapeDtypeStruct((B, G, S, n_sel), BF16)],
        compiler_params=_cparams(3),
        name="nsa_cmp_attn",
    )(q_r, kvc, kvc, small)


def _sel_attn_kernel(q_ref, k_ref, ext_ref, v_ref, sel_ref, sm_ref, o_ref, qx_ref, m_ref, acc_ref, p_ref, a_ref,
                     *, nh_gdn, hpg, tk):
    tq = q_ref.shape[1]
    nb = sel_ref.shape[3]
    g = pl.program_id(1)
    t0 = pl.program_id(2) * tq
    assert tk % tq == 0
    unsel = (sel_ref[0, 0].astype(F32) - 1.0).astype(BF16)
    for h in range(hpg):
        qx_ref[h * tq:(h + 1) * tq, 0:HEAD_DIM] = q_ref[0, :, h * HEAD_DIM:(h + 1) * HEAD_DIM]
        qx_ref[h * tq:(h + 1) * tq, HEAD_DIM:HEAD_DIM + nb] = unsel
    m_ref[...] = jnp.full_like(m_ref, NEG_INF)
    acc_ref[...] = jnp.zeros_like(acc_ref)
    p_ref[...] = jnp.zeros_like(p_ref)
    a_ref[...] = jnp.ones_like(a_ref)
    ones = jnp.ones((tk, HEAD_DIM), BF16)
    pr = hpg * tq // N_ROW_PARTS
    parts = [slice(part * pr, (part + 1) * pr) for part in range(N_ROW_PARTS)]

    def apply_values(kt):
        k0 = pl.multiple_of(kt * tk, tk)
        vx = jnp.concatenate([v_ref[0, pl.ds(k0, tk), :], ones], axis=1)
        for rows in parts:
            acc_ref[rows, :] = _tile_lanes(a_ref[rows, :], 2) * acc_ref[rows, :] + _dot(p_ref[rows, :], vx)

    def scores(kt, causal):
        k0 = pl.multiple_of(kt * tk, tk)
        kx = jnp.concatenate([k_ref[0, pl.ds(k0, tk), :], ext_ref[pl.ds(k0, tk), :]], axis=1)
        if causal:
            t_row = t0 + lax.broadcasted_iota(jnp.int32, (tq, tk), 0)
            key = k0 + lax.broadcasted_iota(jnp.int32, (tq, tk), 1)
            keep = jnp.concatenate([key <= t_row] * (hpg // N_ROW_PARTS), axis=0)
        for rows in parts:
            s = _dot_nt(qx_ref[rows, :], kx)
            if causal:
                s = jnp.where(keep, s, NEG_INF)
            m_old = m_ref[rows, :]
            m_new = jnp.maximum(m_old, jnp.max(s, axis=-1, keepdims=True))
            a_ref[rows, :] = jnp.exp2(m_old - m_new)
            p_ref[rows, :] = jnp.exp2(s - _tile_lanes(m_new, tk // HEAD_DIM)).astype(BF16)
            m_ref[rows, :] = m_new

    kt_diag = t0 // tk

    def body(kt, carry):
        apply_values(jnp.maximum(kt - 1, 0))
        scores(kt, False)
        return carry

    lax.fori_loop(0, kt_diag, body, 0)
    apply_values(jnp.maximum(kt_diag - 1, 0))
    scores(kt_diag, True)
    apply_values(kt_diag)
    for h in range(hpg):
        rows = slice(h * tq, (h + 1) * tq)
        gate = _gate_col(sm_ref, nh_gdn, g, hpg, h, 1)
        out = acc_ref[rows, 0:HEAD_DIM] / acc_ref[rows, HEAD_DIM:2 * HEAD_DIM]
        o_ref[0, :, h * HEAD_DIM:(h + 1) * HEAD_DIM] = (out * gate).astype(o_ref.dtype)


def _sel_attn(q_r, ksel_r, big, vsel_blk, sel, small, nh_gdn):
    B, S, WQ = q_r.shape
    G = NSA_GROUPS
    hpg = WQ // HEAD_DIM // G
    nbp = sel.shape[3]
    tq = min(S, 512)
    tk = min(S, 512)
    rows = hpg * tq
    ext = jnp.where(jnp.arange(S)[:, None] // NSA_SEL_LEN == jnp.arange(nbp)[None, :], MASK_BIG, 0.0).astype(BF16)
    return pl.pallas_call(
        functools.partial(_sel_attn_kernel, nh_gdn=nh_gdn, hpg=hpg, tk=tk),
        grid=(B, G, S // tq),
        in_specs=[pl.BlockSpec((1, tq, hpg * HEAD_DIM), lambda b, g, i: (b, i, g)),
                  pl.BlockSpec((1, S, HEAD_DIM), lambda b, g, i: (b, 0, g)),
                  pl.BlockSpec((S, nbp), lambda b, g, i: (0, 0)),
                  pl.BlockSpec((1, S, HEAD_DIM), lambda b, g, i: (b, 0, vsel_blk + g)),
                  pl.BlockSpec((1, 1, tq, nbp), lambda b, g, i: (b, g, i, 0)),
                  pl.BlockSpec((1, tq, small.shape[2]), lambda b, g, i: (b, i, 0))],
        out_specs=pl.BlockSpec((1, tq, hpg * HEAD_DIM), lambda b, g, i: (b, i, g)),
        out_shape=jax.ShapeDtypeStruct((B, S, WQ), BF16),
        scratch_shapes=[pltpu.VMEM((rows, HEAD_DIM + nbp), BF16),
                        pltpu.VMEM((rows, HEAD_DIM), F32),
                        pltpu.VMEM((rows, 2 * HEAD_DIM), F32),
                        pltpu.VMEM((rows, tk), BF16),
                        pltpu.VMEM((rows, HEAD_DIM), F32)],
        compiler_params=_cparams(3),
        name="nsa_sel_attn",
    )(q_r, ksel_r, ext, big, sel, small)


def _win_attn_kernel(q_ref, k_ref, v_ref, sm_ref, o_ref, *, nh_gdn, hpg):
    tq = q_ref.shape[1]
    ts = min(tq, WIN_SUB_TILE)
    span = ts + NSA_WINDOW
    g = pl.program_id(1)
    for sub in range(tq // ts):
        qrows = slice(sub * ts, (sub + 1) * ts)
        t0 = pl.program_id(2) * tq + sub * ts
        k0 = pl.multiple_of(jnp.maximum(t0 - NSA_WINDOW, 0), ts)
        k = k_ref[0, pl.ds(k0, span), :]
        v = v_ref[0, pl.ds(k0, span), :]
        t_row = t0 + lax.broadcasted_iota(jnp.int32, (ts, span), 0)
        spos = k0 + lax.broadcasted_iota(jnp.int32, (ts, span), 1)
        mask = (spos <= t_row) & (spos > t_row - NSA_WINDOW)
        vx = jnp.concatenate([v, jnp.ones((span, HEAD_DIM), BF16)], axis=1)
        q4 = jnp.concatenate([q_ref[0, qrows, h * HEAD_DIM:(h + 1) * HEAD_DIM] for h in range(hpg)], axis=0)
        s = jnp.where(jnp.concatenate([mask] * hpg, axis=0), _dot_nt(q4, k), NEG_INF)
        e = jnp.exp2(s - jnp.max(s, axis=-1, keepdims=True))
        acc = _dot(e.astype(BF16), vx)
        out = acc[:, :HEAD_DIM] / acc[:, HEAD_DIM:]
        for h in range(hpg):
            gate = _gate_col(sm_ref, nh_gdn, g, hpg, h, 2)[qrows]
            o_ref[0, qrows, h * HEAD_DIM:(h + 1) * HEAD_DIM] = (out[h * ts:(h + 1) * ts] * gate).astype(o_ref.dtype)


def _win_attn(q_r, kwin_r, big, vwin_blk, small, nh_gdn):
    B, S, WQ = q_r.shape
    G = NSA_GROUPS
    hpg = WQ // HEAD_DIM // G
    tq = min(S, 512)
    assert S >= tq + NSA_WINDOW
    return pl.pallas_call(
        functools.partial(_win_attn_kernel, nh_gdn=nh_gdn, hpg=hpg),
        grid=(B, G, S // tq),
        in_specs=[pl.BlockSpec((1, tq, hpg * HEAD_DIM), lambda b, g, i: (b, i, g)),
                  pl.BlockSpec((1, S, HEAD_DIM), lambda b, g, i: (b, 0, g)),
                  pl.BlockSpec((1, S, HEAD_DIM), lambda b, g, i: (b, 0, vwin_blk + g)),
                  pl.BlockSpec((1, tq, small.shape[2]), lambda b, g, i: (b, i, 0))],
        out_specs=pl.BlockSpec((1, tq, hpg * HEAD_DIM), lambda b, g, i: (b, i, g)),
        out_shape=jax.ShapeDtypeStruct((B, S, WQ), BF16),
        compiler_params=_cparams(3),
        name="nsa_win_attn",
    )(q_r, kwin_r, big, small)


def _post_norm(x, y, gt, g, b, alpha):
    r = alpha * x + (1.0 + gt) * y
    return _normalize_rows(r) * g + b


def _mixer_out_kernel(oa_ref, oc_ref, os_ref, ow_ref, ma_ref, mb_ref, x_ref, gt_ref, lg_ref, lb_ref,
                      wg_ref, wn_ref, wo_ref, o_ref, *, alpha):
    y_a = _dot(oa_ref[0], wg_ref[...])
    o_b = oc_ref[0].astype(F32) + os_ref[0].astype(F32) + ow_ref[0].astype(F32)
    y_b = _dot(o_b.astype(BF16), wn_ref[...])
    mixed = _sigmoid(ma_ref[0].astype(F32)) * y_a + _sigmoid(mb_ref[0].astype(F32)) * y_b
    y = _dot(mixed.astype(BF16), wo_ref[...])
    o_ref[0] = _post_norm(x_ref[0], y, gt_ref[0], lg_ref[...], lb_ref[...], alpha)


def _mixer_out(o_a, o_c, o_s, o_w, big, ma_blk, mb_blk, x, ada_l, gt_col, ln_g, ln_b, wg, wn, wo, alpha):
    B, S, D = x.shape
    tm = min(S, 512)
    row = lambda blk: pl.BlockSpec((1, tm, D), lambda b, i: (b, i, blk))
    full = lambda shape: pl.BlockSpec(shape, lambda b, i: (0,) * len(shape))
    return pl.pallas_call(
        functools.partial(_mixer_out_kernel, alpha=alpha),
        grid=(B, S // tm),
        in_specs=[row(0), row(0), row(0), row(0), row(ma_blk), row(mb_blk), row(0),
                  pl.BlockSpec((1, 1, D), lambda b, i: (b, 0, gt_col)),
                  full((1, D)), full((1, D)), full((D, D)), full((D, D)), full((D, D))],
        out_specs=row(0),
        out_shape=jax.ShapeDtypeStruct((B, S, D), F32),
        compiler_params=_cparams(2),
        name="mixer_out",
    )(o_a, o_c, o_s, o_w, big, big, x, ada_l, ln_g.reshape(1, D), ln_b.reshape(1, D), wg, wn, wo)


def _first_argmax(vals, lane, big):
    m = jnp.max(vals, axis=-1, keepdims=True)
    first = jnp.min(jnp.where(vals == m, lane, big), axis=-1, keepdims=True)
    return m, first


def _moe_router_kernel(x_ref, sh_ref, sc_ref, rwh_ref, rwl_ref, rb_ref, wsg_ref, wsu_ref, wsd_ref,
                       h_ref, idx_ref, wt_ref, shared_ref, rank_ref, cnt_ref):
    h = _normalize_rows(x_ref[0]) * (1.0 + sc_ref[0]) + sh_ref[0]
    h16 = h.astype(BF16)
    h_ref[0] = h16
    h_lo = (h - h16.astype(F32)).astype(BF16)
    rw_hi = rwh_ref[...]
    logits = _dot_nt(rw_hi, h16) + _dot_nt(rwl_ref[...], h16) + _dot_nt(rw_hi, h_lo)
    scores = _sigmoid(logits)
    biased = scores + rb_ref[...]
    E, tm = scores.shape
    NG = N_EXPERT_GROUPS
    per = E // NG
    b3 = biased.reshape(NG, per, tm)
    in_grp = lax.broadcasted_iota(jnp.int32, (NG, per, tm), 1).astype(F32)
    m1 = jnp.max(b3, axis=1, keepdims=True)
    first = jnp.min(jnp.where(b3 == m1, in_grp, float(per)), axis=1, keepdims=True)
    m2 = jnp.max(jnp.where(in_grp == first, -jnp.inf, b3), axis=1, keepdims=True)
    gscore = (m1 + m2).reshape(NG, tm)
    g_id = lax.broadcasted_iota(jnp.int32, (NG, tm), 0).astype(F32)
    allowed = jnp.zeros((NG, tm), F32)
    for _ in range(TOPK_GROUPS):
        m = jnp.max(gscore, axis=0, keepdims=True)
        first_g = jnp.min(jnp.where(gscore == m, g_id, float(NG)), axis=0, keepdims=True)
        pick = g_id == first_g
        allowed = jnp.where(pick, 1.0, allowed)
        gscore = jnp.where(pick, -jnp.inf, gscore)
    cand = jnp.where(allowed.reshape(NG, 1, tm) > 0.5, b3, NEG_INF).reshape(E, tm)
    e_id = lax.broadcasted_iota(jnp.int32, (E, tm), 0).astype(F32)
    idx_rows, wt_rows, picks = [], [], []
    wsum = jnp.zeros((1, tm), F32)
    chosen = jnp.zeros((E, tm), F32)
    for kk in range(TOP_K):
        m = jnp.max(cand, axis=0, keepdims=True)
        first_e = jnp.min(jnp.where(cand == m, e_id, float(E)), axis=0, keepdims=True)
        pick = e_id == first_e
        w = jnp.sum(jnp.where(pick, scores, 0.0), axis=0, keepdims=True)
        cand = jnp.where(pick, -jnp.inf, cand)
        chosen = jnp.where(pick, 1.0, chosen)
        idx_rows.append(first_e)
        wt_rows.append(w)
        picks.append(pick)
        wsum = wsum + w
    idx_ref[0] = jnp.concatenate(idx_rows, axis=0).astype(jnp.int32)
    wt_ref[0] = jnp.concatenate(wt_rows, axis=0) / wsum * ROUTED_SCALE
    @pl.when((pl.program_id(0) == 0) & (pl.program_id(1) == 0))
    def _():
        cnt_ref[...] = jnp.zeros_like(cnt_ref)

    si = lax.broadcasted_iota(jnp.int32, (tm, tm), 0)
    ti = lax.broadcasted_iota(jnp.int32, (tm, tm), 1)
    before = jnp.where(si < ti, 1.0, 0.0).astype(BF16)
    prior = _dot(chosen.astype(BF16), before) + cnt_ref[...]
    rank_ref[0] = jnp.concatenate([jnp.sum(jnp.where(pk, prior, 0.0), axis=0, keepdims=True) for pk in picks],
                                  axis=0).astype(jnp.int32)
    cnt_ref[...] = cnt_ref[...] + jnp.sum(chosen, axis=1, keepdims=True)
    act = _silu(_dot(h16, wsg_ref[...])) * _dot(h16, wsu_ref[...])
    shared_ref[0] = _dot(act.astype(BF16), wsd_ref[...])


def _moe_router(x, ada_l, sh_col, sc_col, rwt_hi, rwt_lo, router_b, wsg, wsu, wsd):
    B, S, D = x.shape
    E = rwt_hi.shape[0]
    FF = wsg.shape[1]
    tm = min(S, 512)
    row = lambda w: pl.BlockSpec((1, tm, w), lambda b, i: (b, i, 0))
    krow = pl.BlockSpec((1, TOP_K, tm), lambda b, i: (b, 0, i))
    full = lambda shape: pl.BlockSpec(shape, lambda b, i: (0,) * len(shape))
    return pl.pallas_call(
        _moe_router_kernel,
        grid=(B, S // tm),
        in_specs=[row(D),
                  pl.BlockSpec((1, 1, D), lambda b, i: (b, 0, sh_col)),
                  pl.BlockSpec((1, 1, D), lambda b, i: (b, 0, sc_col)),
                  full((E, D)), full((E, D)), full((E, 1)), full((D, FF)), full((D, FF)), full((FF, D))],
        out_specs=[row(D), krow, krow, row(D), krow, full((E, 1))],
        out_shape=[jax.ShapeDtypeStruct((B, S, D), BF16),
                   jax.ShapeDtypeStruct((B, TOP_K, S), jnp.int32),
                   jax.ShapeDtypeStruct((B, TOP_K, S), F32),
                   jax.ShapeDtypeStruct((B, S, D), F32),
                   jax.ShapeDtypeStruct((B, TOP_K, S), jnp.int32),
                   jax.ShapeDtypeStruct((E, 1), F32)],
        compiler_params=_cparams(2),
        name="moe_router",
    )(x, ada_l, ada_l, rwt_hi, rwt_lo, router_b.reshape(E, 1), wsg, wsu, wsd)


def _moe_experts_kernel(be_ref, nu_ref, xs_ref, wg_ref, wu_ref, wd_ref, *rest, first_block):
    o_ref = rest[-1]

    @pl.when(pl.program_id(0) + first_block < nu_ref[0])
    def _():
        xb = xs_ref[...]
        act = _silu(_dot(xb, wg_ref[0, 0].astype(BF16))) * _dot(xb, wu_ref[0, 0].astype(BF16))
        o_ref[...] = _dot(act.astype(BF16), wd_ref[0, 0].astype(BF16)).astype(o_ref.dtype)


def _moe_experts(block_expert, n_used, xs, layer, wg, wu, wd, first_block, total_rows, prev=None):
    rows, D = xs.shape
    BM = MOE_ROW_BLOCK
    FF = wg.shape[3]
    fb = first_block
    in_specs = [pl.BlockSpec((BM, D), lambda i, be, nu: (i, 0)),
                pl.BlockSpec((1, 1, D, FF), lambda i, be, nu: (layer, be[i + fb], 0, 0)),
                pl.BlockSpec((1, 1, D, FF), lambda i, be, nu: (layer, be[i + fb], 0, 0)),
                pl.BlockSpec((1, 1, FF, D), lambda i, be, nu: (layer, be[i + fb], 0, 0))]
    args = [block_expert, n_used, xs, wg, wu, wd]
    aliases = {}
    if prev is not None:
        in_specs.append(pl.BlockSpec(memory_space=pl.ANY))
        args.append(prev)
        aliases = {len(args) - 1: 0}
    grid_spec = pltpu.PrefetchScalarGridSpec(
        num_scalar_prefetch=2,
        grid=(rows // BM,),
        in_specs=in_specs,
        out_specs=pl.BlockSpec((BM, D), lambda i, be, nu: (i + fb, 0)),
    )
    return pl.pallas_call(
        functools.partial(_moe_experts_kernel, first_block=fb),
        grid_spec=grid_spec,
        out_shape=jax.ShapeDtypeStruct((total_rows, D), BF16),
        input_output_aliases=aliases,
        compiler_params=_cparams(1),
        name="moe_experts",
    )(*args)


def _moe_out_kernel(x_ref, shared_ref, yg_ref, wt_ref, gt_ref, lg_ref, lb_ref, *rest, alpha):
    o_ref = rest[-1]
    y = shared_ref[0]
    wt = wt_ref[0]
    for kk in range(TOP_K):
        y = y + wt[:, kk:kk + 1] * yg_ref[kk, 0].astype(F32)
    o_ref[0] = _post_norm(x_ref[0], y, gt_ref[0], lg_ref[...], lb_ref[...], alpha)


def _moe_out(x, shared, yg, wt, ada_l, gt_col, ln_g, ln_b, alpha, first_b, prev=None):
    B, S, D = x.shape
    nb = yg.shape[1]
    fb = first_b
    tm = min(S, 256)
    row = lambda w: pl.BlockSpec((1, tm, w), lambda b, i: (b + fb, i, 0))
    full = lambda shape: pl.BlockSpec(shape, lambda b, i: (0,) * len(shape))
    in_specs = [row(D), row(D),
                pl.BlockSpec((TOP_K, 1, tm, D), lambda b, i: (0, b, i, 0)),
                row(wt.shape[2]),
                pl.BlockSpec((1, 1, D), lambda b, i: (b + fb, 0, gt_col)),
                full((1, D)), full((1, D))]
    args = [x, shared, yg, wt, ada_l, ln_g.reshape(1, D), ln_b.reshape(1, D)]
    aliases = {}
    if prev is not None:
        in_specs.append(pl.BlockSpec(memory_space=pl.ANY))
        args.append(prev)
        aliases = {len(args) - 1: 0}
    return pl.pallas_call(
        functools.partial(_moe_out_kernel, alpha=alpha),
        grid=(nb, S // tm),
        in_specs=in_specs,
        out_specs=row(D),
        out_shape=jax.ShapeDtypeStruct((B, S, D), F32),
        input_output_aliases=aliases,
        compiler_params=_cparams(2),
        name="moe_out",
    )(*args)


def _moe_dest_kernel(start_ref, e_ref, r_ref, o_ref):
    e = e_ref[...]
    acc = r_ref[...]
    for j in range(start_ref.shape[0]):
        acc = acc + jnp.where(e == j, start_ref[j], 0)
    o_ref[...] = acc


def _routing_layout(expert_idx, rank, counts):
    B, K, S = expert_idx.shape
    T = B * S
    n_experts = counts.shape[0]
    BM = MOE_ROW_BLOCK
    TK = T * K
    counts = counts.reshape(n_experts).astype(jnp.int32)
    padded = (counts + BM - 1) // BM * BM
    pad_end = jnp.cumsum(padded)
    pad_start = (pad_end - padded).astype(jnp.int32)
    ts = min(S, 2048)
    spec = pl.BlockSpec((1, K, ts), lambda b, i, tab: (b, 0, i))
    dest = pl.pallas_call(
        _moe_dest_kernel,
        grid_spec=pltpu.PrefetchScalarGridSpec(num_scalar_prefetch=1, grid=(B, S // ts),
                                               in_specs=[spec, spec], out_specs=spec),
        out_shape=jax.ShapeDtypeStruct((B, K, S), jnp.int32),
        compiler_params=_cparams(2),
        name="moe_dest",
    )(pad_start, expert_idx, rank)
    dest = jnp.swapaxes(dest, 0, 1).reshape(TK)
    n_blocks = (TK + n_experts * (BM - 1) + BM - 1) // BM
    rows = n_blocks * BM
    token = jnp.tile(jnp.arange(T, dtype=jnp.int32), K)
    row_tok = (jnp.arange(rows, dtype=jnp.int32) % T).at[dest].add(token - dest % T, mode='promise_in_bounds')
    block_start = jnp.arange(n_blocks, dtype=jnp.int32) * BM
    block_expert = jnp.minimum(jnp.sum(pad_end[None, :] <= block_start[:, None], axis=1),
                               n_experts - 1).astype(jnp.int32)
    n_used = (pad_end[-1] // BM).astype(jnp.int32).reshape(1)
    return row_tok, dest, block_expert, n_used


def kernel(x, c, positions, ada_w, ada_b, w_in, gdn_conv_w, gdn_a_log, gdn_dt_bias, gdn_norm_w, w_gdn_branch, nsa_cmp_pos_k, nsa_cmp_pos_v, nsa_cmp_k_w1, nsa_cmp_k_w2, nsa_cmp_v_w1, nsa_cmp_v_w2, w_nsa_branch, w_out, ln1_g, ln1_b, router_w, router_bias, w_sh_gate, w_sh_up, w_sh_down, w_e_gate, w_e_up, w_e_down, ln2_g, ln2_b):
    B, S, D = x.shape
    L = ada_w.shape[0]
    T = B * S
    G = NSA_GROUPS
    nh_gdn = gdn_a_log.shape[1]
    gdn_w = nh_gdn * HEAD_DIM
    nsa_w = w_nsa_branch.shape[1]
    kvw = G * HEAD_DIM
    E = router_w.shape[2]
    alpha = (2.0 * L) ** 0.25
    assert gdn_w == D and nsa_w == D and S % NSA_SEL_LEN == 0

    splits = (3 * gdn_w, gdn_w, nh_gdn, nh_gdn, nsa_w, 6 * kvw, 3 * (nsa_w // HEAD_DIM), D, D)
    offs = [0]
    for s_ in splits:
        offs.append(offs[-1] + s_)
    seg = lambda i: slice(offs[i], offs[i + 1])
    w_big = jnp.concatenate([w_in[:, :, seg(0)], w_in[:, :, seg(1)], w_in[:, :, seg(4)],
                             w_in[:, :, seg(7)], w_in[:, :, seg(8)], w_in[:, :, seg(5)]], axis=-1).astype(BF16)
    n_small = splits[2] + splits[3] + splits[6]
    w_small = jnp.concatenate([w_in[:, :, seg(2)], w_in[:, :, seg(3)], w_in[:, :, seg(6)],
                               jnp.zeros((L, D, 128 - n_small), w_in.dtype)], axis=-1).astype(BF16)
    Z_BLK, Q_BLK, MA_BLK, MB_BLK = 3, 4, 5, 6
    kv0 = 7 * D
    n_big = w_big.shape[2]
    tn_big = n_big // 4 if (n_big // 4) % 128 == 0 else 128

    wgb = w_gdn_branch.astype(BF16)
    wnb = w_nsa_branch.astype(BF16)
    wob = w_out.astype(BF16)
    wsg = w_sh_gate.astype(BF16)
    wsu = w_sh_up.astype(BF16)
    wsd = w_sh_down.astype(BF16)
    rwt = jnp.swapaxes(router_w, 1, 2)
    rwt_hi = rwt.astype(BF16)
    rwt_lo = (rwt - rwt_hi.astype(F32)).astype(BF16)
    cmp_w1 = jnp.stack([nsa_cmp_k_w1, nsa_cmp_v_w1], axis=1).astype(BF16)
    cmp_w2 = jnp.stack([nsa_cmp_k_w2, nsa_cmp_v_w2], axis=1).astype(BF16)
    cmp_pos = jnp.stack([nsa_cmp_pos_k, nsa_cmp_pos_v], axis=1).reshape(L, 2, 1, NSA_CMP_LEN * HEAD_DIM)

    ada = _ada(c, ada_w, ada_b)
    cosf, sinf = _rope_tables(positions)
    NC = S // NSA_CMP_STRIDE
    last = jnp.minimum(jnp.arange(NC) * NSA_CMP_STRIDE + NSA_CMP_LEN - 1, S - 1)
    cosc = cosf[:, last]
    sinc = sinf[:, last]
    n_sel = -(-(S // NSA_SEL_LEN) // 128) * 128

    for l in range(L):
        ada_l = ada[l].reshape(B, 1, 6 * D)
        big, small, cmp_in = _proj(x, ada_l, 0, 1, w_big[l], w_small[l], tn_big, kv0, 2 * G)
        small_t = jnp.swapaxes(small[:, :, nh_gdn:2 * nh_gdn].reshape(B, S // GDN_CHUNK, GDN_CHUNK, nh_gdn), 2, 3)
        q_a, k_a, v_a = _gdn_prep(big, gdn_conv_w[l], gdn_w)
        o_a = _gdn(q_a, k_a, v_a, big, Z_BLK, small, small_t, gdn_a_log[l], gdn_dt_bias[l], gdn_norm_w[l])
        q_r, ksel_r, kwin_r = _rope_apply(big, Q_BLK, (kv0 + 2 * kvw) // kvw, (kv0 + 4 * kvw) // kvw,
                                          nsa_w, kvw, cosf, sinf)
        hb = cmp_in.reshape(B, 2 * G, NC, NSA_CMP_STRIDE * HEAD_DIM)
        kvc = _compress(hb, cmp_pos[l], cmp_w1[l], cmp_w2[l], cosc, sinc)
        o_c, sel = _cmp_attn(q_r, kvc, small, nh_gdn, n_sel)
        o_s = _sel_attn(q_r, ksel_r, big, (kv0 + 3 * kvw) // HEAD_DIM, sel, small, nh_gdn)
        o_w = _win_attn(q_r, kwin_r, big, (kv0 + 5 * kvw) // HEAD_DIM, small, nh_gdn)
        x = _mixer_out(o_a, o_c, o_s, o_w, big, MA_BLK, MB_BLK, x, ada_l, 2, ln1_g[l], ln1_b[l],
                       wgb[l], wnb[l], wob[l], alpha)
        h2, eidx, ewt, shared, rank, counts = _moe_router(x, ada_l, 3, 4, rwt_hi[l], rwt_lo[l], router_bias[l],
                                                          wsg[l], wsu[l], wsd[l])
        row_tok, dest, block_expert, n_used = _routing_layout(eidx, rank, counts)
        rows = row_tok.shape[0]
        n_blocks = rows // MOE_ROW_BLOCK
        h2f = h2.reshape(T, D)
        bounds = [n_blocks * p // MOE_DISPATCH_PARTS for p in range(MOE_DISPATCH_PARTS + 1)]
        yb = None
        for lo, hi in zip(bounds[:-1], bounds[1:]):
            xs = h2f.at[row_tok[lo * MOE_ROW_BLOCK:hi * MOE_ROW_BLOCK]].get(mode='promise_in_bounds')
            yb = _moe_experts(block_expert, n_used, xs, l, w_e_gate, w_e_up, w_e_down, lo, rows, prev=yb)
        n_cp = MOE_COMBINE_PARTS if B % MOE_COMBINE_PARTS == 0 else 1
        bp = B // n_cp
        dest3 = dest.reshape(TOP_K, B, S)
        wt = jnp.swapaxes(ewt, 1, 2)
        x_new = None
        for p in range(n_cp):
            idx = dest3[:, p * bp:(p + 1) * bp].reshape(-1)
            yg = yb.at[idx].get(mode='promise_in_bounds').reshape(TOP_K, bp, S, D)
            x_new = _moe_out(x, shared, yg, wt, ada_l, 5, ln2_g[l], ln2_b[l], alpha, p * bp, prev=x_new)
        x = x_new
    return x
```

```python
import functools
import math

import jax
import jax.numpy as jnp
from jax import lax
from jax.experimental import pallas as pl
from jax.experimental.pallas import tpu as pltpu

F32 = jnp.float32
BF16 = jnp.bfloat16

HEAD_DIM = 128
GDN_CONV = 4
GDN_CHUNK = 64
NSA_GROUPS = 2
NSA_CMP_LEN = 32
NSA_CMP_STRIDE = 16
NSA_SEL_LEN = 64
NSA_SEL_TOP = 16
NSA_WINDOW = 512
ROPE_THETA = 10000.0
N_EXPERT_GROUPS = 8
TOPK_GROUPS = 4
TOP_K = 8
ROUTED_SCALE = 2.5
LN_EPS = 1e-5
NEG_INF = -1e30
SEL_FORCE = 1e6
MOE_ROW_BLOCK = 512
Q_SCALE_LOG2E = (HEAD_DIM ** -0.5) * math.log2(math.e)
MASK_BIG = 2.0 ** 100
N_ROW_PARTS = 2
GDN_PHASE1_CHUNKS = 4
WIN_SUB_TILE = 128
MOE_DISPATCH_PARTS = 4
MOE_COMBINE_PARTS = 2
_ARB = "arbitrary"


def _cparams(n_axes):
    return pltpu.CompilerParams(dimension_semantics=(_ARB,) * n_axes)


def _sigmoid(x):
    return 1.0 / (1.0 + jnp.exp(-x))


def _silu(x):
    return x * _sigmoid(x)


def _dot(a, b):
    return jnp.dot(a, b, preferred_element_type=F32)


def _dot_nt(a, b):
    return lax.dot_general(a, b, (((1,), (1,)), ((), ())), preferred_element_type=F32)


def _dot_tn(a, b):
    return lax.dot_general(a, b, (((0,), (0,)), ((), ())), preferred_element_type=F32)


def _tile_lanes(x, n):
    return jnp.concatenate([x] * n, axis=1)


def _split3(x):
    x0 = x.astype(BF16)
    r1 = x - x0.astype(F32)
    x1 = r1.astype(BF16)
    x2 = (r1 - x1.astype(F32)).astype(BF16)
    return x0, x1, x2


def _dot_sel_right(x, sel_bf16):
    x0, x1, x2 = _split3(x)
    return _dot(x0, sel_bf16) + _dot(x1, sel_bf16) + _dot(x2, sel_bf16)


def _dot_sel_left(sel_bf16, x):
    x0, x1, x2 = _split3(x)
    return _dot(sel_bf16, x0) + _dot(sel_bf16, x1) + _dot(sel_bf16, x2)


def _normalize_rows(x):
    mu = jnp.mean(x, axis=-1, keepdims=True)
    xc = x - mu
    var = jnp.mean(xc * xc, axis=-1, keepdims=True)
    return xc * lax.rsqrt(var + LN_EPS)


def _ada_kernel(c_ref, w_ref, b_ref, o_ref):
    cond = _silu(c_ref[...])
    o_ref[0] = jnp.dot(cond, w_ref[0], preferred_element_type=F32,
                       precision=lax.Precision.HIGHEST) + b_ref[0]


def _ada(c, ada_w, ada_b):
    L, D, N = ada_w.shape
    B = c.shape[0]
    tn = min(N, 1536)
    return pl.pallas_call(
        _ada_kernel,
        grid=(L, N // tn),
        in_specs=[pl.BlockSpec((B, D), lambda l, j: (0, 0)),
                  pl.BlockSpec((1, D, tn), lambda l, j: (l, 0, j)),
                  pl.BlockSpec((1, 1, tn), lambda l, j: (l, 0, j))],
        out_specs=pl.BlockSpec((1, B, tn), lambda l, j: (l, 0, j)),
        out_shape=jax.ShapeDtypeStruct((L, B, N), F32),
        compiler_params=_cparams(2),
        name="ada",
    )(c, ada_w, ada_b.reshape(L, 1, N))


def _proj_kernel(x_ref, sh_ref, sc_ref, w_ref, ws_ref, o_ref, os_ref, oc_ref, h_ref, *, head_tile, head_off):
    @pl.when(pl.program_id(2) == 0)
    def _():
        h = _normalize_rows(x_ref[0]) * (1.0 + sc_ref[0]) + sh_ref[0]
        h_ref[...] = h.astype(BF16)
        os_ref[0] = _dot(h_ref[...], ws_ref[...])

    res = _dot(h_ref[...], w_ref[...]).astype(o_ref.dtype)
    o_ref[0] = res

    @pl.when(pl.program_id(2) == head_tile)
    def _():
        for jj in range(oc_ref.shape[1]):
            oc_ref[0, jj] = res[:, head_off + jj * HEAD_DIM:head_off + (jj + 1) * HEAD_DIM]


def _proj(x, ada_l, sh_col, sc_col, w, w_small, tn, head_col, n_head_cols):
    B, S, D = x.shape
    N = w.shape[1]
    NS = w_small.shape[1]
    tm = min(S, 1024)
    head_tile, head_off = head_col // tn, head_col % tn
    assert head_off % HEAD_DIM == 0 and head_off + n_head_cols * HEAD_DIM <= tn
    return pl.pallas_call(
        functools.partial(_proj_kernel, head_tile=head_tile, head_off=head_off),
        grid=(B, S // tm, N // tn),
        in_specs=[pl.BlockSpec((1, tm, D), lambda b, i, j: (b, i, 0)),
                  pl.BlockSpec((1, 1, D), lambda b, i, j: (b, 0, sh_col)),
                  pl.BlockSpec((1, 1, D), lambda b, i, j: (b, 0, sc_col)),
                  pl.BlockSpec((D, tn), lambda b, i, j: (0, j)),
                  pl.BlockSpec((D, NS), lambda b, i, j: (0, 0))],
        out_specs=[pl.BlockSpec((1, tm, tn), lambda b, i, j: (b, i, j)),
                   pl.BlockSpec((1, tm, NS), lambda b, i, j: (b, i, 0)),
                   pl.BlockSpec((1, n_head_cols, tm, HEAD_DIM), lambda b, i, j: (b, 0, i, 0))],
        out_shape=[jax.ShapeDtypeStruct((B, S, N), BF16),
                   jax.ShapeDtypeStruct((B, S, NS), F32),
                   jax.ShapeDtypeStruct((B, n_head_cols, S, HEAD_DIM), BF16)],
        scratch_shapes=[pltpu.VMEM((tm, D), BF16)],
        compiler_params=_cparams(3),
        name="proj",
    )(x, ada_l, ada_l, w, w_small)


def _gdn_prep_kernel(x_ref, w_ref, q_ref, k_ref, v_ref, carry_ref):
    ts = x_ref.shape[1]
    width = q_ref.shape[2]
    nh = width // HEAD_DIM

    @pl.when(pl.program_id(1) == 0)
    def _():
        carry_ref[...] = jnp.zeros_like(carry_ref)

    for part, o_ref in enumerate((q_ref, k_ref, v_ref)):
        for h in range(nh):
            c0 = part * width + h * HEAD_DIM
            cols = slice(c0, c0 + HEAD_DIM)
            xx = jnp.concatenate([carry_ref[:, cols], x_ref[0, :, cols].astype(F32)], axis=0)
            w = w_ref[:, cols]
            y = xx[8:8 + ts] * w[GDN_CONV - 1:GDN_CONV]
            for kk in range(GDN_CONV - 1):
                off = 8 - (GDN_CONV - 1) + kk
                y = y + xx[off:off + ts] * w[kk:kk + 1]
            y = _silu(y)
            if part < 2:
                norm = lax.rsqrt(jnp.sum(y * y, axis=-1, keepdims=True) + 1e-6)
                if part == 0:
                    norm = norm * (HEAD_DIM ** -0.5)
                y = y * norm
            o_ref[0, :, h * HEAD_DIM:(h + 1) * HEAD_DIM] = y.astype(o_ref.dtype)
    carry_ref[...] = x_ref[0, ts - 8:ts, :].astype(F32)


def _gdn_prep(big, conv_w, width):
    B, S, _ = big.shape
    ts = min(S, 512)
    out = jax.ShapeDtypeStruct((B, S, width), BF16)
    ospec = pl.BlockSpec((1, ts, width), lambda b, i: (b, i, 0))
    return pl.pallas_call(
        _gdn_prep_kernel,
        grid=(B, S // ts),
        in_specs=[pl.BlockSpec((1, ts, 3 * width), lambda b, i: (b, i, 0)),
                  pl.BlockSpec((GDN_CONV, 3 * width), lambda b, i: (0, 0))],
        out_specs=[ospec, ospec, ospec],
        out_shape=[out, out, out],
        scratch_shapes=[pltpu.VMEM((8, 3 * width), F32)],
        compiler_params=_cparams(2),
        name="gdn_prep",
    )(big, conv_w)


def _softplus(x):
    return jnp.maximum(x, 0.0) + jnp.log(1.0 + jnp.exp(-jnp.abs(x)))


def _gdn_kernel(q_ref, k_ref, v_ref, z_ref, sm_ref, smt_ref, alog_ref, alogt_ref, dtb_ref, dtbt_ref,
                nw_ref, o_ref, state_ref, u_s, wq_s, attn_s, kd_s, dec_s):
    ts = q_ref.shape[1]
    nh = q_ref.shape[2] // HEAD_DIM
    C = GDN_CHUNK
    R = 2 * C
    npair = nh // 2
    nchunks = ts // C

    @pl.when(pl.program_id(1) == 0)
    def _():
        state_ref[...] = jnp.zeros_like(state_ref)

    ci = lax.broadcasted_iota(jnp.int32, (C, C), 0)
    cj = lax.broadcasted_iota(jnp.int32, (C, C), 1)
    tril = jnp.where(ci >= cj, 1.0, 0.0).astype(BF16)
    triu = jnp.where(cj >= ci, 1.0, 0.0).astype(BF16)
    ii = lax.broadcasted_iota(jnp.int32, (R, R), 0)
    jj = lax.broadcasted_iota(jnp.int32, (R, R), 1)
    same_head = (ii // C) == (jj // C)
    incl = same_head & (ii >= jj)
    strict = same_head & (ii > jj)
    eye = jnp.where(ii == jj, 1.0, 0.0)
    nw = nw_ref[...]

    def pair_rows(ref, rows, p):
        return jnp.concatenate([ref[0, rows, (2 * p) * HEAD_DIM:(2 * p + 1) * HEAD_DIM],
                                ref[0, rows, (2 * p + 1) * HEAD_DIM:(2 * p + 2) * HEAD_DIM]], axis=0)

    def pair_col(x, p):
        return jnp.concatenate([x[:, 2 * p:2 * p + 1], x[:, 2 * p + 1:2 * p + 2]], axis=0)

    def phase1(j, carry):
        cs = [j * GDN_PHASE1_CHUNKS + d for d in range(GDN_PHASE1_CHUNKS)]
        rows_c, beta_c, gc_c, gcr_c, glb_c = [], [], [], [], []
        for c in cs:
            rows = slice(c * C, (c + 1) * C)
            sm = sm_ref[0, rows, :]
            g = -jnp.exp(alog_ref[...]) * _softplus(sm[:, nh:2 * nh] + dtb_ref[...])
            gt = -jnp.exp(alogt_ref[...]) * _softplus(smt_ref[0, c] + dtbt_ref[...])
            gc = _dot_sel_left(tril, g)
            g_last = gc[C - 1:C, :]
            dec_s[c] = jnp.exp(g_last)
            rows_c.append(rows)
            beta_c.append(_sigmoid(sm[:, 0:nh]))
            gc_c.append(gc)
            gcr_c.append(_dot_sel_right(gt, triu))
            glb_c.append(jnp.broadcast_to(g_last, (C, nh)))
        items = [(ci, p) for ci in range(len(cs)) for p in range(npair)]
        idx = range(len(items))
        q2 = [pair_rows(q_ref, rows_c[ci], p).astype(F32) for ci, p in items]
        k2 = [pair_rows(k_ref, rows_c[ci], p).astype(F32) for ci, p in items]
        b2 = [pair_col(beta_c[ci], p) for ci, p in items]
        g_col = [pair_col(gc_c[ci], p) for ci, p in items]
        kb = [k2[i] * b2[i] for i in idx]
        kq = [_dot_nt(jnp.concatenate([kb[i], q2[i]], axis=0).astype(BF16), k2[i].astype(BF16))
              for i in idx]
        decay = []
        for i, (ci, p) in enumerate(items):
            gcr = gcr_c[ci]
            g_row = jnp.concatenate([gcr[2 * p:2 * p + 1, :], gcr[2 * p + 1:2 * p + 2, :]], axis=1)
            decay.append(jnp.where(incl, jnp.exp(jnp.where(incl, g_col[i] - g_row, 0.0)), 0.0))
        a = [jnp.where(strict, kq[i][:R] * decay[i], 0.0) for i in idx]
        for i, (ci, p) in enumerate(items):
            attn_s[cs[ci], p] = (kq[i][R:] * decay[i]).astype(BF16)
        x = [eye - a[i] for i in idx]
        pw = a
        n = 2
        while n < C:
            pw16 = [pw[i].astype(BF16) for i in idx]
            pw = [_dot(pw16[i], pw16[i]) for i in idx]
            x = [x[i] + _dot(x[i].astype(BF16), pw[i].astype(BF16)) for i in idx]
            n *= 2
        eg = [jnp.exp(g_col[i]) for i in idx]
        sol = []
        for i, (ci, p) in enumerate(items):
            v2 = pair_rows(v_ref, rows_c[ci], p).astype(F32)
            r = jnp.concatenate([v2 * b2[i], kb[i] * eg[i]], axis=1)
            sol.append(_dot(x[i].astype(BF16), r.astype(BF16)))
        for i, (ci, p) in enumerate(items):
            c = cs[ci]
            u_s[c, p] = sol[i][:, :HEAD_DIM]
            w = sol[i][:, HEAD_DIM:]
            qg = q2[i] * eg[i]
            for e in range(2):
                wq_s[c, 2 * p + e] = jnp.concatenate([w[e * C:(e + 1) * C], qg[e * C:(e + 1) * C]],
                                                     axis=0).astype(BF16)
            kd_s[c, p] = (k2[i] * jnp.exp(pair_col(glb_c[ci], p) - g_col[i])).astype(BF16)
        return carry

    def phase2(c):
        rows = slice(c * C, (c + 1) * C)
        dec = dec_s[c]
        res = [_dot(wq_s[c, h], state_ref[h].astype(BF16)) for h in range(nh)]
        for p in range(npair):
            ws = jnp.concatenate([res[2 * p][:C], res[2 * p + 1][:C]], axis=0)
            qs = jnp.concatenate([res[2 * p][C:], res[2 * p + 1][C:]], axis=0)
            v_new = u_s[c, p] - ws
            v16 = v_new.astype(BF16)
            o2 = qs + _dot(attn_s[c, p], v16)
            kd = kd_s[c, p]
            for e in range(2):
                h = 2 * p + e
                part = slice(e * C, (e + 1) * C)
                state_ref[h] = state_ref[h] * dec[:, h:h + 1] + _dot_tn(kd[part], v16[part])
                o = o2[part]
                o = o * lax.rsqrt(jnp.mean(o * o, axis=-1, keepdims=True) + 1e-6) * nw
                cols = slice(h * HEAD_DIM, (h + 1) * HEAD_DIM)
                z = z_ref[0, rows, cols].astype(F32)
                o_ref[0, rows, cols] = (o * _silu(z)).astype(o_ref.dtype)

    def phase2_group(j):
        for d in range(GDN_PHASE1_CHUNKS):
            phase2(j * GDN_PHASE1_CHUNKS + d)

    assert nchunks % GDN_PHASE1_CHUNKS == 0
    ngroups = nchunks // GDN_PHASE1_CHUNKS
    phase1(0, 0)
    for j in range(1, ngroups):
        phase2_group(j - 1)
        phase1(j, 0)
    phase2_group(ngroups - 1)


def _gdn(q, k, v, big, z_blk, small, small_t, a_log, dt_bias, norm_w):
    B, S, W = q.shape
    nh = W // HEAD_DIM
    assert nh % 2 == 0 and 2 * GDN_CHUNK == HEAD_DIM
    ts = min(S, 512)
    nc = ts // GDN_CHUNK
    spec = pl.BlockSpec((1, ts, W), lambda b, i: (b, i, 0))
    full = lambda shape: pl.BlockSpec(shape, lambda b, i: (0,) * len(shape))
    return pl.pallas_call(
        _gdn_kernel,
        grid=(B, S // ts),
        in_specs=[spec, spec, spec,
                  pl.BlockSpec((1, ts, W), lambda b, i: (b, i, z_blk)),
                  pl.BlockSpec((1, ts, small.shape[2]), lambda b, i: (b, i, 0)),
                  pl.BlockSpec((1, ts // GDN_CHUNK, nh, GDN_CHUNK), lambda b, i: (b, i, 0, 0)),
                  full((1, nh)), full((nh, 1)), full((1, nh)), full((nh, 1)),
                  full((1, HEAD_DIM))],
        out_specs=spec,
        out_shape=jax.ShapeDtypeStruct((B, S, W), BF16),
        scratch_shapes=[pltpu.VMEM((nh, HEAD_DIM, HEAD_DIM), F32),
                        pltpu.VMEM((nc, nh // 2, 2 * GDN_CHUNK, HEAD_DIM), F32),
                        pltpu.VMEM((nc, nh, 2 * GDN_CHUNK, HEAD_DIM), BF16),
                        pltpu.VMEM((nc, nh // 2, 2 * GDN_CHUNK, 2 * GDN_CHUNK), BF16),
                        pltpu.VMEM((nc, nh // 2, 2 * GDN_CHUNK, HEAD_DIM), BF16),
                        pltpu.VMEM((nc, 1, nh), F32)],
        compiler_params=_cparams(2),
        name="gdn",
    )(q, k, v, big, small, small_t, a_log.reshape(1, nh), a_log.reshape(nh, 1),
      dt_bias.reshape(1, nh), dt_bias.reshape(nh, 1), norm_w.reshape(1, HEAD_DIM))


def _rope_table_kernel(pos_ref, invf_ref, sign_ref, cos_ref, sin_ref):
    ang = pos_ref[0].astype(F32) * invf_ref[...]
    cos_ref[0] = jnp.cos(ang)
    sin_ref[0] = jnp.sin(ang) * sign_ref[...]


def _rope_tables(positions):
    B, S = positions.shape
    half = HEAD_DIM // 2
    inv = ROPE_THETA ** (-jnp.arange(half, dtype=F32) / half)
    invf = jnp.concatenate([inv, inv]).reshape(1, HEAD_DIM)
    sign = jnp.concatenate([-jnp.ones((half,), F32), jnp.ones((half,), F32)]).reshape(1, HEAD_DIM)
    ts = min(S, 1024)
    out = jax.ShapeDtypeStruct((B, S, HEAD_DIM), F32)
    ospec = pl.BlockSpec((1, ts, HEAD_DIM), lambda b, i: (b, i, 0))
    return pl.pallas_call(
        _rope_table_kernel,
        grid=(B, S // ts),
        in_specs=[pl.BlockSpec((1, ts, 1), lambda b, i: (b, i, 0)),
                  pl.BlockSpec((1, HEAD_DIM), lambda b, i: (0, 0)),
                  pl.BlockSpec((1, HEAD_DIM), lambda b, i: (0, 0))],
        out_specs=[ospec, ospec],
        out_shape=[out, out],
        compiler_params=_cparams(2),
        name="rope_table",
    )(positions.reshape(B, S, 1), invf, sign)


def _rope_rows(x, cosf, sinf):
    return x * cosf + pltpu.roll(x, HEAD_DIM // 2, 1) * sinf


def _rope_apply_kernel(q_ref, ks_ref, kw_ref, cos_ref, sin_ref, qo_ref, kso_ref, kwo_ref):
    cosf = cos_ref[0]
    sinf = sin_ref[0]
    for src, dst, mult in ((q_ref, qo_ref, Q_SCALE_LOG2E), (ks_ref, kso_ref, None), (kw_ref, kwo_ref, None)):
        for h in range(src.shape[2] // HEAD_DIM):
            cols = slice(h * HEAD_DIM, (h + 1) * HEAD_DIM)
            r = _rope_rows(src[0, :, cols].astype(F32), cosf, sinf)
            if mult is not None:
                r = r * mult
            dst[0, :, cols] = r.astype(dst.dtype)


def _rope_apply(big, q_blk, ksel_blk, kwin_blk, wq, wkv, cosf, sinf):
    B, S, _ = big.shape
    ts = min(S, 512)
    tab = pl.BlockSpec((1, ts, HEAD_DIM), lambda b, i: (b, i, 0))
    return pl.pallas_call(
        _rope_apply_kernel,
        grid=(B, S // ts),
        in_specs=[pl.BlockSpec((1, ts, wq), lambda b, i: (b, i, q_blk)),
                  pl.BlockSpec((1, ts, wkv), lambda b, i: (b, i, ksel_blk)),
                  pl.BlockSpec((1, ts, wkv), lambda b, i: (b, i, kwin_blk)),
                  tab, tab],
        out_specs=[pl.BlockSpec((1, ts, wq), lambda b, i: (b, i, 0)),
                   pl.BlockSpec((1, ts, wkv), lambda b, i: (b, i, 0)),
                   pl.BlockSpec((1, ts, wkv), lambda b, i: (b, i, 0))],
        out_shape=[jax.ShapeDtypeStruct((B, S, wq), BF16),
                   jax.ShapeDtypeStruct((B, S, wkv), BF16),
                   jax.ShapeDtypeStruct((B, S, wkv), BF16)],
        compiler_params=_cparams(2),
        name="rope_apply",
    )(big, big, big, cosf, sinf)


def _compress_kernel(hb_ref, pos_ref, w1_ref, w2_ref, cos_ref, sin_ref, o_ref):
    hb = hb_ref[0, 0]
    w1 = w1_ref[0]
    half = hb.shape[1]
    p0 = _dot(hb, w1[:half])
    p1 = _dot(hb, w1[half:])
    nc = p0.shape[0]
    pos8 = jnp.broadcast_to(pos_ref[0], (8, 2 * half)).astype(BF16)
    pb = _dot(pos8, w1)[0:1]
    pre = p0 + pltpu.roll(p1, nc - 1, 0) + pb
    out = _dot(_silu(pre).astype(BF16), w2_ref[0])
    roped = _rope_rows(out, cos_ref[0], sin_ref[0])
    is_key = pl.program_id(1) < NSA_GROUPS
    o_ref[0, 0] = jnp.where(is_key, roped, out).astype(o_ref.dtype)


def _compress(hb, pos_flat, w1, w2, cosc, sinc):
    B, J, NC, HW = hb.shape
    G = NSA_GROUPS
    return pl.pallas_call(
        _compress_kernel,
        grid=(B, J),
        in_specs=[pl.BlockSpec((1, 1, NC, HW), lambda b, j: (b, j, 0, 0)),
                  pl.BlockSpec((1, 1, 2 * HW), lambda b, j: (j // G, 0, 0)),
                  pl.BlockSpec((1, 2 * HW, HEAD_DIM), lambda b, j: (j // G, 0, 0)),
                  pl.BlockSpec((1, HEAD_DIM, HEAD_DIM), lambda b, j: (j // G, 0, 0)),
                  pl.BlockSpec((1, NC, HEAD_DIM), lambda b, j: (b, 0, 0)),
                  pl.BlockSpec((1, NC, HEAD_DIM), lambda b, j: (b, 0, 0))],
        out_specs=pl.BlockSpec((1, 1, NC, HEAD_DIM), lambda b, j: (b, j, 0, 0)),
        out_shape=jax.ShapeDtypeStruct((B, J, NC, HEAD_DIM), BF16),
        compiler_params=_cparams(2),
        name="nsa_compress",
    )(hb, pos_flat, w1, w2, cosc, sinc)


def _gate_col(g_ref, nh_gdn, group, hpg, h, branch):
    col = None
    for gi in range(NSA_GROUPS):
        c = 2 * nh_gdn + (gi * hpg + h) * 3 + branch
        cand = g_ref[0, :, c:c + 1]
        col = cand if col is None else jnp.where(group == gi, cand, col)
    return _sigmoid(col)


def _cmp_attn_kernel(q_ref, kc_ref, vc_ref, sm_ref, o_ref, sel_ref, *, nh_gdn, hpg):
    tq = q_ref.shape[1]
    nc = kc_ref.shape[2]
    nb = sel_ref.shape[3]
    g = pl.program_id(1)
    t0 = pl.program_id(2) * tq
    t_nc = t0 + lax.broadcasted_iota(jnp.int32, (tq, nc), 0)
    n_nc = lax.broadcasted_iota(jnp.int32, (tq, nc), 1)
    valid = (n_nc * NSA_CMP_STRIDE + (NSA_CMP_LEN - 1)) <= t_nc
    t_col = t0 + lax.broadcasted_iota(jnp.int32, (tq, 1), 0)
    has_valid = jnp.where(t_col >= NSA_CMP_LEN - 1, 1.0, 0.0)
    kc = kc_ref[0, 0]
    vc = vc_ref[0, 0]
    psum = jnp.zeros((tq, nc), F32)
    for h in range(hpg):
        cols = slice(h * HEAD_DIM, (h + 1) * HEAD_DIM)
        s = jnp.where(valid, _dot_nt(q_ref[0, :, cols], kc), NEG_INF)
        e = jnp.exp2(s - jnp.max(s, axis=-1, keepdims=True))
        p = e * (has_valid / jnp.sum(e, axis=-1, keepdims=True))
        gate = _gate_col(sm_ref, nh_gdn, g, hpg, h, 0)
        o_ref[0, :, cols] = (_dot(p.astype(BF16), vc) * gate).astype(o_ref.dtype)
        psum = psum + p
    sj = lax.broadcasted_iota(jnp.int32, (nb, nc), 0) * NSA_SEL_LEN
    cn = lax.broadcasted_iota(jnp.int32, (nb, nc), 1) * NSA_CMP_STRIDE
    overlap_t = jnp.where((cn <= sj + (NSA_SEL_LEN - 1)) & (cn + (NSA_CMP_LEN - 1) >= sj), 1.0, 0.0).astype(BF16)
    p0, p1, p2 = _split3(psum)
    imp = _dot_nt(overlap_t, p0) + _dot_nt(overlap_t, p1) + _dot_nt(overlap_t, p2)
    t_nb = t0 + lax.broadcasted_iota(jnp.int32, (nb, tq), 1)
    blk = lax.broadcasted_iota(jnp.int32, (nb, tq), 0)
    cur = t_nb // NSA_SEL_LEN
    forced = (blk == 0) | (blk == cur) | (blk == cur - 1)
    score = jnp.where(forced, SEL_FORCE, jnp.where(blk * NSA_SEL_LEN <= t_nb, imp, -1.0))
    sel = jnp.zeros((nb, tq), F32)
    blk_f = blk.astype(F32)
    for _ in range(min(NSA_SEL_TOP, nb)):
        m = jnp.max(score, axis=0, keepdims=True)
        first = jnp.min(jnp.where(score == m, blk_f, float(nb)), axis=0, keepdims=True)
        pick = blk_f == first
        sel = jnp.where(pick, 1.0, sel)
        score = jnp.where(pick, -jnp.inf, score)
    sel_ref[0, 0] = sel.T.astype(sel_ref.dtype)


def _cmp_attn(q_r, kvc, small, nh_gdn, n_sel):
    B, S, WQ = q_r.shape
    G = NSA_GROUPS
    hpg = WQ // HEAD_DIM // G
    NC = kvc.shape[2]
    tq = min(S, 512)
    return pl.pallas_call(
        functools.partial(_cmp_attn_kernel, nh_gdn=nh_gdn, hpg=hpg),
        grid=(B, G, S // tq),
        in_specs=[pl.BlockSpec((1, tq, hpg * HEAD_DIM), lambda b, g, i: (b, i, g)),
                  pl.BlockSpec((1, 1, NC, HEAD_DIM), lambda b, g, i: (b, g, 0, 0)),
                  pl.BlockSpec((1, 1, NC, HEAD_DIM), lambda b, g, i: (b, G + g, 0, 0)),
                  pl.BlockSpec((1, tq, small.shape[2]), lambda b, g, i: (b, i, 0))],
        out_specs=[pl.BlockSpec((1, tq, hpg * HEAD_DIM), lambda b, g, i: (b, i, g)),
                   pl.BlockSpec((1, 1, tq, n_sel), lambda b, g, i: (b, g, i, 0))],
        out_shape=[jax.ShapeDtypeStruct((B, S, WQ), BF16),
                   jax.ShapeDtypeStruct((B, G, S, n_sel), BF16)],
        compiler_params=_cparams(3),
        name="nsa_cmp_attn",
    )(q_r, kvc, kvc, small)


def _sel_attn_kernel(q_ref, k_ref, ext_ref, v_ref, sel_ref, sm_ref, o_ref, qx_ref, m_ref, acc_ref, p_ref, a_ref,
                     *, nh_gdn, hpg, tk):
    tq = q_ref.shape[1]
    nb = sel_ref.shape[3]
    g = pl.program_id(1)
    t0 = pl.program_id(2) * tq
    assert tk % tq == 0
    unsel = (sel_ref[0, 0].astype(F32) - 1.0).astype(BF16)
    for h in range(hpg):
        qx_ref[h * tq:(h + 1) * tq, 0:HEAD_DIM] = q_ref[0, :, h * HEAD_DIM:(h + 1) * HEAD_DIM]
        qx_ref[h * tq:(h + 1) * tq, HEAD_DIM:HEAD_DIM + nb] = unsel
    m_ref[...] = jnp.full_like(m_ref, NEG_INF)
    acc_ref[...] = jnp.zeros_like(acc_ref)
    p_ref[...] = jnp.zeros_like(p_ref)
    a_ref[...] = jnp.ones_like(a_ref)
    ones = jnp.ones((tk, HEAD_DIM), BF16)
    pr = hpg * tq // N_ROW_PARTS
    parts = [slice(part * pr, (part + 1) * pr) for part in range(N_ROW_PARTS)]

    def apply_values(kt):
        k0 = pl.multiple_of(kt * tk, tk)
        vx = jnp.concatenate([v_ref[0, pl.ds(k0, tk), :], ones], axis=1)
        for rows in parts:
            acc_ref[rows, :] = _tile_lanes(a_ref[rows, :], 2) * acc_ref[rows, :] + _dot(p_ref[rows, :], vx)

    def scores(kt, causal):
        k0 = pl.multiple_of(kt * tk, tk)
        kx = jnp.concatenate([k_ref[0, pl.ds(k0, tk), :], ext_ref[pl.ds(k0, tk), :]], axis=1)
        if causal:
            t_row = t0 + lax.broadcasted_iota(jnp.int32, (tq, tk), 0)
            key = k0 + lax.broadcasted_iota(jnp.int32, (tq, tk), 1)
            keep = jnp.concatenate([key <= t_row] * (hpg // N_ROW_PARTS), axis=0)
        for rows in parts:
            s = _dot_nt(qx_ref[rows, :], kx)
            if causal:
                s = jnp.where(keep, s, NEG_INF)
            m_old = m_ref[rows, :]
            m_new = jnp.maximum(m_old, jnp.max(s, axis=-1, keepdims=True))
            a_ref[rows, :] = jnp.exp2(m_old - m_new)
            p_ref[rows, :] = jnp.exp2(s - _tile_lanes(m_new, tk // HEAD_DIM)).astype(BF16)
            m_ref[rows, :] = m_new

    kt_diag = t0 // tk

    def body(kt, carry):
        apply_values(jnp.maximum(kt - 1, 0))
        scores(kt, False)
        return carry

    lax.fori_loop(0, kt_diag, body, 0)
    apply_values(jnp.maximum(kt_diag - 1, 0))
    scores(kt_diag, True)
    apply_values(kt_diag)
    for h in range(hpg):
        rows = slice(h * tq, (h + 1) * tq)
        gate = _gate_col(sm_ref, nh_gdn, g, hpg, h, 1)
        out = acc_ref[rows, 0:HEAD_DIM] / acc_ref[rows, HEAD_DIM:2 * HEAD_DIM]
        o_ref[0, :, h * HEAD_DIM:(h + 1) * HEAD_DIM] = (out * gate).astype(o_ref.dtype)


def _sel_attn(q_r, ksel_r, big, vsel_blk, sel, small, nh_gdn):
    B, S, WQ = q_r.shape
    G = NSA_GROUPS
    hpg = WQ // HEAD_DIM // G
    nbp = sel.shape[3]
    tq = min(S, 512)
    tk = min(S, 512)
    rows = hpg * tq
    ext = jnp.where(jnp.arange(S)[:, None] // NSA_SEL_LEN == jnp.arange(nbp)[None, :], MASK_BIG, 0.0).astype(BF16)
    return pl.pallas_call(
        functools.partial(_sel_attn_kernel, nh_gdn=nh_gdn, hpg=hpg, tk=tk),
        grid=(B, G, S // tq),
        in_specs=[pl.BlockSpec((1, tq, hpg * HEAD_DIM), lambda b, g, i: (b, i, g)),
                  pl.BlockSpec((1, S, HEAD_DIM), lambda b, g, i: (b, 0, g)),
                  pl.BlockSpec((S, nbp), lambda b, g, i: (0, 0)),
                  pl.BlockSpec((1, S, HEAD_DIM), lambda b, g, i: (b, 0, vsel_blk + g)),
                  pl.BlockSpec((1, 1, tq, nbp), lambda b, g, i: (b, g, i, 0)),
                  pl.BlockSpec((1, tq, small.shape[2]), lambda b, g, i: (b, i, 0))],
        out_specs=pl.BlockSpec((1, tq, hpg * HEAD_DIM), lambda b, g, i: (b, i, g)),
        out_shape=jax.ShapeDtypeStruct((B, S, WQ), BF16),
        scratch_shapes=[pltpu.VMEM((rows, HEAD_DIM + nbp), BF16),
                        pltpu.VMEM((rows, HEAD_DIM), F32),
                        pltpu.VMEM((rows, 2 * HEAD_DIM), F32),
                        pltpu.VMEM((rows, tk), BF16),
                        pltpu.VMEM((rows, HEAD_DIM), F32)],
        compiler_params=_cparams(3),
        name="nsa_sel_attn",
    )(q_r, ksel_r, ext, big, sel, small)


def _win_attn_kernel(q_ref, k_ref, v_ref, sm_ref, o_ref, *, nh_gdn, hpg):
    tq = q_ref.shape[1]
    ts = min(tq, WIN_SUB_TILE)
    span = ts + NSA_WINDOW
    g = pl.program_id(1)
    for sub in range(tq // ts):
        qrows = slice(sub * ts, (sub + 1) * ts)
        t0 = pl.program_id(2) * tq + sub * ts
        k0 = pl.multiple_of(jnp.maximum(t0 - NSA_WINDOW, 0), ts)
        k = k_ref[0, pl.ds(k0, span), :]
        v = v_ref[0, pl.ds(k0, span), :]
        t_row = t0 + lax.broadcasted_iota(jnp.int32, (ts, span), 0)
        spos = k0 + lax.broadcasted_iota(jnp.int32, (ts, span), 1)
        mask = (spos <= t_row) & (spos > t_row - NSA_WINDOW)
        vx = jnp.concatenate([v, jnp.ones((span, HEAD_DIM), BF16)], axis=1)
        q4 = jnp.concatenate([q_ref[0, qrows, h * HEAD_DIM:(h + 1) * HEAD_DIM] for h in range(hpg)], axis=0)
        s = jnp.where(jnp.concatenate([mask] * hpg, axis=0), _dot_nt(q4, k), NEG_INF)
        e = jnp.exp2(s - jnp.max(s, axis=-1, keepdims=True))
        acc = _dot(e.astype(BF16), vx)
        out = acc[:, :HEAD_DIM] / acc[:, HEAD_DIM:]
        for h in range(hpg):
            gate = _gate_col(sm_ref, nh_gdn, g, hpg, h, 2)[qrows]
            o_ref[0, qrows, h * HEAD_DIM:(h + 1) * HEAD_DIM] = (out[h * ts:(h + 1) * ts] * gate).astype(o_ref.dtype)


def _win_attn(q_r, kwin_r, big, vwin_blk, small, nh_gdn):
    B, S, WQ = q_r.shape
    G = NSA_GROUPS
    hpg = WQ // HEAD_DIM // G
    tq = min(S, 512)
    assert S >= tq + NSA_WINDOW
    return pl.pallas_call(
        functools.partial(_win_attn_kernel, nh_gdn=nh_gdn, hpg=hpg),
        grid=(B, G, S // tq),
        in_specs=[pl.BlockSpec((1, tq, hpg * HEAD_DIM), lambda b, g, i: (b, i, g)),
                  pl.BlockSpec((1, S, HEAD_DIM), lambda b, g, i: (b, 0, g)),
                  pl.BlockSpec((1, S, HEAD_DIM), lambda b, g, i: (b, 0, vwin_blk + g)),
                  pl.BlockSpec((1, tq, small.shape[2]), lambda b, g, i: (b, i, 0))],
        out_specs=pl.BlockSpec((1, tq, hpg * HEAD_DIM), lambda b, g, i: (b, i, g)),
        out_shape=jax.ShapeDtypeStruct((B, S, WQ), BF16),
        compiler_params=_cparams(3),
        name="nsa_win_attn",
    )(q_r, kwin_r, big, small)


def _post_norm(x, y, gt, g, b, alpha):
    r = alpha * x + (1.0 + gt) * y
    return _normalize_rows(r) * g + b


def _mixer_out_kernel(oa_ref, oc_ref, os_ref, ow_ref, ma_ref, mb_ref, x_ref, gt_ref, lg_ref, lb_ref,
                      wg_ref, wn_ref, wo_ref, o_ref, *, alpha):
    y_a = _dot(oa_ref[0], wg_ref[...])
    o_b = oc_ref[0].astype(F32) + os_ref[0].astype(F32) + ow_ref[0].astype(F32)
    y_b = _dot(o_b.astype(BF16), wn_ref[...])
    mixed = _sigmoid(ma_ref[0].astype(F32)) * y_a + _sigmoid(mb_ref[0].astype(F32)) * y_b
    y = _dot(mixed.astype(BF16), wo_ref[...])
    o_ref[0] = _post_norm(x_ref[0], y, gt_ref[0], lg_ref[...], lb_ref[...], alpha)


def _mixer_out(o_a, o_c, o_s, o_w, big, ma_blk, mb_blk, x, ada_l, gt_col, ln_g, ln_b, wg, wn, wo, alpha):
    B, S, D = x.shape
    tm = min(S, 512)
    row = lambda blk: pl.BlockSpec((1, tm, D), lambda b, i: (b, i, blk))
    full = lambda shape: pl.BlockSpec(shape, lambda b, i: (0,) * len(shape))
    return pl.pallas_call(
        functools.partial(_mixer_out_kernel, alpha=alpha),
        grid=(B, S // tm),
        in_specs=[row(0), row(0), row(0), row(0), row(ma_blk), row(mb_blk), row(0),
                  pl.BlockSpec((1, 1, D), lambda b, i: (b, 0, gt_col)),
                  full((1, D)), full((1, D)), full((D, D)), full((D, D)), full((D, D))],
        out_specs=row(0),
        out_shape=jax.ShapeDtypeStruct((B, S, D), F32),
        compiler_params=_cparams(2),
        name="mixer_out",
    )(o_a, o_c, o_s, o_w, big, big, x, ada_l, ln_g.reshape(1, D), ln_b.reshape(1, D), wg, wn, wo)


def _moe_router_kernel(x_ref, sh_ref, sc_ref, rwh_ref, rwl_ref, rb_ref, wsg_ref, wsu_ref, wsd_ref,
                       h_ref, idx_ref, wt_ref, shared_ref, rank_ref, cnt_ref):
    h = _normalize_rows(x_ref[0]) * (1.0 + sc_ref[0]) + sh_ref[0]
    h16 = h.astype(BF16)
    h_ref[0] = h16
    h_lo = (h - h16.astype(F32)).astype(BF16)
    rw_hi = rwh_ref[...]
    logits = _dot_nt(rw_hi, h16) + _dot_nt(rwl_ref[...], h16) + _dot_nt(rw_hi, h_lo)
    scores = _sigmoid(logits)
    biased = scores + rb_ref[...]
    E, tm = scores.shape
    NG = N_EXPERT_GROUPS
    per = E // NG
    b3 = biased.reshape(NG, per, tm)
    in_grp = lax.broadcasted_iota(jnp.int32, (NG, per, tm), 1).astype(F32)
    m1 = jnp.max(b3, axis=1, keepdims=True)
    first = jnp.min(jnp.where(b3 == m1, in_grp, float(per)), axis=1, keepdims=True)
    m2 = jnp.max(jnp.where(in_grp == first, -jnp.inf, b3), axis=1, keepdims=True)
    gscore = (m1 + m2).reshape(NG, tm)
    g_id = lax.broadcasted_iota(jnp.int32, (NG, tm), 0).astype(F32)
    allowed = jnp.zeros((NG, tm), F32)
    for _ in range(TOPK_GROUPS):
        m = jnp.max(gscore, axis=0, keepdims=True)
        first_g = jnp.min(jnp.where(gscore == m, g_id, float(NG)), axis=0, keepdims=True)
        pick = g_id == first_g
        allowed = jnp.where(pick, 1.0, allowed)
        gscore = jnp.where(pick, -jnp.inf, gscore)
    cand = jnp.where(allowed.reshape(NG, 1, tm) > 0.5, b3, NEG_INF).reshape(E, tm)
    e_id = lax.broadcasted_iota(jnp.int32, (E, tm), 0).astype(F32)
    idx_rows, wt_rows, picks = [], [], []
    wsum = jnp.zeros((1, tm), F32)
    chosen = jnp.zeros((E, tm), F32)
    for kk in range(TOP_K):
        m = jnp.max(cand, axis=0, keepdims=True)
        first_e = jnp.min(jnp.where(cand == m, e_id, float(E)), axis=0, keepdims=True)
        pick = e_id == first_e
        w = jnp.sum(jnp.where(pick, scores, 0.0), axis=0, keepdims=True)
        cand = jnp.where(pick, -jnp.inf, cand)
        chosen = jnp.where(pick, 1.0, chosen)
        idx_rows.append(first_e)
        wt_rows.append(w)
        picks.append(pick)
        wsum = wsum + w
    idx_ref[0] = jnp.concatenate(idx_rows, axis=0).astype(jnp.int32)
    wt_ref[0] = jnp.concatenate(wt_rows, axis=0) / wsum * ROUTED_SCALE
    @pl.when((pl.program_id(0) == 0) & (pl.program_id(1) == 0))
    def _():
        cnt_ref[...] = jnp.zeros_like(cnt_ref)

    si = lax.broadcasted_iota(jnp.int32, (tm, tm), 0)
    ti = lax.broadcasted_iota(jnp.int32, (tm, tm), 1)
    before = jnp.where(si < ti, 1.0, 0.0).astype(BF16)
    prior = _dot(chosen.astype(BF16), before) + cnt_ref[...]
    rank_ref[0] = jnp.concatenate([jnp.sum(jnp.where(pk, prior, 0.0), axis=0, keepdims=True) for pk in picks],
                                  axis=0).astype(jnp.int32)
    cnt_ref[...] = cnt_ref[...] + jnp.sum(chosen, axis=1, keepdims=True)
    act = _silu(_dot(h16, wsg_ref[...])) * _dot(h16, wsu_ref[...])
    shared_ref[0] = _dot(act.astype(BF16), wsd_ref[...])


def _moe_router(x, ada_l, sh_col, sc_col, rwt_hi, rwt_lo, router_b, wsg, wsu, wsd):
    B, S, D = x.shape
    E = rwt_hi.shape[0]
    FF = wsg.shape[1]
    tm = min(S, 512)
    row = lambda w: pl.BlockSpec((1, tm, w), lambda b, i: (b, i, 0))
    krow = pl.BlockSpec((1, TOP_K, tm), lambda b, i: (b, 0, i))
    full = lambda shape: pl.BlockSpec(shape, lambda b, i: (0,) * len(shape))
    return pl.pallas_call(
        _moe_router_kernel,
        grid=(B, S // tm),
        in_specs=[row(D),
                  pl.BlockSpec((1, 1, D), lambda b, i: (b, 0, sh_col)),
                  pl.BlockSpec((1, 1, D), lambda b, i: (b, 0, sc_col)),
                  full((E, D)), full((E, D)), full((E, 1)), full((D, FF)), full((D, FF)), full((FF, D))],
        out_specs=[row(D), krow, krow, row(D), krow, full((E, 1))],
        out_shape=[jax.ShapeDtypeStruct((B, S, D), BF16),
                   jax.ShapeDtypeStruct((B, TOP_K, S), jnp.int32),
                   jax.ShapeDtypeStruct((B, TOP_K, S), F32),
                   jax.ShapeDtypeStruct((B, S, D), F32),
                   jax.ShapeDtypeStruct((B, TOP_K, S), jnp.int32),
                   jax.ShapeDtypeStruct((E, 1), F32)],
        compiler_params=_cparams(2),
        name="moe_router",
    )(x, ada_l, ada_l, rwt_hi, rwt_lo, router_b.reshape(E, 1), wsg, wsu, wsd)


def _moe_experts_kernel(be_ref, nu_ref, xs_ref, wg_ref, wu_ref, wd_ref, *rest, first_block):
    o_ref = rest[-1]

    @pl.when(pl.program_id(0) + first_block < nu_ref[0])
    def _():
        xb = xs_ref[...]
        act = _silu(_dot(xb, wg_ref[0, 0].astype(BF16))) * _dot(xb, wu_ref[0, 0].astype(BF16))
        o_ref[...] = _dot(act.astype(BF16), wd_ref[0, 0].astype(BF16)).astype(o_ref.dtype)


def _moe_experts(block_expert, n_used, xs, layer, wg, wu, wd, first_block, total_rows, prev=None):
    rows, D = xs.shape
    BM = MOE_ROW_BLOCK
    FF = wg.shape[3]
    fb = first_block
    in_specs = [pl.BlockSpec((BM, D), lambda i, be, nu: (i, 0)),
                pl.BlockSpec((1, 1, D, FF), lambda i, be, nu: (layer, be[i + fb], 0, 0)),
                pl.BlockSpec((1, 1, D, FF), lambda i, be, nu: (layer, be[i + fb], 0, 0)),
                pl.BlockSpec((1, 1, FF, D), lambda i, be, nu: (layer, be[i + fb], 0, 0))]
    args = [block_expert, n_used, xs, wg, wu, wd]
    aliases = {}
    if prev is not None:
        in_specs.append(pl.BlockSpec(memory_space=pl.ANY))
        args.append(prev)
        aliases = {len(args) - 1: 0}
    grid_spec = pltpu.PrefetchScalarGridSpec(
        num_scalar_prefetch=2,
        grid=(rows // BM,),
        in_specs=in_specs,
        out_specs=pl.BlockSpec((BM, D), lambda i, be, nu: (i + fb, 0)),
    )
    return pl.pallas_call(
        functools.partial(_moe_experts_kernel, first_block=fb),
        grid_spec=grid_spec,
        out_shape=jax.ShapeDtypeStruct((total_rows, D), BF16),
        input_output_aliases=aliases,
        compiler_params=_cparams(1),
        name="moe_experts",
    )(*args)


def _moe_out_kernel(x_ref, shared_ref, yg_ref, wt_ref, gt_ref, lg_ref, lb_ref, *rest, alpha):
    o_ref = rest[-1]
    y = shared_ref[0]
    wt = wt_ref[0]
    for kk in range(TOP_K):
        y = y + wt[:, kk:kk + 1] * yg_ref[kk, 0].astype(F32)
    o_ref[0] = _post_norm(x_ref[0], y, gt_ref[0], lg_ref[...], lb_ref[...], alpha)


def _moe_out(x, shared, yg, wt, ada_l, gt_col, ln_g, ln_b, alpha, first_b, prev=None):
    B, S, D = x.shape
    nb = yg.shape[1]
    fb = first_b
    tm = min(S, 256)
    row = lambda w: pl.BlockSpec((1, tm, w), lambda b, i: (b + fb, i, 0))
    full = lambda shape: pl.BlockSpec(shape, lambda b, i: (0,) * len(shape))
    in_specs = [row(D), row(D),
                pl.BlockSpec((TOP_K, 1, tm, D), lambda b, i: (0, b, i, 0)),
                row(wt.shape[2]),
                pl.BlockSpec((1, 1, D), lambda b, i: (b + fb, 0, gt_col)),
                full((1, D)), full((1, D))]
    args = [x, shared, yg, wt, ada_l, ln_g.reshape(1, D), ln_b.reshape(1, D)]
    aliases = {}
    if prev is not None:
        in_specs.append(pl.BlockSpec(memory_space=pl.ANY))
        args.append(prev)
        aliases = {len(args) - 1: 0}
    return pl.pallas_call(
        functools.partial(_moe_out_kernel, alpha=alpha),
        grid=(nb, S // tm),
        in_specs=in_specs,
        out_specs=row(D),
        out_shape=jax.ShapeDtypeStruct((B, S, D), F32),
        input_output_aliases=aliases,
        compiler_params=_cparams(2),
        name="moe_out",
    )(*args)


def _moe_dest_kernel(start_ref, e_ref, r_ref, o_ref):
    e = e_ref[...]
    acc = r_ref[...]
    for j in range(start_ref.shape[0]):
        acc = acc + jnp.where(e == j, start_ref[j], 0)
    o_ref[...] = acc


def _routing_layout(expert_idx, rank, counts):
    B, K, S = expert_idx.shape
    T = B * S
    n_experts = counts.shape[0]
    BM = MOE_ROW_BLOCK
    TK = T * K
    counts = counts.reshape(n_experts).astype(jnp.int32)
    padded = (counts + BM - 1) // BM * BM
    pad_end = jnp.cumsum(padded)
    pad_start = (pad_end - padded).astype(jnp.int32)
    ts = min(S, 2048)
    spec = pl.BlockSpec((1, K, ts), lambda b, i, tab: (b, 0, i))
    dest = pl.pallas_call(
        _moe_dest_kernel,
        grid_spec=pltpu.PrefetchScalarGridSpec(num_scalar_prefetch=1, grid=(B, S // ts),
                                               in_specs=[spec, spec], out_specs=spec),
        out_shape=jax.ShapeDtypeStruct((B, K, S), jnp.int32),
        compiler_params=_cparams(2),
        name="moe_dest",
    )(pad_start, expert_idx, rank)
    dest = jnp.swapaxes(dest, 0, 1).reshape(TK)
    n_blocks = (TK + n_experts * (BM - 1) + BM - 1) // BM
    rows = n_blocks * BM
    token = jnp.tile(jnp.arange(T, dtype=jnp.int32), K)
    row_tok = (jnp.arange(rows, dtype=jnp.int32) % T).at[dest].add(token - dest % T, mode='promise_in_bounds')
    block_start = jnp.arange(n_blocks, dtype=jnp.int32) * BM
    block_expert = jnp.minimum(jnp.sum(pad_end[None, :] <= block_start[:, None], axis=1),
                               n_experts - 1).astype(jnp.int32)
    n_used = (pad_end[-1] // BM).astype(jnp.int32).reshape(1)
    return row_tok, dest, block_expert, n_used


def kernel(x, c, positions, ada_w, ada_b, w_in, gdn_conv_w, gdn_a_log, gdn_dt_bias, gdn_norm_w, w_gdn_branch, nsa_cmp_pos_k, nsa_cmp_pos_v, nsa_cmp_k_w1, nsa_cmp_k_w2, nsa_cmp_v_w1, nsa_cmp_v_w2, w_nsa_branch, w_out, ln1_g, ln1_b, router_w, router_bias, w_sh_gate, w_sh_up, w_sh_down, w_e_gate, w_e_up, w_e_down, ln2_g, ln2_b):
    B, S, D = x.shape
    L = ada_w.shape[0]
    T = B * S
    G = NSA_GROUPS
    nh_gdn = gdn_a_log.shape[1]
    gdn_w = nh_gdn * HEAD_DIM
    nsa_w = w_nsa_branch.shape[1]
    kvw = G * HEAD_DIM
    alpha = (2.0 * L) ** 0.25
    assert gdn_w == D and nsa_w == D and S % NSA_SEL_LEN == 0

    splits = (3 * gdn_w, gdn_w, nh_gdn, nh_gdn, nsa_w, 6 * kvw, 3 * (nsa_w // HEAD_DIM), D, D)
    offs = [0]
    for s_ in splits:
        offs.append(offs[-1] + s_)
    seg = lambda i: slice(offs[i], offs[i + 1])
    w_big = jnp.concatenate([w_in[:, :, seg(0)], w_in[:, :, seg(1)], w_in[:, :, seg(4)],
                             w_in[:, :, seg(7)], w_in[:, :, seg(8)], w_in[:, :, seg(5)]], axis=-1).astype(BF16)
    n_small = splits[2] + splits[3] + splits[6]
    w_small = jnp.concatenate([w_in[:, :, seg(2)], w_in[:, :, seg(3)], w_in[:, :, seg(6)],
                               jnp.zeros((L, D, 128 - n_small), w_in.dtype)], axis=-1).astype(BF16)
    Z_BLK, Q_BLK, MA_BLK, MB_BLK = 3, 4, 5, 6
    kv0 = 7 * D
    n_big = w_big.shape[2]
    tn_big = n_big // 4 if (n_big // 4) % 128 == 0 else 128

    wgb = w_gdn_branch.astype(BF16)
    wnb = w_nsa_branch.astype(BF16)
    wob = w_out.astype(BF16)
    wsg = w_sh_gate.astype(BF16)
    wsu = w_sh_up.astype(BF16)
    wsd = w_sh_down.astype(BF16)
    rwt = jnp.swapaxes(router_w, 1, 2)
    rwt_hi = rwt.astype(BF16)
    rwt_lo = (rwt - rwt_hi.astype(F32)).astype(BF16)
    cmp_w1 = jnp.stack([nsa_cmp_k_w1, nsa_cmp_v_w1], axis=1).astype(BF16)
    cmp_w2 = jnp.stack([nsa_cmp_k_w2, nsa_cmp_v_w2], axis=1).astype(BF16)
    cmp_pos = jnp.stack([nsa_cmp_pos_k, nsa_cmp_pos_v], axis=1).reshape(L, 2, 1, NSA_CMP_LEN * HEAD_DIM)

    ada = _ada(c, ada_w, ada_b)
    cosf, sinf = _rope_tables(positions)
    NC = S // NSA_CMP_STRIDE
    last = jnp.minimum(jnp.arange(NC) * NSA_CMP_STRIDE + NSA_CMP_LEN - 1, S - 1)
    cosc = cosf[:, last]
    sinc = sinf[:, last]
    n_sel = -(-(S // NSA_SEL_LEN) // 128) * 128

    for l in range(L):
        ada_l = ada[l].reshape(B, 1, 6 * D)
        big, small, cmp_in = _proj(x, ada_l, 0, 1, w_big[l], w_small[l], tn_big, kv0, 2 * G)
        small_t = jnp.swapaxes(small[:, :, nh_gdn:2 * nh_gdn].reshape(B, S // GDN_CHUNK, GDN_CHUNK, nh_gdn), 2, 3)
        q_a, k_a, v_a = _gdn_prep(big, gdn_conv_w[l], gdn_w)
        o_a = _gdn(q_a, k_a, v_a, big, Z_BLK, small, small_t, gdn_a_log[l], gdn_dt_bias[l], gdn_norm_w[l])
        q_r, ksel_r, kwin_r = _rope_apply(big, Q_BLK, (kv0 + 2 * kvw) // kvw, (kv0 + 4 * kvw) // kvw,
                                          nsa_w, kvw, cosf, sinf)
        hb = cmp_in.reshape(B, 2 * G, NC, NSA_CMP_STRIDE * HEAD_DIM)
        kvc = _compress(hb, cmp_pos[l], cmp_w1[l], cmp_w2[l], cosc, sinc)
        o_c, sel = _cmp_attn(q_r, kvc, small, nh_gdn, n_sel)
        o_s = _sel_attn(q_r, ksel_r, big, (kv0 + 3 * kvw) // HEAD_DIM, sel, small, nh_gdn)
        o_w = _win_attn(q_r, kwin_r, big, (kv0 + 5 * kvw) // HEAD_DIM, small, nh_gdn)
        x = _mixer_out(o_a, o_c, o_s, o_w, big, MA_BLK, MB_BLK, x, ada_l, 2, ln1_g[l], ln1_b[l],
                       wgb[l], wnb[l], wob[l], alpha)
        h2, eidx, ewt, shared, rank, counts = _moe_router(x, ada_l, 3, 4, rwt_hi[l], rwt_lo[l], router_bias[l],
                                                          wsg[l], wsu[l], wsd[l])
        row_tok, dest, block_expert, n_used = _routing_layout(eidx, rank, counts)
        rows = row_tok.shape[0]
        n_blocks = rows // MOE_ROW_BLOCK
        h2f = h2.reshape(T, D)
        bounds = [n_blocks * p // MOE_DISPATCH_PARTS for p in range(MOE_DISPATCH_PARTS + 1)]
        yb = None
        for lo, hi in zip(bounds[:-1], bounds[1:]):
            xs = h2f.at[row_tok[lo * MOE_ROW_BLOCK:hi * MOE_ROW_BLOCK]].get(mode='promise_in_bounds')
            yb = _moe_experts(block_expert, n_used, xs, l, w_e_gate, w_e_up, w_e_down, lo, rows, prev=yb)
        n_cp = MOE_COMBINE_PARTS if B % MOE_COMBINE_PARTS == 0 else 1
        bp = B // n_cp
        dest3 = dest.reshape(TOP_K, B, S)
        wt = jnp.swapaxes(ewt, 1, 2)
        x_new = None
        for p in range(n_cp):
            idx = dest3[:, p * bp:(p + 1) * bp].reshape(-1)
            yg = yb.at[idx].get(mode='promise_in_bounds').reshape(TOP_K, bp, S, D)
            x_new = _moe_out(x, shared, yg, wt, ada_l, 5, ln2_g[l], ln2_b[l], alpha, p * bp, prev=x_new)
        x = x_new
    return x
```

```python
import functools
import math

import jax
import jax.numpy as jnp
from jax import lax
from jax.experimental import pallas as pl
from jax.experimental.pallas import tpu as pltpu

F32 = jnp.float32
BF16 = jnp.bfloat16

HEAD_DIM = 128
GDN_CONV = 4
GDN_CHUNK = 64
NSA_GROUPS = 2
NSA_CMP_LEN = 32
NSA_CMP_STRIDE = 16
NSA_SEL_LEN = 64
NSA_SEL_TOP = 16
NSA_WINDOW = 512
ROPE_THETA = 10000.0
N_EXPERT_GROUPS = 8
TOPK_GROUPS = 4
TOP_K = 8
ROUTED_SCALE = 2.5
LN_EPS = 1e-5
NEG_INF = -1e30
SEL_FORCE = 1e6
MOE_ROW_BLOCK = 512
Q_SCALE_LOG2E = (HEAD_DIM ** -0.5) * math.log2(math.e)
MASK_BIG = 2.0 ** 100
N_ROW_PARTS = 2
GDN_PHASE1_CHUNKS = 4
WIN_SUB_TILE = 128
MOE_DISPATCH_PARTS = 8
MOE_COMBINE_PARTS = 2
_ARB = "arbitrary"


def _cparams(n_axes):
    return pltpu.CompilerParams(dimension_semantics=(_ARB,) * n_axes)


def _sigmoid(x):
    return 1.0 / (1.0 + jnp.exp(-x))


def _silu(x):
    return x * _sigmoid(x)


def _dot(a, b):
    return jnp.dot(a, b, preferred_element_type=F32)


def _dot_nt(a, b):
    return lax.dot_general(a, b, (((1,), (1,)), ((), ())), preferred_element_type=F32)


def _dot_tn(a, b):
    return lax.dot_general(a, b, (((0,), (0,)), ((), ())), preferred_element_type=F32)


def _tile_lanes(x, n):
    return jnp.concatenate([x] * n, axis=1)


def _split3(x):
    x0 = x.astype(BF16)
    r1 = x - x0.astype(F32)
    x1 = r1.astype(BF16)
    x2 = (r1 - x1.astype(F32)).astype(BF16)
    return x0, x1, x2


def _dot_sel_right(x, sel_bf16):
    x0, x1, x2 = _split3(x)
    return _dot(x0, sel_bf16) + _dot(x1, sel_bf16) + _dot(x2, sel_bf16)


def _dot_sel_left(sel_bf16, x):
    x0, x1, x2 = _split3(x)
    return _dot(sel_bf16, x0) + _dot(sel_bf16, x1) + _dot(sel_bf16, x2)


def _normalize_rows(x):
    mu = jnp.mean(x, axis=-1, keepdims=True)
    xc = x - mu
    var = jnp.mean(xc * xc, axis=-1, keepdims=True)
    return xc * lax.rsqrt(var + LN_EPS)


def _ada_kernel(c_ref, w_ref, b_ref, o_ref):
    cond = _silu(c_ref[...])
    o_ref[0] = jnp.dot(cond, w_ref[0], preferred_element_type=F32,
                       precision=lax.Precision.HIGHEST) + b_ref[0]


def _ada(c, ada_w, ada_b):
    L, D, N = ada_w.shape
    B = c.shape[0]
    tn = min(N, 1536)
    return pl.pallas_call(
        _ada_kernel,
        grid=(L, N // tn),
        in_specs=[pl.BlockSpec((B, D), lambda l, j: (0, 0)),
                  pl.BlockSpec((1, D, tn), lambda l, j: (l, 0, j)),
                  pl.BlockSpec((1, 1, tn), lambda l, j: (l, 0, j))],
        out_specs=pl.BlockSpec((1, B, tn), lambda l, j: (l, 0, j)),
        out_shape=jax.ShapeDtypeStruct((L, B, N), F32),
        compiler_params=_cparams(2),
        name="ada",
    )(c, ada_w, ada_b.reshape(L, 1, N))


def _proj_kernel(x_ref, sh_ref, sc_ref, w_ref, ws_ref, o_ref, os_ref, oc_ref, h_ref, *, head_tile, head_off):
    @pl.when(pl.program_id(2) == 0)
    def _():
        h = _normalize_rows(x_ref[0]) * (1.0 + sc_ref[0]) + sh_ref[0]
        h_ref[...] = h.astype(BF16)
        os_ref[0] = _dot(h_ref[...], ws_ref[...])

    res = _dot(h_ref[...], w_ref[...]).astype(o_ref.dtype)
    o_ref[0] = res

    @pl.when(pl.program_id(2) == head_tile)
    def _():
        for jj in range(oc_ref.shape[1]):
            oc_ref[0, jj] = res[:, head_off + jj * HEAD_DIM:head_off + (jj + 1) * HEAD_DIM]


def _proj(x, ada_l, sh_col, sc_col, w, w_small, tn, head_col, n_head_cols):
    B, S, D = x.shape
    N = w.shape[1]
    NS = w_small.shape[1]
    tm = min(S, 1024)
    head_tile, head_off = head_col // tn, head_col % tn
    assert head_off % HEAD_DIM == 0 and head_off + n_head_cols * HEAD_DIM <= tn
    return pl.pallas_call(
        functools.partial(_proj_kernel, head_tile=head_tile, head_off=head_off),
        grid=(B, S // tm, N // tn),
        in_specs=[pl.BlockSpec((1, tm, D), lambda b, i, j: (b, i, 0)),
                  pl.BlockSpec((1, 1, D), lambda b, i, j: (b, 0, sh_col)),
                  pl.BlockSpec((1, 1, D), lambda b, i, j: (b, 0, sc_col)),
                  pl.BlockSpec((D, tn), lambda b, i, j: (0, j)),
                  pl.BlockSpec((D, NS), lambda b, i, j: (0, 0))],
        out_specs=[pl.BlockSpec((1, tm, tn), lambda b, i, j: (b, i, j)),
                   pl.BlockSpec((1, tm, NS), lambda b, i, j: (b, i, 0)),
                   pl.BlockSpec((1, n_head_cols, tm, HEAD_DIM), lambda b, i, j: (b, 0, i, 0))],
        out_shape=[jax.ShapeDtypeStruct((B, S, N), BF16),
                   jax.ShapeDtypeStruct((B, S, NS), F32),
                   jax.ShapeDtypeStruct((B, n_head_cols, S, HEAD_DIM), BF16)],
        scratch_shapes=[pltpu.VMEM((tm, D), BF16)],
        compiler_params=_cparams(3),
        name="proj",
    )(x, ada_l, ada_l, w, w_small)


def _gdn_prep_kernel(x_ref, w_ref, q_ref, k_ref, v_ref, carry_ref):
    ts = x_ref.shape[1]
    width = q_ref.shape[2]
    nh = width // HEAD_DIM

    @pl.when(pl.program_id(1) == 0)
    def _():
        carry_ref[...] = jnp.zeros_like(carry_ref)

    for part, o_ref in enumerate((q_ref, k_ref, v_ref)):
        for h in range(nh):
            c0 = part * width + h * HEAD_DIM
            cols = slice(c0, c0 + HEAD_DIM)
            xx = jnp.concatenate([carry_ref[:, cols], x_ref[0, :, cols].astype(F32)], axis=0)
            w = w_ref[:, cols]
            y = xx[8:8 + ts] * w[GDN_CONV - 1:GDN_CONV]
            for kk in range(GDN_CONV - 1):
                off = 8 - (GDN_CONV - 1) + kk
                y = y + xx[off:off + ts] * w[kk:kk + 1]
            y = _silu(y)
            if part < 2:
                norm = lax.rsqrt(jnp.sum(y * y, axis=-1, keepdims=True) + 1e-6)
                if part == 0:
                    norm = norm * (HEAD_DIM ** -0.5)
                y = y * norm
            o_ref[0, :, h * HEAD_DIM:(h + 1) * HEAD_DIM] = y.astype(o_ref.dtype)
    carry_ref[...] = x_ref[0, ts - 8:ts, :].astype(F32)


def _gdn_prep(big, conv_w, width):
    B, S, _ = big.shape
    ts = min(S, 512)
    out = jax.ShapeDtypeStruct((B, S, width), BF16)
    ospec = pl.BlockSpec((1, ts, width), lambda b, i: (b, i, 0))
    return pl.pallas_call(
        _gdn_prep_kernel,
        grid=(B, S // ts),
        in_specs=[pl.BlockSpec((1, ts, 3 * width), lambda b, i: (b, i, 0)),
                  pl.BlockSpec((GDN_CONV, 3 * width), lambda b, i: (0, 0))],
        out_specs=[ospec, ospec, ospec],
        out_shape=[out, out, out],
        scratch_shapes=[pltpu.VMEM((8, 3 * width), F32)],
        compiler_params=_cparams(2),
        name="gdn_prep",
    )(big, conv_w)


def _softplus(x):
    return jnp.maximum(x, 0.0) + jnp.log(1.0 + jnp.exp(-jnp.abs(x)))


def _gdn_kernel(q_ref, k_ref, v_ref, z_ref, sm_ref, smt_ref, alog_ref, alogt_ref, dtb_ref, dtbt_ref,
                nw_ref, o_ref, state_ref, u_s, wq_s, attn_s, kd_s, dec_s):
    ts = q_ref.shape[1]
    nh = q_ref.shape[2] // HEAD_DIM
    C = GDN_CHUNK
    R = 2 * C
    npair = nh // 2
    nchunks = ts // C

    @pl.when(pl.program_id(1) == 0)
    def _():
        state_ref[...] = jnp.zeros_like(state_ref)

    ci = lax.broadcasted_iota(jnp.int32, (C, C), 0)
    cj = lax.broadcasted_iota(jnp.int32, (C, C), 1)
    tril = jnp.where(ci >= cj, 1.0, 0.0).astype(BF16)
    triu = jnp.where(cj >= ci, 1.0, 0.0).astype(BF16)
    ii = lax.broadcasted_iota(jnp.int32, (R, R), 0)
    jj = lax.broadcasted_iota(jnp.int32, (R, R), 1)
    same_head = (ii // C) == (jj // C)
    incl = same_head & (ii >= jj)
    strict = same_head & (ii > jj)
    eye = jnp.where(ii == jj, 1.0, 0.0)
    nw = nw_ref[...]

    def pair_rows(ref, rows, p):
        return jnp.concatenate([ref[0, rows, (2 * p) * HEAD_DIM:(2 * p + 1) * HEAD_DIM],
                                ref[0, rows, (2 * p + 1) * HEAD_DIM:(2 * p + 2) * HEAD_DIM]], axis=0)

    def pair_col(x, p):
        return jnp.concatenate([x[:, 2 * p:2 * p + 1], x[:, 2 * p + 1:2 * p + 2]], axis=0)

    def phase1(j, carry):
        cs = [j * GDN_PHASE1_CHUNKS + d for d in range(GDN_PHASE1_CHUNKS)]
        rows_c, beta_c, gc_c, gcr_c, glb_c = [], [], [], [], []
        for c in cs:
            rows = slice(c * C, (c + 1) * C)
            sm = sm_ref[0, rows, :]
            g = -jnp.exp(alog_ref[...]) * _softplus(sm[:, nh:2 * nh] + dtb_ref[...])
            gt = -jnp.exp(alogt_ref[...]) * _softplus(smt_ref[0, c] + dtbt_ref[...])
            gc = _dot_sel_left(tril, g)
            g_last = gc[C - 1:C, :]
            dec_s[c] = jnp.exp(g_last)
            rows_c.append(rows)
            beta_c.append(_sigmoid(sm[:, 0:nh]))
            gc_c.append(gc)
            gcr_c.append(_dot_sel_right(gt, triu))
            glb_c.append(jnp.broadcast_to(g_last, (C, nh)))
        items = [(ci, p) for ci in range(len(cs)) for p in range(npair)]
        idx = range(len(items))
        q2 = [pair_rows(q_ref, rows_c[ci], p).astype(F32) for ci, p in items]
        k2 = [pair_rows(k_ref, rows_c[ci], p).astype(F32) for ci, p in items]
        b2 = [pair_col(beta_c[ci], p) for ci, p in items]
        g_col = [pair_col(gc_c[ci], p) for ci, p in items]
        kb = [k2[i] * b2[i] for i in idx]
        kq = [_dot_nt(jnp.concatenate([kb[i], q2[i]], axis=0).astype(BF16), k2[i].astype(BF16))
              for i in idx]
        decay = []
        for i, (ci, p) in enumerate(items):
            gcr = gcr_c[ci]
            g_row = jnp.concatenate([gcr[2 * p:2 * p + 1, :], gcr[2 * p + 1:2 * p + 2, :]], axis=1)
            decay.append(jnp.where(incl, jnp.exp(jnp.where(incl, g_col[i] - g_row, 0.0)), 0.0))
        a = [jnp.where(strict, kq[i][:R] * decay[i], 0.0) for i in idx]
        for i, (ci, p) in enumerate(items):
            attn_s[cs[ci], p] = (kq[i][R:] * decay[i]).astype(BF16)
        x = [eye - a[i] for i in idx]
        pw = a
        n = 2
        while n < C:
            pw16 = [pw[i].astype(BF16) for i in idx]
            pw = [_dot(pw16[i], pw16[i]) for i in idx]
            x = [x[i] + _dot(x[i].astype(BF16), pw[i].astype(BF16)) for i in idx]
            n *= 2
        eg = [jnp.exp(g_col[i]) for i in idx]
        sol = []
        for i, (ci, p) in enumerate(items):
            v2 = pair_rows(v_ref, rows_c[ci], p).astype(F32)
            r = jnp.concatenate([v2 * b2[i], kb[i] * eg[i]], axis=1)
            sol.append(_dot(x[i].astype(BF16), r.astype(BF16)))
        for i, (ci, p) in enumerate(items):
            c = cs[ci]
            u_s[c, p] = sol[i][:, :HEAD_DIM]
            w = sol[i][:, HEAD_DIM:]
            qg = q2[i] * eg[i]
            for e in range(2):
                wq_s[c, 2 * p + e] = jnp.concatenate([w[e * C:(e + 1) * C], qg[e * C:(e + 1) * C]],
                                                     axis=0).astype(BF16)
            kd_s[c, p] = (k2[i] * jnp.exp(pair_col(glb_c[ci], p) - g_col[i])).astype(BF16)
        return carry

    def phase2(c):
        rows = slice(c * C, (c + 1) * C)
        dec = dec_s[c]
        res = [_dot(wq_s[c, h], state_ref[h].astype(BF16)) for h in range(nh)]
        for p in range(npair):
            ws = jnp.concatenate([res[2 * p][:C], res[2 * p + 1][:C]], axis=0)
            qs = jnp.concatenate([res[2 * p][C:], res[2 * p + 1][C:]], axis=0)
            v_new = u_s[c, p] - ws
            v16 = v_new.astype(BF16)
            o2 = qs + _dot(attn_s[c, p], v16)
            kd = kd_s[c, p]
            for e in range(2):
                h = 2 * p + e
                part = slice(e * C, (e + 1) * C)
                state_ref[h] = state_ref[h] * dec[:, h:h + 1] + _dot_tn(kd[part], v16[part])
                o = o2[part]
                o = o * lax.rsqrt(jnp.mean(o * o, axis=-1, keepdims=True) + 1e-6) * nw
                cols = slice(h * HEAD_DIM, (h + 1) * HEAD_DIM)
                z = z_ref[0, rows, cols].astype(F32)
                o_ref[0, rows, cols] = (o * _silu(z)).astype(o_ref.dtype)

    def phase2_group(j):
        for d in range(GDN_PHASE1_CHUNKS):
            phase2(j * GDN_PHASE1_CHUNKS + d)

    assert nchunks % GDN_PHASE1_CHUNKS == 0
    ngroups = nchunks // GDN_PHASE1_CHUNKS
    phase1(0, 0)
    for j in range(1, ngroups):
        phase2_group(j - 1)
        phase1(j, 0)
    phase2_group(ngroups - 1)


def _gdn(q, k, v, big, z_blk, small, small_t, a_log, dt_bias, norm_w):
    B, S, W = q.shape
    nh = W // HEAD_DIM
    assert nh % 2 == 0 and 2 * GDN_CHUNK == HEAD_DIM
    ts = min(S, 512)
    nc = ts // GDN_CHUNK
    spec = pl.BlockSpec((1, ts, W), lambda b, i: (b, i, 0))
    full = lambda shape: pl.BlockSpec(shape, lambda b, i: (0,) * len(shape))
    return pl.pallas_call(
        _gdn_kernel,
        grid=(B, S // ts),
        in_specs=[spec, spec, spec,
                  pl.BlockSpec((1, ts, W), lambda b, i: (b, i, z_blk)),
                  pl.BlockSpec((1, ts, small.shape[2]), lambda b, i: (b, i, 0)),
                  pl.BlockSpec((1, ts // GDN_CHUNK, nh, GDN_CHUNK), lambda b, i: (b, i, 0, 0)),
                  full((1, nh)), full((nh, 1)), full((1, nh)), full((nh, 1)),
                  full((1, HEAD_DIM))],
        out_specs=spec,
        out_shape=jax.ShapeDtypeStruct((B, S, W), BF16),
        scratch_shapes=[pltpu.VMEM((nh, HEAD_DIM, HEAD_DIM), F32),
                        pltpu.VMEM((nc, nh // 2, 2 * GDN_CHUNK, HEAD_DIM), F32),
                        pltpu.VMEM((nc, nh, 2 * GDN_CHUNK, HEAD_DIM), BF16),
                        pltpu.VMEM((nc, nh // 2, 2 * GDN_CHUNK, 2 * GDN_CHUNK), BF16),
                        pltpu.VMEM((nc, nh // 2, 2 * GDN_CHUNK, HEAD_DIM), BF16),
                        pltpu.VMEM((nc, 1, nh), F32)],
        compiler_params=_cparams(2),
        name="gdn",
    )(q, k, v, big, small, small_t, a_log.reshape(1, nh), a_log.reshape(nh, 1),
      dt_bias.reshape(1, nh), dt_bias.reshape(nh, 1), norm_w.reshape(1, HEAD_DIM))


def _rope_table_kernel(pos_ref, invf_ref, sign_ref, cos_ref, sin_ref):
    ang = pos_ref[0].astype(F32) * invf_ref[...]
    cos_ref[0] = jnp.cos(ang)
    sin_ref[0] = jnp.sin(ang) * sign_ref[...]


def _rope_tables(positions):
    B, S = positions.shape
    half = HEAD_DIM // 2
    inv = ROPE_THETA ** (-jnp.arange(half, dtype=F32) / half)
    invf = jnp.concatenate([inv, inv]).reshape(1, HEAD_DIM)
    sign = jnp.concatenate([-jnp.ones((half,), F32), jnp.ones((half,), F32)]).reshape(1, HEAD_DIM)
    ts = min(S, 1024)
    out = jax.ShapeDtypeStruct((B, S, HEAD_DIM), F32)
    ospec = pl.BlockSpec((1, ts, HEAD_DIM), lambda b, i: (b, i, 0))
    return pl.pallas_call(
        _rope_table_kernel,
        grid=(B, S // ts),
        in_specs=[pl.BlockSpec((1, ts, 1), lambda b, i: (b, i, 0)),
                  pl.BlockSpec((1, HEAD_DIM), lambda b, i: (0, 0)),
                  pl.BlockSpec((1, HEAD_DIM), lambda b, i: (0, 0))],
        out_specs=[ospec, ospec],
        out_shape=[out, out],
        compiler_params=_cparams(2),
        name="rope_table",
    )(positions.reshape(B, S, 1), invf, sign)


def _rope_rows(x, cosf, sinf):
    return x * cosf + pltpu.roll(x, HEAD_DIM // 2, 1) * sinf


def _rope_apply_kernel(q_ref, ks_ref, kw_ref, cos_ref, sin_ref, qo_ref, kso_ref, kwo_ref):
    cosf = cos_ref[0]
    sinf = sin_ref[0]
    for src, dst, mult in ((q_ref, qo_ref, Q_SCALE_LOG2E), (ks_ref, kso_ref, None), (kw_ref, kwo_ref, None)):
        for h in range(src.shape[2] // HEAD_DIM):
            cols = slice(h * HEAD_DIM, (h + 1) * HEAD_DIM)
            r = _rope_rows(src[0, :, cols].astype(F32), cosf, sinf)
            if mult is not None:
                r = r * mult
            dst[0, :, cols] = r.astype(dst.dtype)


def _rope_apply(big, q_blk, ksel_blk, kwin_blk, wq, wkv, cosf, sinf):
    B, S, _ = big.shape
    ts = min(S, 512)
    tab = pl.BlockSpec((1, ts, HEAD_DIM), lambda b, i: (b, i, 0))
    return pl.pallas_call(
        _rope_apply_kernel,
        grid=(B, S // ts),
        in_specs=[pl.BlockSpec((1, ts, wq), lambda b, i: (b, i, q_blk)),
                  pl.BlockSpec((1, ts, wkv), lambda b, i: (b, i, ksel_blk)),
                  pl.BlockSpec((1, ts, wkv), lambda b, i: (b, i, kwin_blk)),
                  tab, tab],
        out_specs=[pl.BlockSpec((1, ts, wq), lambda b, i: (b, i, 0)),
                   pl.BlockSpec((1, ts, wkv), lambda b, i: (b, i, 0)),
                   pl.BlockSpec((1, ts, wkv), lambda b, i: (b, i, 0))],
        out_shape=[jax.ShapeDtypeStruct((B, S, wq), BF16),
                   jax.ShapeDtypeStruct((B, S, wkv), BF16),
                   jax.ShapeDtypeStruct((B, S, wkv), BF16)],
        compiler_params=_cparams(2),
        name="rope_apply",
    )(big, big, big, cosf, sinf)


def _compress_kernel(hb_ref, pos_ref, w1_ref, w2_ref, cos_ref, sin_ref, o_ref):
    hb = hb_ref[0, 0]
    w1 = w1_ref[0]
    half = hb.shape[1]
    p0 = _dot(hb, w1[:half])
    p1 = _dot(hb, w1[half:])
    nc = p0.shape[0]
    pos8 = jnp.broadcast_to(pos_ref[0], (8, 2 * half)).astype(BF16)
    pb = _dot(pos8, w1)[0:1]
    pre = p0 + pltpu.roll(p1, nc - 1, 0) + pb
    out = _dot(_silu(pre).astype(BF16), w2_ref[0])
    roped = _rope_rows(out, cos_ref[0], sin_ref[0])
    is_key = pl.program_id(1) < NSA_GROUPS
    o_ref[0, 0] = jnp.where(is_key, roped, out).astype(o_ref.dtype)


def _compress(hb, pos_flat, w1, w2, cosc, sinc):
    B, J, NC, HW = hb.shape
    G = NSA_GROUPS
    return pl.pallas_call(
        _compress_kernel,
        grid=(B, J),
        in_specs=[pl.BlockSpec((1, 1, NC, HW), lambda b, j: (b, j, 0, 0)),
                  pl.BlockSpec((1, 1, 2 * HW), lambda b, j: (j // G, 0, 0)),
                  pl.BlockSpec((1, 2 * HW, HEAD_DIM), lambda b, j: (j // G, 0, 0)),
                  pl.BlockSpec((1, HEAD_DIM, HEAD_DIM), lambda b, j: (j // G, 0, 0)),
                  pl.BlockSpec((1, NC, HEAD_DIM), lambda b, j: (b, 0, 0)),
                  pl.BlockSpec((1, NC, HEAD_DIM), lambda b, j: (b, 0, 0))],
        out_specs=pl.BlockSpec((1, 1, NC, HEAD_DIM), lambda b, j: (b, j, 0, 0)),
        out_shape=jax.ShapeDtypeStruct((B, J, NC, HEAD_DIM), BF16),
        compiler_params=_cparams(2),
        name="nsa_compress",
    )(hb, pos_flat, w1, w2, cosc, sinc)


def _gate_col(g_ref, nh_gdn, group, hpg, h, branch):
    col = None
    for gi in range(NSA_GROUPS):
        c = 2 * nh_gdn + (gi * hpg + h) * 3 + branch
        cand = g_ref[0, :, c:c + 1]
        col = cand if col is None else jnp.where(group == gi, cand, col)
    return _sigmoid(col)


def _cmp_attn_kernel(q_ref, kc_ref, vc_ref, sm_ref, o_ref, sel_ref, *, nh_gdn, hpg):
    tq = q_ref.shape[1]
    nc = kc_ref.shape[2]
    nb = sel_ref.shape[3]
    g = pl.program_id(1)
    t0 = pl.program_id(2) * tq
    t_nc = t0 + lax.broadcasted_iota(jnp.int32, (tq, nc), 0)
    n_nc = lax.broadcasted_iota(jnp.int32, (tq, nc), 1)
    valid = (n_nc * NSA_CMP_STRIDE + (NSA_CMP_LEN - 1)) <= t_nc
    t_col = t0 + lax.broadcasted_iota(jnp.int32, (tq, 1), 0)
    has_valid = jnp.where(t_col >= NSA_CMP_LEN - 1, 1.0, 0.0)
    kc = kc_ref[0, 0]
    vc = vc_ref[0, 0]
    psum = jnp.zeros((tq, nc), F32)
    for h in range(hpg):
        cols = slice(h * HEAD_DIM, (h + 1) * HEAD_DIM)
        s = jnp.where(valid, _dot_nt(q_ref[0, :, cols], kc), NEG_INF)
        e = jnp.exp2(s - jnp.max(s, axis=-1, keepdims=True))
        p = e * (has_valid / jnp.sum(e, axis=-1, keepdims=True))
        gate = _gate_col(sm_ref, nh_gdn, g, hpg, h, 0)
        o_ref[0, :, cols] = (_dot(p.astype(BF16), vc) * gate).astype(o_ref.dtype)
        psum = psum + p
    sj = lax.broadcasted_iota(jnp.int32, (nb, nc), 0) * NSA_SEL_LEN
    cn = lax.broadcasted_iota(jnp.int32, (nb, nc), 1) * NSA_CMP_STRIDE
    overlap_t = jnp.where((cn <= sj + (NSA_SEL_LEN - 1)) & (cn + (NSA_CMP_LEN - 1) >= sj), 1.0, 0.0).astype(BF16)
    p0, p1, p2 = _split3(psum)
    imp = _dot_nt(overlap_t, p0) + _dot_nt(overlap_t, p1) + _dot_nt(overlap_t, p2)
    t_nb = t0 + lax.broadcasted_iota(jnp.int32, (nb, tq), 1)
    blk = lax.broadcasted_iota(jnp.int32, (nb, tq), 0)
    cur = t_nb // NSA_SEL_LEN
    forced = (blk == 0) | (blk == cur) | (blk == cur - 1)
    score = jnp.where(forced, SEL_FORCE, jnp.where(blk * NSA_SEL_LEN <= t_nb, imp, -1.0))
    blk_f = blk.astype(F32)
    for _ in range(min(NSA_SEL_TOP, nb)):
        m = jnp.max(score, axis=0, keepdims=True)
        first = jnp.min(jnp.where(score == m, blk_f, float(nb)), axis=0, keepdims=True)
        score = jnp.where(blk_f == first, -jnp.inf, score)
    sel_ref[0, 0] = jnp.where(score == -jnp.inf, 1.0, 0.0).T.astype(sel_ref.dtype)


def _cmp_attn(q_r, kvc, small, nh_gdn, n_sel):
    B, S, WQ = q_r.shape
    G = NSA_GROUPS
    hpg = WQ // HEAD_DIM // G
    NC = kvc.shape[2]
    tq = min(S, 512)
    return pl.pallas_call(
        functools.partial(_cmp_attn_kernel, nh_gdn=nh_gdn, hpg=hpg),
        grid=(B, G, S // tq),
        in_specs=[pl.BlockSpec((1, tq, hpg * HEAD_DIM), lambda b, g, i: (b, i, g)),
                  pl.BlockSpec((1, 1, NC, HEAD_DIM), lambda b, g, i: (b, g, 0, 0)),
                  pl.BlockSpec((1, 1, NC, HEAD_DIM), lambda b, g, i: (b, G + g, 0, 0)),
                  pl.BlockSpec((1, tq, small.shape[2]), lambda b, g, i: (b, i, 0))],
        out_specs=[pl.BlockSpec((1, tq, hpg * HEAD_DIM), lambda b, g, i: (b, i, g)),
                   pl.BlockSpec((1, 1, tq, n_sel), lambda b, g, i: (b, g, i, 0))],
        out_shape=[jax.ShapeDtypeStruct((B, S, WQ), BF16),
                   jax.ShapeDtypeStruct((B, G, S, n_sel), BF16)],
        compiler_params=_cparams(3),
        name="nsa_cmp_attn",
    )(q_r, kvc, kvc, small)


def _sel_attn_kernel(q_ref, k_ref, ext_ref, v_ref, sel_ref, sm_ref, o_ref, qx_ref, m_ref, acc_ref, p_ref, a_ref,
                     *, nh_gdn, hpg, tk):
    tq = q_ref.shape[1]
    nb = sel_ref.shape[3]
    g = pl.program_id(1)
    t0 = pl.program_id(2) * tq
    assert tk % tq == 0
    unsel = (sel_ref[0, 0].astype(F32) - 1.0).astype(BF16)
    for h in range(hpg):
        qx_ref[h * tq:(h + 1) * tq, 0:HEAD_DIM] = q_ref[0, :, h * HEAD_DIM:(h + 1) * HEAD_DIM]
        qx_ref[h * tq:(h + 1) * tq, HEAD_DIM:HEAD_DIM + nb] = unsel
    m_ref[...] = jnp.full_like(m_ref, NEG_INF)
    acc_ref[...] = jnp.zeros_like(acc_ref)
    p_ref[...] = jnp.zeros_like(p_ref)
    a_ref[...] = jnp.ones_like(a_ref)
    ones = jnp.ones((tk, HEAD_DIM), BF16)
    pr = hpg * tq // N_ROW_PARTS
    parts = [slice(part * pr, (part + 1) * pr) for part in range(N_ROW_PARTS)]

    def apply_values(kt):
        k0 = pl.multiple_of(kt * tk, tk)
        vx = jnp.concatenate([v_ref[0, pl.ds(k0, tk), :], ones], axis=1)
        for rows in parts:
            acc_ref[rows, :] = _tile_lanes(a_ref[rows, :], 2) * acc_ref[rows, :] + _dot(p_ref[rows, :], vx)

    def scores(kt, causal):
        k0 = pl.multiple_of(kt * tk, tk)
        kx = jnp.concatenate([k_ref[0, pl.ds(k0, tk), :], ext_ref[pl.ds(k0, tk), :]], axis=1)
        if causal:
            t_row = t0 + lax.broadcasted_iota(jnp.int32, (tq, tk), 0)
            key = k0 + lax.broadcasted_iota(jnp.int32, (tq, tk), 1)
            keep = jnp.concatenate([key <= t_row] * (hpg // N_ROW_PARTS), axis=0)
        for rows in parts:
            s = _dot_nt(qx_ref[rows, :], kx)
            if causal:
                s = jnp.where(keep, s, NEG_INF)
            m_old = m_ref[rows, :]
            m_new = jnp.maximum(m_old, jnp.max(s, axis=-1, keepdims=True))
            a_ref[rows, :] = jnp.exp2(m_old - m_new)
            p_ref[rows, :] = jnp.exp2(s - _tile_lanes(m_new, tk // HEAD_DIM)).astype(BF16)
            m_ref[rows, :] = m_new

    kt_diag = t0 // tk

    def body(kt, carry):
        apply_values(jnp.maximum(kt - 1, 0))
        scores(kt, False)
        return carry

    lax.fori_loop(0, kt_diag, body, 0)
    apply_values(jnp.maximum(kt_diag - 1, 0))
    scores(kt_diag, True)
    apply_values(kt_diag)
    for h in range(hpg):
        rows = slice(h * tq, (h + 1) * tq)
        gate = _gate_col(sm_ref, nh_gdn, g, hpg, h, 1)
        out = acc_ref[rows, 0:HEAD_DIM] / acc_ref[rows, HEAD_DIM:2 * HEAD_DIM]
        o_ref[0, :, h * HEAD_DIM:(h + 1) * HEAD_DIM] = (out * gate).astype(o_ref.dtype)


def _sel_attn(q_r, ksel_r, big, vsel_blk, sel, small, nh_gdn):
    B, S, WQ = q_r.shape
    G = NSA_GROUPS
    hpg = WQ // HEAD_DIM // G
    nbp = sel.shape[3]
    tq = min(S, 512)
    tk = min(S, 512)
    rows = hpg * tq
    ext = jnp.where(jnp.arange(S)[:, None] // NSA_SEL_LEN == jnp.arange(nbp)[None, :], MASK_BIG, 0.0).astype(BF16)
    return pl.pallas_call(
        functools.partial(_sel_attn_kernel, nh_gdn=nh_gdn, hpg=hpg, tk=tk),
        grid=(B, G, S // tq),
        in_specs=[pl.BlockSpec((1, tq, hpg * HEAD_DIM), lambda b, g, i: (b, i, g)),
                  pl.BlockSpec((1, S, HEAD_DIM), lambda b, g, i: (b, 0, g)),
                  pl.BlockSpec((S, nbp), lambda b, g, i: (0, 0)),
                  pl.BlockSpec((1, S, HEAD_DIM), lambda b, g, i: (b, 0, vsel_blk + g)),
                  pl.BlockSpec((1, 1, tq, nbp), lambda b, g, i: (b, g, i, 0)),
                  pl.BlockSpec((1, tq, small.shape[2]), lambda b, g, i: (b, i, 0))],
        out_specs=pl.BlockSpec((1, tq, hpg * HEAD_DIM), lambda b, g, i: (b, i, g)),
        out_shape=jax.ShapeDtypeStruct((B, S, WQ), BF16),
        scratch_shapes=[pltpu.VMEM((rows, HEAD_DIM + nbp), BF16),
                        pltpu.VMEM((rows, HEAD_DIM), F32),
                        pltpu.VMEM((rows, 2 * HEAD_DIM), F32),
                        pltpu.VMEM((rows, tk), BF16),
                        pltpu.VMEM((rows, HEAD_DIM), F32)],
        compiler_params=_cparams(3),
        name="nsa_sel_attn",
    )(q_r, ksel_r, ext, big, sel, small)


def _win_attn_kernel(q_ref, k_ref, v_ref, sm_ref, o_ref, *, nh_gdn, hpg):
    tq = q_ref.shape[1]
    ts = min(tq, WIN_SUB_TILE)
    span = ts + NSA_WINDOW
    g = pl.program_id(1)
    for sub in range(tq // ts):
        qrows = slice(sub * ts, (sub + 1) * ts)
        t0 = pl.program_id(2) * tq + sub * ts
        k0 = pl.multiple_of(jnp.maximum(t0 - NSA_WINDOW, 0), ts)
        k = k_ref[0, pl.ds(k0, span), :]
        v = v_ref[0, pl.ds(k0, span), :]
        t_row = t0 + lax.broadcasted_iota(jnp.int32, (ts, span), 0)
        spos = k0 + lax.broadcasted_iota(jnp.int32, (ts, span), 1)
        mask = (spos <= t_row) & (spos > t_row - NSA_WINDOW)
        vx = jnp.concatenate([v, jnp.ones((span, HEAD_DIM), BF16)], axis=1)
        q4 = jnp.concatenate([q_ref[0, qrows, h * HEAD_DIM:(h + 1) * HEAD_DIM] for h in range(hpg)], axis=0)
        s = jnp.where(jnp.concatenate([mask] * hpg, axis=0), _dot_nt(q4, k), NEG_INF)
        e = jnp.exp2(s - jnp.max(s, axis=-1, keepdims=True))
        acc = _dot(e.astype(BF16), vx)
        out = acc[:, :HEAD_DIM] / acc[:, HEAD_DIM:]
        for h in range(hpg):
            gate = _gate_col(sm_ref, nh_gdn, g, hpg, h, 2)[qrows]
            o_ref[0, qrows, h * HEAD_DIM:(h + 1) * HEAD_DIM] = (out[h * ts:(h + 1) * ts] * gate).astype(o_ref.dtype)


def _win_attn(q_r, kwin_r, big, vwin_blk, small, nh_gdn):
    B, S, WQ = q_r.shape
    G = NSA_GROUPS
    hpg = WQ // HEAD_DIM // G
    tq = min(S, 512)
    assert S >= tq + NSA_WINDOW
    return pl.pallas_call(
        functools.partial(_win_attn_kernel, nh_gdn=nh_gdn, hpg=hpg),
        grid=(B, G, S // tq),
        in_specs=[pl.BlockSpec((1, tq, hpg * HEAD_DIM), lambda b, g, i: (b, i, g)),
                  pl.BlockSpec((1, S, HEAD_DIM), lambda b, g, i: (b, 0, g)),
                  pl.BlockSpec((1, S, HEAD_DIM), lambda b, g, i: (b, 0, vwin_blk + g)),
                  pl.BlockSpec((1, tq, small.shape[2]), lambda b, g, i: (b, i, 0))],
        out_specs=pl.BlockSpec((1, tq, hpg * HEAD_DIM), lambda b, g, i: (b, i, g)),
        out_shape=jax.ShapeDtypeStruct((B, S, WQ), BF16),
        compiler_params=_cparams(3),
        name="nsa_win_attn",
    )(q_r, kwin_r, big, small)


def _post_norm(x, y, gt, g, b, alpha):
    r = alpha * x + (1.0 + gt) * y
    return _normalize_rows(r) * g + b


def _mixer_out_kernel(oa_ref, oc_ref, os_ref, ow_ref, ma_ref, mb_ref, x_ref, gt_ref, lg_ref, lb_ref,
                      wg_ref, wn_ref, wo_ref, o_ref, *, alpha):
    y_a = _dot(oa_ref[0], wg_ref[...])
    o_b = oc_ref[0].astype(F32) + os_ref[0].astype(F32) + ow_ref[0].astype(F32)
    y_b = _dot(o_b.astype(BF16), wn_ref[...])
    mixed = _sigmoid(ma_ref[0].astype(F32)) * y_a + _sigmoid(mb_ref[0].astype(F32)) * y_b
    y = _dot(mixed.astype(BF16), wo_ref[...])
    o_ref[0] = _post_norm(x_ref[0], y, gt_ref[0], lg_ref[...], lb_ref[...], alpha)


def _mixer_out(o_a, o_c, o_s, o_w, big, ma_blk, mb_blk, x, ada_l, gt_col, ln_g, ln_b, wg, wn, wo, alpha):
    B, S, D = x.shape
    tm = min(S, 512)
    row = lambda blk: pl.BlockSpec((1, tm, D), lambda b, i: (b, i, blk))
    full = lambda shape: pl.BlockSpec(shape, lambda b, i: (0,) * len(shape))
    return pl.pallas_call(
        functools.partial(_mixer_out_kernel, alpha=alpha),
        grid=(B, S // tm),
        in_specs=[row(0), row(0), row(0), row(0), row(ma_blk), row(mb_blk), row(0),
                  pl.BlockSpec((1, 1, D), lambda b, i: (b, 0, gt_col)),
                  full((1, D)), full((1, D)), full((D, D)), full((D, D)), full((D, D))],
        out_specs=row(0),
        out_shape=jax.ShapeDtypeStruct((B, S, D), F32),
        compiler_params=_cparams(2),
        name="mixer_out",
    )(o_a, o_c, o_s, o_w, big, big, x, ada_l, ln_g.reshape(1, D), ln_b.reshape(1, D), wg, wn, wo)


def _moe_router_kernel(x_ref, sh_ref, sc_ref, rwh_ref, rwl_ref, rb_ref,
                       h_ref, idx_ref, wt_ref, rank_ref, cnt_ref):
    h = _normalize_rows(x_ref[0]) * (1.0 + sc_ref[0]) + sh_ref[0]
    h16 = h.astype(BF16)
    h_ref[0] = h16
    h_lo = (h - h16.astype(F32)).astype(BF16)
    rw_hi = rwh_ref[...]
    logits = _dot_nt(rw_hi, h16) + _dot_nt(rwl_ref[...], h16) + _dot_nt(rw_hi, h_lo)
    scores = _sigmoid(logits)
    biased = scores + rb_ref[...]
    E, tm = scores.shape
    NG = N_EXPERT_GROUPS
    per = E // NG
    b3 = biased.reshape(NG, per, tm)
    in_grp = lax.broadcasted_iota(jnp.int32, (NG, per, tm), 1).astype(F32)
    m1 = jnp.max(b3, axis=1, keepdims=True)
    first = jnp.min(jnp.where(b3 == m1, in_grp, float(per)), axis=1, keepdims=True)
    m2 = jnp.max(jnp.where(in_grp == first, -jnp.inf, b3), axis=1, keepdims=True)
    gscore = (m1 + m2).reshape(NG, tm)
    g_id = lax.broadcasted_iota(jnp.int32, (NG, tm), 0).astype(F32)
    allowed = jnp.zeros((NG, tm), F32)
    for _ in range(TOPK_GROUPS):
        m = jnp.max(gscore, axis=0, keepdims=True)
        first_g = jnp.min(jnp.where(gscore == m, g_id, float(NG)), axis=0, keepdims=True)
        pick = g_id == first_g
        allowed = jnp.where(pick, 1.0, allowed)
        gscore = jnp.where(pick, -jnp.inf, gscore)
    cand = jnp.where(allowed.reshape(NG, 1, tm) > 0.5, b3, NEG_INF).reshape(E, tm)
    e_id = lax.broadcasted_iota(jnp.int32, (E, tm), 0).astype(F32)
    idx_rows, wt_rows, picks = [], [], []
    wsum = jnp.zeros((1, tm), F32)
    chosen = jnp.zeros((E, tm), F32)
    for kk in range(TOP_K):
        m = jnp.max(cand, axis=0, keepdims=True)
        first_e = jnp.min(jnp.where(cand == m, e_id, float(E)), axis=0, keepdims=True)
        pick = e_id == first_e
        w = jnp.sum(jnp.where(pick, scores, 0.0), axis=0, keepdims=True)
        cand = jnp.where(pick, -jnp.inf, cand)
        chosen = jnp.where(pick, 1.0, chosen)
        idx_rows.append(first_e)
        wt_rows.append(w)
        picks.append(pick)
        wsum = wsum + w
    idx_ref[0] = jnp.concatenate(idx_rows, axis=0).astype(jnp.int32)
    wt_ref[0] = jnp.concatenate(wt_rows, axis=0) / wsum * ROUTED_SCALE
    @pl.when((pl.program_id(0) == 0) & (pl.program_id(1) == 0))
    def _():
        cnt_ref[...] = jnp.zeros_like(cnt_ref)

    si = lax.broadcasted_iota(jnp.int32, (tm, tm), 0)
    ti = lax.broadcasted_iota(jnp.int32, (tm, tm), 1)
    before = jnp.where(si < ti, 1.0, 0.0).astype(BF16)
    prior = _dot(chosen.astype(BF16), before) + cnt_ref[...]
    rank_ref[0] = jnp.concatenate([jnp.sum(jnp.where(pk, prior, 0.0), axis=0, keepdims=True) for pk in picks],
                                  axis=0).astype(jnp.int32)
    cnt_ref[...] = cnt_ref[...] + jnp.sum(chosen, axis=1, keepdims=True)


def _shared_expert_kernel(h_ref, wsg_ref, wsu_ref, wsd_ref, o_ref):
    h16 = h_ref[0]
    act = _silu(_dot(h16, wsg_ref[...])) * _dot(h16, wsu_ref[...])
    o_ref[0] = _dot(act.astype(BF16), wsd_ref[...])


def _shared_expert(h, wsg, wsu, wsd):
    B, S, D = h.shape
    FF = wsg.shape[1]
    tm = min(S, 1024)
    row = pl.BlockSpec((1, tm, D), lambda b, i: (b, i, 0))
    full = lambda shape: pl.BlockSpec(shape, lambda b, i: (0,) * len(shape))
    return pl.pallas_call(
        _shared_expert_kernel,
        grid=(B, S // tm),
        in_specs=[row, full((D, FF)), full((D, FF)), full((FF, D))],
        out_specs=row,
        out_shape=jax.ShapeDtypeStruct((B, S, D), F32),
        compiler_params=_cparams(2),
        name="moe_shared",
    )(h, wsg, wsu, wsd)


def _moe_router(x, ada_l, sh_col, sc_col, rwt_hi, rwt_lo, router_b):
    B, S, D = x.shape
    E = rwt_hi.shape[0]
    tm = min(S, 512)
    row = lambda w: pl.BlockSpec((1, tm, w), lambda b, i: (b, i, 0))
    krow = pl.BlockSpec((1, TOP_K, tm), lambda b, i: (b, 0, i))
    full = lambda shape: pl.BlockSpec(shape, lambda b, i: (0,) * len(shape))
    return pl.pallas_call(
        _moe_router_kernel,
        grid=(B, S // tm),
        in_specs=[row(D),
                  pl.BlockSpec((1, 1, D), lambda b, i: (b, 0, sh_col)),
                  pl.BlockSpec((1, 1, D), lambda b, i: (b, 0, sc_col)),
                  full((E, D)), full((E, D)), full((E, 1))],
        out_specs=[row(D), krow, krow, krow, full((E, 1))],
        out_shape=[jax.ShapeDtypeStruct((B, S, D), BF16),
                   jax.ShapeDtypeStruct((B, TOP_K, S), jnp.int32),
                   jax.ShapeDtypeStruct((B, TOP_K, S), F32),
                   jax.ShapeDtypeStruct((B, TOP_K, S), jnp.int32),
                   jax.ShapeDtypeStruct((E, 1), F32)],
        compiler_params=_cparams(2),
        name="moe_router",
    )(x, ada_l, ada_l, rwt_hi, rwt_lo, router_b.reshape(E, 1))


def _moe_experts_kernel(be_ref, nu_ref, xs_ref, wg_ref, wu_ref, wd_ref, *rest, first_block):
    o_ref = rest[-1]

    @pl.when(pl.program_id(0) + first_block < nu_ref[0])
    def _():
        xb = xs_ref[...]
        act = _silu(_dot(xb, wg_ref[0, 0].astype(BF16))) * _dot(xb, wu_ref[0, 0].astype(BF16))
        o_ref[...] = _dot(act.astype(BF16), wd_ref[0, 0].astype(BF16)).astype(o_ref.dtype)


def _moe_experts(block_expert, n_used, xs, layer, wg, wu, wd, first_block, total_rows, prev=None):
    rows, D = xs.shape
    BM = MOE_ROW_BLOCK
    FF = wg.shape[3]
    fb = first_block
    in_specs = [pl.BlockSpec((BM, D), lambda i, be, nu: (i, 0)),
                pl.BlockSpec((1, 1, D, FF), lambda i, be, nu: (layer, be[i + fb], 0, 0)),
                pl.BlockSpec((1, 1, D, FF), lambda i, be, nu: (layer, be[i + fb], 0, 0)),
                pl.BlockSpec((1, 1, FF, D), lambda i, be, nu: (layer, be[i + fb], 0, 0))]
    args = [block_expert, n_used, xs, wg, wu, wd]
    aliases = {}
    if prev is not None:
        in_specs.append(pl.BlockSpec(memory_space=pl.ANY))
        args.append(prev)
        aliases = {len(args) - 1: 0}
    grid_spec = pltpu.PrefetchScalarGridSpec(
        num_scalar_prefetch=2,
        grid=(rows // BM,),
        in_specs=in_specs,
        out_specs=pl.BlockSpec((BM, D), lambda i, be, nu: (i + fb, 0)),
    )
    return pl.pallas_call(
        functools.partial(_moe_experts_kernel, first_block=fb),
        grid_spec=grid_spec,
        out_shape=jax.ShapeDtypeStruct((total_rows, D), BF16),
        input_output_aliases=aliases,
        compiler_params=_cparams(1),
        name="moe_experts",
    )(*args)


def _moe_out_kernel(x_ref, shared_ref, yg_ref, wt_ref, gt_ref, lg_ref, lb_ref, *rest, alpha):
    o_ref = rest[-1]
    y = shared_ref[0]
    wt = wt_ref[0]
    for kk in range(TOP_K):
        y = y + wt[:, kk:kk + 1] * yg_ref[kk, 0].astype(F32)
    o_ref[0] = _post_norm(x_ref[0], y, gt_ref[0], lg_ref[...], lb_ref[...], alpha)


def _moe_out(x, shared, yg, wt, ada_l, gt_col, ln_g, ln_b, alpha, first_b, prev=None):
    B, S, D = x.shape
    nb = yg.shape[1]
    fb = first_b
    tm = min(S, 256)
    row = lambda w: pl.BlockSpec((1, tm, w), lambda b, i: (b + fb, i, 0))
    full = lambda shape: pl.BlockSpec(shape, lambda b, i: (0,) * len(shape))
    in_specs = [row(D), row(D),
                pl.BlockSpec((TOP_K, 1, tm, D), lambda b, i: (0, b, i, 0)),
                row(wt.shape[2]),
                pl.BlockSpec((1, 1, D), lambda b, i: (b + fb, 0, gt_col)),
                full((1, D)), full((1, D))]
    args = [x, shared, yg, wt, ada_l, ln_g.reshape(1, D), ln_b.reshape(1, D)]
    aliases = {}
    if prev is not None:
        in_specs.append(pl.BlockSpec(memory_space=pl.ANY))
        args.append(prev)
        aliases = {len(args) - 1: 0}
    return pl.pallas_call(
        functools.partial(_moe_out_kernel, alpha=alpha),
        grid=(nb, S // tm),
        in_specs=in_specs,
        out_specs=row(D),
        out_shape=jax.ShapeDtypeStruct((B, S, D), F32),
        input_output_aliases=aliases,
        compiler_params=_cparams(2),
        name="moe_out",
    )(*args)


def _moe_dest_kernel(start_ref, e_ref, r_ref, o_ref):
    e = e_ref[...]
    acc = r_ref[...]
    for j in range(start_ref.shape[0]):
        acc = acc + jnp.where(e == j, start_ref[j], 0)
    o_ref[...] = acc


def _routing_layout(expert_idx, rank, counts):
    B, K, S = expert_idx.shape
    T = B * S
    n_experts = counts.shape[0]
    BM = MOE_ROW_BLOCK
    TK = T * K
    counts = counts.reshape(n_experts).astype(jnp.int32)
    padded = (counts + BM - 1) // BM * BM
    pad_end = jnp.cumsum(padded)
    pad_start = (pad_end - padded).astype(jnp.int32)
    ts = min(S, 2048)
    spec = pl.BlockSpec((1, K, ts), lambda b, i, tab: (b, 0, i))
    dest = pl.pallas_call(
        _moe_dest_kernel,
        grid_spec=pltpu.PrefetchScalarGridSpec(num_scalar_prefetch=1, grid=(B, S // ts),
                                               in_specs=[spec, spec], out_specs=spec),
        out_shape=jax.ShapeDtypeStruct((B, K, S), jnp.int32),
        compiler_params=_cparams(2),
        name="moe_dest",
    )(pad_start, expert_idx, rank)
    dest = jnp.swapaxes(dest, 0, 1).reshape(TK)
    n_blocks = (TK + n_experts * (BM - 1) + BM - 1) // BM
    rows = n_blocks * BM
    token = jnp.tile(jnp.arange(T, dtype=jnp.int32), K)
    row_tok = (jnp.arange(rows, dtype=jnp.int32) % T).at[dest].add(token - dest % T, mode='promise_in_bounds')
    block_start = jnp.arange(n_blocks, dtype=jnp.int32) * BM
    block_expert = jnp.minimum(jnp.sum(pad_end[None, :] <= block_start[:, None], axis=1),
                               n_experts - 1).astype(jnp.int32)
    n_used = (pad_end[-1] // BM).astype(jnp.int32).reshape(1)
    return row_tok, dest, block_expert, n_used


def kernel(x, c, positions, ada_w, ada_b, w_in, gdn_conv_w, gdn_a_log, gdn_dt_bias, gdn_norm_w, w_gdn_branch, nsa_cmp_pos_k, nsa_cmp_pos_v, nsa_cmp_k_w1, nsa_cmp_k_w2, nsa_cmp_v_w1, nsa_cmp_v_w2, w_nsa_branch, w_out, ln1_g, ln1_b, router_w, router_bias, w_sh_gate, w_sh_up, w_sh_down, w_e_gate, w_e_up, w_e_down, ln2_g, ln2_b):
    B, S, D = x.shape
    L = ada_w.shape[0]
    T = B * S
    G = NSA_GROUPS
    nh_gdn = gdn_a_log.shape[1]
    gdn_w = nh_gdn * HEAD_DIM
    nsa_w = w_nsa_branch.shape[1]
    kvw = G * HEAD_DIM
    alpha = (2.0 * L) ** 0.25
    assert gdn_w == D and nsa_w == D and S % NSA_SEL_LEN == 0

    splits = (3 * gdn_w, gdn_w, nh_gdn, nh_gdn, nsa_w, 6 * kvw, 3 * (nsa_w // HEAD_DIM), D, D)
    offs = [0]
    for s_ in splits:
        offs.append(offs[-1] + s_)
    seg = lambda i: slice(offs[i], offs[i + 1])
    w_big = jnp.concatenate([w_in[:, :, seg(0)], w_in[:, :, seg(1)], w_in[:, :, seg(4)],
                             w_in[:, :, seg(7)], w_in[:, :, seg(8)], w_in[:, :, seg(5)]], axis=-1).astype(BF16)
    n_small = splits[2] + splits[3] + splits[6]
    w_small = jnp.concatenate([w_in[:, :, seg(2)], w_in[:, :, seg(3)], w_in[:, :, seg(6)],
                               jnp.zeros((L, D, 128 - n_small), w_in.dtype)], axis=-1).astype(BF16)
    Z_BLK, Q_BLK, MA_BLK, MB_BLK = 3, 4, 5, 6
    kv0 = 7 * D
    n_big = w_big.shape[2]
    tn_big = n_big // 4 if (n_big // 4) % 128 == 0 else 128

    wgb = w_gdn_branch.astype(BF16)
    wnb = w_nsa_branch.astype(BF16)
    wob = w_out.astype(BF16)
    wsg = w_sh_gate.astype(BF16)
    wsu = w_sh_up.astype(BF16)
    wsd = w_sh_down.astype(BF16)
    rwt = jnp.swapaxes(router_w, 1, 2)
    rwt_hi = rwt.astype(BF16)
    rwt_lo = (rwt - rwt_hi.astype(F32)).astype(BF16)
    cmp_w1 = jnp.stack([nsa_cmp_k_w1, nsa_cmp_v_w1], axis=1).astype(BF16)
    cmp_w2 = jnp.stack([nsa_cmp_k_w2, nsa_cmp_v_w2], axis=1).astype(BF16)
    cmp_pos = jnp.stack([nsa_cmp_pos_k, nsa_cmp_pos_v], axis=1).reshape(L, 2, 1, NSA_CMP_LEN * HEAD_DIM)

    ada = _ada(c, ada_w, ada_b)
    cosf, sinf = _rope_tables(positions)
    NC = S // NSA_CMP_STRIDE
    last = jnp.minimum(jnp.arange(NC) * NSA_CMP_STRIDE + NSA_CMP_LEN - 1, S - 1)
    cosc = cosf[:, last]
    sinc = sinf[:, last]
    n_sel = -(-(S // NSA_SEL_LEN) // 128) * 128

    for l in range(L):
        ada_l = ada[l].reshape(B, 1, 6 * D)
        big, small, cmp_in = _proj(x, ada_l, 0, 1, w_big[l], w_small[l], tn_big, kv0, 2 * G)
        small_t = jnp.swapaxes(small[:, :, nh_gdn:2 * nh_gdn].reshape(B, S // GDN_CHUNK, GDN_CHUNK, nh_gdn), 2, 3)
        q_a, k_a, v_a = _gdn_prep(big, gdn_conv_w[l], gdn_w)
        o_a = _gdn(q_a, k_a, v_a, big, Z_BLK, small, small_t, gdn_a_log[l], gdn_dt_bias[l], gdn_norm_w[l])
        q_r, ksel_r, kwin_r = _rope_apply(big, Q_BLK, (kv0 + 2 * kvw) // kvw, (kv0 + 4 * kvw) // kvw,
                                          nsa_w, kvw, cosf, sinf)
        hb = cmp_in.reshape(B, 2 * G, NC, NSA_CMP_STRIDE * HEAD_DIM)
        kvc = _compress(hb, cmp_pos[l], cmp_w1[l], cmp_w2[l], cosc, sinc)
        o_c, sel = _cmp_attn(q_r, kvc, small, nh_gdn, n_sel)
        o_s = _sel_attn(q_r, ksel_r, big, (kv0 + 3 * kvw) // HEAD_DIM, sel, small, nh_gdn)
        o_w = _win_attn(q_r, kwin_r, big, (kv0 + 5 * kvw) // HEAD_DIM, small, nh_gdn)
        x = _mixer_out(o_a, o_c, o_s, o_w, big, MA_BLK, MB_BLK, x, ada_l, 2, ln1_g[l], ln1_b[l],
                       wgb[l], wnb[l], wob[l], alpha)
        h2, eidx, ewt, rank, counts = _moe_router(x, ada_l, 3, 4, rwt_hi[l], rwt_lo[l], router_bias[l])
        row_tok, dest, block_expert, n_used = _routing_layout(eidx, rank, counts)
        shared = _shared_expert(h2, wsg[l], wsu[l], wsd[l])
        rows = row_tok.shape[0]
        n_blocks = rows // MOE_ROW_BLOCK
        h2f = h2.reshape(T, D)
        bounds = [n_blocks * p // MOE_DISPATCH_PARTS for p in range(MOE_DISPATCH_PARTS + 1)]
        yb = None
        for lo, hi in zip(bounds[:-1], bounds[1:]):
            xs = h2f.at[row_tok[lo * MOE_ROW_BLOCK:hi * MOE_ROW_BLOCK]].get(mode='promise_in_bounds')
            yb = _moe_experts(block_expert, n_used, xs, l, w_e_gate, w_e_up, w_e_down, lo, rows, prev=yb)
        n_cp = MOE_COMBINE_PARTS if B % MOE_COMBINE_PARTS == 0 else 1
        bp = B // n_cp
        dest3 = dest.reshape(TOP_K, B, S)
        wt = jnp.swapaxes(ewt, 1, 2)
        x_new = None
        for p in range(n_cp):
            idx = dest3[:, p * bp:(p + 1) * bp].reshape(-1)
            yg = yb.at[idx].get(mode='promise_in_bounds').reshape(TOP_K, bp, S, D)
            x_new = _moe_out(x, shared, yg, wt, ada_l, 5, ln2_g[l], ln2_b[l], alpha, p * bp, prev=x_new)
        x = x_new
    return x
```

```python
import functools
import math

import jax
import jax.numpy as jnp
from jax import lax
from jax.experimental import pallas as pl
from jax.experimental.pallas import tpu as pltpu

F32 = jnp.float32
BF16 = jnp.bfloat16

HEAD_DIM = 128
GDN_CONV = 4
GDN_CHUNK = 64
NSA_GROUPS = 2
NSA_CMP_LEN = 32
NSA_CMP_STRIDE = 16
NSA_SEL_LEN = 64
NSA_SEL_TOP = 16
NSA_WINDOW = 512
ROPE_THETA = 10000.0
N_EXPERT_GROUPS = 8
TOPK_GROUPS = 4
TOP_K = 8
ROUTED_SCALE = 2.5
LN_EPS = 1e-5
NEG_INF = -1e30
SEL_FORCE = 1e6
MOE_ROW_BLOCK = 512
Q_SCALE_LOG2E = (HEAD_DIM ** -0.5) * math.log2(math.e)
MASK_BIG = 2.0 ** 100
N_ROW_PARTS = 2
GDN_PHASE1_CHUNKS = 4
WIN_SUB_TILE = 128
MOE_DISPATCH_PARTS = 4
MOE_COMBINE_PARTS = 2
_ARB = "arbitrary"


def _cparams(n_axes):
    return pltpu.CompilerParams(dimension_semantics=(_ARB,) * n_axes)


def _sigmoid(x):
    return 1.0 / (1.0 + jnp.exp(-x))


def _silu(x):
    return x * _sigmoid(x)


def _dot(a, b):
    return jnp.dot(a, b, preferred_element_type=F32)


def _dot_nt(a, b):
    return lax.dot_general(a, b, (((1,), (1,)), ((), ())), preferred_element_type=F32)


def _dot_tn(a, b):
    return lax.dot_general(a, b, (((0,), (0,)), ((), ())), preferred_element_type=F32)


def _tile_lanes(x, n):
    return jnp.concatenate([x] * n, axis=1)


def _split3(x):
    x0 = x.astype(BF16)
    r1 = x - x0.astype(F32)
    x1 = r1.astype(BF16)
    x2 = (r1 - x1.astype(F32)).astype(BF16)
    return x0, x1, x2


def _dot_sel_right(x, sel_bf16):
    x0, x1, x2 = _split3(x)
    return _dot(x0, sel_bf16) + _dot(x1, sel_bf16) + _dot(x2, sel_bf16)


def _dot_sel_left(sel_bf16, x):
    x0, x1, x2 = _split3(x)
    return _dot(sel_bf16, x0) + _dot(sel_bf16, x1) + _dot(sel_bf16, x2)


def _normalize_rows(x):
    mu = jnp.mean(x, axis=-1, keepdims=True)
    xc = x - mu
    var = jnp.mean(xc * xc, axis=-1, keepdims=True)
    return xc * lax.rsqrt(var + LN_EPS)


def _ada_kernel(c_ref, w_ref, b_ref, o_ref):
    cond = _silu(c_ref[...])
    o_ref[0] = jnp.dot(cond, w_ref[0], preferred_element_type=F32,
                       precision=lax.Precision.HIGHEST) + b_ref[0]


def _ada(c, ada_w, ada_b):
    L, D, N = ada_w.shape
    B = c.shape[0]
    tn = min(N, 1536)
    return pl.pallas_call(
        _ada_kernel,
        grid=(L, N // tn),
        in_specs=[pl.BlockSpec((B, D), lambda l, j: (0, 0)),
                  pl.BlockSpec((1, D, tn), lambda l, j: (l, 0, j)),
                  pl.BlockSpec((1, 1, tn), lambda l, j: (l, 0, j))],
        out_specs=pl.BlockSpec((1, B, tn), lambda l, j: (l, 0, j)),
        out_shape=jax.ShapeDtypeStruct((L, B, N), F32),
        compiler_params=_cparams(2),
        name="ada",
    )(c, ada_w, ada_b.reshape(L, 1, N))


def _proj_kernel(x_ref, sh_ref, sc_ref, w_ref, ws_ref, o_ref, os_ref, oc_ref, h_ref, *, head_tile, head_off):
    @pl.when(pl.program_id(2) == 0)
    def _():
        h = _normalize_rows(x_ref[0]) * (1.0 + sc_ref[0]) + sh_ref[0]
        h_ref[...] = h.astype(BF16)
        os_ref[0] = _dot(h_ref[...], ws_ref[...])

    res = _dot(h_ref[...], w_ref[...]).astype(o_ref.dtype)
    o_ref[0] = res

    @pl.when(pl.program_id(2) == head_tile)
    def _():
        for jj in range(oc_ref.shape[1]):
            oc_ref[0, jj] = res[:, head_off + jj * HEAD_DIM:head_off + (jj + 1) * HEAD_DIM]


def _proj(x, ada_l, sh_col, sc_col, w, w_small, tn, head_col, n_head_cols):
    B, S, D = x.shape
    N = w.shape[1]
    NS = w_small.shape[1]
    tm = min(S, 1024)
    head_tile, head_off = head_col // tn, head_col % tn
    assert head_off % HEAD_DIM == 0 and head_off + n_head_cols * HEAD_DIM <= tn
    return pl.pallas_call(
        functools.partial(_proj_kernel, head_tile=head_tile, head_off=head_off),
        grid=(B, S // tm, N // tn),
        in_specs=[pl.BlockSpec((1, tm, D), lambda b, i, j: (b, i, 0)),
                  pl.BlockSpec((1, 1, D), lambda b, i, j: (b, 0, sh_col)),
                  pl.BlockSpec((1, 1, D), lambda b, i, j: (b, 0, sc_col)),
                  pl.BlockSpec((D, tn), lambda b, i, j: (0, j)),
                  pl.BlockSpec((D, NS), lambda b, i, j: (0, 0))],
        out_specs=[pl.BlockSpec((1, tm, tn), lambda b, i, j: (b, i, j)),
                   pl.BlockSpec((1, tm, NS), lambda b, i, j: (b, i, 0)),
                   pl.BlockSpec((1, n_head_cols, tm, HEAD_DIM), lambda b, i, j: (b, 0, i, 0))],
        out_shape=[jax.ShapeDtypeStruct((B, S, N), BF16),
                   jax.ShapeDtypeStruct((B, S, NS), F32),
                   jax.ShapeDtypeStruct((B, n_head_cols, S, HEAD_DIM), BF16)],
        scratch_shapes=[pltpu.VMEM((tm, D), BF16)],
        compiler_params=_cparams(3),
        name="proj",
    )(x, ada_l, ada_l, w, w_small)


def _gdn_prep_kernel(x_ref, w_ref, q_ref, k_ref, v_ref, carry_ref):
    ts = x_ref.shape[1]
    width = q_ref.shape[2]
    nh = width // HEAD_DIM

    @pl.when(pl.program_id(1) == 0)
    def _():
        carry_ref[...] = jnp.zeros_like(carry_ref)

    for part, o_ref in enumerate((q_ref, k_ref, v_ref)):
        for h in range(nh):
            c0 = part * width + h * HEAD_DIM
            cols = slice(c0, c0 + HEAD_DIM)
            xx = jnp.concatenate([carry_ref[:, cols], x_ref[0, :, cols].astype(F32)], axis=0)
            w = w_ref[:, cols]
            y = xx[8:8 + ts] * w[GDN_CONV - 1:GDN_CONV]
            for kk in range(GDN_CONV - 1):
                off = 8 - (GDN_CONV - 1) + kk
                y = y + xx[off:off + ts] * w[kk:kk + 1]
            y = _silu(y)
            if part < 2:
                norm = lax.rsqrt(jnp.sum(y * y, axis=-1, keepdims=True) + 1e-6)
                if part == 0:
                    norm = norm * (HEAD_DIM ** -0.5)
                y = y * norm
            o_ref[0, :, h * HEAD_DIM:(h + 1) * HEAD_DIM] = y.astype(o_ref.dtype)
    carry_ref[...] = x_ref[0, ts - 8:ts, :].astype(F32)


def _gdn_prep(big, conv_w, width):
    B, S, _ = big.shape
    ts = min(S, 512)
    out = jax.ShapeDtypeStruct((B, S, width), BF16)
    ospec = pl.BlockSpec((1, ts, width), lambda b, i: (b, i, 0))
    return pl.pallas_call(
        _gdn_prep_kernel,
        grid=(B, S // ts),
        in_specs=[pl.BlockSpec((1, ts, 3 * width), lambda b, i: (b, i, 0)),
                  pl.BlockSpec((GDN_CONV, 3 * width), lambda b, i: (0, 0))],
        out_specs=[ospec, ospec, ospec],
        out_shape=[out, out, out],
        scratch_shapes=[pltpu.VMEM((8, 3 * width), F32)],
        compiler_params=_cparams(2),
        name="gdn_prep",
    )(big, conv_w)


def _softplus(x):
    return jnp.maximum(x, 0.0) + jnp.log(1.0 + jnp.exp(-jnp.abs(x)))


def _gdn_kernel(q_ref, k_ref, v_ref, z_ref, sm_ref, smt_ref, alog_ref, alogt_ref, dtb_ref, dtbt_ref,
                nw_ref, o_ref, state_ref, u_s, wq_s, attn_s, kd_s, dec_s):
    ts = q_ref.shape[1]
    nh = q_ref.shape[2] // HEAD_DIM
    C = GDN_CHUNK
    R = 2 * C
    npair = nh // 2
    nchunks = ts // C

    @pl.when(pl.program_id(1) == 0)
    def _():
        state_ref[...] = jnp.zeros_like(state_ref)

    ci = lax.broadcasted_iota(jnp.int32, (C, C), 0)
    cj = lax.broadcasted_iota(jnp.int32, (C, C), 1)
    tril = jnp.where(ci >= cj, 1.0, 0.0).astype(BF16)
    triu = jnp.where(cj >= ci, 1.0, 0.0).astype(BF16)
    ii = lax.broadcasted_iota(jnp.int32, (R, R), 0)
    jj = lax.broadcasted_iota(jnp.int32, (R, R), 1)
    same_head = (ii // C) == (jj // C)
    incl = same_head & (ii >= jj)
    strict = same_head & (ii > jj)
    eye = jnp.where(ii == jj, 1.0, 0.0)
    nw = nw_ref[...]

    def pair_rows(ref, rows, p):
        return jnp.concatenate([ref[0, rows, (2 * p) * HEAD_DIM:(2 * p + 1) * HEAD_DIM],
                                ref[0, rows, (2 * p + 1) * HEAD_DIM:(2 * p + 2) * HEAD_DIM]], axis=0)

    def pair_col(x, p):
        return jnp.concatenate([x[:, 2 * p:2 * p + 1], x[:, 2 * p + 1:2 * p + 2]], axis=0)

    def phase1(j, carry):
        cs = [j * GDN_PHASE1_CHUNKS + d for d in range(GDN_PHASE1_CHUNKS)]
        rows_c, beta_c, gc_c, gcr_c, glb_c = [], [], [], [], []
        for c in cs:
            rows = slice(c * C, (c + 1) * C)
            sm = sm_ref[0, rows, :]
            g = -jnp.exp(alog_ref[...]) * _softplus(sm[:, nh:2 * nh] + dtb_ref[...])
            gt = -jnp.exp(alogt_ref[...]) * _softplus(smt_ref[0, c] + dtbt_ref[...])
            gc = _dot_sel_left(tril, g)
            g_last = gc[C - 1:C, :]
            dec_s[c] = jnp.exp(g_last)
            rows_c.append(rows)
            beta_c.append(_sigmoid(sm[:, 0:nh]))
            gc_c.append(gc)
            gcr_c.append(_dot_sel_right(gt, triu))
            glb_c.append(jnp.broadcast_to(g_last, (C, nh)))
        items = [(ci, p) for ci in range(len(cs)) for p in range(npair)]
        idx = range(len(items))
        q2 = [pair_rows(q_ref, rows_c[ci], p).astype(F32) for ci, p in items]
        k2 = [pair_rows(k_ref, rows_c[ci], p).astype(F32) for ci, p in items]
        b2 = [pair_col(beta_c[ci], p) for ci, p in items]
        g_col = [pair_col(gc_c[ci], p) for ci, p in items]
        kb = [k2[i] * b2[i] for i in idx]
        kq = [_dot_nt(jnp.concatenate([kb[i], q2[i]], axis=0).astype(BF16), k2[i].astype(BF16))
              for i in idx]
        decay = []
        for i, (ci, p) in enumerate(items):
            gcr = gcr_c[ci]
            g_row = jnp.concatenate([gcr[2 * p:2 * p + 1, :], gcr[2 * p + 1:2 * p + 2, :]], axis=1)
            decay.append(jnp.where(incl, jnp.exp(jnp.where(incl, g_col[i] - g_row, 0.0)), 0.0))
        a = [jnp.where(strict, kq[i][:R] * decay[i], 0.0) for i in idx]
        for i, (ci, p) in enumerate(items):
            attn_s[cs[ci], p] = (kq[i][R:] * decay[i]).astype(BF16)
        x = [eye - a[i] for i in idx]
        pw = a
        n = 2
        while n < C:
            pw16 = [pw[i].astype(BF16) for i in idx]
            pw = [_dot(pw16[i], pw16[i]) for i in idx]
            x = [x[i] + _dot(x[i].astype(BF16), pw[i].astype(BF16)) for i in idx]
            n *= 2
        eg = [jnp.exp(g_col[i]) for i in idx]
        sol = []
        for i, (ci, p) in enumerate(items):
            v2 = pair_rows(v_ref, rows_c[ci], p).astype(F32)
            r = jnp.concatenate([v2 * b2[i], kb[i] * eg[i]], axis=1)
            sol.append(_dot(x[i].astype(BF16), r.astype(BF16)))
        for i, (ci, p) in enumerate(items):
            c = cs[ci]
            u_s[c, p] = sol[i][:, :HEAD_DIM]
            w = sol[i][:, HEAD_DIM:]
            qg = q2[i] * eg[i]
            for e in range(2):
                wq_s[c, 2 * p + e] = jnp.concatenate([w[e * C:(e + 1) * C], qg[e * C:(e + 1) * C]],
                                                     axis=0).astype(BF16)
            kd_s[c, p] = (k2[i] * jnp.exp(pair_col(glb_c[ci], p) - g_col[i])).astype(BF16)
        return carry

    def phase2(c):
        rows = slice(c * C, (c + 1) * C)
        dec = dec_s[c]
        res = [_dot(wq_s[c, h], state_ref[h].astype(BF16)) for h in range(nh)]
        for p in range(npair):
            ws = jnp.concatenate([res[2 * p][:C], res[2 * p + 1][:C]], axis=0)
            qs = jnp.concatenate([res[2 * p][C:], res[2 * p + 1][C:]], axis=0)
            v_new = u_s[c, p] - ws
            v16 = v_new.astype(BF16)
            o2 = qs + _dot(attn_s[c, p], v16)
            kd = kd_s[c, p]
            for e in range(2):
                h = 2 * p + e
                part = slice(e * C, (e + 1) * C)
                state_ref[h] = state_ref[h] * dec[:, h:h + 1] + _dot_tn(kd[part], v16[part])
                o = o2[part]
                o = o * lax.rsqrt(jnp.mean(o * o, axis=-1, keepdims=True) + 1e-6) * nw
                cols = slice(h * HEAD_DIM, (h + 1) * HEAD_DIM)
                z = z_ref[0, rows, cols].astype(F32)
                o_ref[0, rows, cols] = (o * _silu(z)).astype(o_ref.dtype)

    def phase2_group(j):
        for d in range(GDN_PHASE1_CHUNKS):
            phase2(j * GDN_PHASE1_CHUNKS + d)

    assert nchunks % GDN_PHASE1_CHUNKS == 0
    ngroups = nchunks // GDN_PHASE1_CHUNKS
    phase1(0, 0)
    for j in range(1, ngroups):
        phase2_group(j - 1)
        phase1(j, 0)
    phase2_group(ngroups - 1)


def _gdn(q, k, v, big, z_blk, small, small_t, a_log, dt_bias, norm_w):
    B, S, W = q.shape
    nh = W // HEAD_DIM
    assert nh % 2 == 0 and 2 * GDN_CHUNK == HEAD_DIM
    ts = min(S, 512)
    nc = ts // GDN_CHUNK
    spec = pl.BlockSpec((1, ts, W), lambda b, i: (b, i, 0))
    full = lambda shape: pl.BlockSpec(shape, lambda b, i: (0,) * len(shape))
    return pl.pallas_call(
        _gdn_kernel,
        grid=(B, S // ts),
        in_specs=[spec, spec, spec,
                  pl.BlockSpec((1, ts, W), lambda b, i: (b, i, z_blk)),
                  pl.BlockSpec((1, ts, small.shape[2]), lambda b, i: (b, i, 0)),
                  pl.BlockSpec((1, ts // GDN_CHUNK, nh, GDN_CHUNK), lambda b, i: (b, i, 0, 0)),
                  full((1, nh)), full((nh, 1)), full((1, nh)), full((nh, 1)),
                  full((1, HEAD_DIM))],
        out_specs=spec,
        out_shape=jax.ShapeDtypeStruct((B, S, W), BF16),
        scratch_shapes=[pltpu.VMEM((nh, HEAD_DIM, HEAD_DIM), F32),
                        pltpu.VMEM((nc, nh // 2, 2 * GDN_CHUNK, HEAD_DIM), F32),
                        pltpu.VMEM((nc, nh, 2 * GDN_CHUNK, HEAD_DIM), BF16),
                        pltpu.VMEM((nc, nh // 2, 2 * GDN_CHUNK, 2 * GDN_CHUNK), BF16),
                        pltpu.VMEM((nc, nh // 2, 2 * GDN_CHUNK, HEAD_DIM), BF16),
                        pltpu.VMEM((nc, 1, nh), F32)],
        compiler_params=_cparams(2),
        name="gdn",
    )(q, k, v, big, small, small_t, a_log.reshape(1, nh), a_log.reshape(nh, 1),
      dt_bias.reshape(1, nh), dt_bias.reshape(nh, 1), norm_w.reshape(1, HEAD_DIM))


def _rope_table_kernel(pos_ref, invf_ref, sign_ref, cos_ref, sin_ref):
    ang = pos_ref[0].astype(F32) * invf_ref[...]
    cos_ref[0] = jnp.cos(ang)
    sin_ref[0] = jnp.sin(ang) * sign_ref[...]


def _rope_tables(positions):
    B, S = positions.shape
    half = HEAD_DIM // 2
    inv = ROPE_THETA ** (-jnp.arange(half, dtype=F32) / half)
    invf = jnp.concatenate([inv, inv]).reshape(1, HEAD_DIM)
    sign = jnp.concatenate([-jnp.ones((half,), F32), jnp.ones((half,), F32)]).reshape(1, HEAD_DIM)
    ts = min(S, 1024)
    out = jax.ShapeDtypeStruct((B, S, HEAD_DIM), F32)
    ospec = pl.BlockSpec((1, ts, HEAD_DIM), lambda b, i: (b, i, 0))
    return pl.pallas_call(
        _rope_table_kernel,
        grid=(B, S // ts),
        in_specs=[pl.BlockSpec((1, ts, 1), lambda b, i: (b, i, 0)),
                  pl.BlockSpec((1, HEAD_DIM), lambda b, i: (0, 0)),
                  pl.BlockSpec((1, HEAD_DIM), lambda b, i: (0, 0))],
        out_specs=[ospec, ospec],
        out_shape=[out, out],
        compiler_params=_cparams(2),
        name="rope_table",
    )(positions.reshape(B, S, 1), invf, sign)


def _rope_rows(x, cosf, sinf):
    return x * cosf + pltpu.roll(x, HEAD_DIM // 2, 1) * sinf


def _rope_apply_kernel(q_ref, ks_ref, kw_ref, cos_ref, sin_ref, qo_ref, kso_ref, kwo_ref):
    cosf = cos_ref[0]
    sinf = sin_ref[0]
    for src, dst, mult in ((q_ref, qo_ref, Q_SCALE_LOG2E), (ks_ref, kso_ref, None), (kw_ref, kwo_ref, None)):
        for h in range(src.shape[2] // HEAD_DIM):
            cols = slice(h * HEAD_DIM, (h + 1) * HEAD_DIM)
            r = _rope_rows(src[0, :, cols].astype(F32), cosf, sinf)
            if mult is not None:
                r = r * mult
            dst[0, :, cols] = r.astype(dst.dtype)


def _rope_apply(big, q_blk, ksel_blk, kwin_blk, wq, wkv, cosf, sinf):
    B, S, _ = big.shape
    ts = min(S, 512)
    tab = pl.BlockSpec((1, ts, HEAD_DIM), lambda b, i: (b, i, 0))
    return pl.pallas_call(
        _rope_apply_kernel,
        grid=(B, S // ts),
        in_specs=[pl.BlockSpec((1, ts, wq), lambda b, i: (b, i, q_blk)),
                  pl.BlockSpec((1, ts, wkv), lambda b, i: (b, i, ksel_blk)),
                  pl.BlockSpec((1, ts, wkv), lambda b, i: (b, i, kwin_blk)),
                  tab, tab],
        out_specs=[pl.BlockSpec((1, ts, wq), lambda b, i: (b, i, 0)),
                   pl.BlockSpec((1, ts, wkv), lambda b, i: (b, i, 0)),
                   pl.BlockSpec((1, ts, wkv), lambda b, i: (b, i, 0))],
        out_shape=[jax.ShapeDtypeStruct((B, S, wq), BF16),
                   jax.ShapeDtypeStruct((B, S, wkv), BF16),
                   jax.ShapeDtypeStruct((B, S, wkv), BF16)],
        compiler_params=_cparams(2),
        name="rope_apply",
    )(big, big, big, cosf, sinf)


def _compress_kernel(hb_ref, pos_ref, w1_ref, w2_ref, cos_ref, sin_ref, o_ref):
    hb = hb_ref[0, 0]
    w1 = w1_ref[0]
    half = hb.shape[1]
    p0 = _dot(hb, w1[:half])
    p1 = _dot(hb, w1[half:])
    nc = p0.shape[0]
    pos8 = jnp.broadcast_to(pos_ref[0], (8, 2 * half)).astype(BF16)
    pb = _dot(pos8, w1)[0:1]
    pre = p0 + pltpu.roll(p1, nc - 1, 0) + pb
    out = _dot(_silu(pre).astype(BF16), w2_ref[0])
    roped = _rope_rows(out, cos_ref[0], sin_ref[0])
    is_key = pl.program_id(1) < NSA_GROUPS
    o_ref[0, 0] = jnp.where(is_key, roped, out).astype(o_ref.dtype)


def _compress(hb, pos_flat, w1, w2, cosc, sinc):
    B, J, NC, HW = hb.shape
    G = NSA_GROUPS
    return pl.pallas_call(
        _compress_kernel,
        grid=(B, J),
        in_specs=[pl.BlockSpec((1, 1, NC, HW), lambda b, j: (b, j, 0, 0)),
                  pl.BlockSpec((1, 1, 2 * HW), lambda b, j: (j // G, 0, 0)),
                  pl.BlockSpec((1, 2 * HW, HEAD_DIM), lambda b, j: (j // G, 0, 0)),
                  pl.BlockSpec((1, HEAD_DIM, HEAD_DIM), lambda b, j: (j // G, 0, 0)),
                  pl.BlockSpec((1, NC, HEAD_DIM), lambda b, j: (b, 0, 0)),
                  pl.BlockSpec((1, NC, HEAD_DIM), lambda b, j: (b, 0, 0))],
        out_specs=pl.BlockSpec((1, 1, NC, HEAD_DIM), lambda b, j: (b, j, 0, 0)),
        out_shape=jax.ShapeDtypeStruct((B, J, NC, HEAD_DIM), BF16),
        compiler_params=_cparams(2),
        name="nsa_compress",
    )(hb, pos_flat, w1, w2, cosc, sinc)


def _gate_col(g_ref, nh_gdn, group, hpg, h, branch):
    col = None
    for gi in range(NSA_GROUPS):
        c = 2 * nh_gdn + (gi * hpg + h) * 3 + branch
        cand = g_ref[0, :, c:c + 1]
        col = cand if col is None else jnp.where(group == gi, cand, col)
    return _sigmoid(col)


def _cmp_attn_kernel(q_ref, kc_ref, vc_ref, sm_ref, o_ref, sel_ref, *, nh_gdn, hpg):
    tq = q_ref.shape[1]
    nc = kc_ref.shape[2]
    nb = sel_ref.shape[3]
    g = pl.program_id(1)
    t0 = pl.program_id(2) * tq
    t_nc = t0 + lax.broadcasted_iota(jnp.int32, (tq, nc), 0)
    n_nc = lax.broadcasted_iota(jnp.int32, (tq, nc), 1)
    valid = (n_nc * NSA_CMP_STRIDE + (NSA_CMP_LEN - 1)) <= t_nc
    t_col = t0 + lax.broadcasted_iota(jnp.int32, (tq, 1), 0)
    has_valid = jnp.where(t_col >= NSA_CMP_LEN - 1, 1.0, 0.0)
    kc = kc_ref[0, 0]
    vc = vc_ref[0, 0]
    psum = jnp.zeros((tq, nc), F32)
    for h in range(hpg):
        cols = slice(h * HEAD_DIM, (h + 1) * HEAD_DIM)
        s = jnp.where(valid, _dot_nt(q_ref[0, :, cols], kc), NEG_INF)
        e = jnp.exp2(s - jnp.max(s, axis=-1, keepdims=True))
        p = e * (has_valid / jnp.sum(e, axis=-1, keepdims=True))
        gate = _gate_col(sm_ref, nh_gdn, g, hpg, h, 0)
        o_ref[0, :, cols] = (_dot(p.astype(BF16), vc) * gate).astype(o_ref.dtype)
        psum = psum + p
    sj = lax.broadcasted_iota(jnp.int32, (nb, nc), 0) * NSA_SEL_LEN
    cn = lax.broadcasted_iota(jnp.int32, (nb, nc), 1) * NSA_CMP_STRIDE
    overlap_t = jnp.where((cn <= sj + (NSA_SEL_LEN - 1)) & (cn + (NSA_CMP_LEN - 1) >= sj), 1.0, 0.0).astype(BF16)
    p0, p1, p2 = _split3(psum)
    imp = _dot_nt(overlap_t, p0) + _dot_nt(overlap_t, p1) + _dot_nt(overlap_t, p2)
    t_nb = t0 + lax.broadcasted_iota(jnp.int32, (nb, tq), 1)
    blk = lax.broadcasted_iota(jnp.int32, (nb, tq), 0)
    cur = t_nb // NSA_SEL_LEN
    forced = (blk == 0) | (blk == cur) | (blk == cur - 1)
    score = jnp.where(forced, SEL_FORCE, jnp.where(blk * NSA_SEL_LEN <= t_nb, imp, -1.0))
    blk_f = blk.astype(F32)
    for _ in range(min(NSA_SEL_TOP, nb)):
        m = jnp.max(score, axis=0, keepdims=True)
        first = jnp.min(jnp.where(score == m, blk_f, float(nb)), axis=0, keepdims=True)
        score = jnp.where(blk_f == first, -jnp.inf, score)
    sel_ref[0, 0] = jnp.where(score == -jnp.inf, 1.0, 0.0).T.astype(sel_ref.dtype)


def _cmp_attn(q_r, kvc, small, nh_gdn, n_sel):
    B, S, WQ = q_r.shape
    G = NSA_GROUPS
    hpg = WQ // HEAD_DIM // G
    NC = kvc.shape[2]
    tq = min(S, 512)
    return pl.pallas_call(
        functools.partial(_cmp_attn_kernel, nh_gdn=nh_gdn, hpg=hpg),
        grid=(B, G, S // tq),
        in_specs=[pl.BlockSpec((1, tq, hpg * HEAD_DIM), lambda b, g, i: (b, i, g)),
                  pl.BlockSpec((1, 1, NC, HEAD_DIM), lambda b, g, i: (b, g, 0, 0)),
                  pl.BlockSpec((1, 1, NC, HEAD_DIM), lambda b, g, i: (b, G + g, 0, 0)),
                  pl.BlockSpec((1, tq, small.shape[2]), lambda b, g, i: (b, i, 0))],
        out_specs=[pl.BlockSpec((1, tq, hpg * HEAD_DIM), lambda b, g, i: (b, i, g)),
                   pl.BlockSpec((1, 1, tq, n_sel), lambda b, g, i: (b, g, i, 0))],
        out_shape=[jax.ShapeDtypeStruct((B, S, WQ), BF16),
                   jax.ShapeDtypeStruct((B, G, S, n_sel), BF16)],
        compiler_params=_cparams(3),
        name="nsa_cmp_attn",
    )(q_r, kvc, kvc, small)


def _sel_attn_kernel(q_ref, k_ref, ext_ref, v_ref, sel_ref, sm_ref, o_ref, qx_ref, m_ref, acc_ref, p_ref, a_ref,
                     *, nh_gdn, hpg, tk):
    tq = q_ref.shape[1]
    nb = sel_ref.shape[3]
    g = pl.program_id(1)
    t0 = pl.program_id(2) * tq
    assert tk % tq == 0
    unsel = (sel_ref[0, 0].astype(F32) - 1.0).astype(BF16)
    for h in range(hpg):
        qx_ref[h * tq:(h + 1) * tq, 0:HEAD_DIM] = q_ref[0, :, h * HEAD_DIM:(h + 1) * HEAD_DIM]
        qx_ref[h * tq:(h + 1) * tq, HEAD_DIM:HEAD_DIM + nb] = unsel
    m_ref[...] = jnp.full_like(m_ref, NEG_INF)
    acc_ref[...] = jnp.zeros_like(acc_ref)
    p_ref[...] = jnp.zeros_like(p_ref)
    a_ref[...] = jnp.ones_like(a_ref)
    ones = jnp.ones((tk, HEAD_DIM), BF16)
    pr = hpg * tq // N_ROW_PARTS
    parts = [slice(part * pr, (part + 1) * pr) for part in range(N_ROW_PARTS)]

    def apply_values(kt):
        k0 = pl.multiple_of(kt * tk, tk)
        vx = jnp.concatenate([v_ref[0, pl.ds(k0, tk), :], ones], axis=1)
        for rows in parts:
            acc_ref[rows, :] = _tile_lanes(a_ref[rows, :], 2) * acc_ref[rows, :] + _dot(p_ref[rows, :], vx)

    def scores(kt, causal):
        k0 = pl.multiple_of(kt * tk, tk)
        kx = jnp.concatenate([k_ref[0, pl.ds(k0, tk), :], ext_ref[pl.ds(k0, tk), :]], axis=1)
        if causal:
            t_row = t0 + lax.broadcasted_iota(jnp.int32, (tq, tk), 0)
            key = k0 + lax.broadcasted_iota(jnp.int32, (tq, tk), 1)
            keep = jnp.concatenate([key <= t_row] * (hpg // N_ROW_PARTS), axis=0)
        for rows in parts:
            s = _dot_nt(qx_ref[rows, :], kx)
            if causal:
                s = jnp.where(keep, s, NEG_INF)
            m_old = m_ref[rows, :]
            m_new = jnp.maximum(m_old, jnp.max(s, axis=-1, keepdims=True))
            a_ref[rows, :] = jnp.exp2(m_old - m_new)
            p_ref[rows, :] = jnp.exp2(s - _tile_lanes(m_new, tk // HEAD_DIM)).astype(BF16)
            m_ref[rows, :] = m_new

    kt_diag = t0 // tk

    def body(kt, carry):
        apply_values(jnp.maximum(kt - 1, 0))
        scores(kt, False)
        return carry

    lax.fori_loop(0, kt_diag, body, 0)
    apply_values(jnp.maximum(kt_diag - 1, 0))
    scores(kt_diag, True)
    apply_values(kt_diag)
    for h in range(hpg):
        rows = slice(h * tq, (h + 1) * tq)
        gate = _gate_col(sm_ref, nh_gdn, g, hpg, h, 1)
        out = acc_ref[rows, 0:HEAD_DIM] / acc_ref[rows, HEAD_DIM:2 * HEAD_DIM]
        o_ref[0, :, h * HEAD_DIM:(h + 1) * HEAD_DIM] = (out * gate).astype(o_ref.dtype)


def _sel_attn(q_r, ksel_r, big, vsel_blk, sel, small, nh_gdn):
    B, S, WQ = q_r.shape
    G = NSA_GROUPS
    hpg = WQ // HEAD_DIM // G
    nbp = sel.shape[3]
    tq = min(S, 512)
    tk = min(S, 512)
    rows = hpg * tq
    ext = jnp.where(jnp.arange(S)[:, None] // NSA_SEL_LEN == jnp.arange(nbp)[None, :], MASK_BIG, 0.0).astype(BF16)
    return pl.pallas_call(
        functools.partial(_sel_attn_kernel, nh_gdn=nh_gdn, hpg=hpg, tk=tk),
        grid=(B, G, S // tq),
        in_specs=[pl.BlockSpec((1, tq, hpg * HEAD_DIM), lambda b, g, i: (b, i, g)),
                  pl.BlockSpec((1, S, HEAD_DIM), lambda b, g, i: (b, 0, g)),
                  pl.BlockSpec((S, nbp), lambda b, g, i: (0, 0)),
                  pl.BlockSpec((1, S, HEAD_DIM), lambda b, g, i: (b, 0, vsel_blk + g)),
                  pl.BlockSpec((1, 1, tq, nbp), lambda b, g, i: (b, g, i, 0)),
                  pl.BlockSpec((1, tq, small.shape[2]), lambda b, g, i: (b, i, 0))],
        out_specs=pl.BlockSpec((1, tq, hpg * HEAD_DIM), lambda b, g, i: (b, i, g)),
        out_shape=jax.ShapeDtypeStruct((B, S, WQ), BF16),
        scratch_shapes=[pltpu.VMEM((rows, HEAD_DIM + nbp), BF16),
                        pltpu.VMEM((rows, HEAD_DIM), F32),
                        pltpu.VMEM((rows, 2 * HEAD_DIM), F32),
                        pltpu.VMEM((rows, tk), BF16),
                        pltpu.VMEM((rows, HEAD_DIM), F32)],
        compiler_params=_cparams(3),
        name="nsa_sel_attn",
    )(q_r, ksel_r, ext, big, sel, small)


def _win_attn_kernel(q_ref, k_ref, v_ref, sm_ref, o_ref, *, nh_gdn, hpg):
    tq = q_ref.shape[1]
    ts = min(tq, WIN_SUB_TILE)
    span = ts + NSA_WINDOW
    g = pl.program_id(1)
    for sub in range(tq // ts):
        qrows = slice(sub * ts, (sub + 1) * ts)
        t0 = pl.program_id(2) * tq + sub * ts
        k0 = pl.multiple_of(jnp.maximum(t0 - NSA_WINDOW, 0), ts)
        k = k_ref[0, pl.ds(k0, span), :]
        v = v_ref[0, pl.ds(k0, span), :]
        t_row = t0 + lax.broadcasted_iota(jnp.int32, (ts, span), 0)
        spos = k0 + lax.broadcasted_iota(jnp.int32, (ts, span), 1)
        mask = (spos <= t_row) & (spos > t_row - NSA_WINDOW)
        vx = jnp.concatenate([v, jnp.ones((span, HEAD_DIM), BF16)], axis=1)
        q4 = jnp.concatenate([q_ref[0, qrows, h * HEAD_DIM:(h + 1) * HEAD_DIM] for h in range(hpg)], axis=0)
        s = jnp.where(jnp.concatenate([mask] * hpg, axis=0), _dot_nt(q4, k), NEG_INF)
        e = jnp.exp2(s - jnp.max(s, axis=-1, keepdims=True))
        acc = _dot(e.astype(BF16), vx)
        out = acc[:, :HEAD_DIM] / acc[:, HEAD_DIM:]
        for h in range(hpg):
            gate = _gate_col(sm_ref, nh_gdn, g, hpg, h, 2)[qrows]
            o_ref[0, qrows, h * HEAD_DIM:(h + 1) * HEAD_DIM] = (out[h * ts:(h + 1) * ts] * gate).astype(o_ref.dtype)


def _win_attn(q_r, kwin_r, big, vwin_blk, small, nh_gdn):
    B, S, WQ = q_r.shape
    G = NSA_GROUPS
    hpg = WQ // HEAD_DIM // G
    tq = min(S, 512)
    assert S >= tq + NSA_WINDOW
    return pl.pallas_call(
        functools.partial(_win_attn_kernel, nh_gdn=nh_gdn, hpg=hpg),
        grid=(B, G, S // tq),
        in_specs=[pl.BlockSpec((1, tq, hpg * HEAD_DIM), lambda b, g, i: (b, i, g)),
                  pl.BlockSpec((1, S, HEAD_DIM), lambda b, g, i: (b, 0, g)),
                  pl.BlockSpec((1, S, HEAD_DIM), lambda b, g, i: (b, 0, vwin_blk + g)),
                  pl.BlockSpec((1, tq, small.shape[2]), lambda b, g, i: (b, i, 0))],
        out_specs=pl.BlockSpec((1, tq, hpg * HEAD_DIM), lambda b, g, i: (b, i, g)),
        out_shape=jax.ShapeDtypeStruct((B, S, WQ), BF16),
        compiler_params=_cparams(3),
        name="nsa_win_attn",
    )(q_r, kwin_r, big, small)


def _post_norm(x, y, gt, g, b, alpha):
    r = alpha * x + (1.0 + gt) * y
    return _normalize_rows(r) * g + b


def _mixer_out_kernel(oa_ref, oc_ref, os_ref, ow_ref, ma_ref, mb_ref, x_ref, gt_ref, lg_ref, lb_ref,
                      wg_ref, wn_ref, wo_ref, o_ref, *, alpha):
    y_a = _dot(oa_ref[0], wg_ref[...])
    o_b = oc_ref[0].astype(F32) + os_ref[0].astype(F32) + ow_ref[0].astype(F32)
    y_b = _dot(o_b.astype(BF16), wn_ref[...])
    mixed = _sigmoid(ma_ref[0].astype(F32)) * y_a + _sigmoid(mb_ref[0].astype(F32)) * y_b
    y = _dot(mixed.astype(BF16), wo_ref[...])
    o_ref[0] = _post_norm(x_ref[0], y, gt_ref[0], lg_ref[...], lb_ref[...], alpha)


def _mixer_out(o_a, o_c, o_s, o_w, big, ma_blk, mb_blk, x, ada_l, gt_col, ln_g, ln_b, wg, wn, wo, alpha):
    B, S, D = x.shape
    tm = min(S, 512)
    row = lambda blk: pl.BlockSpec((1, tm, D), lambda b, i: (b, i, blk))
    full = lambda shape: pl.BlockSpec(shape, lambda b, i: (0,) * len(shape))
    return pl.pallas_call(
        functools.partial(_mixer_out_kernel, alpha=alpha),
        grid=(B, S // tm),
        in_specs=[row(0), row(0), row(0), row(0), row(ma_blk), row(mb_blk), row(0),
                  pl.BlockSpec((1, 1, D), lambda b, i: (b, 0, gt_col)),
                  full((1, D)), full((1, D)), full((D, D)), full((D, D)), full((D, D))],
        out_specs=row(0),
        out_shape=jax.ShapeDtypeStruct((B, S, D), F32),
        compiler_params=_cparams(2),
        name="mixer_out",
    )(o_a, o_c, o_s, o_w, big, big, x, ada_l, ln_g.reshape(1, D), ln_b.reshape(1, D), wg, wn, wo)


def _moe_router_kernel(x_ref, sh_ref, sc_ref, rwh_ref, rwl_ref, rb_ref,
                       h_ref, idx_ref, wt_ref, rank_ref, cnt_ref):
    h = _normalize_rows(x_ref[0]) * (1.0 + sc_ref[0]) + sh_ref[0]
    h16 = h.astype(BF16)
    h_ref[0] = h16
    h_lo = (h - h16.astype(F32)).astype(BF16)
    rw_hi = rwh_ref[...]
    logits = _dot_nt(rw_hi, h16) + _dot_nt(rwl_ref[...], h16) + _dot_nt(rw_hi, h_lo)
    scores = _sigmoid(logits)
    biased = scores + rb_ref[...]
    E, tm = scores.shape
    NG = N_EXPERT_GROUPS
    per = E // NG
    b3 = biased.reshape(NG, per, tm)
    in_grp = lax.broadcasted_iota(jnp.int32, (NG, per, tm), 1).astype(F32)
    m1 = jnp.max(b3, axis=1, keepdims=True)
    first = jnp.min(jnp.where(b3 == m1, in_grp, float(per)), axis=1, keepdims=True)
    m2 = jnp.max(jnp.where(in_grp == first, -jnp.inf, b3), axis=1, keepdims=True)
    gscore = (m1 + m2).reshape(NG, tm)
    g_id = lax.broadcasted_iota(jnp.int32, (NG, tm), 0).astype(F32)
    allowed = jnp.zeros((NG, tm), F32)
    for _ in range(TOPK_GROUPS):
        m = jnp.max(gscore, axis=0, keepdims=True)
        first_g = jnp.min(jnp.where(gscore == m, g_id, float(NG)), axis=0, keepdims=True)
        pick = g_id == first_g
        allowed = jnp.where(pick, 1.0, allowed)
        gscore = jnp.where(pick, -jnp.inf, gscore)
    cand = jnp.where(allowed.reshape(NG, 1, tm) > 0.5, b3, NEG_INF).reshape(E, tm)
    e_id = lax.broadcasted_iota(jnp.int32, (E, tm), 0).astype(F32)
    idx_rows, wt_rows, picks = [], [], []
    wsum = jnp.zeros((1, tm), F32)
    chosen = jnp.zeros((E, tm), F32)
    for kk in range(TOP_K):
        m = jnp.max(cand, axis=0, keepdims=True)
        first_e = jnp.min(jnp.where(cand == m, e_id, float(E)), axis=0, keepdims=True)
        pick = e_id == first_e
        w = jnp.sum(jnp.where(pick, scores, 0.0), axis=0, keepdims=True)
        cand = jnp.where(pick, -jnp.inf, cand)
        chosen = jnp.where(pick, 1.0, chosen)
        idx_rows.append(first_e)
        wt_rows.append(w)
        picks.append(pick)
        wsum = wsum + w
    idx_ref[0] = jnp.concatenate(idx_rows, axis=0).astype(jnp.int32)
    wt_ref[0] = jnp.concatenate(wt_rows, axis=0) / wsum * ROUTED_SCALE
    @pl.when((pl.program_id(0) == 0) & (pl.program_id(1) == 0))
    def _():
        cnt_ref[...] = jnp.zeros_like(cnt_ref)

    si = lax.broadcasted_iota(jnp.int32, (tm, tm), 0)
    ti = lax.broadcasted_iota(jnp.int32, (tm, tm), 1)
    before = jnp.where(si < ti, 1.0, 0.0).astype(BF16)
    prior = _dot(chosen.astype(BF16), before) + cnt_ref[...]
    rank_ref[0] = jnp.concatenate([jnp.sum(jnp.where(pk, prior, 0.0), axis=0, keepdims=True) for pk in picks],
                                  axis=0).astype(jnp.int32)
    cnt_ref[...] = cnt_ref[...] + jnp.sum(chosen, axis=1, keepdims=True)


def _shared_expert_kernel(h_ref, wsg_ref, wsu_ref, wsd_ref, o_ref):
    h16 = h_ref[0]
    act = _silu(_dot(h16, wsg_ref[...])) * _dot(h16, wsu_ref[...])
    o_ref[0] = _dot(act.astype(BF16), wsd_ref[...])


def _shared_expert(h, wsg, wsu, wsd):
    B, S, D = h.shape
    FF = wsg.shape[1]
    tm = min(S, 1024)
    row = pl.BlockSpec((1, tm, D), lambda b, i: (b, i, 0))
    full = lambda shape: pl.BlockSpec(shape, lambda b, i: (0,) * len(shape))
    return pl.pallas_call(
        _shared_expert_kernel,
        grid=(B, S // tm),
        in_specs=[row, full((D, FF)), full((D, FF)), full((FF, D))],
        out_specs=row,
        out_shape=jax.ShapeDtypeStruct((B, S, D), F32),
        compiler_params=_cparams(2),
        name="moe_shared",
    )(h, wsg, wsu, wsd)


def _moe_router(x, ada_l, sh_col, sc_col, rwt_hi, rwt_lo, router_b):
    B, S, D = x.shape
    E = rwt_hi.shape[0]
    tm = min(S, 512)
    row = lambda w: pl.BlockSpec((1, tm, w), lambda b, i: (b, i, 0))
    krow = pl.BlockSpec((1, TOP_K, tm), lambda b, i: (b, 0, i))
    full = lambda shape: pl.BlockSpec(shape, lambda b, i: (0,) * len(shape))
    return pl.pallas_call(
        _moe_router_kernel,
        grid=(B, S // tm),
        in_specs=[row(D),
                  pl.BlockSpec((1, 1, D), lambda b, i: (b, 0, sh_col)),
                  pl.BlockSpec((1, 1, D), lambda b, i: (b, 0, sc_col)),
                  full((E, D)), full((E, D)), full((E, 1))],
        out_specs=[row(D), krow, krow, krow, full((E, 1))],
        out_shape=[jax.ShapeDtypeStruct((B, S, D), BF16),
                   jax.ShapeDtypeStruct((B, TOP_K, S), jnp.int32),
                   jax.ShapeDtypeStruct((B, TOP_K, S), F32),
                   jax.ShapeDtypeStruct((B, TOP_K, S), jnp.int32),
                   jax.ShapeDtypeStruct((E, 1), F32)],
        compiler_params=_cparams(2),
        name="moe_router",
    )(x, ada_l, ada_l, rwt_hi, rwt_lo, router_b.reshape(E, 1))


def _moe_experts_kernel(be_ref, nu_ref, xs_ref, wg_ref, wu_ref, wd_ref, *rest, first_block):
    o_ref = rest[-1]

    used = pl.program_id(0) + first_block < nu_ref[0]

    @pl.when(used)
    def _():
        xb = xs_ref[...]
        act = _silu(_dot(xb, wg_ref[0, 0].astype(BF16))) * _dot(xb, wu_ref[0, 0].astype(BF16))
        o_ref[...] = _dot(act.astype(BF16), wd_ref[0, 0].astype(BF16)).astype(o_ref.dtype)

    @pl.when(jnp.logical_not(used))
    def _():
        o_ref[...] = jnp.zeros_like(o_ref)


def _moe_experts(block_expert, n_used, xs, layer, wg, wu, wd, first_block, total_rows, prev=None):
    rows, D = xs.shape
    BM = MOE_ROW_BLOCK
    FF = wg.shape[3]
    fb = first_block
    in_specs = [pl.BlockSpec((BM, D), lambda i, be, nu: (i, 0)),
                pl.BlockSpec((1, 1, D, FF), lambda i, be, nu: (layer, be[i + fb], 0, 0)),
                pl.BlockSpec((1, 1, D, FF), lambda i, be, nu: (layer, be[i + fb], 0, 0)),
                pl.BlockSpec((1, 1, FF, D), lambda i, be, nu: (layer, be[i + fb], 0, 0))]
    args = [block_expert, n_used, xs, wg, wu, wd]
    aliases = {}
    if prev is not None:
        in_specs.append(pl.BlockSpec(memory_space=pl.ANY))
        args.append(prev)
        aliases = {len(args) - 1: 0}
    grid_spec = pltpu.PrefetchScalarGridSpec(
        num_scalar_prefetch=2,
        grid=(rows // BM,),
        in_specs=in_specs,
        out_specs=pl.BlockSpec((BM, D), lambda i, be, nu: (i + fb, 0)),
    )
    return pl.pallas_call(
        functools.partial(_moe_experts_kernel, first_block=fb),
        grid_spec=grid_spec,
        out_shape=jax.ShapeDtypeStruct((total_rows, D), BF16),
        input_output_aliases=aliases,
        compiler_params=_cparams(1),
        name="moe_experts",
    )(*args)


def _moe_out_kernel(x_ref, shared_ref, yg_ref, wt_ref, gt_ref, lg_ref, lb_ref, o_ref, *, alpha):
    y = shared_ref[0]
    wt = wt_ref[0]
    for kk in range(TOP_K):
        y = y + wt[:, kk:kk + 1] * yg_ref[kk, 0].astype(F32)
    o_ref[0] = _post_norm(x_ref[0], y, gt_ref[0], lg_ref[...], lb_ref[...], alpha)


def _moe_out(x, shared, yg, wt, ada_l, gt_col, ln_g, ln_b, alpha, first_b):
    B, S, D = x.shape
    nb = yg.shape[1]
    fb = first_b
    tm = min(S, 256)
    row = lambda w: pl.BlockSpec((1, tm, w), lambda b, i: (b + fb, i, 0))
    full = lambda shape: pl.BlockSpec(shape, lambda b, i: (0,) * len(shape))
    return pl.pallas_call(
        functools.partial(_moe_out_kernel, alpha=alpha),
        grid=(nb, S // tm),
        in_specs=[row(D), row(D),
                  pl.BlockSpec((TOP_K, 1, tm, D), lambda b, i: (0, b, i, 0)),
                  row(wt.shape[2]),
                  pl.BlockSpec((1, 1, D), lambda b, i: (b + fb, 0, gt_col)),
                  full((1, D)), full((1, D))],
        out_specs=row(D),
        out_shape=jax.ShapeDtypeStruct((B, S, D), F32),
        input_output_aliases={0: 0},
        compiler_params=_cparams(2),
        name="moe_out",
    )(x, shared, yg, wt, ada_l, ln_g.reshape(1, D), ln_b.reshape(1, D))


def _moe_dest_kernel(start_ref, e_ref, r_ref, o_ref):
    e = e_ref[...]
    acc = r_ref[...]
    for j in range(start_ref.shape[0]):
        acc = acc + jnp.where(e == j, start_ref[j], 0)
    o_ref[...] = acc


def _routing_layout(expert_idx, rank, counts):
    B, K, S = expert_idx.shape
    T = B * S
    n_experts = counts.shape[0]
    BM = MOE_ROW_BLOCK
    TK = T * K
    counts = counts.reshape(n_experts).astype(jnp.int32)
    padded = (counts + BM - 1) // BM * BM
    pad_end = jnp.cumsum(padded)
    pad_start = (pad_end - padded).astype(jnp.int32)
    ts = min(S, 2048)
    spec = pl.BlockSpec((1, K, ts), lambda b, i, tab: (b, 0, i))
    dest = pl.pallas_call(
        _moe_dest_kernel,
        grid_spec=pltpu.PrefetchScalarGridSpec(num_scalar_prefetch=1, grid=(B, S // ts),
                                               in_specs=[spec, spec], out_specs=spec),
        out_shape=jax.ShapeDtypeStruct((B, K, S), jnp.int32),
        compiler_params=_cparams(2),
        name="moe_dest",
    )(pad_start, expert_idx, rank)
    dest = jnp.swapaxes(dest, 0, 1).reshape(TK)
    n_blocks = (TK + n_experts * (BM - 1) + BM - 1) // BM
    rows = n_blocks * BM
    token = jnp.tile(jnp.arange(T, dtype=jnp.int32), K)
    row_tok = (jnp.arange(rows, dtype=jnp.int32) % T).at[dest].add(token - dest % T, mode='promise_in_bounds')
    block_start = jnp.arange(n_blocks, dtype=jnp.int32) * BM
    block_expert = jnp.minimum(jnp.sum(pad_end[None, :] <= block_start[:, None], axis=1),
                               n_experts - 1).astype(jnp.int32)
    n_used = (pad_end[-1] // BM).astype(jnp.int32).reshape(1)
    return row_tok, dest, block_expert, n_used


def kernel(x, c, positions, ada_w, ada_b, w_in, gdn_conv_w, gdn_a_log, gdn_dt_bias, gdn_norm_w, w_gdn_branch, nsa_cmp_pos_k, nsa_cmp_pos_v, nsa_cmp_k_w1, nsa_cmp_k_w2, nsa_cmp_v_w1, nsa_cmp_v_w2, w_nsa_branch, w_out, ln1_g, ln1_b, router_w, router_bias, w_sh_gate, w_sh_up, w_sh_down, w_e_gate, w_e_up, w_e_down, ln2_g, ln2_b):
    B, S, D = x.shape
    L = ada_w.shape[0]
    T = B * S
    G = NSA_GROUPS
    nh_gdn = gdn_a_log.shape[1]
    gdn_w = nh_gdn * HEAD_DIM
    nsa_w = w_nsa_branch.shape[1]
    kvw = G * HEAD_DIM
    alpha = (2.0 * L) ** 0.25
    assert gdn_w == D and nsa_w == D and S % NSA_SEL_LEN == 0

    splits = (3 * gdn_w, gdn_w, nh_gdn, nh_gdn, nsa_w, 6 * kvw, 3 * (nsa_w // HEAD_DIM), D, D)
    offs = [0]
    for s_ in splits:
        offs.append(offs[-1] + s_)
    seg = lambda i: slice(offs[i], offs[i + 1])
    w_big = jnp.concatenate([w_in[:, :, seg(0)], w_in[:, :, seg(1)], w_in[:, :, seg(4)],
                             w_in[:, :, seg(7)], w_in[:, :, seg(8)], w_in[:, :, seg(5)]], axis=-1).astype(BF16)
    n_small = splits[2] + splits[3] + splits[6]
    w_small = jnp.concatenate([w_in[:, :, seg(2)], w_in[:, :, seg(3)], w_in[:, :, seg(6)],
                               jnp.zeros((L, D, 128 - n_small), w_in.dtype)], axis=-1).astype(BF16)
    Z_BLK, Q_BLK, MA_BLK, MB_BLK = 3, 4, 5, 6
    kv0 = 7 * D
    n_big = w_big.shape[2]
    tn_big = n_big // 4 if (n_big // 4) % 128 == 0 else 128

    wgb = w_gdn_branch.astype(BF16)
    wnb = w_nsa_branch.astype(BF16)
    wob = w_out.astype(BF16)
    wsg = w_sh_gate.astype(BF16)
    wsu = w_sh_up.astype(BF16)
    wsd = w_sh_down.astype(BF16)
    rwt = jnp.swapaxes(router_w, 1, 2)
    rwt_hi = rwt.astype(BF16)
    rwt_lo = (rwt - rwt_hi.astype(F32)).astype(BF16)
    cmp_w1 = jnp.stack([nsa_cmp_k_w1, nsa_cmp_v_w1], axis=1).astype(BF16)
    cmp_w2 = jnp.stack([nsa_cmp_k_w2, nsa_cmp_v_w2], axis=1).astype(BF16)
    cmp_pos = jnp.stack([nsa_cmp_pos_k, nsa_cmp_pos_v], axis=1).reshape(L, 2, 1, NSA_CMP_LEN * HEAD_DIM)

    ada = _ada(c, ada_w, ada_b)
    cosf, sinf = _rope_tables(positions)
    NC = S // NSA_CMP_STRIDE
    last = jnp.minimum(jnp.arange(NC) * NSA_CMP_STRIDE + NSA_CMP_LEN - 1, S - 1)
    cosc = cosf[:, last]
    sinc = sinf[:, last]
    n_sel = -(-(S // NSA_SEL_LEN) // 128) * 128
    yb = None

    for l in range(L):
        ada_l = ada[l].reshape(B, 1, 6 * D)
        big, small, cmp_in = _proj(x, ada_l, 0, 1, w_big[l], w_small[l], tn_big, kv0, 2 * G)
        small_t = jnp.swapaxes(small[:, :, nh_gdn:2 * nh_gdn].reshape(B, S // GDN_CHUNK, GDN_CHUNK, nh_gdn), 2, 3)
        q_a, k_a, v_a = _gdn_prep(big, gdn_conv_w[l], gdn_w)
        o_a = _gdn(q_a, k_a, v_a, big, Z_BLK, small, small_t, gdn_a_log[l], gdn_dt_bias[l], gdn_norm_w[l])
        q_r, ksel_r, kwin_r = _rope_apply(big, Q_BLK, (kv0 + 2 * kvw) // kvw, (kv0 + 4 * kvw) // kvw,
                                          nsa_w, kvw, cosf, sinf)
        hb = cmp_in.reshape(B, 2 * G, NC, NSA_CMP_STRIDE * HEAD_DIM)
        kvc = _compress(hb, cmp_pos[l], cmp_w1[l], cmp_w2[l], cosc, sinc)
        o_c, sel = _cmp_attn(q_r, kvc, small, nh_gdn, n_sel)
        o_s = _sel_attn(q_r, ksel_r, big, (kv0 + 3 * kvw) // HEAD_DIM, sel, small, nh_gdn)
        o_w = _win_attn(q_r, kwin_r, big, (kv0 + 5 * kvw) // HEAD_DIM, small, nh_gdn)
        x = _mixer_out(o_a, o_c, o_s, o_w, big, MA_BLK, MB_BLK, x, ada_l, 2, ln1_g[l], ln1_b[l],
                       wgb[l], wnb[l], wob[l], alpha)
        h2, eidx, ewt, rank, counts = _moe_router(x, ada_l, 3, 4, rwt_hi[l], rwt_lo[l], router_bias[l])
        row_tok, dest, block_expert, n_used = _routing_layout(eidx, rank, counts)
        shared = _shared_expert(h2, wsg[l], wsu[l], wsd[l])
        rows = row_tok.shape[0]
        n_blocks = rows // MOE_ROW_BLOCK
        h2f = h2.reshape(T, D)
        bounds = [n_blocks * p // MOE_DISPATCH_PARTS for p in range(MOE_DISPATCH_PARTS + 1)]
        yb = jnp.zeros((rows, D), BF16) if yb is None else yb
        for lo, hi in zip(bounds[:-1], bounds[1:]):
            xs = h2f.at[row_tok[lo * MOE_ROW_BLOCK:hi * MOE_ROW_BLOCK]].get(mode='promise_in_bounds')
            yb = _moe_experts(block_expert, n_used, xs, l, w_e_gate, w_e_up, w_e_down, lo, rows, prev=yb)
        n_cp = MOE_COMBINE_PARTS if B % MOE_COMBINE_PARTS == 0 else 1
        bp = B // n_cp
        dest3 = dest.reshape(TOP_K, B, S)
        wt = jnp.swapaxes(ewt, 1, 2)
        for p in range(n_cp):
            idx = dest3[:, p * bp:(p + 1) * bp].reshape(-1)
            yg = yb.at[idx].get(mode='promise_in_bounds').reshape(TOP_K, bp, S, D)
            x = _moe_out(x, shared, yg, wt, ada_l, 5, ln2_g[l], ln2_b[l], alpha, p * bp)
    return x
```

```python
import functools
import math

import jax
import jax.numpy as jnp
from jax import lax
from jax.experimental import pallas as pl
from jax.experimental.pallas import tpu as pltpu

F32 = jnp.float32
BF16 = jnp.bfloat16

HEAD_DIM = 128
GDN_CONV = 4
GDN_CHUNK = 64
NSA_GROUPS = 2
NSA_CMP_LEN = 32
NSA_CMP_STRIDE = 16
NSA_SEL_LEN = 64
NSA_SEL_TOP = 16
NSA_WINDOW = 512
ROPE_THETA = 10000.0
N_EXPERT_GROUPS = 8
TOPK_GROUPS = 4
TOP_K = 8
ROUTED_SCALE = 2.5
LN_EPS = 1e-5
NEG_INF = -1e30
SEL_FORCE = 1e6
MOE_ROW_BLOCK = 1024
Q_SCALE_LOG2E = (HEAD_DIM ** -0.5) * math.log2(math.e)
MASK_BIG = 2.0 ** 100
N_ROW_PARTS = 2
GDN_PHASE1_CHUNKS = 4
WIN_SUB_TILE = 128
MOE_DISPATCH_PARTS = 4
MOE_COMBINE_PARTS = 2
_ARB = "arbitrary"


def _cparams(n_axes):
    return pltpu.CompilerParams(dimension_semantics=(_ARB,) * n_axes)


def _sigmoid(x):
    return 1.0 / (1.0 + jnp.exp(-x))


def _silu(x):
    return x * _sigmoid(x)


def _dot(a, b):
    return jnp.dot(a, b, preferred_element_type=F32)


def _dot_nt(a, b):
    return lax.dot_general(a, b, (((1,), (1,)), ((), ())), preferred_element_type=F32)


def _dot_tn(a, b):
    return lax.dot_general(a, b, (((0,), (0,)), ((), ())), preferred_element_type=F32)


def _tile_lanes(x, n):
    return jnp.concatenate([x] * n, axis=1)


def _split3(x):
    x0 = x.astype(BF16)
    r1 = x - x0.astype(F32)
    x1 = r1.astype(BF16)
    x2 = (r1 - x1.astype(F32)).astype(BF16)
    return x0, x1, x2


def _dot_sel_right(x, sel_bf16):
    x0, x1, x2 = _split3(x)
    return _dot(x0, sel_bf16) + _dot(x1, sel_bf16) + _dot(x2, sel_bf16)


def _dot_sel_left(sel_bf16, x):
    x0, x1, x2 = _split3(x)
    return _dot(sel_bf16, x0) + _dot(sel_bf16, x1) + _dot(sel_bf16, x2)


def _normalize_rows(x):
    mu = jnp.mean(x, axis=-1, keepdims=True)
    xc = x - mu
    var = jnp.mean(xc * xc, axis=-1, keepdims=True)
    return xc * lax.rsqrt(var + LN_EPS)


def _ada_kernel(c_ref, w_ref, b_ref, o_ref):
    cond = _silu(c_ref[...])
    o_ref[0] = jnp.dot(cond, w_ref[0], preferred_element_type=F32,
                       precision=lax.Precision.HIGHEST) + b_ref[0]


def _ada(c, ada_w, ada_b):
    L, D, N = ada_w.shape
    B = c.shape[0]
    tn = min(N, 1536)
    return pl.pallas_call(
        _ada_kernel,
        grid=(L, N // tn),
        in_specs=[pl.BlockSpec((B, D), lambda l, j: (0, 0)),
                  pl.BlockSpec((1, D, tn), lambda l, j: (l, 0, j)),
                  pl.BlockSpec((1, 1, tn), lambda l, j: (l, 0, j))],
        out_specs=pl.BlockSpec((1, B, tn), lambda l, j: (l, 0, j)),
        out_shape=jax.ShapeDtypeStruct((L, B, N), F32),
        compiler_params=_cparams(2),
        name="ada",
    )(c, ada_w, ada_b.reshape(L, 1, N))


def _proj_kernel(x_ref, sh_ref, sc_ref, w_ref, ws_ref, o_ref, os_ref, oc_ref, h_ref, *, head_tile, head_off):
    @pl.when(pl.program_id(2) == 0)
    def _():
        h = _normalize_rows(x_ref[0]) * (1.0 + sc_ref[0]) + sh_ref[0]
        h_ref[...] = h.astype(BF16)
        os_ref[0] = _dot(h_ref[...], ws_ref[...])

    res = _dot(h_ref[...], w_ref[...]).astype(o_ref.dtype)
    o_ref[0] = res

    @pl.when(pl.program_id(2) == head_tile)
    def _():
        for jj in range(oc_ref.shape[1]):
            oc_ref[0, jj] = res[:, head_off + jj * HEAD_DIM:head_off + (jj + 1) * HEAD_DIM]


def _proj(x, ada_l, sh_col, sc_col, w, w_small, tn, head_col, n_head_cols):
    B, S, D = x.shape
    N = w.shape[1]
    NS = w_small.shape[1]
    tm = min(S, 1024)
    head_tile, head_off = head_col // tn, head_col % tn
    assert head_off % HEAD_DIM == 0 and head_off + n_head_cols * HEAD_DIM <= tn
    return pl.pallas_call(
        functools.partial(_proj_kernel, head_tile=head_tile, head_off=head_off),
        grid=(B, S // tm, N // tn),
        in_specs=[pl.BlockSpec((1, tm, D), lambda b, i, j: (b, i, 0)),
                  pl.BlockSpec((1, 1, D), lambda b, i, j: (b, 0, sh_col)),
                  pl.BlockSpec((1, 1, D), lambda b, i, j: (b, 0, sc_col)),
                  pl.BlockSpec((D, tn), lambda b, i, j: (0, j)),
                  pl.BlockSpec((D, NS), lambda b, i, j: (0, 0))],
        out_specs=[pl.BlockSpec((1, tm, tn), lambda b, i, j: (b, i, j)),
                   pl.BlockSpec((1, tm, NS), lambda b, i, j: (b, i, 0)),
                   pl.BlockSpec((1, n_head_cols, tm, HEAD_DIM), lambda b, i, j: (b, 0, i, 0))],
        out_shape=[jax.ShapeDtypeStruct((B, S, N), BF16),
                   jax.ShapeDtypeStruct((B, S, NS), F32),
                   jax.ShapeDtypeStruct((B, n_head_cols, S, HEAD_DIM), BF16)],
        scratch_shapes=[pltpu.VMEM((tm, D), BF16)],
        compiler_params=_cparams(3),
        name="proj",
    )(x, ada_l, ada_l, w, w_small)


def _gdn_prep_kernel(x_ref, w_ref, q_ref, k_ref, v_ref, carry_ref):
    ts = x_ref.shape[1]
    width = q_ref.shape[2]
    nh = width // HEAD_DIM

    @pl.when(pl.program_id(1) == 0)
    def _():
        carry_ref[...] = jnp.zeros_like(carry_ref)

    for part, o_ref in enumerate((q_ref, k_ref, v_ref)):
        for h in range(nh):
            c0 = part * width + h * HEAD_DIM
            cols = slice(c0, c0 + HEAD_DIM)
            xx = jnp.concatenate([carry_ref[:, cols], x_ref[0, :, cols].astype(F32)], axis=0)
            w = w_ref[:, cols]
            y = xx[8:8 + ts] * w[GDN_CONV - 1:GDN_CONV]
            for kk in range(GDN_CONV - 1):
                off = 8 - (GDN_CONV - 1) + kk
                y = y + xx[off:off + ts] * w[kk:kk + 1]
            y = _silu(y)
            if part < 2:
                norm = lax.rsqrt(jnp.sum(y * y, axis=-1, keepdims=True) + 1e-6)
                if part == 0:
                    norm = norm * (HEAD_DIM ** -0.5)
                y = y * norm
            o_ref[0, :, h * HEAD_DIM:(h + 1) * HEAD_DIM] = y.astype(o_ref.dtype)
    carry_ref[...] = x_ref[0, ts - 8:ts, :].astype(F32)


def _gdn_prep(big, conv_w, width):
    B, S, _ = big.shape
    ts = min(S, 512)
    out = jax.ShapeDtypeStruct((B, S, width), BF16)
    ospec = pl.BlockSpec((1, ts, width), lambda b, i: (b, i, 0))
    return pl.pallas_call(
        _gdn_prep_kernel,
        grid=(B, S // ts),
        in_specs=[pl.BlockSpec((1, ts, 3 * width), lambda b, i: (b, i, 0)),
                  pl.BlockSpec((GDN_CONV, 3 * width), lambda b, i: (0, 0))],
        out_specs=[ospec, ospec, ospec],
        out_shape=[out, out, out],
        scratch_shapes=[pltpu.VMEM((8, 3 * width), F32)],
        compiler_params=_cparams(2),
        name="gdn_prep",
    )(big, conv_w)


def _softplus(x):
    return jnp.maximum(x, 0.0) + jnp.log(1.0 + jnp.exp(-jnp.abs(x)))


def _gdn_kernel(q_ref, k_ref, v_ref, z_ref, sm_ref, smt_ref, alog_ref, alogt_ref, dtb_ref, dtbt_ref,
                nw_ref, o_ref, state_ref, u_s, wq_s, attn_s, kd_s, dec_s):
    ts = q_ref.shape[1]
    nh = q_ref.shape[2] // HEAD_DIM
    C = GDN_CHUNK
    R = 2 * C
    npair = nh // 2
    nchunks = ts // C

    @pl.when(pl.program_id(1) == 0)
    def _():
        state_ref[...] = jnp.zeros_like(state_ref)

    ci = lax.broadcasted_iota(jnp.int32, (C, C), 0)
    cj = lax.broadcasted_iota(jnp.int32, (C, C), 1)
    tril = jnp.where(ci >= cj, 1.0, 0.0).astype(BF16)
    triu = jnp.where(cj >= ci, 1.0, 0.0).astype(BF16)
    ii = lax.broadcasted_iota(jnp.int32, (R, R), 0)
    jj = lax.broadcasted_iota(jnp.int32, (R, R), 1)
    same_head = (ii // C) == (jj // C)
    incl = same_head & (ii >= jj)
    strict = same_head & (ii > jj)
    eye = jnp.where(ii == jj, 1.0, 0.0)
    nw = nw_ref[...]

    def pair_rows(ref, rows, p):
        return jnp.concatenate([ref[0, rows, (2 * p) * HEAD_DIM:(2 * p + 1) * HEAD_DIM],
                                ref[0, rows, (2 * p + 1) * HEAD_DIM:(2 * p + 2) * HEAD_DIM]], axis=0)

    def pair_col(x, p):
        return jnp.concatenate([x[:, 2 * p:2 * p + 1], x[:, 2 * p + 1:2 * p + 2]], axis=0)

    def phase1(j, carry):
        cs = [j * GDN_PHASE1_CHUNKS + d for d in range(GDN_PHASE1_CHUNKS)]
        rows_c, beta_c, gc_c, gcr_c, glb_c = [], [], [], [], []
        for c in cs:
            rows = slice(c * C, (c + 1) * C)
            sm = sm_ref[0, rows, :]
            g = -jnp.exp(alog_ref[...]) * _softplus(sm[:, nh:2 * nh] + dtb_ref[...])
            gt = -jnp.exp(alogt_ref[...]) * _softplus(smt_ref[0, c] + dtbt_ref[...])
            gc = _dot_sel_left(tril, g)
            g_last = gc[C - 1:C, :]
            dec_s[c] = jnp.exp(g_last)
            rows_c.append(rows)
            beta_c.append(_sigmoid(sm[:, 0:nh]))
            gc_c.append(gc)
            gcr_c.append(_dot_sel_right(gt, triu))
            glb_c.append(jnp.broadcast_to(g_last, (C, nh)))
        items = [(ci, p) for ci in range(len(cs)) for p in range(npair)]
        idx = range(len(items))
        q2 = [pair_rows(q_ref, rows_c[ci], p).astype(F32) for ci, p in items]
        k2 = [pair_rows(k_ref, rows_c[ci], p).astype(F32) for ci, p in items]
        b2 = [pair_col(beta_c[ci], p) for ci, p in items]
        g_col = [pair_col(gc_c[ci], p) for ci, p in items]
        kb = [k2[i] * b2[i] for i in idx]
        kq = [_dot_nt(jnp.concatenate([kb[i], q2[i]], axis=0).astype(BF16), k2[i].astype(BF16))
              for i in idx]
        decay = []
        for i, (ci, p) in enumerate(items):
            gcr = gcr_c[ci]
            g_row = jnp.concatenate([gcr[2 * p:2 * p + 1, :], gcr[2 * p + 1:2 * p + 2, :]], axis=1)
            decay.append(jnp.where(incl, jnp.exp(jnp.where(incl, g_col[i] - g_row, 0.0)), 0.0))
        a = [jnp.where(strict, kq[i][:R] * decay[i], 0.0) for i in idx]
        for i, (ci, p) in enumerate(items):
            attn_s[cs[ci], p] = (kq[i][R:] * decay[i]).astype(BF16)
        x = [eye - a[i] for i in idx]
        pw = a
        n = 2
        while n < C:
            pw16 = [pw[i].astype(BF16) for i in idx]
            pw = [_dot(pw16[i], pw16[i]) for i in idx]
            x = [x[i] + _dot(x[i].astype(BF16), pw[i].astype(BF16)) for i in idx]
            n *= 2
        eg = [jnp.exp(g_col[i]) for i in idx]
        sol = []
        for i, (ci, p) in enumerate(items):
            v2 = pair_rows(v_ref, rows_c[ci], p).astype(F32)
            r = jnp.concatenate([v2 * b2[i], kb[i] * eg[i]], axis=1)
            sol.append(_dot(x[i].astype(BF16), r.astype(BF16)))
        for i, (ci, p) in enumerate(items):
            c = cs[ci]
            u_s[c, p] = sol[i][:, :HEAD_DIM]
            w = sol[i][:, HEAD_DIM:]
            qg = q2[i] * eg[i]
            for e in range(2):
                wq_s[c, 2 * p + e] = jnp.concatenate([w[e * C:(e + 1) * C], qg[e * C:(e + 1) * C]],
                                                     axis=0).astype(BF16)
            kd_s[c, p] = (k2[i] * jnp.exp(pair_col(glb_c[ci], p) - g_col[i])).astype(BF16)
        return carry

    def phase2(c):
        rows = slice(c * C, (c + 1) * C)
        dec = dec_s[c]
        res = [_dot(wq_s[c, h], state_ref[h].astype(BF16)) for h in range(nh)]
        for p in range(npair):
            ws = jnp.concatenate([res[2 * p][:C], res[2 * p + 1][:C]], axis=0)
            qs = jnp.concatenate([res[2 * p][C:], res[2 * p + 1][C:]], axis=0)
            v_new = u_s[c, p] - ws
            v16 = v_new.astype(BF16)
            o2 = qs + _dot(attn_s[c, p], v16)
            kd = kd_s[c, p]
            for e in range(2):
                h = 2 * p + e
                part = slice(e * C, (e + 1) * C)
                state_ref[h] = state_ref[h] * dec[:, h:h + 1] + _dot_tn(kd[part], v16[part])
                o = o2[part]
                o = o * lax.rsqrt(jnp.mean(o * o, axis=-1, keepdims=True) + 1e-6) * nw
                cols = slice(h * HEAD_DIM, (h + 1) * HEAD_DIM)
                z = z_ref[0, rows, cols].astype(F32)
                o_ref[0, rows, cols] = (o * _silu(z)).astype(o_ref.dtype)

    def phase2_group(j):
        for d in range(GDN_PHASE1_CHUNKS):
            phase2(j * GDN_PHASE1_CHUNKS + d)

    assert nchunks % GDN_PHASE1_CHUNKS == 0
    ngroups = nchunks // GDN_PHASE1_CHUNKS
    phase1(0, 0)
    for j in range(1, ngroups):
        phase2_group(j - 1)
        phase1(j, 0)
    phase2_group(ngroups - 1)


def _gdn(q, k, v, big, z_blk, small, small_t, a_log, dt_bias, norm_w):
    B, S, W = q.shape
    nh = W // HEAD_DIM
    assert nh % 2 == 0 and 2 * GDN_CHUNK == HEAD_DIM
    ts = min(S, 512)
    nc = ts // GDN_CHUNK
    spec = pl.BlockSpec((1, ts, W), lambda b, i: (b, i, 0))
    full = lambda shape: pl.BlockSpec(shape, lambda b, i: (0,) * len(shape))
    return pl.pallas_call(
        _gdn_kernel,
        grid=(B, S // ts),
        in_specs=[spec, spec, spec,
                  pl.BlockSpec((1, ts, W), lambda b, i: (b, i, z_blk)),
                  pl.BlockSpec((1, ts, small.shape[2]), lambda b, i: (b, i, 0)),
                  pl.BlockSpec((1, ts // GDN_CHUNK, nh, GDN_CHUNK), lambda b, i: (b, i, 0, 0)),
                  full((1, nh)), full((nh, 1)), full((1, nh)), full((nh, 1)),
                  full((1, HEAD_DIM))],
        out_specs=spec,
        out_shape=jax.ShapeDtypeStruct((B, S, W), BF16),
        scratch_shapes=[pltpu.VMEM((nh, HEAD_DIM, HEAD_DIM), F32),
                        pltpu.VMEM((nc, nh // 2, 2 * GDN_CHUNK, HEAD_DIM), F32),
                        pltpu.VMEM((nc, nh, 2 * GDN_CHUNK, HEAD_DIM), BF16),
                        pltpu.VMEM((nc, nh // 2, 2 * GDN_CHUNK, 2 * GDN_CHUNK), BF16),
                        pltpu.VMEM((nc, nh // 2, 2 * GDN_CHUNK, HEAD_DIM), BF16),
                        pltpu.VMEM((nc, 1, nh), F32)],
        compiler_params=_cparams(2),
        name="gdn",
    )(q, k, v, big, small, small_t, a_log.reshape(1, nh), a_log.reshape(nh, 1),
      dt_bias.reshape(1, nh), dt_bias.reshape(nh, 1), norm_w.reshape(1, HEAD_DIM))


def _rope_table_kernel(pos_ref, invf_ref, sign_ref, cos_ref, sin_ref):
    ang = pos_ref[0].astype(F32) * invf_ref[...]
    cos_ref[0] = jnp.cos(ang)
    sin_ref[0] = jnp.sin(ang) * sign_ref[...]


def _rope_tables(positions):
    B, S = positions.shape
    half = HEAD_DIM // 2
    inv = ROPE_THETA ** (-jnp.arange(half, dtype=F32) / half)
    invf = jnp.concatenate([inv, inv]).reshape(1, HEAD_DIM)
    sign = jnp.concatenate([-jnp.ones((half,), F32), jnp.ones((half,), F32)]).reshape(1, HEAD_DIM)
    ts = min(S, 1024)
    out = jax.ShapeDtypeStruct((B, S, HEAD_DIM), F32)
    ospec = pl.BlockSpec((1, ts, HEAD_DIM), lambda b, i: (b, i, 0))
    return pl.pallas_call(
        _rope_table_kernel,
        grid=(B, S // ts),
        in_specs=[pl.BlockSpec((1, ts, 1), lambda b, i: (b, i, 0)),
                  pl.BlockSpec((1, HEAD_DIM), lambda b, i: (0, 0)),
                  pl.BlockSpec((1, HEAD_DIM), lambda b, i: (0, 0))],
        out_specs=[ospec, ospec],
        out_shape=[out, out],
        compiler_params=_cparams(2),
        name="rope_table",
    )(positions.reshape(B, S, 1), invf, sign)


def _rope_rows(x, cosf, sinf):
    return x * cosf + pltpu.roll(x, HEAD_DIM // 2, 1) * sinf


def _rope_apply_kernel(q_ref, ks_ref, kw_ref, cos_ref, sin_ref, qo_ref, kso_ref, kwo_ref):
    cosf = cos_ref[0]
    sinf = sin_ref[0]
    for src, dst, mult in ((q_ref, qo_ref, Q_SCALE_LOG2E), (ks_ref, kso_ref, None), (kw_ref, kwo_ref, None)):
        for h in range(src.shape[2] // HEAD_DIM):
            cols = slice(h * HEAD_DIM, (h + 1) * HEAD_DIM)
            r = _rope_rows(src[0, :, cols].astype(F32), cosf, sinf)
            if mult is not None:
                r = r * mult
            dst[0, :, cols] = r.astype(dst.dtype)


def _rope_apply(big, q_blk, ksel_blk, kwin_blk, wq, wkv, cosf, sinf):
    B, S, _ = big.shape
    ts = min(S, 512)
    tab = pl.BlockSpec((1, ts, HEAD_DIM), lambda b, i: (b, i, 0))
    return pl.pallas_call(
        _rope_apply_kernel,
        grid=(B, S // ts),
        in_specs=[pl.BlockSpec((1, ts, wq), lambda b, i: (b, i, q_blk)),
                  pl.BlockSpec((1, ts, wkv), lambda b, i: (b, i, ksel_blk)),
                  pl.BlockSpec((1, ts, wkv), lambda b, i: (b, i, kwin_blk)),
                  tab, tab],
        out_specs=[pl.BlockSpec((1, ts, wq), lambda b, i: (b, i, 0)),
                   pl.BlockSpec((1, ts, wkv), lambda b, i: (b, i, 0)),
                   pl.BlockSpec((1, ts, wkv), lambda b, i: (b, i, 0))],
        out_shape=[jax.ShapeDtypeStruct((B, S, wq), BF16),
                   jax.ShapeDtypeStruct((B, S, wkv), BF16),
                   jax.ShapeDtypeStruct((B, S, wkv), BF16)],
        compiler_params=_cparams(2),
        name="rope_apply",
    )(big, big, big, cosf, sinf)


def _compress_kernel(hb_ref, pos_ref, w1_ref, w2_ref, cos_ref, sin_ref, o_ref):
    hb = hb_ref[0, 0]
    w1 = w1_ref[0]
    half = hb.shape[1]
    p0 = _dot(hb, w1[:half])
    p1 = _dot(hb, w1[half:])
    nc = p0.shape[0]
    pos8 = jnp.broadcast_to(pos_ref[0], (8, 2 * half)).astype(BF16)
    pb = _dot(pos8, w1)[0:1]
    pre = p0 + pltpu.roll(p1, nc - 1, 0) + pb
    out = _dot(_silu(pre).astype(BF16), w2_ref[0])
    roped = _rope_rows(out, cos_ref[0], sin_ref[0])
    is_key = pl.program_id(1) < NSA_GROUPS
    o_ref[0, 0] = jnp.where(is_key, roped, out).astype(o_ref.dtype)


def _compress(hb, pos_flat, w1, w2, cosc, sinc):
    B, J, NC, HW = hb.shape
    G = NSA_GROUPS
    return pl.pallas_call(
        _compress_kernel,
        grid=(B, J),
        in_specs=[pl.BlockSpec((1, 1, NC, HW), lambda b, j: (b, j, 0, 0)),
                  pl.BlockSpec((1, 1, 2 * HW), lambda b, j: (j // G, 0, 0)),
                  pl.BlockSpec((1, 2 * HW, HEAD_DIM), lambda b, j: (j // G, 0, 0)),
                  pl.BlockSpec((1, HEAD_DIM, HEAD_DIM), lambda b, j: (j // G, 0, 0)),
                  pl.BlockSpec((1, NC, HEAD_DIM), lambda b, j: (b, 0, 0)),
                  pl.BlockSpec((1, NC, HEAD_DIM), lambda b, j: (b, 0, 0))],
        out_specs=pl.BlockSpec((1, 1, NC, HEAD_DIM), lambda b, j: (b, j, 0, 0)),
        out_shape=jax.ShapeDtypeStruct((B, J, NC, HEAD_DIM), BF16),
        compiler_params=_cparams(2),
        name="nsa_compress",
    )(hb, pos_flat, w1, w2, cosc, sinc)


def _gate_col(g_ref, nh_gdn, group, hpg, h, branch):
    col = None
    for gi in range(NSA_GROUPS):
        c = 2 * nh_gdn + (gi * hpg + h) * 3 + branch
        cand = g_ref[0, :, c:c + 1]
        col = cand if col is None else jnp.where(group == gi, cand, col)
    return _sigmoid(col)


def _cmp_attn_kernel(q_ref, kc_ref, vc_ref, sm_ref, o_ref, sel_ref, *, nh_gdn, hpg):
    tq = q_ref.shape[1]
    nc = kc_ref.shape[2]
    nb = sel_ref.shape[3]
    g = pl.program_id(1)
    t0 = pl.program_id(2) * tq
    t_nc = t0 + lax.broadcasted_iota(jnp.int32, (tq, nc), 0)
    n_nc = lax.broadcasted_iota(jnp.int32, (tq, nc), 1)
    valid = (n_nc * NSA_CMP_STRIDE + (NSA_CMP_LEN - 1)) <= t_nc
    t_col = t0 + lax.broadcasted_iota(jnp.int32, (tq, 1), 0)
    has_valid = jnp.where(t_col >= NSA_CMP_LEN - 1, 1.0, 0.0)
    kc = kc_ref[0, 0]
    vc = vc_ref[0, 0]
    psum = jnp.zeros((tq, nc), F32)
    for h in range(hpg):
        cols = slice(h * HEAD_DIM, (h + 1) * HEAD_DIM)
        s = jnp.where(valid, _dot_nt(q_ref[0, :, cols], kc), NEG_INF)
        e = jnp.exp2(s - jnp.max(s, axis=-1, keepdims=True))
        p = e * (has_valid / jnp.sum(e, axis=-1, keepdims=True))
        gate = _gate_col(sm_ref, nh_gdn, g, hpg, h, 0)
        o_ref[0, :, cols] = (_dot(p.astype(BF16), vc) * gate).astype(o_ref.dtype)
        psum = psum + p
    sj = lax.broadcasted_iota(jnp.int32, (nb, nc), 0) * NSA_SEL_LEN
    cn = lax.broadcasted_iota(jnp.int32, (nb, nc), 1) * NSA_CMP_STRIDE
    overlap_t = jnp.where((cn <= sj + (NSA_SEL_LEN - 1)) & (cn + (NSA_CMP_LEN - 1) >= sj), 1.0, 0.0).astype(BF16)
    p0, p1, p2 = _split3(psum)
    imp = _dot_nt(overlap_t, p0) + _dot_nt(overlap_t, p1) + _dot_nt(overlap_t, p2)
    t_nb = t0 + lax.broadcasted_iota(jnp.int32, (nb, tq), 1)
    blk = lax.broadcasted_iota(jnp.int32, (nb, tq), 0)
    cur = t_nb // NSA_SEL_LEN
    forced = (blk == 0) | (blk == cur) | (blk == cur - 1)
    score = jnp.where(forced, SEL_FORCE, jnp.where(blk * NSA_SEL_LEN <= t_nb, imp, -1.0))
    blk_f = blk.astype(F32)
    for _ in range(min(NSA_SEL_TOP, nb)):
        m = jnp.max(score, axis=0, keepdims=True)
        first = jnp.min(jnp.where(score == m, blk_f, float(nb)), axis=0, keepdims=True)
        score = jnp.where(blk_f == first, -jnp.inf, score)
    sel_ref[0, 0] = jnp.where(score == -jnp.inf, 1.0, 0.0).T.astype(sel_ref.dtype)


def _cmp_attn(q_r, kvc, small, nh_gdn, n_sel):
    B, S, WQ = q_r.shape
    G = NSA_GROUPS
    hpg = WQ // HEAD_DIM // G
    NC = kvc.shape[2]
    tq = min(S, 512)
    return pl.pallas_call(
        functools.partial(_cmp_attn_kernel, nh_gdn=nh_gdn, hpg=hpg),
        grid=(B, G, S // tq),
        in_specs=[pl.BlockSpec((1, tq, hpg * HEAD_DIM), lambda b, g, i: (b, i, g)),
                  pl.BlockSpec((1, 1, NC, HEAD_DIM), lambda b, g, i: (b, g, 0, 0)),
                  pl.BlockSpec((1, 1, NC, HEAD_DIM), lambda b, g, i: (b, G + g, 0, 0)),
                  pl.BlockSpec((1, tq, small.shape[2]), lambda b, g, i: (b, i, 0))],
        out_specs=[pl.BlockSpec((1, tq, hpg * HEAD_DIM), lambda b, g, i: (b, i, g)),
                   pl.BlockSpec((1, 1, tq, n_sel), lambda b, g, i: (b, g, i, 0))],
        out_shape=[jax.ShapeDtypeStruct((B, S, WQ), BF16),
                   jax.ShapeDtypeStruct((B, G, S, n_sel), BF16)],
        compiler_params=_cparams(3),
        name="nsa_cmp_attn",
    )(q_r, kvc, kvc, small)


def _sel_attn_kernel(q_ref, k_ref, ext_ref, v_ref, sel_ref, sm_ref, o_ref, qx_ref, m_ref, acc_ref, p_ref, a_ref,
                     *, nh_gdn, hpg, tk):
    tq = q_ref.shape[1]
    nb = sel_ref.shape[3]
    g = pl.program_id(1)
    t0 = pl.program_id(2) * tq
    assert tk % tq == 0
    unsel = (sel_ref[0, 0].astype(F32) - 1.0).astype(BF16)
    for h in range(hpg):
        qx_ref[h * tq:(h + 1) * tq, 0:HEAD_DIM] = q_ref[0, :, h * HEAD_DIM:(h + 1) * HEAD_DIM]
        qx_ref[h * tq:(h + 1) * tq, HEAD_DIM:HEAD_DIM + nb] = unsel
    m_ref[...] = jnp.full_like(m_ref, NEG_INF)
    acc_ref[...] = jnp.zeros_like(acc_ref)
    p_ref[...] = jnp.zeros_like(p_ref)
    a_ref[...] = jnp.ones_like(a_ref)
    ones = jnp.ones((tk, HEAD_DIM), BF16)
    pr = hpg * tq // N_ROW_PARTS
    parts = [slice(part * pr, (part + 1) * pr) for part in range(N_ROW_PARTS)]

    def apply_values(kt):
        k0 = pl.multiple_of(kt * tk, tk)
        vx = jnp.concatenate([v_ref[0, pl.ds(k0, tk), :], ones], axis=1)
        for rows in parts:
            acc_ref[rows, :] = _tile_lanes(a_ref[rows, :], 2) * acc_ref[rows, :] + _dot(p_ref[rows, :], vx)

    def scores(kt, causal):
        k0 = pl.multiple_of(kt * tk, tk)
        kx = jnp.concatenate([k_ref[0, pl.ds(k0, tk), :], ext_ref[pl.ds(k0, tk), :]], axis=1)
        if causal:
            t_row = t0 + lax.broadcasted_iota(jnp.int32, (tq, tk), 0)
            key = k0 + lax.broadcasted_iota(jnp.int32, (tq, tk), 1)
            keep = jnp.concatenate([key <= t_row] * (hpg // N_ROW_PARTS), axis=0)
        for rows in parts:
            s = _dot_nt(qx_ref[rows, :], kx)
            if causal:
                s = jnp.where(keep, s, NEG_INF)
            m_old = m_ref[rows, :]
            m_new = jnp.maximum(m_old, jnp.max(s, axis=-1, keepdims=True))
            a_ref[rows, :] = jnp.exp2(m_old - m_new)
            p_ref[rows, :] = jnp.exp2(s - _tile_lanes(m_new, tk // HEAD_DIM)).astype(BF16)
            m_ref[rows, :] = m_new

    kt_diag = t0 // tk

    def body(kt, carry):
        apply_values(jnp.maximum(kt - 1, 0))
        scores(kt, False)
        return carry

    lax.fori_loop(0, kt_diag, body, 0)
    apply_values(jnp.maximum(kt_diag - 1, 0))
    scores(kt_diag, True)
    apply_values(kt_diag)
    for h in range(hpg):
        rows = slice(h * tq, (h + 1) * tq)
        gate = _gate_col(sm_ref, nh_gdn, g, hpg, h, 1)
        out = acc_ref[rows, 0:HEAD_DIM] / acc_ref[rows, HEAD_DIM:2 * HEAD_DIM]
        o_ref[0, :, h * HEAD_DIM:(h + 1) * HEAD_DIM] = (out * gate).astype(o_ref.dtype)


def _sel_attn(q_r, ksel_r, big, vsel_blk, sel, small, nh_gdn):
    B, S, WQ = q_r.shape
    G = NSA_GROUPS
    hpg = WQ // HEAD_DIM // G
    nbp = sel.shape[3]
    tq = min(S, 512)
    tk = min(S, 512)
    rows = hpg * tq
    ext = jnp.where(jnp.arange(S)[:, None] // NSA_SEL_LEN == jnp.arange(nbp)[None, :], MASK_BIG, 0.0).astype(BF16)
    return pl.pallas_call(
        functools.partial(_sel_attn_kernel, nh_gdn=nh_gdn, hpg=hpg, tk=tk),
        grid=(B, G, S // tq),
        in_specs=[pl.BlockSpec((1, tq, hpg * HEAD_DIM), lambda b, g, i: (b, i, g)),
                  pl.BlockSpec((1, S, HEAD_DIM), lambda b, g, i: (b, 0, g)),
                  pl.BlockSpec((S, nbp), lambda b, g, i: (0, 0)),
                  pl.BlockSpec((1, S, HEAD_DIM), lambda b, g, i: (b, 0, vsel_blk + g)),
                  pl.BlockSpec((1, 1, tq, nbp), lambda b, g, i: (b, g, i, 0)),
                  pl.BlockSpec((1, tq, small.shape[2]), lambda b, g, i: (b, i, 0))],
        out_specs=pl.BlockSpec((1, tq, hpg * HEAD_DIM), lambda b, g, i: (b, i, g)),
        out_shape=jax.ShapeDtypeStruct((B, S, WQ), BF16),
        scratch_shapes=[pltpu.VMEM((rows, HEAD_DIM + nbp), BF16),
                        pltpu.VMEM((rows, HEAD_DIM), F32),
                        pltpu.VMEM((rows, 2 * HEAD_DIM), F32),
                        pltpu.VMEM((rows, tk), BF16),
                        pltpu.VMEM((rows, HEAD_DIM), F32)],
        compiler_params=_cparams(3),
        name="nsa_sel_attn",
    )(q_r, ksel_r, ext, big, sel, small)


def _win_attn_kernel(q_ref, k_ref, v_ref, sm_ref, o_ref, *, nh_gdn, hpg):
    tq = q_ref.shape[1]
    ts = min(tq, WIN_SUB_TILE)
    span = ts + NSA_WINDOW
    g = pl.program_id(1)
    for sub in range(tq // ts):
        qrows = slice(sub * ts, (sub + 1) * ts)
        t0 = pl.program_id(2) * tq + sub * ts
        k0 = pl.multiple_of(jnp.maximum(t0 - NSA_WINDOW, 0), ts)
        k = k_ref[0, pl.ds(k0, span), :]
        v = v_ref[0, pl.ds(k0, span), :]
        t_row = t0 + lax.broadcasted_iota(jnp.int32, (ts, span), 0)
        spos = k0 + lax.broadcasted_iota(jnp.int32, (ts, span), 1)
        mask = (spos <= t_row) & (spos > t_row - NSA_WINDOW)
        vx = jnp.concatenate([v, jnp.ones((span, HEAD_DIM), BF16)], axis=1)
        q4 = jnp.concatenate([q_ref[0, qrows, h * HEAD_DIM:(h + 1) * HEAD_DIM] for h in range(hpg)], axis=0)
        s = jnp.where(jnp.concatenate([mask] * hpg, axis=0), _dot_nt(q4, k), NEG_INF)
        e = jnp.exp2(s - jnp.max(s, axis=-1, keepdims=True))
        acc = _dot(e.astype(BF16), vx)
        out = acc[:, :HEAD_DIM] / acc[:, HEAD_DIM:]
        for h in range(hpg):
            gate = _gate_col(sm_ref, nh_gdn, g, hpg, h, 2)[qrows]
            o_ref[0, qrows, h * HEAD_DIM:(h + 1) * HEAD_DIM] = (out[h * ts:(h + 1) * ts] * gate).astype(o_ref.dtype)


def _win_attn(q_r, kwin_r, big, vwin_blk, small, nh_gdn):
    B, S, WQ = q_r.shape
    G = NSA_GROUPS
    hpg = WQ // HEAD_DIM // G
    tq = min(S, 512)
    assert S >= tq + NSA_WINDOW
    return pl.pallas_call(
        functools.partial(_win_attn_kernel, nh_gdn=nh_gdn, hpg=hpg),
        grid=(B, G, S // tq),
        in_specs=[pl.BlockSpec((1, tq, hpg * HEAD_DIM), lambda b, g, i: (b, i, g)),
                  pl.BlockSpec((1, S, HEAD_DIM), lambda b, g, i: (b, 0, g)),
                  pl.BlockSpec((1, S, HEAD_DIM), lambda b, g, i: (b, 0, vwin_blk + g)),
                  pl.BlockSpec((1, tq, small.shape[2]), lambda b, g, i: (b, i, 0))],
        out_specs=pl.BlockSpec((1, tq, hpg * HEAD_DIM), lambda b, g, i: (b, i, g)),
        out_shape=jax.ShapeDtypeStruct((B, S, WQ), BF16),
        compiler_params=_cparams(3),
        name="nsa_win_attn",
    )(q_r, kwin_r, big, small)


def _post_norm(x, y, gt, g, b, alpha):
    r = alpha * x + (1.0 + gt) * y
    return _normalize_rows(r) * g + b


def _mixer_out_kernel(oa_ref, oc_ref, os_ref, ow_ref, ma_ref, mb_ref, x_ref, gt_ref, lg_ref, lb_ref,
                      wg_ref, wn_ref, wo_ref, o_ref, *, alpha):
    y_a = _dot(oa_ref[0], wg_ref[...])
    o_b = oc_ref[0].astype(F32) + os_ref[0].astype(F32) + ow_ref[0].astype(F32)
    y_b = _dot(o_b.astype(BF16), wn_ref[...])
    mixed = _sigmoid(ma_ref[0].astype(F32)) * y_a + _sigmoid(mb_ref[0].astype(F32)) * y_b
    y = _dot(mixed.astype(BF16), wo_ref[...])
    o_ref[0] = _post_norm(x_ref[0], y, gt_ref[0], lg_ref[...], lb_ref[...], alpha)


def _mixer_out(o_a, o_c, o_s, o_w, big, ma_blk, mb_blk, x, ada_l, gt_col, ln_g, ln_b, wg, wn, wo, alpha):
    B, S, D = x.shape
    tm = min(S, 512)
    row = lambda blk: pl.BlockSpec((1, tm, D), lambda b, i: (b, i, blk))
    full = lambda shape: pl.BlockSpec(shape, lambda b, i: (0,) * len(shape))
    return pl.pallas_call(
        functools.partial(_mixer_out_kernel, alpha=alpha),
        grid=(B, S // tm),
        in_specs=[row(0), row(0), row(0), row(0), row(ma_blk), row(mb_blk), row(0),
                  pl.BlockSpec((1, 1, D), lambda b, i: (b, 0, gt_col)),
                  full((1, D)), full((1, D)), full((D, D)), full((D, D)), full((D, D))],
        out_specs=row(0),
        out_shape=jax.ShapeDtypeStruct((B, S, D), F32),
        compiler_params=_cparams(2),
        name="mixer_out",
    )(o_a, o_c, o_s, o_w, big, big, x, ada_l, ln_g.reshape(1, D), ln_b.reshape(1, D), wg, wn, wo)


def _moe_router_kernel(x_ref, sh_ref, sc_ref, rwh_ref, rwl_ref, rb_ref,
                       h_ref, idx_ref, wt_ref, rank_ref, cnt_ref):
    h = _normalize_rows(x_ref[0]) * (1.0 + sc_ref[0]) + sh_ref[0]
    h16 = h.astype(BF16)
    h_ref[0] = h16
    h_lo = (h - h16.astype(F32)).astype(BF16)
    rw_hi = rwh_ref[...]
    logits = _dot_nt(rw_hi, h16) + _dot_nt(rwl_ref[...], h16) + _dot_nt(rw_hi, h_lo)
    scores = _sigmoid(logits)
    biased = scores + rb_ref[...]
    E, tm = scores.shape
    NG = N_EXPERT_GROUPS
    per = E // NG
    b3 = biased.reshape(NG, per, tm)
    in_grp = lax.broadcasted_iota(jnp.int32, (NG, per, tm), 1).astype(F32)
    m1 = jnp.max(b3, axis=1, keepdims=True)
    first = jnp.min(jnp.where(b3 == m1, in_grp, float(per)), axis=1, keepdims=True)
    m2 = jnp.max(jnp.where(in_grp == first, -jnp.inf, b3), axis=1, keepdims=True)
    gscore = (m1 + m2).reshape(NG, tm)
    g_id = lax.broadcasted_iota(jnp.int32, (NG, tm), 0).astype(F32)
    allowed = jnp.zeros((NG, tm), F32)
    for _ in range(TOPK_GROUPS):
        m = jnp.max(gscore, axis=0, keepdims=True)
        first_g = jnp.min(jnp.where(gscore == m, g_id, float(NG)), axis=0, keepdims=True)
        pick = g_id == first_g
        allowed = jnp.where(pick, 1.0, allowed)
        gscore = jnp.where(pick, -jnp.inf, gscore)
    cand = jnp.where(allowed.reshape(NG, 1, tm) > 0.5, b3, NEG_INF).reshape(E, tm)
    e_id = lax.broadcasted_iota(jnp.int32, (E, tm), 0).astype(F32)
    idx_rows, wt_rows, picks = [], [], []
    wsum = jnp.zeros((1, tm), F32)
    chosen = jnp.zeros((E, tm), F32)
    for kk in range(TOP_K):
        m = jnp.max(cand, axis=0, keepdims=True)
        first_e = jnp.min(jnp.where(cand == m, e_id, float(E)), axis=0, keepdims=True)
        pick = e_id == first_e
        w = jnp.sum(jnp.where(pick, scores, 0.0), axis=0, keepdims=True)
        cand = jnp.where(pick, -jnp.inf, cand)
        chosen = jnp.where(pick, 1.0, chosen)
        idx_rows.append(first_e)
        wt_rows.append(w)
        picks.append(pick)
        wsum = wsum + w
    idx_ref[0] = jnp.concatenate(idx_rows, axis=0).astype(jnp.int32)
    wt_ref[0] = jnp.concatenate(wt_rows, axis=0) / wsum * ROUTED_SCALE
    @pl.when((pl.program_id(0) == 0) & (pl.program_id(1) == 0))
    def _():
        cnt_ref[...] = jnp.zeros_like(cnt_ref)

    si = lax.broadcasted_iota(jnp.int32, (tm, tm), 0)
    ti = lax.broadcasted_iota(jnp.int32, (tm, tm), 1)
    before = jnp.where(si < ti, 1.0, 0.0).astype(BF16)
    prior = _dot(chosen.astype(BF16), before) + cnt_ref[...]
    rank_ref[0] = jnp.concatenate([jnp.sum(jnp.where(pk, prior, 0.0), axis=0, keepdims=True) for pk in picks],
                                  axis=0).astype(jnp.int32)
    cnt_ref[...] = cnt_ref[...] + jnp.sum(chosen, axis=1, keepdims=True)


def _shared_expert_kernel(h_ref, wsg_ref, wsu_ref, wsd_ref, o_ref):
    h16 = h_ref[0]
    act = _silu(_dot(h16, wsg_ref[...])) * _dot(h16, wsu_ref[...])
    o_ref[0] = _dot(act.astype(BF16), wsd_ref[...])


def _shared_expert(h, wsg, wsu, wsd):
    B, S, D = h.shape
    FF = wsg.shape[1]
    tm = min(S, 1024)
    row = pl.BlockSpec((1, tm, D), lambda b, i: (b, i, 0))
    full = lambda shape: pl.BlockSpec(shape, lambda b, i: (0,) * len(shape))
    return pl.pallas_call(
        _shared_expert_kernel,
        grid=(B, S // tm),
        in_specs=[row, full((D, FF)), full((D, FF)), full((FF, D))],
        out_specs=row,
        out_shape=jax.ShapeDtypeStruct((B, S, D), F32),
        compiler_params=_cparams(2),
        name="moe_shared",
    )(h, wsg, wsu, wsd)


def _moe_router(x, ada_l, sh_col, sc_col, rwt_hi, rwt_lo, router_b):
    B, S, D = x.shape
    E = rwt_hi.shape[0]
    tm = min(S, 512)
    row = lambda w: pl.BlockSpec((1, tm, w), lambda b, i: (b, i, 0))
    krow = pl.BlockSpec((1, TOP_K, tm), lambda b, i: (b, 0, i))
    full = lambda shape: pl.BlockSpec(shape, lambda b, i: (0,) * len(shape))
    return pl.pallas_call(
        _moe_router_kernel,
        grid=(B, S // tm),
        in_specs=[row(D),
                  pl.BlockSpec((1, 1, D), lambda b, i: (b, 0, sh_col)),
                  pl.BlockSpec((1, 1, D), lambda b, i: (b, 0, sc_col)),
                  full((E, D)), full((E, D)), full((E, 1))],
        out_specs=[row(D), krow, krow, krow, full((E, 1))],
        out_shape=[jax.ShapeDtypeStruct((B, S, D), BF16),
                   jax.ShapeDtypeStruct((B, TOP_K, S), jnp.int32),
                   jax.ShapeDtypeStruct((B, TOP_K, S), F32),
                   jax.ShapeDtypeStruct((B, TOP_K, S), jnp.int32),
                   jax.ShapeDtypeStruct((E, 1), F32)],
        compiler_params=_cparams(2),
        name="moe_router",
    )(x, ada_l, ada_l, rwt_hi, rwt_lo, router_b.reshape(E, 1))


def _moe_experts_kernel(be_ref, nu_ref, xs_ref, wg_ref, wu_ref, wd_ref, *rest, first_block):
    o_ref = rest[-1]

    used = pl.program_id(0) + first_block < nu_ref[0]

    @pl.when(used)
    def _():
        xb = xs_ref[...]
        act = _silu(_dot(xb, wg_ref[0, 0].astype(BF16))) * _dot(xb, wu_ref[0, 0].astype(BF16))
        o_ref[...] = _dot(act.astype(BF16), wd_ref[0, 0].astype(BF16)).astype(o_ref.dtype)

    @pl.when(jnp.logical_not(used))
    def _():
        o_ref[...] = jnp.zeros_like(o_ref)


def _moe_experts(block_expert, n_used, xs, layer, wg, wu, wd, first_block, total_rows, prev=None):
    rows, D = xs.shape
    BM = MOE_ROW_BLOCK
    FF = wg.shape[3]
    fb = first_block
    in_specs = [pl.BlockSpec((BM, D), lambda i, be, nu: (i, 0)),
                pl.BlockSpec((1, 1, D, FF), lambda i, be, nu: (layer, be[i + fb], 0, 0)),
                pl.BlockSpec((1, 1, D, FF), lambda i, be, nu: (layer, be[i + fb], 0, 0)),
                pl.BlockSpec((1, 1, FF, D), lambda i, be, nu: (layer, be[i + fb], 0, 0))]
    args = [block_expert, n_used, xs, wg, wu, wd]
    aliases = {}
    if prev is not None:
        in_specs.append(pl.BlockSpec(memory_space=pl.ANY))
        args.append(prev)
        aliases = {len(args) - 1: 0}
    grid_spec = pltpu.PrefetchScalarGridSpec(
        num_scalar_prefetch=2,
        grid=(rows // BM,),
        in_specs=in_specs,
        out_specs=pl.BlockSpec((BM, D), lambda i, be, nu: (i + fb, 0)),
    )
    return pl.pallas_call(
        functools.partial(_moe_experts_kernel, first_block=fb),
        grid_spec=grid_spec,
        out_shape=jax.ShapeDtypeStruct((total_rows, D), BF16),
        input_output_aliases=aliases,
        compiler_params=_cparams(1),
        name="moe_experts",
    )(*args)


def _moe_out_kernel(x_ref, shared_ref, yg_ref, wt_ref, gt_ref, lg_ref, lb_ref, o_ref, *, alpha):
    y = shared_ref[0]
    wt = wt_ref[0]
    for kk in range(TOP_K):
        y = y + wt[:, kk:kk + 1] * yg_ref[kk, 0].astype(F32)
    o_ref[0] = _post_norm(x_ref[0], y, gt_ref[0], lg_ref[...], lb_ref[...], alpha)


def _moe_out(x, shared, yg, wt, ada_l, gt_col, ln_g, ln_b, alpha, first_b):
    B, S, D = x.shape
    nb = yg.shape[1]
    fb = first_b
    tm = min(S, 256)
    row = lambda w: pl.BlockSpec((1, tm, w), lambda b, i: (b + fb, i, 0))
    full = lambda shape: pl.BlockSpec(shape, lambda b, i: (0,) * len(shape))
    return pl.pallas_call(
        functools.partial(_moe_out_kernel, alpha=alpha),
        grid=(nb, S // tm),
        in_specs=[row(D), row(D),
                  pl.BlockSpec((TOP_K, 1, tm, D), lambda b, i: (0, b, i, 0)),
                  row(wt.shape[2]),
                  pl.BlockSpec((1, 1, D), lambda b, i: (b + fb, 0, gt_col)),
                  full((1, D)), full((1, D))],
        out_specs=row(D),
        out_shape=jax.ShapeDtypeStruct((B, S, D), F32),
        input_output_aliases={0: 0},
        compiler_params=_cparams(2),
        name="moe_out",
    )(x, shared, yg, wt, ada_l, ln_g.reshape(1, D), ln_b.reshape(1, D))


def _moe_dest_kernel(start_ref, e_ref, r_ref, o_ref):
    e = e_ref[...]
    acc = r_ref[...]
    for j in range(start_ref.shape[0]):
        acc = acc + jnp.where(e == j, start_ref[j], 0)
    o_ref[...] = acc


def _routing_layout(expert_idx, rank, counts):
    B, K, S = expert_idx.shape
    T = B * S
    n_experts = counts.shape[0]
    BM = MOE_ROW_BLOCK
    TK = T * K
    counts = counts.reshape(n_experts).astype(jnp.int32)
    padded = (counts + BM - 1) // BM * BM
    pad_end = jnp.cumsum(padded)
    pad_start = (pad_end - padded).astype(jnp.int32)
    ts = min(S, 2048)
    spec = pl.BlockSpec((1, K, ts), lambda b, i, tab: (b, 0, i))
    dest = pl.pallas_call(
        _moe_dest_kernel,
        grid_spec=pltpu.PrefetchScalarGridSpec(num_scalar_prefetch=1, grid=(B, S // ts),
                                               in_specs=[spec, spec], out_specs=spec),
        out_shape=jax.ShapeDtypeStruct((B, K, S), jnp.int32),
        compiler_params=_cparams(2),
        name="moe_dest",
    )(pad_start, expert_idx, rank)
    dest = jnp.swapaxes(dest, 0, 1).reshape(TK)
    n_blocks = (TK + n_experts * (BM - 1) + BM - 1) // BM
    rows = n_blocks * BM
    token = jnp.tile(jnp.arange(T, dtype=jnp.int32), K)
    row_tok = (jnp.arange(rows, dtype=jnp.int32) % T).at[dest].add(token - dest % T, mode='promise_in_bounds')
    block_start = jnp.arange(n_blocks, dtype=jnp.int32) * BM
    block_expert = jnp.minimum(jnp.sum(pad_end[None, :] <= block_start[:, None], axis=1),
                               n_experts - 1).astype(jnp.int32)
    n_used = (pad_end[-1] // BM).astype(jnp.int32).reshape(1)
    return row_tok, dest, block_expert, n_used


def kernel(x, c, positions, ada_w, ada_b, w_in, gdn_conv_w, gdn_a_log, gdn_dt_bias, gdn_norm_w, w_gdn_branch, nsa_cmp_pos_k, nsa_cmp_pos_v, nsa_cmp_k_w1, nsa_cmp_k_w2, nsa_cmp_v_w1, nsa_cmp_v_w2, w_nsa_branch, w_out, ln1_g, ln1_b, router_w, router_bias, w_sh_gate, w_sh_up, w_sh_down, w_e_gate, w_e_up, w_e_down, ln2_g, ln2_b):
    B, S, D = x.shape
    L = ada_w.shape[0]
    T = B * S
    G = NSA_GROUPS
    nh_gdn = gdn_a_log.shape[1]
    gdn_w = nh_gdn * HEAD_DIM
    nsa_w = w_nsa_branch.shape[1]
    kvw = G * HEAD_DIM
    alpha = (2.0 * L) ** 0.25
    assert gdn_w == D and nsa_w == D and S % NSA_SEL_LEN == 0

    splits = (3 * gdn_w, gdn_w, nh_gdn, nh_gdn, nsa_w, 6 * kvw, 3 * (nsa_w // HEAD_DIM), D, D)
    offs = [0]
    for s_ in splits:
        offs.append(offs[-1] + s_)
    seg = lambda i: slice(offs[i], offs[i + 1])
    w_big = jnp.concatenate([w_in[:, :, seg(0)], w_in[:, :, seg(1)], w_in[:, :, seg(4)],
                             w_in[:, :, seg(7)], w_in[:, :, seg(8)], w_in[:, :, seg(5)]], axis=-1).astype(BF16)
    n_small = splits[2] + splits[3] + splits[6]
    w_small = jnp.concatenate([w_in[:, :, seg(2)], w_in[:, :, seg(3)], w_in[:, :, seg(6)],
                               jnp.zeros((L, D, 128 - n_small), w_in.dtype)], axis=-1).astype(BF16)
    Z_BLK, Q_BLK, MA_BLK, MB_BLK = 3, 4, 5, 6
    kv0 = 7 * D
    n_big = w_big.shape[2]
    tn_big = n_big // 4 if (n_big // 4) % 128 == 0 else 128

    wgb = w_gdn_branch.astype(BF16)
    wnb = w_nsa_branch.astype(BF16)
    wob = w_out.astype(BF16)
    wsg = w_sh_gate.astype(BF16)
    wsu = w_sh_up.astype(BF16)
    wsd = w_sh_down.astype(BF16)
    rwt = jnp.swapaxes(router_w, 1, 2)
    rwt_hi = rwt.astype(BF16)
    rwt_lo = (rwt - rwt_hi.astype(F32)).astype(BF16)
    cmp_w1 = jnp.stack([nsa_cmp_k_w1, nsa_cmp_v_w1], axis=1).astype(BF16)
    cmp_w2 = jnp.stack([nsa_cmp_k_w2, nsa_cmp_v_w2], axis=1).astype(BF16)
    cmp_pos = jnp.stack([nsa_cmp_pos_k, nsa_cmp_pos_v], axis=1).reshape(L, 2, 1, NSA_CMP_LEN * HEAD_DIM)

    ada = _ada(c, ada_w, ada_b)
    cosf, sinf = _rope_tables(positions)
    NC = S // NSA_CMP_STRIDE
    last = jnp.minimum(jnp.arange(NC) * NSA_CMP_STRIDE + NSA_CMP_LEN - 1, S - 1)
    cosc = cosf[:, last]
    sinc = sinf[:, last]
    n_sel = -(-(S // NSA_SEL_LEN) // 128) * 128

    for l in range(L):
        ada_l = ada[l].reshape(B, 1, 6 * D)
        big, small, cmp_in = _proj(x, ada_l, 0, 1, w_big[l], w_small[l], tn_big, kv0, 2 * G)
        small_t = jnp.swapaxes(small[:, :, nh_gdn:2 * nh_gdn].reshape(B, S // GDN_CHUNK, GDN_CHUNK, nh_gdn), 2, 3)
        q_a, k_a, v_a = _gdn_prep(big, gdn_conv_w[l], gdn_w)
        o_a = _gdn(q_a, k_a, v_a, big, Z_BLK, small, small_t, gdn_a_log[l], gdn_dt_bias[l], gdn_norm_w[l])
        q_r, ksel_r, kwin_r = _rope_apply(big, Q_BLK, (kv0 + 2 * kvw) // kvw, (kv0 + 4 * kvw) // kvw,
                                          nsa_w, kvw, cosf, sinf)
        hb = cmp_in.reshape(B, 2 * G, NC, NSA_CMP_STRIDE * HEAD_DIM)
        kvc = _compress(hb, cmp_pos[l], cmp_w1[l], cmp_w2[l], cosc, sinc)
        o_c, sel = _cmp_attn(q_r, kvc, small, nh_gdn, n_sel)
        o_s = _sel_attn(q_r, ksel_r, big, (kv0 + 3 * kvw) // HEAD_DIM, sel, small, nh_gdn)
        o_w = _win_attn(q_r, kwin_r, big, (kv0 + 5 * kvw) // HEAD_DIM, small, nh_gdn)
        x = _mixer_out(o_a, o_c, o_s, o_w, big, MA_BLK, MB_BLK, x, ada_l, 2, ln1_g[l], ln1_b[l],
                       wgb[l], wnb[l], wob[l], alpha)
        h2, eidx, ewt, rank, counts = _moe_router(x, ada_l, 3, 4, rwt_hi[l], rwt_lo[l], router_bias[l])
        row_tok, dest, block_expert, n_used = _routing_layout(eidx, rank, counts)
        shared = _shared_expert(h2, wsg[l], wsu[l], wsd[l])
        rows = row_tok.shape[0]
        n_blocks = rows // MOE_ROW_BLOCK
        h2f = h2.reshape(T, D)
        bounds = [n_blocks * p // MOE_DISPATCH_PARTS for p in range(MOE_DISPATCH_PARTS + 1)]
        yb = None
        for lo, hi in zip(bounds[:-1], bounds[1:]):
            xs = h2f.at[row_tok[lo * MOE_ROW_BLOCK:hi * MOE_ROW_BLOCK]].get(mode='promise_in_bounds')
            yb = _moe_experts(block_expert, n_used, xs, l, w_e_gate, w_e_up, w_e_down, lo, rows, prev=yb)
        n_cp = MOE_COMBINE_PARTS if B % MOE_COMBINE_PARTS == 0 else 1
        bp = B // n_cp
        dest3 = dest.reshape(TOP_K, B, S)
        wt = jnp.swapaxes(ewt, 1, 2)
        for p in range(n_cp):
            idx = dest3[:, p * bp:(p + 1) * bp].reshape(-1)
            yg = yb.at[idx].get(mode='promise_in_bounds').reshape(TOP_K, bp, S, D)
            x = _moe_out(x, shared, yg, wt, ada_l, 5, ln2_g[l], ln2_b[l], alpha, p * bp)
    return x
```
